```python
import jax, jax.numpy as jnp
from jax import lax
import numpy as np

D_MODEL = 2048
BATCH = 32
SEQ = 256
DEPTH = 4
DEC_BATCH = 4
DEC_SEQ = 4096
PAST_LEN = 256

GRID_W = 64
ROPE_BASE = 10000.0
EPS = 1e-6
D_FF = 5632
N_MOD = 9
GROUP_W = D_MODEL // 4
ATTN_BLOCK = 128

SSD_HEAD_DIM = 64
SSD_HEADS = GROUP_W // SSD_HEAD_DIM
SSD_STATE = 64
SSD_GROUPS = 2
SSD_CONV_K = 3
SSD_CHUNK = 128
SSD_CONV_CH = GROUP_W + 2 * SSD_GROUPS * SSD_STATE
SSD_IN = GROUP_W + SSD_CONV_CH + SSD_HEADS

MLA_HEADS = 4
MLA_NOPE = 128
MLA_ROPE = 64
MLA_V = GROUP_W // MLA_HEADS
MLA_Q_RANK = 384
MLA_KV_RANK = 128
MLA_QK = MLA_NOPE + MLA_ROPE
MLA_IN = MLA_Q_RANK + MLA_KV_RANK + MLA_ROPE
MLA_CACHE_W = MLA_KV_RANK + MLA_ROPE

RET_HEADS = 4
RET_QK = 64
RET_V = GROUP_W // RET_HEADS
RET_CHUNK = 128
RET_IN = 2 * RET_HEADS * RET_QK + 2 * GROUP_W

GLA_HEADS = 4
GLA_QK = 64
GLA_V = GROUP_W // GLA_HEADS
GLA_GATE_RANK = 16
GLA_GATE_TEMP = 16.0
GLA_CHUNK = 16
GLA_IN = 2 * GLA_HEADS * GLA_QK + GROUP_W + GLA_GATE_RANK + GROUP_W

IN_W = SSD_IN + MLA_IN + RET_IN + GLA_IN

kernel_name = 'hybrid_prefix_diffusion_step'

F32 = jnp.float32


def _split(x, sizes):
    return jnp.split(x, [int(i) for i in np.cumsum(sizes)[:-1]], axis=-1)


def _rms(x, g):
    xf = x.astype(F32)
    y = xf * lax.rsqrt(jnp.mean(xf * xf, axis=-1, keepdims=True) + EPS)
    return (y * g.astype(F32)).astype(x.dtype)


def _layernorm(x, g):
    xf = x.astype(F32)
    xc = xf - jnp.mean(xf, axis=-1, keepdims=True)
    y = xc * lax.rsqrt(jnp.mean(xc * xc, axis=-1, keepdims=True) + EPS)
    return (y * g.astype(F32)).astype(x.dtype)


def _modulate(h, shift, scale):
    return h * (1.0 + scale) + shift


def _swiglu(h, wg, wu, wd):
    return (jax.nn.silu(h @ wg) * (h @ wu)) @ wd


def _dwconv(x, w, b):
    y = lax.conv_general_dilated(x, w[:, None, :].astype(x.dtype), (1,), 'SAME',
                                 dimension_numbers=('NWC', 'WIO', 'NWC'),
                                 feature_group_count=x.shape[-1])
    return y + b


def _axial_angles(T, rot_dim):
    n_rows = T // GRID_W
    row = jnp.repeat(jnp.arange(n_rows, dtype=F32), GRID_W)
    col = jnp.tile(jnp.arange(GRID_W, dtype=F32), n_rows)
    d_axis = rot_dim // 2
    inv = ROPE_BASE ** (-jnp.arange(0, d_axis, 2, dtype=F32) / d_axis)
    return row[:, None] * inv, col[:, None] * inv


def _rotate(x, ang):
    cos, sin = jnp.cos(ang)[:, None, :], jnp.sin(ang)[:, None, :]
    x1, x2 = jnp.split(x.astype(F32), 2, axis=-1)
    return jnp.concatenate([x1 * cos - x2 * sin, x1 * sin + x2 * cos], axis=-1)


def _axial_rope(x, angs):
    d_axis = x.shape[-1] // 2
    return jnp.concatenate([_rotate(x[..., :d_axis], angs[0]),
                            _rotate(x[..., d_axis:], angs[1])], axis=-1).astype(x.dtype)


def _chunk(x, L):
    B, T = x.shape[:2]
    return x.reshape(B, T // L, L, *x.shape[2:])


def _carry(h0, decay, upd):
    def step(h, du):
        d, u = du
        return d * h + u, h
    h_last, h_prev = lax.scan(step, h0, (decay, upd))
    return h_prev, h_last


def _scan_scalar(q, k, v, log_a, h0, L):
    B, T, H, _ = q.shape
    V = v.shape[-1]
    qc, kc, vc = _chunk(q.astype(F32), L), _chunk(k.astype(F32), L), _chunk(v.astype(F32), L)
    b = jnp.cumsum(_chunk(log_a.astype(F32), L), axis=2)
    causal = jnp.tril(jnp.ones((L, L), bool))[None, None, :, :, None]
    seg = b[:, :, :, None, :] - b[:, :, None, :, :]
    dec = jnp.where(causal, jnp.exp(jnp.where(causal, seg, 0.0)), 0.0)
    s = jnp.einsum('bcihk,bcjhk->bcijh', qc, kc) * dec
    o = jnp.einsum('bcijh,bcjhv->bcihv', s, vc)
    b_last = b[:, :, -1:]
    upd = jnp.einsum('bcjhk,bcjhv->cbhkv', kc * jnp.exp(b_last - b)[..., None], vc)
    decay = jnp.exp(b_last[:, :, 0]).transpose(1, 0, 2)[..., None, None]
    h_prev, h_last = _carry(h0.astype(F32), decay, upd)
    o = o + jnp.einsum('bcihk,cbhkv->bcihv', qc * jnp.exp(b)[..., None], h_prev)
    return o.reshape(B, T, H, V).astype(v.dtype), h_last


def _scan_vector(q, k, v, log_a, h0, L):
    B, T, H, _ = q.shape
    V = v.shape[-1]
    qc, kc, vc = _chunk(q.astype(F32), L), _chunk(k.astype(F32), L), _chunk(v.astype(F32), L)
    b = jnp.cumsum(_chunk(log_a.astype(F32), L), axis=2)
    causal = jnp.tril(jnp.ones((L, L), bool))[None, None, :, :, None, None]
    seg = b[:, :, :, None] - b[:, :, None]
    dec = jnp.where(causal, jnp.exp(jnp.where(causal, seg, 0.0)), 0.0)
    s = jnp.sum(qc[:, :, :, None] * kc[:, :, None] * dec, axis=-1)
    o = jnp.einsum('bcijh,bcjhv->bcihv', s, vc)
    b_last = b[:, :, -1:]
    upd = jnp.einsum('bcjhk,bcjhv->cbhkv', kc * jnp.exp(b_last - b), vc)
    decay = jnp.exp(b_last[:, :, 0]).transpose(1, 0, 2, 3)[..., None]
    h_prev, h_last = _carry(h0.astype(F32), decay, upd)
    o = o + jnp.einsum('bcihk,cbhkv->bcihv', qc * jnp.exp(b), h_prev)
    return o.reshape(B, T, H, V).astype(v.dtype), h_last


def _bidir(scan, q, k_f, k_b, v, la_f, la_b, h0_f, h0_b, L):
    o_f, h_f = scan(q, k_f, v, la_f, h0_f, L)
    fl = lambda t: jnp.flip(t, axis=1)
    o_b, h_b = scan(fl(q), fl(k_b), fl(v), fl(la_b), h0_b, L)
    return o_f + fl(o_b), h_f, h_b


def _attend(q, k, v):
    B, T, H, Dk = q.shape
    nb = T // ATTN_BLOCK
    scale = Dk ** -0.5
    qb = jnp.moveaxis(q.reshape(B, nb, ATTN_BLOCK, H, Dk), 1, 0)
    def block(qi):
        s = jnp.einsum('bqhd,bshd->bhqs', qi, k, preferred_element_type=F32) * scale
        p = jax.nn.softmax(s, axis=-1).astype(v.dtype)
        return jnp.einsum('bhqs,bshd->bqhd', p, v)
    o = lax.map(block, qb)
    return jnp.moveaxis(o, 0, 1).reshape(B, T, H, v.shape[-1])


def _ssd(u, lp, h0):
    B, T, _ = u.shape
    z, xbc, dt = _split(u, [GROUP_W, SSD_CONV_CH, SSD_HEADS])
    xbc = jax.nn.silu(_dwconv(xbc, lp['ssd_conv_w'], lp['ssd_conv_b']))
    xs, bm, cm = _split(xbc, [GROUP_W, SSD_GROUPS * SSD_STATE, SSD_GROUPS * SSD_STATE])
    xs = xs.reshape(B, T, SSD_HEADS, SSD_HEAD_DIM)
    rep = SSD_HEADS // SSD_GROUPS
    bm = jnp.repeat(bm.reshape(B, T, SSD_GROUPS, SSD_STATE), rep, axis=2)
    cm = jnp.repeat(cm.reshape(B, T, SSD_GROUPS, SSD_STATE), rep, axis=2)
    a = -jnp.exp(lp['ssd_a_log'].astype(F32))
    dt_bias = lp['ssd_dt_bias'].astype(F32)
    dt_f = jax.nn.softplus(dt.astype(F32) + dt_bias[0])
    dt_b = jax.nn.softplus(dt.astype(F32) + dt_bias[1])
    o, h_f, h_b = _bidir(_scan_scalar, cm, bm * dt_f[..., None], bm * dt_b[..., None], xs,
                         dt_f * a[0], dt_b * a[1], h0[:, 0], h0[:, 1], SSD_CHUNK)
    y = o + lp['ssd_d'][:, None].astype(o.dtype) * xs
    y = y.reshape(B, T, GROUP_W) * jax.nn.silu(z)
    y = _rms(y.reshape(B, T, SSD_GROUPS, -1), lp['ssd_norm_g'].reshape(SSD_GROUPS, -1))
    return y.reshape(B, T, GROUP_W), jnp.stack([h_f, h_b], axis=1)


def _mla_kv(kv_lat, lp, angs):
    B, S, _ = kv_lat.shape
    c_kv, k_pe = _split(kv_lat, [MLA_KV_RANK, MLA_ROPE])
    kv = (c_kv @ lp['mla_w_kv_up']).reshape(B, S, MLA_HEADS, MLA_NOPE + MLA_V)
    k_nope, v = _split(kv, [MLA_NOPE, MLA_V])
    k_pe = jnp.broadcast_to(k_pe[:, :, None, :], (B, S, MLA_HEADS, MLA_ROPE)).astype(k_nope.dtype)
    k = _rms(jnp.concatenate([k_nope, k_pe], axis=-1), lp['mla_k_gain'])
    if angs is not None:
        k = jnp.concatenate([k[..., :MLA_NOPE], _axial_rope(k[..., MLA_NOPE:], angs)], axis=-1)
    return k, v


def _mla(u, lp, ctx_kv, angs):
    B, T, _ = u.shape
    c_q, c_kv, k_pe = _split(u, [MLA_Q_RANK, MLA_KV_RANK, MLA_ROPE])
    q = (_rms(c_q, lp['mla_q_lat_gain']) @ lp['mla_w_q_up']).reshape(B, T, MLA_HEADS, MLA_QK)
    q = _rms(q, lp['mla_q_gain'])
    kv_lat = jnp.concatenate([_rms(c_kv, lp['mla_kv_lat_gain']), k_pe], axis=-1)
    k, v = _mla_kv(kv_lat, lp, angs)
    if ctx_kv is not None:
        q = jnp.concatenate([q[..., :MLA_NOPE], _axial_rope(q[..., MLA_NOPE:], angs)], axis=-1)
        k_c, v_c = _mla_kv(ctx_kv.astype(kv_lat.dtype), lp, None)
        k = jnp.concatenate([k, k_c], axis=1)
        v = jnp.concatenate([v, v_c], axis=1)
    o = _attend(q, k, v)
    return o.reshape(B, T, GROUP_W), kv_lat


def _retention(u, lp, h0, angs):
    B, T, _ = u.shape
    q, k, v, g = _split(u, [RET_HEADS * RET_QK, RET_HEADS * RET_QK, GROUP_W, GROUP_W])
    q = q.reshape(B, T, RET_HEADS, RET_QK)
    k = k.reshape(B, T, RET_HEADS, RET_QK) * (RET_QK ** -0.5)
    v = v.reshape(B, T, RET_HEADS, RET_V)
    if angs is not None:
        q, k = _axial_rope(q, angs), _axial_rope(k, angs)
    log_g = jax.nn.log_sigmoid(lp['ret_decay_logit'].astype(F32))
    la_f = jnp.broadcast_to(log_g[0], (B, T, RET_HEADS))
    la_b = jnp.broadcast_to(log_g[1], (B, T, RET_HEADS))
    o, h_f, h_b = _bidir(_scan_scalar, q, k, k, v, la_f, la_b, h0[:, 0], h0[:, 1], RET_CHUNK)
    o = _layernorm(o, lp['ret_norm_g'].reshape(RET_HEADS, RET_V)).reshape(B, T, GROUP_W)
    return jax.nn.silu(g) * o, jnp.stack([h_f, h_b], axis=1)


def _gla(u, lp, h0):
    B, T, _ = u.shape
    q, k, v, g1, r = _split(u, [GLA_HEADS * GLA_QK, GLA_HEADS * GLA_QK, GROUP_W, GLA_GATE_RANK, GROUP_W])
    q = q.reshape(B, T, GLA_HEADS, GLA_QK) * (GLA_QK ** -0.5)
    k = k.reshape(B, T, GLA_HEADS, GLA_QK)
    v = v.reshape(B, T, GLA_HEADS, GLA_V)
    def log_alpha(d):
        logits = (g1 @ lp['gla_w_g2'][d] + lp['gla_b_g'][d]).astype(F32)
        return (jax.nn.log_sigmoid(logits) / GLA_GATE_TEMP).reshape(B, T, GLA_HEADS, GLA_QK)
    o, h_f, h_b = _bidir(_scan_vector, q, k, k, v, log_alpha(0), log_alpha(1), h0[:, 0], h0[:, 1], GLA_CHUNK)
    o = _rms(o, lp['gla_norm_g'].reshape(GLA_HEADS, GLA_V)).reshape(B, T, GROUP_W)
    return jax.nn.silu(r) * o, jnp.stack([h_f, h_b], axis=1)


def _token_mixers(h, lp, ctx_state, angs):
    B = h.shape[0]
    u_ssd, u_mla, u_ret, u_gla = _split(h @ lp['w_in'], [SSD_IN, MLA_IN, RET_IN, GLA_IN])
    if ctx_state is None:
        mla_ctx = None
        ssd_h0 = jnp.zeros((B, 2, SSD_HEADS, SSD_STATE, SSD_HEAD_DIM), F32)
        ret_h0 = jnp.zeros((B, 2, RET_HEADS, RET_QK, RET_V), F32)
        gla_h0 = jnp.zeros((B, 2, GLA_HEADS, GLA_QK, GLA_V), F32)
    else:
        mla_ctx, ssd_h0, ret_h0, gla_h0 = ctx_state
    y_ssd, s_ssd = _ssd(u_ssd, lp, ssd_h0)
    y_mla, kv_lat = _mla(u_mla, lp, mla_ctx, angs)
    y_ret, s_ret = _retention(u_ret, lp, ret_h0, angs)
    y_gla, s_gla = _gla(u_gla, lp, gla_h0)
    y = jnp.concatenate([y_ssd, y_mla, y_ret, y_gla], axis=-1) @ lp['w_out']
    return y, (kv_lat, s_ssd, s_ret, s_gla)


def _layer(x, mod, lp, ctx_state, angs):
    m = jnp.split(mod.astype(x.dtype), N_MOD, axis=-1)
    h = _modulate(_rms(x, lp['norm_ffn1']), m[0], m[1])
    x = x + 0.5 * m[2] * _swiglu(h, lp['ffn1_wg'], lp['ffn1_wu'], lp['ffn1_wd'])
    h = _modulate(_rms(x, lp['norm_mix']), m[3], m[4])
    y, states = _token_mixers(h, lp, ctx_state, angs)
    x = x + m[5] * y
    h = _modulate(_rms(x, lp['norm_ffn2']), m[6], m[7])
    x = x + 0.5 * m[8] * _swiglu(h, lp['ffn2_wg'], lp['ffn2_wu'], lp['ffn2_wd'])
    return x, states


def setup_inputs(seed: int = 0) -> dict:
    key = jax.random.key(seed)
    ks = iter(jax.random.split(key, 48))
    def nrm(shape, s):
        return jax.random.normal(next(ks), shape, jnp.float32) * s
    def gain(shape):
        return 1.0 + nrm(shape, 0.02)
    L, D = DEPTH, D_MODEL
    dt0 = jnp.exp(jax.random.uniform(next(ks), (L, 2, SSD_HEADS), jnp.float32,
                                     float(np.log(1e-3)), float(np.log(1e-1))))
    ret_base = jnp.asarray(np.log(2.0 ** (5 + np.arange(RET_HEADS)) - 1.0), jnp.float32)
    return {
        'x_prompt': nrm((BATCH, SEQ, D), 1.0),
        'x_sample': nrm((DEC_BATCH, DEC_SEQ, D), 1.0),
        'cache_mla_kv': nrm((DEC_BATCH, L, PAST_LEN, MLA_CACHE_W), 1.0),
        'state_ssd': nrm((DEC_BATCH, L, 2, SSD_HEADS, SSD_STATE, SSD_HEAD_DIM), 0.5),
        'state_ret': nrm((DEC_BATCH, L, 2, RET_HEADS, RET_QK, RET_V), 0.5),
        'state_gla': nrm((DEC_BATCH, L, 2, GLA_HEADS, GLA_QK, GLA_V), 0.5),
        'c': nrm((DEC_BATCH, D), 1.0),
        'c_ctx': nrm((D,), 1.0),
        'w_mod': nrm((L, D, N_MOD * D), 0.5 * D ** -0.5),
        'b_mod': nrm((L, N_MOD * D), 0.02),
        'norm_ffn1': gain((L, D)),
        'ffn1_wg': nrm((L, D, D_FF), D ** -0.5),
        'ffn1_wu': nrm((L, D, D_FF), D ** -0.5),
        'ffn1_wd': nrm((L, D_FF, D), D_FF ** -0.5),
        'norm_mix': gain((L, D)),
        'w_in': nrm((L, D, IN_W), D ** -0.5),
        'ssd_conv_w': nrm((L, SSD_CONV_K, SSD_CONV_CH), SSD_CONV_K ** -0.5),
        'ssd_conv_b': nrm((L, SSD_CONV_CH), 0.02),
        'ssd_a_log': jnp.log(jax.random.uniform(next(ks), (L, 2, SSD_HEADS), jnp.float32, 1.0, 16.0)),
        'ssd_dt_bias': dt0 + jnp.log(-jnp.expm1(-dt0)),
        'ssd_d': gain((L, SSD_HEADS)),
        'ssd_norm_g': gain((L, GROUP_W)),
        'mla_q_lat_gain': gain((L, MLA_Q_RANK)),
        'mla_w_q_up': nrm((L, MLA_Q_RANK, MLA_HEADS * MLA_QK), MLA_Q_RANK ** -0.5),
        'mla_q_gain': gain((L, MLA_QK)),
        'mla_kv_lat_gain': gain((L, MLA_KV_RANK)),
        'mla_w_kv_up': nrm((L, MLA_KV_RANK, MLA_HEADS * (MLA_NOPE + MLA_V)), MLA_KV_RANK ** -0.5),
        'mla_k_gain': gain((L, MLA_QK)),
        'ret_decay_logit': ret_base + nrm((L, 2, RET_HEADS), 0.01),
        'ret_norm_g': gain((L, GROUP_W)),
        'gla_w_g2': nrm((L, 2, GLA_GATE_RANK, GLA_HEADS * GLA_QK), GLA_GATE_RANK ** -0.5),
        'gla_b_g': nrm((L, 2, GLA_HEADS * GLA_QK), 0.01),
        'gla_norm_g': gain((L, GROUP_W)),
        'w_out': nrm((L, D, D), D ** -0.5),
        'norm_ffn2': gain((L, D)),
        'ffn2_wg': nrm((L, D, D_FF), D ** -0.5),
        'ffn2_wu': nrm((L, D, D_FF), D ** -0.5),
        'ffn2_wd': nrm((L, D_FF, D), D_FF ** -0.5),
    }


def reference(x_prompt, x_sample, cache_mla_kv, state_ssd, state_ret, state_gla, c, c_ctx,
              w_mod, b_mod, norm_ffn1, ffn1_wg, ffn1_wu, ffn1_wd, norm_mix, w_in,
              ssd_conv_w, ssd_conv_b, ssd_a_log, ssd_dt_bias, ssd_d, ssd_norm_g,
              mla_q_lat_gain, mla_w_q_up, mla_q_gain, mla_kv_lat_gain, mla_w_kv_up, mla_k_gain,
              ret_decay_logit, ret_norm_g, gla_w_g2, gla_b_g, gla_norm_g, w_out,
              norm_ffn2, ffn2_wg, ffn2_wu, ffn2_wd):
    stacked = dict(norm_ffn1=norm_ffn1, ffn1_wg=ffn1_wg, ffn1_wu=ffn1_wu, ffn1_wd=ffn1_wd,
                   norm_mix=norm_mix, w_in=w_in, ssd_conv_w=ssd_conv_w, ssd_conv_b=ssd_conv_b,
                   ssd_a_log=ssd_a_log, ssd_dt_bias=ssd_dt_bias, ssd_d=ssd_d, ssd_norm_g=ssd_norm_g,
                   mla_q_lat_gain=mla_q_lat_gain, mla_w_q_up=mla_w_q_up, mla_q_gain=mla_q_gain,
                   mla_kv_lat_gain=mla_kv_lat_gain, mla_w_kv_up=mla_w_kv_up, mla_k_gain=mla_k_gain,
                   ret_decay_logit=ret_decay_logit, ret_norm_g=ret_norm_g,
                   gla_w_g2=gla_w_g2, gla_b_g=gla_b_g, gla_norm_g=gla_norm_g, w_out=w_out,
                   norm_ffn2=norm_ffn2, ffn2_wg=ffn2_wg, ffn2_wu=ffn2_wu, ffn2_wd=ffn2_wd)
    angs = _axial_angles(x_sample.shape[1], MLA_ROPE)
    y_p, y_s = x_prompt, x_sample
    kv_list, ssd_list, ret_list, gla_list = [], [], [], []
    for l in range(DEPTH):
        lp = {name: arr[l] for name, arr in stacked.items()}
        mod_ctx = (jax.nn.silu(c_ctx) @ w_mod[l] + b_mod[l])[None, None, :]
        mod_lat = (jax.nn.silu(c) @ w_mod[l] + b_mod[l])[:, None, :]
        y_p, (kv_lat, s_ssd, s_ret, s_gla) = _layer(y_p, mod_ctx, lp, None, None)
        kv_list.append(kv_lat)
        ssd_list.append(s_ssd)
        ret_list.append(s_ret)
        gla_list.append(s_gla)
        ctx_state = (cache_mla_kv[:, l], state_ssd[:, l], state_ret[:, l], state_gla[:, l])
        y_s, _ = _layer(y_s, mod_lat, lp, ctx_state, angs)
    new_mla_kv = jnp.stack(kv_list, axis=1)
    new_ssd = jnp.stack(ssd_list, axis=1)
    new_ret = jnp.stack(ret_list, axis=1)
    new_gla = jnp.stack(gla_list, axis=1)
    return (y_p, y_s, new_mla_kv, new_ssd, new_ret, new_gla)
```

```python
import functools

import jax
import jax.numpy as jnp
import numpy as np
from jax import lax
from jax.experimental import pallas as pl
from jax.experimental.pallas import tpu as pltpu

F32 = jnp.float32
BF16 = jnp.bfloat16

D_MODEL = 2048
BATCH = 32
SEQ = 256
DEPTH = 4
DEC_BATCH = 4
DEC_SEQ = 4096
PAST_LEN = 256
GRID_W = 64
ROPE_BASE = 10000.0
EPS = 1e-6
D_FF = 5632
N_MOD = 9
GROUP_W = D_MODEL // 4
ATTN_BLOCK = 128

SSD_HEAD_DIM = 64
SSD_HEADS = GROUP_W // SSD_HEAD_DIM
SSD_STATE = 64
SSD_GROUPS = 2
SSD_CONV_K = 3
SSD_CHUNK = 128
SSD_CONV_CH = GROUP_W + 2 * SSD_GROUPS * SSD_STATE
SSD_IN = GROUP_W + SSD_CONV_CH + SSD_HEADS

MLA_HEADS = 4
MLA_NOPE = 128
MLA_ROPE = 64
MLA_V = GROUP_W // MLA_HEADS
MLA_Q_RANK = 384
MLA_KV_RANK = 128
MLA_QK = MLA_NOPE + MLA_ROPE
MLA_IN = MLA_Q_RANK + MLA_KV_RANK + MLA_ROPE
MLA_CACHE_W = MLA_KV_RANK + MLA_ROPE

RET_HEADS = 4
RET_QK = 64
RET_V = GROUP_W // RET_HEADS
RET_CHUNK = 128
RET_IN = 2 * RET_HEADS * RET_QK + 2 * GROUP_W

GLA_HEADS = 4
GLA_QK = 64
GLA_V = GROUP_W // GLA_HEADS
GLA_GATE_RANK = 16
GLA_GATE_TEMP = 16.0
GLA_CHUNK = 16
GLA_IN = 2 * GLA_HEADS * GLA_QK + GROUP_W + GLA_GATE_RANK + GROUP_W

IN_W = SSD_IN + MLA_IN + RET_IN + GLA_IN

LANE = 128
CTX_ROWS = BATCH * SEQ
LAT_ROWS = DEC_BATCH * DEC_SEQ
ROWS = CTX_ROWS + LAT_ROWS
MOD_ROWS = 8
VMEM_LIMIT = 56 * 1024 * 1024

MAIN_SSD_Z = 0
MAIN_SSD_XBC = MAIN_SSD_Z + GROUP_W
MAIN_MLA_CQ = MAIN_SSD_XBC + SSD_CONV_CH
MAIN_RET = MAIN_MLA_CQ + MLA_Q_RANK
MAIN_GLA = MAIN_RET + RET_IN
MAIN_USED = MAIN_GLA + 2 * GLA_HEADS * GLA_QK + 2 * GROUP_W
MAIN_W = 4864
MAIN_TN = MAIN_W // 2
SIDE_CKV = 0
SIDE_KPE = SIDE_CKV + MLA_KV_RANK
SIDE_DT = 2 * LANE
SIDE_G1 = SIDE_DT + SSD_HEADS
SIDE_W = 3 * LANE


def _in_proj_columns():
    o_ssd, o_mla, o_ret, o_gla = 0, SSD_IN, SSD_IN + MLA_IN, SSD_IN + MLA_IN + RET_IN
    main = np.full((MAIN_W,), -1, np.int64)
    main[MAIN_SSD_Z:MAIN_SSD_Z + GROUP_W + SSD_CONV_CH] = o_ssd + np.arange(GROUP_W + SSD_CONV_CH)
    main[MAIN_MLA_CQ:MAIN_MLA_CQ + MLA_Q_RANK] = o_mla + np.arange(MLA_Q_RANK)
    main[MAIN_RET:MAIN_RET + RET_IN] = o_ret + np.arange(RET_IN)
    n_qkv = 2 * GLA_HEADS * GLA_QK + GROUP_W
    main[MAIN_GLA:MAIN_GLA + n_qkv] = o_gla + np.arange(n_qkv)
    main[MAIN_GLA + n_qkv:MAIN_GLA + n_qkv + GROUP_W] = o_gla + n_qkv + GLA_GATE_RANK + np.arange(GROUP_W)
    side = np.full((SIDE_W,), -1, np.int64)
    side[SIDE_CKV:SIDE_CKV + MLA_KV_RANK + MLA_ROPE] = o_mla + MLA_Q_RANK + np.arange(MLA_KV_RANK + MLA_ROPE)
    side[SIDE_DT:SIDE_DT + SSD_HEADS] = o_ssd + GROUP_W + SSD_CONV_CH + np.arange(SSD_HEADS)
    side[SIDE_G1:SIDE_G1 + GLA_GATE_RANK] = o_gla + n_qkv + np.arange(GLA_GATE_RANK)
    return main, side


def _gather_columns(w, idx):
    safe = np.where(idx < 0, 0, idx)
    out = jnp.take(w, jnp.asarray(safe, jnp.int32), axis=2)
    return jnp.where(jnp.asarray(idx >= 0)[None, None, :], out, 0.0)


def _mod_row(i, tm):
    ctx_tiles = CTX_ROWS // tm
    per_seq = DEC_SEQ // tm
    return jnp.where(i < ctx_tiles, 0, 1 + (i - ctx_tiles) // per_seq)


def _mod_spec(layer, which, tm, n_grid):
    if n_grid == 1:
        return pl.BlockSpec((1, 1, D_MODEL), lambda i: (layer * MOD_ROWS + _mod_row(i, tm), 0, which))
    return pl.BlockSpec((1, 1, D_MODEL), lambda i, j: (layer * MOD_ROWS + _mod_row(i, tm), 0, which))


def _mod_kernel(c_ref, w_ref, b_ref, o_ref):
    c = c_ref[...]
    s = c * jax.nn.sigmoid(c)
    hi = s.astype(BF16)
    lo = (s - hi.astype(F32)).astype(BF16)
    w = w_ref[...].astype(BF16)
    o_ref[...] = (jnp.dot(hi, w, preferred_element_type=F32)
                  + jnp.dot(lo, w, preferred_element_type=F32) + b_ref[...])


def _modulation(c_all, w_mod, b_mod):
    tn = 1024
    n = N_MOD * D_MODEL
    return pl.pallas_call(
        _mod_kernel,
        grid=(DEPTH, n // tn),
        in_specs=[pl.BlockSpec((MOD_ROWS, D_MODEL), lambda l, j: (0, 0)),
                  pl.BlockSpec((None, D_MODEL, tn), lambda l, j: (l, 0, j)),
                  pl.BlockSpec((None, 1, tn), lambda l, j: (l, 0, j))],
        out_specs=pl.BlockSpec((None, MOD_ROWS, tn), lambda l, j: (l, 0, j)),
        out_shape=jax.ShapeDtypeStruct((DEPTH, MOD_ROWS, n), F32),
        compiler_params=pltpu.CompilerParams(dimension_semantics=("arbitrary", "arbitrary"),
                                             vmem_limit_bytes=VMEM_LIMIT),
        name="modulation",
    )(c_all, w_mod, b_mod.reshape(DEPTH, 1, n))


def _norm_modulate(x, g, shift, scale):
    r = lax.rsqrt(jnp.mean(x * x, axis=-1, keepdims=True) + EPS)
    return (x * r * g) * (1.0 + scale) + shift


def _ffn_kernel(x_ref, g_ref, shift_ref, scale_ref, gate_ref, wg_ref, wu_ref, wd_ref, o_ref, h_ref):
    j = pl.program_id(1)

    @pl.when(j == 0)
    def _():
        h_ref[...] = _norm_modulate(x_ref[...], g_ref[...], shift_ref[0], scale_ref[0]).astype(BF16)

    h = h_ref[...]
    g = jnp.dot(h, wg_ref[...], preferred_element_type=F32)
    u = jnp.dot(h, wu_ref[...], preferred_element_type=F32)
    a = (g * jax.nn.sigmoid(g) * u).astype(BF16)
    y = jnp.dot(a, wd_ref[...], preferred_element_type=F32)

    @pl.when(j == 0)
    def _():
        o_ref[...] = y

    @pl.when(j > 0)
    def _():
        o_ref[...] += y

    @pl.when(j == pl.num_programs(1) - 1)
    def _():
        o_ref[...] = x_ref[...] + 0.5 * gate_ref[0] * o_ref[...]


def _ffn(x, mod, norm_g, wg, wu, wd, layer, mod_base, tm=512, tf=512):
    row = pl.BlockSpec((tm, D_MODEL), lambda i, j: (i, 0))
    return pl.pallas_call(
        _ffn_kernel,
        grid=(ROWS // tm, D_FF // tf),
        in_specs=[row,
                  pl.BlockSpec((None, 1, D_MODEL), lambda i, j: (layer, 0, 0)),
                  _mod_spec(layer, mod_base + 0, tm, 2),
                  _mod_spec(layer, mod_base + 1, tm, 2),
                  _mod_spec(layer, mod_base + 2, tm, 2),
                  pl.BlockSpec((None, D_MODEL, tf), lambda i, j: (layer, 0, j)),
                  pl.BlockSpec((None, D_MODEL, tf), lambda i, j: (layer, 0, j)),
                  pl.BlockSpec((None, tf, D_MODEL), lambda i, j: (layer, j, 0))],
        out_specs=row,
        out_shape=jax.ShapeDtypeStruct((ROWS, D_MODEL), F32),
        scratch_shapes=[pltpu.VMEM((tm, D_MODEL), BF16)],
        compiler_params=pltpu.CompilerParams(dimension_semantics=("parallel", "arbitrary"),
                                             vmem_limit_bytes=VMEM_LIMIT),
        name="ffn",
    )(x, norm_g, mod, mod, mod, wg, wu, wd)


def _in_proj_kernel(x_ref, g_ref, shift_ref, scale_ref, w_ref, o_ref, h_ref):
    @pl.when(pl.program_id(1) == 0)
    def _():
        h_ref[...] = _norm_modulate(x_ref[...], g_ref[...], shift_ref[0], scale_ref[0]).astype(BF16)

    o_ref[...] = jnp.dot(h_ref[...], w_ref[...], preferred_element_type=F32).astype(o_ref.dtype)


def _in_proj(x, mod, norm_g, w, layer, out_dtype, tn, tm=512):
    n = w.shape[-1]
    return pl.pallas_call(
        _in_proj_kernel,
        grid=(ROWS // tm, n // tn),
        in_specs=[pl.BlockSpec((tm, D_MODEL), lambda i, j: (i, 0)),
                  pl.BlockSpec((None, 1, D_MODEL), lambda i, j: (layer, 0, 0)),
                  _mod_spec(layer, 3, tm, 2),
                  _mod_spec(layer, 4, tm, 2),
                  pl.BlockSpec((None, D_MODEL, tn), lambda i, j: (layer, 0, j))],
        out_specs=pl.BlockSpec((tm, tn), lambda i, j: (i, j)),
        out_shape=jax.ShapeDtypeStruct((ROWS, n), out_dtype),
        scratch_shapes=[pltpu.VMEM((tm, D_MODEL), BF16)],
        compiler_params=pltpu.CompilerParams(dimension_semantics=("parallel", "arbitrary"),
                                             vmem_limit_bytes=VMEM_LIMIT),
        name="in_proj",
    )(x, norm_g, mod, mod, w)


def _out_proj_kernel(x_ref, y0_ref, y1_ref, y2_ref, y3_ref, gate_ref, w_ref, o_ref):
    acc = jnp.dot(y0_ref[...], w_ref[0 * GROUP_W:1 * GROUP_W, :], preferred_element_type=F32)
    acc += jnp.dot(y1_ref[...], w_ref[1 * GROUP_W:2 * GROUP_W, :], preferred_element_type=F32)
    acc += jnp.dot(y2_ref[...], w_ref[2 * GROUP_W:3 * GROUP_W, :], preferred_element_type=F32)
    acc += jnp.dot(y3_ref[...], w_ref[3 * GROUP_W:4 * GROUP_W, :], preferred_element_type=F32)
    o_ref[...] = x_ref[...] + gate_ref[0] * acc


def _out_proj(x, ys, mod, w, layer, tm=512):
    row = pl.BlockSpec((tm, D_MODEL), lambda i: (i, 0))
    yspec = pl.BlockSpec((tm, GROUP_W), lambda i: (i, 0))
    return pl.pallas_call(
        _out_proj_kernel,
        grid=(ROWS // tm,),
        in_specs=[row, yspec, yspec, yspec, yspec,
                  _mod_spec(layer, 5, tm, 1),
                  pl.BlockSpec((None, D_MODEL, D_MODEL), lambda i: (layer, 0, 0))],
        out_specs=row,
        out_shape=jax.ShapeDtypeStruct((ROWS, D_MODEL), F32),
        compiler_params=pltpu.CompilerParams(dimension_semantics=("parallel",),
                                             vmem_limit_bytes=VMEM_LIMIT),
        name="out_proj",
    )(x, *ys, mod, w)


def _split(x, sizes):
    return jnp.split(x, [int(i) for i in np.cumsum(sizes)[:-1]], axis=-1)


def _rms(x, g):
    xf = x.astype(F32)
    y = xf * lax.rsqrt(jnp.mean(xf * xf, axis=-1, keepdims=True) + EPS)
    return (y * g.astype(F32)).astype(x.dtype)


def _layernorm(x, g):
    xf = x.astype(F32)
    xc = xf - jnp.mean(xf, axis=-1, keepdims=True)
    y = xc * lax.rsqrt(jnp.mean(xc * xc, axis=-1, keepdims=True) + EPS)
    return (y * g.astype(F32)).astype(x.dtype)


def _dwconv(x, w, b):
    y = lax.conv_general_dilated(x, w[:, None, :].astype(x.dtype), (1,), 'SAME',
                                 dimension_numbers=('NWC', 'WIO', 'NWC'),
                                 feature_group_count=x.shape[-1])
    return y + b


def _axial_angles(T, rot_dim):
    n_rows = T // GRID_W
    row = jnp.repeat(jnp.arange(n_rows, dtype=F32), GRID_W)
    col = jnp.tile(jnp.arange(GRID_W, dtype=F32), n_rows)
    d_axis = rot_dim // 2
    inv = ROPE_BASE ** (-jnp.arange(0, d_axis, 2, dtype=F32) / d_axis)
    return row[:, None] * inv, col[:, None] * inv


def _rotate(x, ang):
    cos, sin = jnp.cos(ang)[:, None, :], jnp.sin(ang)[:, None, :]
    x1, x2 = jnp.split(x.astype(F32), 2, axis=-1)
    return jnp.concatenate([x1 * cos - x2 * sin, x1 * sin + x2 * cos], axis=-1)


def _axial_rope(x, angs):
    d_axis = x.shape[-1] // 2
    return jnp.concatenate([_rotate(x[..., :d_axis], angs[0]),
                            _rotate(x[..., d_axis:], angs[1])], axis=-1).astype(x.dtype)


def _chunk(x, L):
    B, T = x.shape[:2]
    return x.reshape(B, T // L, L, *x.shape[2:])


def _carry(h0, decay, upd):
    def step(h, du):
        d, u = du
        return d * h + u, h
    h_last, h_prev = lax.scan(step, h0, (decay, upd))
    return h_prev, h_last


def _scan_scalar(q, k, v, log_a, h0, L):
    B, T, H, _ = q.shape
    V = v.shape[-1]
    qc, kc, vc = _chunk(q.astype(F32), L), _chunk(k.astype(F32), L), _chunk(v.astype(F32), L)
    b = jnp.cumsum(_chunk(log_a.astype(F32), L), axis=2)
    causal = jnp.tril(jnp.ones((L, L), bool))[None, None, :, :, None]
    seg = b[:, :, :, None, :] - b[:, :, None, :, :]
    dec = jnp.where(causal, jnp.exp(jnp.where(causal, seg, 0.0)), 0.0)
    s = jnp.einsum('bcihk,bcjhk->bcijh', qc, kc) * dec
    o = jnp.einsum('bcijh,bcjhv->bcihv', s, vc)
    b_last = b[:, :, -1:]
    upd = jnp.einsum('bcjhk,bcjhv->cbhkv', kc * jnp.exp(b_last - b)[..., None], vc)
    decay = jnp.exp(b_last[:, :, 0]).transpose(1, 0, 2)[..., None, None]
    h_prev, h_last = _carry(h0.astype(F32), decay, upd)
    o = o + jnp.einsum('bcihk,cbhkv->bcihv', qc * jnp.exp(b)[..., None], h_prev)
    return o.reshape(B, T, H, V).astype(v.dtype), h_last


def _scan_vector(q, k, v, log_a, h0, L):
    B, T, H, _ = q.shape
    V = v.shape[-1]
    qc, kc, vc = _chunk(q.astype(F32), L), _chunk(k.astype(F32), L), _chunk(v.astype(F32), L)
    b = jnp.cumsum(_chunk(log_a.astype(F32), L), axis=2)
    causal = jnp.tril(jnp.ones((L, L), bool))[None, None, :, :, None, None]
    seg = b[:, :, :, None] - b[:, :, None]
    dec = jnp.where(causal, jnp.exp(jnp.where(causal, seg, 0.0)), 0.0)
    s = jnp.sum(qc[:, :, :, None] * kc[:, :, None] * dec, axis=-1)
    o = jnp.einsum('bcijh,bcjhv->bcihv', s, vc)
    b_last = b[:, :, -1:]
    upd = jnp.einsum('bcjhk,bcjhv->cbhkv', kc * jnp.exp(b_last - b), vc)
    decay = jnp.exp(b_last[:, :, 0]).transpose(1, 0, 2, 3)[..., None]
    h_prev, h_last = _carry(h0.astype(F32), decay, upd)
    o = o + jnp.einsum('bcihk,cbhkv->bcihv', qc * jnp.exp(b), h_prev)
    return o.reshape(B, T, H, V).astype(v.dtype), h_last


def _bidir(scan, q, k_f, k_b, v, la_f, la_b, h0_f, h0_b, L):
    o_f, h_f = scan(q, k_f, v, la_f, h0_f, L)
    fl = lambda t: jnp.flip(t, axis=1)
    o_b, h_b = scan(fl(q), fl(k_b), fl(v), fl(la_b), h0_b, L)
    return o_f + fl(o_b), h_f, h_b


def _attend(q, k, v):
    B, T, H, Dk = q.shape
    nb = T // ATTN_BLOCK
    scale = Dk ** -0.5
    qb = jnp.moveaxis(q.reshape(B, nb, ATTN_BLOCK, H, Dk), 1, 0)

    def block(qi):
        s = jnp.einsum('bqhd,bshd->bhqs', qi, k, preferred_element_type=F32) * scale
        p = jax.nn.softmax(s, axis=-1).astype(v.dtype)
        return jnp.einsum('bhqs,bshd->bqhd', p, v)
    o = lax.map(block, qb)
    return jnp.moveaxis(o, 0, 1).reshape(B, T, H, v.shape[-1])


def _ssd(z, xbc, dt, lp, h0):
    B, T, _ = z.shape
    xbc = jax.nn.silu(_dwconv(xbc, lp['ssd_conv_w'], lp['ssd_conv_b']))
    xs, bm, cm = _split(xbc, [GROUP_W, SSD_GROUPS * SSD_STATE, SSD_GROUPS * SSD_STATE])
    xs = xs.reshape(B, T, SSD_HEADS, SSD_HEAD_DIM)
    rep = SSD_HEADS // SSD_GROUPS
    bm = jnp.repeat(bm.reshape(B, T, SSD_GROUPS, SSD_STATE), rep, axis=2)
    cm = jnp.repeat(cm.reshape(B, T, SSD_GROUPS, SSD_STATE), rep, axis=2)
    a = -jnp.exp(lp['ssd_a_log'].astype(F32))
    dt_bias = lp['ssd_dt_bias'].astype(F32)
    dt_f = jax.nn.softplus(dt.astype(F32) + dt_bias[0])
    dt_b = jax.nn.softplus(dt.astype(F32) + dt_bias[1])
    o, h_f, h_b = _bidir(_scan_scalar, cm, bm * dt_f[..., None], bm * dt_b[..., None], xs,
                         dt_f * a[0], dt_b * a[1], h0[:, 0], h0[:, 1], SSD_CHUNK)
    y = o + lp['ssd_d'][:, None].astype(o.dtype) * xs
    y = y.reshape(B, T, GROUP_W) * jax.nn.silu(z)
    y = _rms(y.reshape(B, T, SSD_GROUPS, -1), lp['ssd_norm_g'].reshape(SSD_GROUPS, -1))
    return y.reshape(B, T, GROUP_W), jnp.stack([h_f, h_b], axis=1)


def _mla_kv(kv_lat, lp, angs):
    B, S, _ = kv_lat.shape
    c_kv, k_pe = _split(kv_lat, [MLA_KV_RANK, MLA_ROPE])
    kv = (c_kv @ lp['mla_w_kv_up']).reshape(B, S, MLA_HEADS, MLA_NOPE + MLA_V)
    k_nope, v = _split(kv, [MLA_NOPE, MLA_V])
    k_pe = jnp.broadcast_to(k_pe[:, :, None, :], (B, S, MLA_HEADS, MLA_ROPE)).astype(k_nope.dtype)
    k = _rms(jnp.concatenate([k_nope, k_pe], axis=-1), lp['mla_k_gain'])
    if angs is not None:
        k = jnp.concatenate([k[..., :MLA_NOPE], _axial_rope(k[..., MLA_NOPE:], angs)], axis=-1)
    return k, v


def _mla(c_q, c_kv, k_pe, lp, ctx_kv, angs):
    B, T, _ = c_q.shape
    q = (_rms(c_q, lp['mla_q_lat_gain']) @ lp['mla_w_q_up']).reshape(B, T, MLA_HEADS, MLA_QK)
    q = _rms(q, lp['mla_q_gain'])
    kv_lat = jnp.concatenate([_rms(c_kv, lp['mla_kv_lat_gain']), k_pe], axis=-1)
    k, v = _mla_kv(kv_lat, lp, angs)
    if ctx_kv is not None:
        q = jnp.concatenate([q[..., :MLA_NOPE], _axial_rope(q[..., MLA_NOPE:], angs)], axis=-1)
        k_c, v_c = _mla_kv(ctx_kv.astype(kv_lat.dtype), lp, None)
        k = jnp.concatenate([k, k_c], axis=1)
        v = jnp.concatenate([v, v_c], axis=1)
    o = _attend(q, k, v)
    return o.reshape(B, T, GROUP_W), kv_lat


def _retention(q, k, v, g, lp, h0, angs):
    B, T, _ = q.shape
    q = q.reshape(B, T, RET_HEADS, RET_QK)
    k = k.reshape(B, T, RET_HEADS, RET_QK) * (RET_QK ** -0.5)
    v = v.reshape(B, T, RET_HEADS, RET_V)
    if angs is not None:
        q, k = _axial_rope(q, angs), _axial_rope(k, angs)
    log_g = jax.nn.log_sigmoid(lp['ret_decay_logit'].astype(F32))
    la_f = jnp.broadcast_to(log_g[0], (B, T, RET_HEADS))
    la_b = jnp.broadcast_to(log_g[1], (B, T, RET_HEADS))
    o, h_f, h_b = _bidir(_scan_scalar, q, k, k, v, la_f, la_b, h0[:, 0], h0[:, 1], RET_CHUNK)
    o = _layernorm(o, lp['ret_norm_g'].reshape(RET_HEADS, RET_V)).reshape(B, T, GROUP_W)
    return jax.nn.silu(g) * o, jnp.stack([h_f, h_b], axis=1)


def _gla(q, k, v, g1, r, lp, h0):
    B, T, _ = q.shape
    q = q.reshape(B, T, GLA_HEADS, GLA_QK) * (GLA_QK ** -0.5)
    k = k.reshape(B, T, GLA_HEADS, GLA_QK)
    v = v.reshape(B, T, GLA_HEADS, GLA_V)

    def log_alpha(d):
        logits = (g1 @ lp['gla_w_g2'][d] + lp['gla_b_g'][d]).astype(F32)
        return (jax.nn.log_sigmoid(logits) / GLA_GATE_TEMP).reshape(B, T, GLA_HEADS, GLA_QK)
    o, h_f, h_b = _bidir(_scan_vector, q, k, k, v, log_alpha(0), log_alpha(1), h0[:, 0], h0[:, 1], GLA_CHUNK)
    o = _rms(o, lp['gla_norm_g'].reshape(GLA_HEADS, GLA_V)).reshape(B, T, GROUP_W)
    return jax.nn.silu(r) * o, jnp.stack([h_f, h_b], axis=1)


def _mixers_group(um, us, lp, ctx_state, angs):
    B = um.shape[0]
    umf = um.astype(F32)
    z = umf[..., MAIN_SSD_Z:MAIN_SSD_Z + GROUP_W]
    xbc = umf[..., MAIN_SSD_XBC:MAIN_SSD_XBC + SSD_CONV_CH]
    c_q = umf[..., MAIN_MLA_CQ:MAIN_MLA_CQ + MLA_Q_RANK]
    rq, rk, rv, rg = _split(umf[..., MAIN_RET:MAIN_RET + RET_IN],
                            [RET_HEADS * RET_QK, RET_HEADS * RET_QK, GROUP_W, GROUP_W])
    gq, gk, gv, gr = _split(umf[..., MAIN_GLA:MAIN_USED],
                            [GLA_HEADS * GLA_QK, GLA_HEADS * GLA_QK, GROUP_W, GROUP_W])
    c_kv = us[..., SIDE_CKV:SIDE_CKV + MLA_KV_RANK]
    k_pe = us[..., SIDE_KPE:SIDE_KPE + MLA_ROPE]
    dt = us[..., SIDE_DT:SIDE_DT + SSD_HEADS]
    g1 = us[..., SIDE_G1:SIDE_G1 + GLA_GATE_RANK]
    if ctx_state is None:
        mla_ctx = None
        ssd_h0 = jnp.zeros((B, 2, SSD_HEADS, SSD_STATE, SSD_HEAD_DIM), F32)
        ret_h0 = jnp.zeros((B, 2, RET_HEADS, RET_QK, RET_V), F32)
        gla_h0 = jnp.zeros((B, 2, GLA_HEADS, GLA_QK, GLA_V), F32)
    else:
        mla_ctx, ssd_h0, ret_h0, gla_h0 = ctx_state
    y_ssd, s_ssd = _ssd(z, xbc, dt, lp, ssd_h0)
    y_mla, kv_lat = _mla(c_q, c_kv, k_pe, lp, mla_ctx, angs)
    y_ret, s_ret = _retention(rq, rk, rv, rg, lp, ret_h0, angs)
    y_gla, s_gla = _gla(gq, gk, gv, g1, gr, lp, gla_h0)
    return (y_ssd, y_mla, y_ret, y_gla), (kv_lat, s_ssd, s_ret, s_gla)


def kernel(x_prompt, x_sample, cache_mla_kv, state_ssd, state_ret, state_gla, c, c_ctx, w_mod, b_mod, norm_ffn1, ffn1_wg, ffn1_wu, ffn1_wd, norm_mix, w_in, ssd_conv_w, ssd_conv_b, ssd_a_log, ssd_dt_bias, ssd_d, ssd_norm_g, mla_q_lat_gain, mla_w_q_up, mla_q_gain, mla_kv_lat_gain, mla_w_kv_up, mla_k_gain, ret_decay_logit, ret_norm_g, gla_w_g2, gla_b_g, gla_norm_g, w_out, norm_ffn2, ffn2_wg, ffn2_wu, ffn2_wd):
    stacked = dict(ssd_conv_w=ssd_conv_w, ssd_conv_b=ssd_conv_b,
                   ssd_a_log=ssd_a_log, ssd_dt_bias=ssd_dt_bias, ssd_d=ssd_d, ssd_norm_g=ssd_norm_g,
                   mla_q_lat_gain=mla_q_lat_gain, mla_w_q_up=mla_w_q_up, mla_q_gain=mla_q_gain,
                   mla_kv_lat_gain=mla_kv_lat_gain, mla_w_kv_up=mla_w_kv_up, mla_k_gain=mla_k_gain,
                   ret_decay_logit=ret_decay_logit, ret_norm_g=ret_norm_g,
                   gla_w_g2=gla_w_g2, gla_b_g=gla_b_g, gla_norm_g=gla_norm_g)
    angs = _axial_angles(DEC_SEQ, MLA_ROPE)

    main_idx, side_idx = _in_proj_columns()
    w_main = _gather_columns(w_in, main_idx).astype(BF16)
    w_side = _gather_columns(w_in, side_idx).astype(BF16)
    w_out_b = w_out.astype(BF16)
    f1 = (ffn1_wg.astype(BF16), ffn1_wu.astype(BF16), ffn1_wd.astype(BF16))
    f2 = (ffn2_wg.astype(BF16), ffn2_wu.astype(BF16), ffn2_wd.astype(BF16))
    g_ffn1 = norm_ffn1.reshape(DEPTH, 1, D_MODEL)
    g_mix = norm_mix.reshape(DEPTH, 1, D_MODEL)
    g_ffn2 = norm_ffn2.reshape(DEPTH, 1, D_MODEL)

    c_all = jnp.zeros((MOD_ROWS, D_MODEL), F32).at[0].set(c_ctx).at[1:1 + DEC_BATCH].set(c)
    mod = _modulation(c_all, w_mod, b_mod).reshape(DEPTH * MOD_ROWS, 1, N_MOD * D_MODEL)

    x = jnp.concatenate([x_prompt.reshape(CTX_ROWS, D_MODEL), x_sample.reshape(LAT_ROWS, D_MODEL)], axis=0)
    kv_list, ssd_list, ret_list, gla_list = [], [], [], []
    for l in range(DEPTH):
        lp = {name: arr[l] for name, arr in stacked.items()}
        x = _ffn(x, mod, g_ffn1, *f1, l, 0)
        um = _in_proj(x, mod, g_mix, w_main, l, BF16, MAIN_TN)
        us = _in_proj(x, mod, g_mix, w_side, l, F32, SIDE_W)
        ys_c, (kv_lat, s_ssd, s_ret, s_gla) = _mixers_group(
            um[:CTX_ROWS].reshape(BATCH, SEQ, MAIN_W), us[:CTX_ROWS].reshape(BATCH, SEQ, SIDE_W), lp, None, None)
        ctx_state = (cache_mla_kv[:, l], state_ssd[:, l], state_ret[:, l], state_gla[:, l])
        ys_l, _ = _mixers_group(
            um[CTX_ROWS:].reshape(DEC_BATCH, DEC_SEQ, MAIN_W), us[CTX_ROWS:].reshape(DEC_BATCH, DEC_SEQ, SIDE_W),
            lp, ctx_state, angs)
        ys = [jnp.concatenate([a.reshape(CTX_ROWS, GROUP_W), b.reshape(LAT_ROWS, GROUP_W)], axis=0).astype(BF16)
              for a, b in zip(ys_c, ys_l)]
        kv_list.append(kv_lat)
        ssd_list.append(s_ssd)
        ret_list.append(s_ret)
        gla_list.append(s_gla)
        x = _out_proj(x, ys, mod, w_out_b, l)
        x = _ffn(x, mod, g_ffn2, *f2, l, 6)
    y_p = x[:CTX_ROWS].reshape(BATCH, SEQ, D_MODEL)
    y_s = x[CTX_ROWS:].reshape(DEC_BATCH, DEC_SEQ, D_MODEL)
    return (y_p, y_s, jnp.stack(kv_list, axis=1), jnp.stack(ssd_list, axis=1),
            jnp.stack(ret_list, axis=1), jnp.stack(gla_list, axis=1))
```

```python
import functools

import jax
import jax.numpy as jnp
import numpy as np
from jax import lax
from jax.experimental import pallas as pl
from jax.experimental.pallas import tpu as pltpu

F32 = jnp.float32
BF16 = jnp.bfloat16

D_MODEL = 2048
BATCH = 32
SEQ = 256
DEPTH = 4
DEC_BATCH = 4
DEC_SEQ = 4096
PAST_LEN = 256
GRID_W = 64
ROPE_BASE = 10000.0
EPS = 1e-6
D_FF = 5632
N_MOD = 9
GROUP_W = D_MODEL // 4
ATTN_BLOCK = 128

SSD_HEAD_DIM = 64
SSD_HEADS = GROUP_W // SSD_HEAD_DIM
SSD_STATE = 64
SSD_GROUPS = 2
SSD_CONV_K = 3
SSD_CHUNK = 128
SSD_CONV_CH = GROUP_W + 2 * SSD_GROUPS * SSD_STATE
SSD_IN = GROUP_W + SSD_CONV_CH + SSD_HEADS

MLA_HEADS = 4
MLA_NOPE = 128
MLA_ROPE = 64
MLA_V = GROUP_W // MLA_HEADS
MLA_Q_RANK = 384
MLA_KV_RANK = 128
MLA_QK = MLA_NOPE + MLA_ROPE
MLA_IN = MLA_Q_RANK + MLA_KV_RANK + MLA_ROPE
MLA_CACHE_W = MLA_KV_RANK + MLA_ROPE

RET_HEADS = 4
RET_QK = 64
RET_V = GROUP_W // RET_HEADS
RET_CHUNK = 128
RET_IN = 2 * RET_HEADS * RET_QK + 2 * GROUP_W

GLA_HEADS = 4
GLA_QK = 64
GLA_V = GROUP_W // GLA_HEADS
GLA_GATE_RANK = 16
GLA_GATE_TEMP = 16.0
GLA_CHUNK = 16
GLA_IN = 2 * GLA_HEADS * GLA_QK + GROUP_W + GLA_GATE_RANK + GROUP_W

IN_W = SSD_IN + MLA_IN + RET_IN + GLA_IN

LANE = 128
CTX_ROWS = BATCH * SEQ
LAT_ROWS = DEC_BATCH * DEC_SEQ
ROWS = CTX_ROWS + LAT_ROWS
MOD_ROWS = 8
VMEM_LIMIT = 56 * 1024 * 1024
TOK_TM = 512
MLA_HEAD_W = 2 * LANE

MAIN_SSD_Z = 0
MAIN_SSD_XS = 512
MAIN_RET_V = 1024
MAIN_RET_G = 1536
MAIN_GLA_V = 2048
MAIN_GLA_R = 2560
MAIN_SSD_BC = 3072
MAIN_RET_Q = 3328
MAIN_RET_K = 3584
MAIN_GLA_Q = 3840
MAIN_GLA_K = 4096
MAIN_MLA_CQ = 4352
MAIN_W = 4864
MAIN_TN = MAIN_W // 2
SIDE_CKV = 0
SIDE_KPE = SIDE_CKV + MLA_KV_RANK
SIDE_DT = 2 * LANE
SIDE_G1 = SIDE_DT + SSD_HEADS
SIDE_W = 3 * LANE


def _in_proj_columns():
    o_ssd, o_mla, o_ret, o_gla = 0, SSD_IN, SSD_IN + MLA_IN, SSD_IN + MLA_IN + RET_IN
    main = np.full((MAIN_W,), -1, np.int64)

    def put(dst, src, n):
        main[dst:dst + n] = src + np.arange(n)
    qk = RET_HEADS * RET_QK
    put(MAIN_SSD_Z, o_ssd, GROUP_W)
    put(MAIN_SSD_XS, o_ssd + GROUP_W, GROUP_W)
    put(MAIN_SSD_BC, o_ssd + 2 * GROUP_W, 2 * SSD_GROUPS * SSD_STATE)
    put(MAIN_MLA_CQ, o_mla, MLA_Q_RANK)
    put(MAIN_RET_Q, o_ret, qk)
    put(MAIN_RET_K, o_ret + qk, qk)
    put(MAIN_RET_V, o_ret + 2 * qk, GROUP_W)
    put(MAIN_RET_G, o_ret + 2 * qk + GROUP_W, GROUP_W)
    put(MAIN_GLA_Q, o_gla, qk)
    put(MAIN_GLA_K, o_gla + qk, qk)
    put(MAIN_GLA_V, o_gla + 2 * qk, GROUP_W)
    put(MAIN_GLA_R, o_gla + 2 * qk + GROUP_W + GLA_GATE_RANK, GROUP_W)
    side = np.full((SIDE_W,), -1, np.int64)
    side[SIDE_CKV:SIDE_CKV + MLA_KV_RANK + MLA_ROPE] = o_mla + MLA_Q_RANK + np.arange(MLA_KV_RANK + MLA_ROPE)
    side[SIDE_DT:SIDE_DT + SSD_HEADS] = o_ssd + GROUP_W + SSD_CONV_CH + np.arange(SSD_HEADS)
    side[SIDE_G1:SIDE_G1 + GLA_GATE_RANK] = o_gla + 2 * qk + GROUP_W + np.arange(GLA_GATE_RANK)
    return main, side


def _gather_columns(w, idx):
    safe = np.where(idx < 0, 0, idx)
    out = jnp.take(w, jnp.asarray(safe, jnp.int32), axis=-1)
    return jnp.where(jnp.asarray(idx >= 0), out, 0.0)


def _mod_row(i, tm):
    ctx_tiles = CTX_ROWS // tm
    per_seq = DEC_SEQ // tm
    return jnp.where(i < ctx_tiles, 0, 1 + (i - ctx_tiles) // per_seq)


def _mod_spec(layer, which, tm, n_grid):
    if n_grid == 1:
        return pl.BlockSpec((1, 1, D_MODEL), lambda i: (layer * MOD_ROWS + _mod_row(i, tm), 0, which))
    return pl.BlockSpec((1, 1, D_MODEL), lambda i, j: (layer * MOD_ROWS + _mod_row(i, tm), 0, which))


def _dot(a, b):
    return jnp.dot(a, b, preferred_element_type=F32)


def _dot_nt(a, b):
    return lax.dot_general(a, b, (((1,), (1,)), ((), ())), preferred_element_type=F32)


def _dot_tn(a, b):
    return lax.dot_general(a, b, (((0,), (0,)), ((), ())), preferred_element_type=F32)


def _silu(x):
    return x * jax.nn.sigmoid(x)


def _mod_kernel(c_ref, w_ref, b_ref, o_ref):
    s = _silu(c_ref[...])
    hi = s.astype(BF16)
    lo = (s - hi.astype(F32)).astype(BF16)
    w = w_ref[...].astype(BF16)
    o_ref[...] = _dot(hi, w) + _dot(lo, w) + b_ref[...]


def _modulation(c_all, w_mod, b_mod):
    tn = 1024
    n = N_MOD * D_MODEL
    return pl.pallas_call(
        _mod_kernel,
        grid=(DEPTH, n // tn),
        in_specs=[pl.BlockSpec((MOD_ROWS, D_MODEL), lambda l, j: (0, 0)),
                  pl.BlockSpec((None, D_MODEL, tn), lambda l, j: (l, 0, j)),
                  pl.BlockSpec((None, 1, tn), lambda l, j: (l, 0, j))],
        out_specs=pl.BlockSpec((None, MOD_ROWS, tn), lambda l, j: (l, 0, j)),
        out_shape=jax.ShapeDtypeStruct((DEPTH, MOD_ROWS, n), F32),
        compiler_params=pltpu.CompilerParams(dimension_semantics=("arbitrary", "arbitrary"),
                                             vmem_limit_bytes=VMEM_LIMIT),
        name="modulation",
    )(c_all, w_mod, b_mod.reshape(DEPTH, 1, n))


def _norm_modulate(x, g, shift, scale):
    r = lax.rsqrt(jnp.mean(x * x, axis=-1, keepdims=True) + EPS)
    return (x * r * g) * (1.0 + scale) + shift


def _ffn_kernel(x_ref, g_ref, shift_ref, scale_ref, gate_ref, wg_ref, wu_ref, wd_ref, o_ref, h_ref):
    j = pl.program_id(1)

    @pl.when(j == 0)
    def _():
        h_ref[...] = _norm_modulate(x_ref[...], g_ref[...], shift_ref[0], scale_ref[0]).astype(BF16)

    h = h_ref[...]
    g = _dot(h, wg_ref[...])
    u = _dot(h, wu_ref[...])
    a = (_silu(g) * u).astype(BF16)
    y = _dot(a, wd_ref[...])

    @pl.when(j == 0)
    def _():
        o_ref[...] = y

    @pl.when(j > 0)
    def _():
        o_ref[...] += y

    @pl.when(j == pl.num_programs(1) - 1)
    def _():
        o_ref[...] = x_ref[...] + 0.5 * gate_ref[0] * o_ref[...]


def _ffn(x, mod, norm_g, wg, wu, wd, layer, mod_base, tm=TOK_TM, tf=512):
    row = pl.BlockSpec((tm, D_MODEL), lambda i, j: (i, 0))
    return pl.pallas_call(
        _ffn_kernel,
        grid=(ROWS // tm, D_FF // tf),
        in_specs=[row,
                  pl.BlockSpec((None, 1, D_MODEL), lambda i, j: (layer, 0, 0)),
                  _mod_spec(layer, mod_base + 0, tm, 2),
                  _mod_spec(layer, mod_base + 1, tm, 2),
                  _mod_spec(layer, mod_base + 2, tm, 2),
                  pl.BlockSpec((None, D_MODEL, tf), lambda i, j: (layer, 0, j)),
                  pl.BlockSpec((None, D_MODEL, tf), lambda i, j: (layer, 0, j)),
                  pl.BlockSpec((None, tf, D_MODEL), lambda i, j: (layer, j, 0))],
        out_specs=row,
        out_shape=jax.ShapeDtypeStruct((ROWS, D_MODEL), F32),
        scratch_shapes=[pltpu.VMEM((tm, D_MODEL), BF16)],
        compiler_params=pltpu.CompilerParams(dimension_semantics=("parallel", "arbitrary"),
                                             vmem_limit_bytes=VMEM_LIMIT),
        name="ffn",
    )(x, norm_g, mod, mod, mod, wg, wu, wd)


def _in_proj_kernel(x_ref, g_ref, shift_ref, scale_ref, w_ref, o_ref, h_ref):
    @pl.when(pl.program_id(1) == 0)
    def _():
        h_ref[...] = _norm_modulate(x_ref[...], g_ref[...], shift_ref[0], scale_ref[0]).astype(BF16)

    o_ref[...] = _dot(h_ref[...], w_ref[...]).astype(o_ref.dtype)


def _in_proj(x, mod, norm_g, w, layer, out_dtype, tn, tm=TOK_TM):
    n = w.shape[-1]
    return pl.pallas_call(
        _in_proj_kernel,
        grid=(ROWS // tm, n // tn),
        in_specs=[pl.BlockSpec((tm, D_MODEL), lambda i, j: (i, 0)),
                  pl.BlockSpec((None, 1, D_MODEL), lambda i, j: (layer, 0, 0)),
                  _mod_spec(layer, 3, tm, 2),
                  _mod_spec(layer, 4, tm, 2),
                  pl.BlockSpec((None, D_MODEL, tn), lambda i, j: (layer, 0, j))],
        out_specs=pl.BlockSpec((tm, tn), lambda i, j: (i, j)),
        out_shape=jax.ShapeDtypeStruct((ROWS, n), out_dtype),
        scratch_shapes=[pltpu.VMEM((tm, D_MODEL), BF16)],
        compiler_params=pltpu.CompilerParams(dimension_semantics=("parallel", "arbitrary"),
                                             vmem_limit_bytes=VMEM_LIMIT),
        name="in_proj",
    )(x, norm_g, mod, mod, w)


def _out_proj_kernel(x_ref, y0_ref, y1_ref, y2_ref, y3_ref, gate_ref, w_ref, o_ref):
    acc = _dot(y0_ref[...], w_ref[0 * GROUP_W:1 * GROUP_W, :])
    acc += _dot(y1_ref[...], w_ref[1 * GROUP_W:2 * GROUP_W, :])
    acc += _dot(y2_ref[...], w_ref[2 * GROUP_W:3 * GROUP_W, :])
    acc += _dot(y3_ref[...], w_ref[3 * GROUP_W:4 * GROUP_W, :])
    o_ref[...] = x_ref[...] + gate_ref[0] * acc


def _out_proj(x, ys, mod, w, layer, tm=TOK_TM):
    row = pl.BlockSpec((tm, D_MODEL), lambda i: (i, 0))
    yspec = pl.BlockSpec((tm, GROUP_W), lambda i: (i, 0))
    return pl.pallas_call(
        _out_proj_kernel,
        grid=(ROWS // tm,),
        in_specs=[row, yspec, yspec, yspec, yspec,
                  _mod_spec(layer, 5, tm, 1),
                  pl.BlockSpec((None, D_MODEL, D_MODEL), lambda i: (layer, 0, 0))],
        out_specs=row,
        out_shape=jax.ShapeDtypeStruct((ROWS, D_MODEL), F32),
        compiler_params=pltpu.CompilerParams(dimension_semantics=("parallel",),
                                             vmem_limit_bytes=VMEM_LIMIT),
        name="out_proj",
    )(x, *ys, mod, w)


def _rope_tables(T):
    n_rows = T // GRID_W
    row = jnp.repeat(jnp.arange(n_rows, dtype=F32), GRID_W)
    col = jnp.tile(jnp.arange(GRID_W, dtype=F32), n_rows)
    d_axis = MLA_ROPE // 2
    inv = ROPE_BASE ** (-jnp.arange(0, d_axis, 2, dtype=F32) / d_axis)
    ar, ac = row[:, None] * inv, col[:, None] * inv
    cos = jnp.concatenate([jnp.cos(ar), jnp.cos(ar), jnp.cos(ac), jnp.cos(ac)], axis=-1)
    sin = jnp.concatenate([-jnp.sin(ar), jnp.sin(ar), -jnp.sin(ac), jnp.sin(ac)], axis=-1)
    return jnp.tile(cos, (1, 2)), jnp.tile(sin, (1, 2))


def _swap16(x):
    lane = lax.broadcasted_iota(jnp.int32, x.shape, 1)
    up = pltpu.roll(x, LANE - 16, 1)
    down = pltpu.roll(x, 16, 1)
    return jnp.where((lane % 32) < 16, up, down)


def _rope(x, cos, sin):
    return x * cos + _swap16(x) * sin


def _mla_q_kernel(c0_ref, c1_ref, c2_ref, gl_ref, w_ref, gq_ref, cos_ref, sin_ref, o_ref):
    cs = [r[...].astype(F32) for r in (c0_ref, c1_ref, c2_ref)]
    ss = sum(jnp.sum(c * c, axis=-1, keepdims=True) for c in cs)
    r = lax.rsqrt(ss / MLA_Q_RANK + EPS)
    q = sum(_dot((cs[i] * r * gl_ref[:, i * LANE:(i + 1) * LANE]).astype(BF16),
                 w_ref[i * LANE:(i + 1) * LANE, :]) for i in range(3))
    cos, sin = cos_ref[...], sin_ref[...]
    scale = MLA_QK ** -0.5
    for h in range(MLA_HEADS):
        a = q[:, h * MLA_HEAD_W:h * MLA_HEAD_W + LANE]
        b = q[:, h * MLA_HEAD_W + LANE:(h + 1) * MLA_HEAD_W]
        ssq = jnp.sum(a * a, axis=-1, keepdims=True) + jnp.sum(b * b, axis=-1, keepdims=True)
        rh = lax.rsqrt(ssq / MLA_QK + EPS) * scale
        o_ref[:, h * MLA_HEAD_W:h * MLA_HEAD_W + LANE] = (a * rh * gq_ref[:, :LANE]).astype(BF16)
        o_ref[:, h * MLA_HEAD_W + LANE:(h + 1) * MLA_HEAD_W] = _rope(b * rh * gq_ref[:, LANE:], cos, sin).astype(BF16)


def _rope_tile_index(i, tm):
    ctx_tiles = CTX_ROWS // tm
    return jnp.where(i < ctx_tiles, 0, 1 + (i - ctx_tiles) % (DEC_SEQ // tm))


def _mla_q(um, gl, wq, gq, cos_all, sin_all, layer, tm=TOK_TM):
    cq = MAIN_MLA_CQ // LANE
    tab = pl.BlockSpec((tm, LANE), lambda i: (_rope_tile_index(i, tm), 0))
    return pl.pallas_call(
        _mla_q_kernel,
        grid=(ROWS // tm,),
        in_specs=[pl.BlockSpec((tm, LANE), lambda i: (i, cq)),
                  pl.BlockSpec((tm, LANE), lambda i: (i, cq + 1)),
                  pl.BlockSpec((tm, LANE), lambda i: (i, cq + 2)),
                  pl.BlockSpec((None, 1, MLA_Q_RANK), lambda i: (layer, 0, 0)),
                  pl.BlockSpec((None, MLA_Q_RANK, MLA_HEADS * MLA_HEAD_W), lambda i: (layer, 0, 0)),
                  pl.BlockSpec((None, 1, MLA_HEAD_W), lambda i: (layer, 0, 0)),
                  tab, tab],
        out_specs=pl.BlockSpec((tm, MLA_HEADS * MLA_HEAD_W), lambda i: (i, 0)),
        out_shape=jax.ShapeDtypeStruct((ROWS, MLA_HEADS * MLA_HEAD_W), BF16),
        compiler_params=pltpu.CompilerParams(dimension_semantics=("parallel",), vmem_limit_bytes=VMEM_LIMIT),
        name="mla_q",
    )(um, um, um, gl, wq, gq, cos_all, sin_all)


def _mla_kv_kernel(s_ref, gl_ref, w_ref, gk_ref, cos_ref, sin_ref, lat_ref, k_ref, v_ref, *, normalize):
    ckv = s_ref[:, :LANE]
    kpe = s_ref[:, LANE:]
    if normalize:
        ckv = ckv * lax.rsqrt(jnp.mean(ckv * ckv, axis=-1, keepdims=True) + EPS) * gl_ref[...]
    lat_ref[:, :LANE] = ckv
    lat_ref[:, LANE:] = kpe
    kv = _dot(ckv.astype(BF16), w_ref[...])
    ss_pe = jnp.sum(kpe * kpe, axis=-1, keepdims=True)
    cos, sin = cos_ref[...], sin_ref[...]
    for h in range(MLA_HEADS):
        a = kv[:, h * LANE:(h + 1) * LANE]
        rh = lax.rsqrt((jnp.sum(a * a, axis=-1, keepdims=True) + ss_pe) / MLA_QK + EPS)
        k_ref[:, h * MLA_HEAD_W:h * MLA_HEAD_W + LANE] = (a * rh * gk_ref[:, :LANE]).astype(BF16)
        k_ref[:, h * MLA_HEAD_W + LANE:(h + 1) * MLA_HEAD_W] = _rope(kpe * rh * gk_ref[:, LANE:], cos, sin).astype(BF16)
    v_ref[...] = kv[:, MLA_HEADS * LANE:].astype(BF16)


def _mla_kv(src, gl, wkv, gk, cos_all, sin_all, layer, *, normalize, tab_index, tm=TOK_TM):
    rows = src.shape[0]
    tab = pl.BlockSpec((tm, LANE), lambda i: (tab_index(i, tm), 0))
    return pl.pallas_call(
        functools.partial(_mla_kv_kernel, normalize=normalize),
        grid=(rows // tm,),
        in_specs=[pl.BlockSpec((tm, 2 * LANE), lambda i: (i, 0)),
                  pl.BlockSpec((None, 1, MLA_KV_RANK), lambda i: (layer, 0, 0)),
                  pl.BlockSpec((None, MLA_KV_RANK, 2 * MLA_HEADS * LANE), lambda i: (layer, 0, 0)),
                  pl.BlockSpec((None, 1, MLA_HEAD_W), lambda i: (layer, 0, 0)),
                  tab, tab],
        out_specs=[pl.BlockSpec((tm, 2 * LANE), lambda i: (i, 0)),
                   pl.BlockSpec((tm, MLA_HEADS * MLA_HEAD_W), lambda i: (i, 0)),
                   pl.BlockSpec((tm, GROUP_W), lambda i: (i, 0))],
        out_shape=[jax.ShapeDtypeStruct((rows, 2 * LANE), F32),
                   jax.ShapeDtypeStruct((rows, MLA_HEADS * MLA_HEAD_W), BF16),
                   jax.ShapeDtypeStruct((rows, GROUP_W), BF16)],
        compiler_params=pltpu.CompilerParams(dimension_semantics=("parallel",), vmem_limit_bytes=VMEM_LIMIT),
        name="mla_kv",
    )(src, gl, wkv, gk, cos_all, sin_all)


def _attn_kernel(q_ref, k_ref, v_ref, o_ref):
    s = _dot_nt(q_ref[...], k_ref[...])
    m = jnp.max(s, axis=-1, keepdims=True)
    p = jnp.exp(s - m)
    l = jnp.sum(p, axis=-1, keepdims=True)
    o_ref[...] = (_dot(p.astype(BF16), v_ref[...]) / l).astype(o_ref.dtype)


def _attention(q, k, v, n_seq, t, tq):
    s_len = k.shape[1]
    nq = t // tq
    return pl.pallas_call(
        _attn_kernel,
        grid=(n_seq, MLA_HEADS, nq),
        in_specs=[pl.BlockSpec((tq, MLA_HEAD_W), lambda b, h, i: (b * nq + i, h)),
                  pl.BlockSpec((None, s_len, MLA_HEAD_W), lambda b, h, i: (b, 0, h)),
                  pl.BlockSpec((None, s_len, MLA_V), lambda b, h, i: (b, 0, h))],
        out_specs=pl.BlockSpec((tq, MLA_V), lambda b, h, i: (b * nq + i, h)),
        out_shape=jax.ShapeDtypeStruct((n_seq * t, GROUP_W), BF16),
        compiler_params=pltpu.CompilerParams(dimension_semantics=("parallel", "parallel", "arbitrary"),
                                             vmem_limit_bytes=VMEM_LIMIT),
        name="mla_attention",
    )(q, k, v)


def _mla_weights(mla_q_lat_gain, mla_w_q_up, mla_q_gain, mla_kv_lat_gain, mla_w_kv_up, mla_k_gain):
    qcol = np.full((MLA_HEADS * MLA_HEAD_W,), -1, np.int64)
    kvcol = np.zeros((2 * MLA_HEADS * LANE,), np.int64)
    for h in range(MLA_HEADS):
        qcol[h * MLA_HEAD_W:h * MLA_HEAD_W + MLA_QK] = h * MLA_QK + np.arange(MLA_QK)
        kvcol[h * LANE:(h + 1) * LANE] = h * (MLA_NOPE + MLA_V) + np.arange(MLA_NOPE)
        kvcol[(MLA_HEADS + h) * LANE:(MLA_HEADS + h + 1) * LANE] = h * (MLA_NOPE + MLA_V) + MLA_NOPE + np.arange(MLA_V)
    pad = jnp.zeros((DEPTH, MLA_HEAD_W - MLA_QK), F32)
    return dict(
        gl=mla_q_lat_gain.reshape(DEPTH, 1, MLA_Q_RANK),
        wq=_gather_columns(mla_w_q_up, qcol).astype(BF16),
        gq=jnp.concatenate([mla_q_gain, pad], axis=-1).reshape(DEPTH, 1, MLA_HEAD_W),
        gkv=mla_kv_lat_gain.reshape(DEPTH, 1, MLA_KV_RANK),
        wkv=_gather_columns(mla_w_kv_up, kvcol).astype(BF16),
        gk=jnp.concatenate([mla_k_gain, pad], axis=-1).reshape(DEPTH, 1, MLA_HEAD_W))


def _mla_layer(um, us, cache_l, mw, cos_all, sin_all, layer):
    q = _mla_q(um, mw['gl'], mw['wq'], mw['gq'], cos_all, sin_all, layer)
    kv_lat, k, v = _mla_kv(us, mw['gkv'], mw['wkv'], mw['gk'], cos_all, sin_all, layer,
                           normalize=True, tab_index=_rope_tile_index)
    cache2 = jnp.pad(cache_l.reshape(DEC_BATCH * PAST_LEN, MLA_CACHE_W), ((0, 0), (0, 2 * LANE - MLA_CACHE_W)))
    _, k_c, v_c = _mla_kv(cache2, mw['gkv'], mw['wkv'], mw['gk'], cos_all, sin_all, layer,
                          normalize=False, tab_index=lambda i, tm: 0)
    hw = MLA_HEADS * MLA_HEAD_W
    y_ctx = _attention(q[:CTX_ROWS], k[:CTX_ROWS].reshape(BATCH, SEQ, hw),
                       v[:CTX_ROWS].reshape(BATCH, SEQ, GROUP_W), BATCH, SEQ, SEQ)
    k_l = jnp.concatenate([k[CTX_ROWS:].reshape(DEC_BATCH, DEC_SEQ, hw),
                           k_c.reshape(DEC_BATCH, PAST_LEN, hw)], axis=1)
    v_l = jnp.concatenate([v[CTX_ROWS:].reshape(DEC_BATCH, DEC_SEQ, GROUP_W),
                           v_c.reshape(DEC_BATCH, PAST_LEN, GROUP_W)], axis=1)
    y_lat = _attention(q[CTX_ROWS:], k_l, v_l, DEC_BATCH, DEC_SEQ, 256)
    return jnp.concatenate([y_ctx, y_lat], axis=0), kv_lat


def _ret_kernel(*refs, T, C, rope, has_h0, has_state):
    it = iter(refs)
    q_ref, k_ref, v_ref, g_ref = next(it), next(it), next(it), next(it)
    cos_ref, sin_ref = (next(it), next(it)) if rope else (None, None)
    lg_ref, gain_ref = next(it), next(it)
    h0_ref = next(it) if has_h0 else None
    y_ref = next(it)
    st_ref = next(it) if has_state else None
    qs_ref, ks_ref, oacc_ref, dm_ref, tab_ref, hst_ref = (next(it) for _ in range(6))

    nc = T // C
    ii = lax.broadcasted_iota(jnp.int32, (C, C), 0)
    jj = lax.broadcasted_iota(jnp.int32, (C, C), 1)
    dif = (ii - jj).astype(F32)
    lane = lax.broadcasted_iota(jnp.int32, (C, LANE), 1)
    rowi = lax.broadcasted_iota(jnp.int32, (C, LANE), 0).astype(F32)
    m_lo = lane < RET_QK
    hrow = lax.broadcasted_iota(jnp.int32, (LANE, LANE), 0) < RET_QK

    for h in range(RET_HEADS):
        lf, lb = lg_ref[0, h], lg_ref[1, h]
        dm_ref[h] = (jnp.where(dif >= 0, jnp.exp(lf * jnp.maximum(dif, 0.0)), 0.0)
                     + jnp.where(dif <= 0, jnp.exp(lb * jnp.maximum(-dif, 0.0)), 0.0))
    for p in range(2):
        lf = jnp.where(m_lo, lg_ref[0, 2 * p], lg_ref[0, 2 * p + 1])
        lb = jnp.where(m_lo, lg_ref[1, 2 * p], lg_ref[1, 2 * p + 1])
        tab_ref[p, 0] = jnp.exp(lf * (rowi + 1.0))
        tab_ref[p, 1] = jnp.exp(lf * (C - 1.0 - rowi))
        tab_ref[p, 2] = jnp.exp(lb * (C - rowi))
        tab_ref[p, 3] = jnp.exp(lb * rowi)
    if has_h0:
        hst_ref[...] = h0_ref[...]
    else:
        hst_ref[...] = jnp.zeros_like(hst_ref)

    def chunk_decay(d, p):
        return jnp.exp(jnp.where(hrow, lg_ref[d, 2 * p], lg_ref[d, 2 * p + 1]) * float(C))

    def forward(c, carry):
        r0 = pl.multiple_of(c * C, C)
        rows = pl.ds(r0, C)
        for p in range(2):
            cols = slice(p * LANE, (p + 1) * LANE)
            qp = q_ref[rows, cols].astype(F32)
            kp = k_ref[rows, cols].astype(F32) * (RET_QK ** -0.5)
            if rope:
                qp = _rope(qp, cos_ref[rows, :], sin_ref[rows, :])
                kp = _rope(kp, cos_ref[rows, :], sin_ref[rows, :])
            qs_ref[rows, cols] = qp.astype(BF16)
            ks_ref[rows, cols] = kp.astype(BF16)
            kpb = kp.astype(BF16)
            hp = hst_ref[0, p]
            hpb = hp.astype(BF16)
            upd = jnp.zeros((LANE, LANE), F32)
            for e in range(2):
                h = 2 * p + e
                mh = m_lo if e == 0 else jnp.logical_not(m_lo)
                hcols = slice(h * LANE, (h + 1) * LANE)
                qm = jnp.where(mh, qp, 0.0)
                s = _dot_nt(qm.astype(BF16), kpb)
                vh = v_ref[rows, hcols]
                o = _dot((s * dm_ref[h]).astype(BF16), vh)
                o += _dot((qm * tab_ref[p, 0]).astype(BF16), hpb)
                oacc_ref[rows, hcols] = o
                upd += _dot_tn(jnp.where(mh, kp * tab_ref[p, 1], 0.0).astype(BF16), vh)
            hst_ref[0, p] = hp * chunk_decay(0, p) + upd
        return carry

    lax.fori_loop(0, nc, forward, 0)

    def backward(t, carry):
        c = nc - 1 - t
        r0 = pl.multiple_of(c * C, C)
        rows = pl.ds(r0, C)
        for p in range(2):
            cols = slice(p * LANE, (p + 1) * LANE)
            qp = qs_ref[rows, cols].astype(F32)
            kp = ks_ref[rows, cols].astype(F32)
            hp = hst_ref[1, p]
            hpb = hp.astype(BF16)
            upd = jnp.zeros((LANE, LANE), F32)
            for e in range(2):
                h = 2 * p + e
                mh = m_lo if e == 0 else jnp.logical_not(m_lo)
                hcols = slice(h * LANE, (h + 1) * LANE)
                vh = v_ref[rows, hcols]
                o = oacc_ref[rows, hcols] + _dot(jnp.where(mh, qp * tab_ref[p, 2], 0.0).astype(BF16), hpb)
                upd += _dot_tn(jnp.where(mh, kp * tab_ref[p, 3], 0.0).astype(BF16), vh)
                oc = o - jnp.mean(o, axis=-1, keepdims=True)
                yn = oc * lax.rsqrt(jnp.mean(oc * oc, axis=-1, keepdims=True) + EPS) * gain_ref[:, hcols]
                y_ref[rows, hcols] = (_silu(g_ref[rows, hcols].astype(F32)) * yn).astype(y_ref.dtype)
            hst_ref[1, p] = hp * chunk_decay(1, p) + upd
        return carry

    lax.fori_loop(0, nc, backward, 0)
    if has_state:
        st_ref[...] = hst_ref[...]


def _retention(um, lg, gain, h0, rope_tabs, n_seq, T, row_block0):
    C = RET_CHUNK
    rope = rope_tabs is not None
    has_h0 = h0 is not None
    has_state = not has_h0

    def col(off, w):
        return pl.BlockSpec((T, w), lambda b: (row_block0 + b, off // w))
    in_specs = [col(MAIN_RET_Q, 2 * LANE), col(MAIN_RET_K, 2 * LANE), col(MAIN_RET_V, GROUP_W), col(MAIN_RET_G, GROUP_W)]
    args = [um, um, um, um]
    if rope:
        in_specs += [pl.BlockSpec((T, LANE), lambda b: (0, 0))] * 2
        args += list(rope_tabs)
    in_specs += [pl.BlockSpec(memory_space=pltpu.SMEM), pl.BlockSpec((1, GROUP_W), lambda b: (0, 0))]
    args += [lg, gain]
    st_spec = pl.BlockSpec((None, 2, 2, LANE, LANE), lambda b: (b, 0, 0, 0, 0))
    if has_h0:
        in_specs.append(st_spec)
        args.append(h0)
    out_specs = [pl.BlockSpec((T, GROUP_W), lambda b: (b, 0))]
    out_shape = [jax.ShapeDtypeStruct((n_seq * T, GROUP_W), BF16)]
    if has_state:
        out_specs.append(st_spec)
        out_shape.append(jax.ShapeDtypeStruct((n_seq, 2, 2, LANE, LANE), F32))
    res = pl.pallas_call(
        functools.partial(_ret_kernel, T=T, C=C, rope=rope, has_h0=has_h0, has_state=has_state),
        grid=(n_seq,),
        in_specs=in_specs,
        out_specs=out_specs,
        out_shape=out_shape,
        scratch_shapes=[pltpu.VMEM((T, 2 * LANE), BF16), pltpu.VMEM((T, 2 * LANE), BF16),
                        pltpu.VMEM((T, GROUP_W), F32), pltpu.VMEM((RET_HEADS, C, C), F32),
                        pltpu.VMEM((2, 4, C, LANE), F32), pltpu.VMEM((2, 2, LANE, LANE), F32)],
        compiler_params=pltpu.CompilerParams(dimension_semantics=("parallel",), vmem_limit_bytes=VMEM_LIMIT),
        name="retention",
    )(*args)
    return (res[0], res[1]) if has_state else (res[0], None)


def _split3(x):
    x1 = x.astype(BF16)
    r1 = x - x1.astype(F32)
    x2 = r1.astype(BF16)
    x3 = (r1 - x2.astype(F32)).astype(BF16)
    return x1, x2, x3


def _dot01_l(m, x):
    x1, x2, x3 = _split3(x)
    return _dot(m, x1) + _dot(m, x2) + _dot(m, x3)


def _dot01_r(x, m):
    x1, x2, x3 = _split3(x)
    return _dot(x1, m) + _dot(x2, m) + _dot(x3, m)


def _softplus(x):
    return jnp.maximum(x, 0.0) + jnp.log(1.0 + jnp.exp(-jnp.abs(x)))


def _head_expanders():
    e8 = np.zeros((LANE, SSD_HEADS * LANE), np.float32)
    e64 = np.zeros((LANE, GROUP_W), np.float32)
    for h in range(SSD_HEADS):
        e8[h, h * LANE:(h + 1) * LANE] = 1.0
        e64[h, h * SSD_HEAD_DIM:(h + 1) * SSD_HEAD_DIM] = 1.0
    return jnp.asarray(e8, BF16), jnp.asarray(e64, BF16)


def _ssd_kernel(*refs, T, has_h0, has_state):
    it = iter(refs)
    z_ref, xs_ref, bc_ref, dt_ref = next(it), next(it), next(it), next(it)
    wx_ref, wbc_ref, bx_ref, bbc_ref = next(it), next(it), next(it), next(it)
    a_ref, dtb_ref, d_ref, gain_ref, e8_ref, e64_ref = (next(it) for _ in range(6))
    h0_ref = next(it) if has_h0 else None
    y_ref = next(it)
    st_ref = next(it) if has_state else None
    xc_s, bcs_s, oacc_ref, hst_ref = next(it), next(it), next(it), next(it)

    C = SSD_CHUNK
    nc = T // C
    ii = lax.broadcasted_iota(jnp.int32, (C, C), 0)
    jj = lax.broadcasted_iota(jnp.int32, (C, C), 1)
    tril, triu = ii >= jj, ii <= jj
    tril_b, triu_b = tril.astype(BF16), triu.astype(BF16)
    lane = lax.broadcasted_iota(jnp.int32, (C, LANE), 1)
    m_lo = lane < SSD_STATE
    masks = (m_lo, jnp.logical_not(m_lo))
    hrow = lax.broadcasted_iota(jnp.int32, (LANE, LANE), 0) < SSD_STATE

    if has_h0:
        hst_ref[...] = h0_ref[...]
    else:
        hst_ref[...] = jnp.zeros_like(hst_ref)

    def decays(rows, d):
        dtv = _softplus(dt_ref[rows, :] + dtb_ref[d:d + 1, :])
        la = dtv * a_ref[d:d + 1, :]
        return dtv, _dot01_l(tril_b if d == 0 else triu_b, la)

    def conv(ref, w_ref, b_ref, c, r0):
        width = ref.shape[1]
        x = ref[pl.ds(r0, C), :].astype(F32)
        p0 = pl.multiple_of(jnp.maximum(r0 - 16, 0), 16)
        n0 = pl.multiple_of(jnp.minimum(r0 + C, T - 16), 16)
        prev_row = jnp.where(c > 0, ref[pl.ds(p0, 16), :].astype(F32)[15:16], 0.0)
        next_row = jnp.where(c < nc - 1, ref[pl.ds(n0, 16), :].astype(F32)[0:1], 0.0)
        rowi = lax.broadcasted_iota(jnp.int32, (C, width), 0)
        prev = jnp.where(rowi == 0, prev_row, pltpu.roll(x, 1, 0))
        nxt = jnp.where(rowi == C - 1, next_row, pltpu.roll(x, C - 1, 0))
        return _silu(prev * w_ref[0:1, :] + x * w_ref[1:2, :] + nxt * w_ref[2:3, :] + b_ref[...])

    def decay_matrix(qrow, pp):
        return jnp.where(hrow, qrow[:, pp * LANE:(pp + 1) * LANE],
                         qrow[:, (2 + pp) * LANE:(3 + pp) * LANE])

    def forward(c, carry):
        r0 = pl.multiple_of(c * C, C)
        rows = pl.ds(r0, C)
        xc = conv(xs_ref, wx_ref, bx_ref, c, r0)
        bcv = conv(bc_ref, wbc_ref, bbc_ref, c, r0)
        xc_s[rows, :] = xc.astype(BF16)
        bcs_s[rows, :] = bcv.astype(BF16)
        bmat, cmat = bcv[:, :LANE], bcv[:, LANE:]
        bmb = bmat.astype(BF16)
        dtf, bf = decays(rows, 0)
        dtb, bb = decays(rows, 1)
        bf_t, dtf_t, bb_t, dtb_t = bf.T, dtf.T, bb.T, dtb.T
        colf = _dot01_r(bf, e8_ref[...])
        colb = _dot01_r(bb, e8_ref[...])
        qdf = _dot01_r(jnp.exp(bf), e64_ref[...])
        kwf = _dot01_r(dtf * jnp.exp(bf[C - 1:C, :] - bf), e64_ref[...])
        cms = [jnp.where(masks[g], cmat, 0.0).astype(BF16) for g in range(SSD_GROUPS)]
        bms = [jnp.where(masks[g], bmat, 0.0).astype(BF16) for g in range(SSD_GROUPS)]
        scores = [_dot_nt(cms[g], bmb) for g in range(SSD_GROUPS)]
        for pp in range(2):
            hp = hst_ref[0, pp]
            hpb = hp.astype(BF16)
            upd = jnp.zeros((LANE, LANE), F32)
            for g in range(SSD_GROUPS):
                cols = slice((2 * g + pp) * LANE, (2 * g + pp + 1) * LANE)
                xs_pair = xc[:, cols]
                xsb = xs_pair.astype(BF16)
                outs = []
                for e in range(2):
                    h = 4 * g + 2 * pp + e
                    hc = slice(h * LANE, (h + 1) * LANE)
                    ef = jnp.where(tril, jnp.exp(jnp.minimum(colf[:, hc] - bf_t[h:h + 1, :], 0.0)), 0.0) * dtf_t[h:h + 1, :]
                    eb = jnp.where(triu, jnp.exp(jnp.minimum(colb[:, hc] - bb_t[h:h + 1, :], 0.0)), 0.0) * dtb_t[h:h + 1, :]
                    outs.append(_dot((scores[g] * (ef + eb)).astype(BF16), xsb))
                o = jnp.where(m_lo, outs[0], outs[1])
                o += _dot(cms[g], hpb) * qdf[:, cols]
                oacc_ref[rows, cols] = o
                upd += _dot_tn(bms[g], (xs_pair * kwf[:, cols]).astype(BF16))
            hst_ref[0, pp] = hp * decay_matrix(qdf[C - 1:C, :], pp) + upd
        return carry

    lax.fori_loop(0, nc, forward, 0)

    def backward(t, carry):
        c = nc - 1 - t
        r0 = pl.multiple_of(c * C, C)
        rows = pl.ds(r0, C)
        xc = xc_s[rows, :].astype(F32)
        bcv = bcs_s[rows, :]
        bmat, cmat = bcv[:, :LANE], bcv[:, LANE:]
        zero = jnp.zeros_like(bmat)
        dtb, bb = decays(rows, 1)
        qdb = _dot01_r(jnp.exp(bb), e64_ref[...])
        kwb = _dot01_r(dtb * jnp.exp(bb[0:1, :] - bb), e64_ref[...])
        blocks = {}
        for pp in range(2):
            hp = hst_ref[1, pp]
            hpb = hp.astype(BF16)
            upd = jnp.zeros((LANE, LANE), F32)
            for g in range(SSD_GROUPS):
                blk = 2 * g + pp
                cols = slice(blk * LANE, (blk + 1) * LANE)
                xs_pair = xc[:, cols]
                o = oacc_ref[rows, cols] + _dot(jnp.where(masks[g], cmat, zero), hpb) * qdb[:, cols]
                upd += _dot_tn(jnp.where(masks[g], bmat, zero), (xs_pair * kwb[:, cols]).astype(BF16))
                y = (o + d_ref[:, cols] * xs_pair) * _silu(z_ref[rows, cols].astype(F32))
                blocks[blk] = y
            hst_ref[1, pp] = hp * decay_matrix(qdb[0:1, :], pp) + upd
        for g in range(SSD_GROUPS):
            y0, y1 = blocks[2 * g], blocks[2 * g + 1]
            ss = jnp.sum(y0 * y0, axis=-1, keepdims=True) + jnp.sum(y1 * y1, axis=-1, keepdims=True)
            r = lax.rsqrt(ss / (2 * LANE) + EPS)
            for i, yb in enumerate((y0, y1)):
                cols = slice((2 * g + i) * LANE, (2 * g + i + 1) * LANE)
                y_ref[rows, cols] = (yb * r * gain_ref[:, cols]).astype(y_ref.dtype)
        return carry

    lax.fori_loop(0, nc, backward, 0)
    if has_state:
        st_ref[...] = hst_ref[...]


def _ssd_pack_state(st):
    n = st.shape[0]
    st = st.reshape(n, 2, SSD_GROUPS, 2, 2, SSD_STATE, SSD_HEAD_DIM)
    return st.transpose(0, 1, 3, 2, 5, 4, 6).reshape(n, 2, 2, LANE, LANE)


def _ssd_unpack_state(st):
    n = st.shape[0]
    st = st.reshape(n, 2, 2, SSD_GROUPS, SSD_STATE, 2, SSD_HEAD_DIM)
    return st.transpose(0, 1, 3, 2, 5, 4, 6).reshape(n, 2, SSD_HEADS, SSD_STATE, SSD_HEAD_DIM)


def _ssd_params(ssd_conv_w, ssd_conv_b, ssd_a_log, ssd_dt_bias, ssd_d, ssd_norm_g):
    def lanes8(v):
        return jnp.pad(v.astype(F32), ((0, 0), (0, 0), (0, LANE - SSD_HEADS)))
    e8, e64 = _head_expanders()
    return dict(wx=ssd_conv_w[:, :, :GROUP_W], wbc=ssd_conv_w[:, :, GROUP_W:],
                bx=ssd_conv_b[:, None, :GROUP_W], bbc=ssd_conv_b[:, None, GROUP_W:],
                a=lanes8(-jnp.exp(ssd_a_log.astype(F32))), dtb=lanes8(ssd_dt_bias),
                d=jnp.repeat(ssd_d, SSD_HEAD_DIM, axis=-1)[:, None, :], gain=ssd_norm_g[:, None, :],
                e8=e8, e64=e64)


def _ssd_scan(um, us, sp, layer, h0, n_seq, T, row_block0):
    has_h0 = h0 is not None
    has_state = not has_h0

    def col(off, w):
        return pl.BlockSpec((T, w), lambda b: (row_block0 + b, off // w))

    def per_layer(arr):
        return pl.BlockSpec((None,) + arr.shape[1:], lambda b: (layer,) + (0,) * (arr.ndim - 1))

    def const(arr):
        return pl.BlockSpec(arr.shape, lambda b: (0,) * arr.ndim)
    names = ('wx', 'wbc', 'bx', 'bbc', 'a', 'dtb', 'd', 'gain')
    in_specs = [col(MAIN_SSD_Z, GROUP_W), col(MAIN_SSD_XS, GROUP_W), col(MAIN_SSD_BC, 2 * LANE),
                pl.BlockSpec((T, LANE), lambda b: (row_block0 + b, SIDE_DT // LANE))]
    in_specs += [per_layer(sp[n]) for n in names] + [const(sp['e8']), const(sp['e64'])]
    args = [um, um, um, us] + [sp[n] for n in names] + [sp['e8'], sp['e64']]
    st_spec = pl.BlockSpec((None, 2, 2, LANE, LANE), lambda b: (b, 0, 0, 0, 0))
    if has_h0:
        in_specs.append(st_spec)
        args.append(h0)
    out_specs = [pl.BlockSpec((T, GROUP_W), lambda b: (b, 0))]
    out_shape = [jax.ShapeDtypeStruct((n_seq * T, GROUP_W), BF16)]
    if has_state:
        out_specs.append(st_spec)
        out_shape.append(jax.ShapeDtypeStruct((n_seq, 2, 2, LANE, LANE), F32))
    res = pl.pallas_call(
        functools.partial(_ssd_kernel, T=T, has_h0=has_h0, has_state=has_state),
        grid=(n_seq,),
        in_specs=in_specs,
        out_specs=out_specs,
        out_shape=out_shape,
        scratch_shapes=[pltpu.VMEM((T, GROUP_W), BF16), pltpu.VMEM((T, 2 * LANE), BF16),
                        pltpu.VMEM((T, GROUP_W), F32), pltpu.VMEM((2, 2, LANE, LANE), F32)],
        compiler_params=pltpu.CompilerParams(dimension_semantics=("parallel",), vmem_limit_bytes=VMEM_LIMIT),
        name="ssd",
    )(*args)
    return (res[0], res[1]) if has_state else (res[0], None)


def _split(x, sizes):
    return jnp.split(x, [int(i) for i in np.cumsum(sizes)[:-1]], axis=-1)


def _rms(x, g):
    xf = x.astype(F32)
    y = xf * lax.rsqrt(jnp.mean(xf * xf, axis=-1, keepdims=True) + EPS)
    return (y * g.astype(F32)).astype(x.dtype)


def _dwconv(x, w, b):
    y = lax.conv_general_dilated(x, w[:, None, :].astype(x.dtype), (1,), 'SAME',
                                 dimension_numbers=('NWC', 'WIO', 'NWC'),
                                 feature_group_count=x.shape[-1])
    return y + b


def _chunk(x, L):
    B, T = x.shape[:2]
    return x.reshape(B, T // L, L, *x.shape[2:])


def _carry(h0, decay, upd):
    def step(h, du):
        d, u = du
        return d * h + u, h
    h_last, h_prev = lax.scan(step, h0, (decay, upd))
    return h_prev, h_last


def _scan_scalar(q, k, v, log_a, h0, L):
    B, T, H, _ = q.shape
    V = v.shape[-1]
    qc, kc, vc = _chunk(q.astype(F32), L), _chunk(k.astype(F32), L), _chunk(v.astype(F32), L)
    b = jnp.cumsum(_chunk(log_a.astype(F32), L), axis=2)
    causal = jnp.tril(jnp.ones((L, L), bool))[None, None, :, :, None]
    seg = b[:, :, :, None, :] - b[:, :, None, :, :]
    dec = jnp.where(causal, jnp.exp(jnp.where(causal, seg, 0.0)), 0.0)
    s = jnp.einsum('bcihk,bcjhk->bcijh', qc, kc) * dec
    o = jnp.einsum('bcijh,bcjhv->bcihv', s, vc)
    b_last = b[:, :, -1:]
    upd = jnp.einsum('bcjhk,bcjhv->cbhkv', kc * jnp.exp(b_last - b)[..., None], vc)
    decay = jnp.exp(b_last[:, :, 0]).transpose(1, 0, 2)[..., None, None]
    h_prev, h_last = _carry(h0.astype(F32), decay, upd)
    o = o + jnp.einsum('bcihk,cbhkv->bcihv', qc * jnp.exp(b)[..., None], h_prev)
    return o.reshape(B, T, H, V).astype(v.dtype), h_last


def _scan_vector(q, k, v, log_a, h0, L):
    B, T, H, _ = q.shape
    V = v.shape[-1]
    qc, kc, vc = _chunk(q.astype(F32), L), _chunk(k.astype(F32), L), _chunk(v.astype(F32), L)
    b = jnp.cumsum(_chunk(log_a.astype(F32), L), axis=2)
    causal = jnp.tril(jnp.ones((L, L), bool))[None, None, :, :, None, None]
    seg = b[:, :, :, None] - b[:, :, None]
    dec = jnp.where(causal, jnp.exp(jnp.where(causal, seg, 0.0)), 0.0)
    s = jnp.sum(qc[:, :, :, None] * kc[:, :, None] * dec, axis=-1)
    o = jnp.einsum('bcijh,bcjhv->bcihv', s, vc)
    b_last = b[:, :, -1:]
    upd = jnp.einsum('bcjhk,bcjhv->cbhkv', kc * jnp.exp(b_last - b), vc)
    decay = jnp.exp(b_last[:, :, 0]).transpose(1, 0, 2, 3)[..., None]
    h_prev, h_last = _carry(h0.astype(F32), decay, upd)
    o = o + jnp.einsum('bcihk,cbhkv->bcihv', qc * jnp.exp(b), h_prev)
    return o.reshape(B, T, H, V).astype(v.dtype), h_last


def _bidir(scan, q, k_f, k_b, v, la_f, la_b, h0_f, h0_b, L):
    o_f, h_f = scan(q, k_f, v, la_f, h0_f, L)
    fl = lambda t: jnp.flip(t, axis=1)
    o_b, h_b = scan(fl(q), fl(k_b), fl(v), fl(la_b), h0_b, L)
    return o_f + fl(o_b), h_f, h_b


def _ssd(z, xbc, dt, lp, h0):
    B, T, _ = z.shape
    xbc = jax.nn.silu(_dwconv(xbc, lp['ssd_conv_w'], lp['ssd_conv_b']))
    xs, bm, cm = _split(xbc, [GROUP_W, SSD_GROUPS * SSD_STATE, SSD_GROUPS * SSD_STATE])
    xs = xs.reshape(B, T, SSD_HEADS, SSD_HEAD_DIM)
    rep = SSD_HEADS // SSD_GROUPS
    bm = jnp.repeat(bm.reshape(B, T, SSD_GROUPS, SSD_STATE), rep, axis=2)
    cm = jnp.repeat(cm.reshape(B, T, SSD_GROUPS, SSD_STATE), rep, axis=2)
    a = -jnp.exp(lp['ssd_a_log'].astype(F32))
    dt_bias = lp['ssd_dt_bias'].astype(F32)
    dt_f = jax.nn.softplus(dt.astype(F32) + dt_bias[0])
    dt_b = jax.nn.softplus(dt.astype(F32) + dt_bias[1])
    o, h_f, h_b = _bidir(_scan_scalar, cm, bm * dt_f[..., None], bm * dt_b[..., None], xs,
                         dt_f * a[0], dt_b * a[1], h0[:, 0], h0[:, 1], SSD_CHUNK)
    y = o + lp['ssd_d'][:, None].astype(o.dtype) * xs
    y = y.reshape(B, T, GROUP_W) * jax.nn.silu(z)
    y = _rms(y.reshape(B, T, SSD_GROUPS, -1), lp['ssd_norm_g'].reshape(SSD_GROUPS, -1))
    return y.reshape(B, T, GROUP_W), jnp.stack([h_f, h_b], axis=1)


def _gla(q, k, v, g1, r, lp, h0):
    B, T, _ = q.shape
    q = q.reshape(B, T, GLA_HEADS, GLA_QK) * (GLA_QK ** -0.5)
    k = k.reshape(B, T, GLA_HEADS, GLA_QK)
    v = v.reshape(B, T, GLA_HEADS, GLA_V)

    def log_alpha(d):
        logits = (g1 @ lp['gla_w_g2'][d] + lp['gla_b_g'][d]).astype(F32)
        return (jax.nn.log_sigmoid(logits) / GLA_GATE_TEMP).reshape(B, T, GLA_HEADS, GLA_QK)
    o, h_f, h_b = _bidir(_scan_vector, q, k, k, v, log_alpha(0), log_alpha(1), h0[:, 0], h0[:, 1], GLA_CHUNK)
    o = _rms(o, lp['gla_norm_g'].reshape(GLA_HEADS, GLA_V)).reshape(B, T, GROUP_W)
    return jax.nn.silu(r) * o, jnp.stack([h_f, h_b], axis=1)


def _jnp_gla_group(um, us, lp, gla_h0):
    B = um.shape[0]
    umf = um.astype(F32)

    def piece(off, w):
        return umf[..., off:off + w]
    g1 = us[..., SIDE_G1:SIDE_G1 + GLA_GATE_RANK]
    if gla_h0 is None:
        gla_h0 = jnp.zeros((B, 2, GLA_HEADS, GLA_QK, GLA_V), F32)
    return _gla(piece(MAIN_GLA_Q, 2 * LANE), piece(MAIN_GLA_K, 2 * LANE), piece(MAIN_GLA_V, GROUP_W),
                g1, piece(MAIN_GLA_R, GROUP_W), lp, gla_h0)


def kernel(x_prompt, x_sample, cache_mla_kv, state_ssd, state_ret, state_gla, c, c_ctx, w_mod, b_mod, norm_ffn1, ffn1_wg, ffn1_wu, ffn1_wd, norm_mix, w_in, ssd_conv_w, ssd_conv_b, ssd_a_log, ssd_dt_bias, ssd_d, ssd_norm_g, mla_q_lat_gain, mla_w_q_up, mla_q_gain, mla_kv_lat_gain, mla_w_kv_up, mla_k_gain, ret_decay_logit, ret_norm_g, gla_w_g2, gla_b_g, gla_norm_g, w_out, norm_ffn2, ffn2_wg, ffn2_wu, ffn2_wd):
    stacked = dict(gla_w_g2=gla_w_g2, gla_b_g=gla_b_g, gla_norm_g=gla_norm_g)
    sp = _ssd_params(ssd_conv_w, ssd_conv_b, ssd_a_log, ssd_dt_bias, ssd_d, ssd_norm_g)
    ssd_h0 = jnp.stack([_ssd_pack_state(state_ssd[:, l]) for l in range(DEPTH)], axis=1)

    main_idx, side_idx = _in_proj_columns()
    w_main = _gather_columns(w_in, main_idx).astype(BF16)
    w_side = _gather_columns(w_in, side_idx).astype(BF16)
    w_out_b = w_out.astype(BF16)
    f1 = (ffn1_wg.astype(BF16), ffn1_wu.astype(BF16), ffn1_wd.astype(BF16))
    f2 = (ffn2_wg.astype(BF16), ffn2_wu.astype(BF16), ffn2_wd.astype(BF16))
    g_ffn1 = norm_ffn1.reshape(DEPTH, 1, D_MODEL)
    g_mix = norm_mix.reshape(DEPTH, 1, D_MODEL)
    g_ffn2 = norm_ffn2.reshape(DEPTH, 1, D_MODEL)
    mw = _mla_weights(mla_q_lat_gain, mla_w_q_up, mla_q_gain, mla_kv_lat_gain, mla_w_kv_up, mla_k_gain)
    cos_lat, sin_lat = _rope_tables(DEC_SEQ)
    cos_all = jnp.concatenate([jnp.ones((TOK_TM, LANE), F32), cos_lat], axis=0)
    sin_all = jnp.concatenate([jnp.zeros((TOK_TM, LANE), F32), sin_lat], axis=0)
    ret_lg = jax.nn.log_sigmoid(ret_decay_logit.astype(F32))
    ret_gain = ret_norm_g.reshape(DEPTH, 1, GROUP_W)
    ret_h0 = state_ret.reshape(DEC_BATCH, DEPTH, 2, 2, LANE, LANE)

    c_all = jnp.zeros((MOD_ROWS, D_MODEL), F32).at[0].set(c_ctx).at[1:1 + DEC_BATCH].set(c)
    mod = _modulation(c_all, w_mod, b_mod).reshape(DEPTH * MOD_ROWS, 1, N_MOD * D_MODEL)

    x = jnp.concatenate([x_prompt.reshape(CTX_ROWS, D_MODEL), x_sample.reshape(LAT_ROWS, D_MODEL)], axis=0)
    kv_list, ssd_list, ret_list, gla_list = [], [], [], []
    for l in range(DEPTH):
        lp = {name: arr[l] for name, arr in stacked.items()}
        x = _ffn(x, mod, g_ffn1, *f1, l, 0)
        um = _in_proj(x, mod, g_mix, w_main, l, BF16, MAIN_TN)
        us = _in_proj(x, mod, g_mix, w_side, l, F32, SIDE_W)

        y_mla, kv_lat = _mla_layer(um, us, cache_mla_kv[:, l], mw, cos_all, sin_all, l)
        yr_c, s_ret = _retention(um, ret_lg[l], ret_gain[l], None, None, BATCH, SEQ, 0)
        yr_l, _ = _retention(um, ret_lg[l], ret_gain[l], ret_h0[:, l], (cos_lat, sin_lat),
                             DEC_BATCH, DEC_SEQ, CTX_ROWS // DEC_SEQ)
        y_ret = jnp.concatenate([yr_c, yr_l], axis=0)

        ys_c, s_ssd = _ssd_scan(um, us, sp, l, None, BATCH, SEQ, 0)
        ys_l, _ = _ssd_scan(um, us, sp, l, ssd_h0[:, l], DEC_BATCH, DEC_SEQ, CTX_ROWS // DEC_SEQ)
        y_ssd = jnp.concatenate([ys_c, ys_l], axis=0)

        yg_c, s_gla = _jnp_gla_group(
            um[:CTX_ROWS].reshape(BATCH, SEQ, MAIN_W), us[:CTX_ROWS].reshape(BATCH, SEQ, SIDE_W), lp, None)
        yg_l, _ = _jnp_gla_group(
            um[CTX_ROWS:].reshape(DEC_BATCH, DEC_SEQ, MAIN_W), us[CTX_ROWS:].reshape(DEC_BATCH, DEC_SEQ, SIDE_W),
            lp, state_gla[:, l])
        y_gla = jnp.concatenate([yg_c.reshape(CTX_ROWS, GROUP_W), yg_l.reshape(LAT_ROWS, GROUP_W)], axis=0).astype(BF16)

        kv_list.append(kv_lat[:CTX_ROWS, :MLA_CACHE_W].reshape(BATCH, SEQ, MLA_CACHE_W))
        ssd_list.append(_ssd_unpack_state(s_ssd))
        ret_list.append(s_ret.reshape(BATCH, 2, RET_HEADS, RET_QK, RET_V))
        gla_list.append(s_gla)
        x = _out_proj(x, [y_ssd, y_mla, y_ret, y_gla], mod, w_out_b, l)
        x = _ffn(x, mod, g_ffn2, *f2, l, 6)
    y_p = x[:CTX_ROWS].reshape(BATCH, SEQ, D_MODEL)
    y_s = x[CTX_ROWS:].reshape(DEC_BATCH, DEC_SEQ, D_MODEL)
    return (y_p, y_s, jnp.stack(kv_list, axis=1), jnp.stack(ssd_list, axis=1),
            jnp.stack(ret_list, axis=1), jnp.stack(gla_list, axis=1))
```

```python
import functools

import jax
import jax.numpy as jnp
import numpy as np
from jax import lax
from jax.experimental import pallas as pl
from jax.experimental.pallas import tpu as pltpu

F32 = jnp.float32
BF16 = jnp.bfloat16

D_MODEL = 2048
BATCH = 32
SEQ = 256
DEPTH = 4
DEC_BATCH = 4
DEC_SEQ = 4096
PAST_LEN = 256
GRID_W = 64
ROPE_BASE = 10000.0
EPS = 1e-6
D_FF = 5632
N_MOD = 9
GROUP_W = D_MODEL // 4
ATTN_BLOCK = 128

SSD_HEAD_DIM = 64
SSD_HEADS = GROUP_W // SSD_HEAD_DIM
SSD_STATE = 64
SSD_GROUPS = 2
SSD_CONV_K = 3
SSD_CHUNK = 128
SSD_CONV_CH = GROUP_W + 2 * SSD_GROUPS * SSD_STATE
SSD_IN = GROUP_W + SSD_CONV_CH + SSD_HEADS

MLA_HEADS = 4
MLA_NOPE = 128
MLA_ROPE = 64
MLA_V = GROUP_W // MLA_HEADS
MLA_Q_RANK = 384
MLA_KV_RANK = 128
MLA_QK = MLA_NOPE + MLA_ROPE
MLA_IN = MLA_Q_RANK + MLA_KV_RANK + MLA_ROPE
MLA_CACHE_W = MLA_KV_RANK + MLA_ROPE

RET_HEADS = 4
RET_QK = 64
RET_V = GROUP_W // RET_HEADS
RET_CHUNK = 128
RET_IN = 2 * RET_HEADS * RET_QK + 2 * GROUP_W

GLA_HEADS = 4
GLA_QK = 64
GLA_V = GROUP_W // GLA_HEADS
GLA_GATE_RANK = 16
GLA_GATE_TEMP = 16.0
GLA_CHUNK = 16
GLA_IN = 2 * GLA_HEADS * GLA_QK + GROUP_W + GLA_GATE_RANK + GROUP_W

IN_W = SSD_IN + MLA_IN + RET_IN + GLA_IN

LANE = 128
CTX_ROWS = BATCH * SEQ
LAT_ROWS = DEC_BATCH * DEC_SEQ
ROWS = CTX_ROWS + LAT_ROWS
MOD_ROWS = 8
VMEM_LIMIT = 56 * 1024 * 1024
TOK_TM = 512
MLA_HEAD_W = 2 * LANE

MAIN_SSD_Z = 0
MAIN_SSD_XS = 512
MAIN_RET_V = 1024
MAIN_RET_G = 1536
MAIN_GLA_V = 2048
MAIN_GLA_R = 2560
MAIN_SSD_BC = 3072
MAIN_RET_Q = 3328
MAIN_RET_K = 3584
MAIN_GLA_Q = 3840
MAIN_GLA_K = 4096
MAIN_MLA_CQ = 4352
MAIN_W = 4864
MAIN_TN = MAIN_W // 2
SIDE_CKV = 0
SIDE_KPE = SIDE_CKV + MLA_KV_RANK
SIDE_DT = 2 * LANE
SIDE_G1 = SIDE_DT + SSD_HEADS
SIDE_W = 3 * LANE


def _in_proj_columns():
    o_ssd, o_mla, o_ret, o_gla = 0, SSD_IN, SSD_IN + MLA_IN, SSD_IN + MLA_IN + RET_IN
    main = np.full((MAIN_W,), -1, np.int64)

    def put(dst, src, n):
        main[dst:dst + n] = src + np.arange(n)
    qk = RET_HEADS * RET_QK
    put(MAIN_SSD_Z, o_ssd, GROUP_W)
    put(MAIN_SSD_XS, o_ssd + GROUP_W, GROUP_W)
    put(MAIN_SSD_BC, o_ssd + 2 * GROUP_W, 2 * SSD_GROUPS * SSD_STATE)
    put(MAIN_MLA_CQ, o_mla, MLA_Q_RANK)
    put(MAIN_RET_Q, o_ret, qk)
    put(MAIN_RET_K, o_ret + qk, qk)
    put(MAIN_RET_V, o_ret + 2 * qk, GROUP_W)
    put(MAIN_RET_G, o_ret + 2 * qk + GROUP_W, GROUP_W)
    put(MAIN_GLA_Q, o_gla, qk)
    put(MAIN_GLA_K, o_gla + qk, qk)
    put(MAIN_GLA_V, o_gla + 2 * qk, GROUP_W)
    put(MAIN_GLA_R, o_gla + 2 * qk + GROUP_W + GLA_GATE_RANK, GROUP_W)
    side = np.full((SIDE_W,), -1, np.int64)
    side[SIDE_CKV:SIDE_CKV + MLA_KV_RANK + MLA_ROPE] = o_mla + MLA_Q_RANK + np.arange(MLA_KV_RANK + MLA_ROPE)
    side[SIDE_DT:SIDE_DT + SSD_HEADS] = o_ssd + GROUP_W + SSD_CONV_CH + np.arange(SSD_HEADS)
    side[SIDE_G1:SIDE_G1 + GLA_GATE_RANK] = o_gla + 2 * qk + GROUP_W + np.arange(GLA_GATE_RANK)
    return main, side


def _gather_columns(w, idx):
    safe = np.where(idx < 0, 0, idx)
    out = jnp.take(w, jnp.asarray(safe, jnp.int32), axis=-1)
    return jnp.where(jnp.asarray(idx >= 0), out, 0.0)


def _mod_row(i, tm):
    ctx_tiles = CTX_ROWS // tm
    per_seq = DEC_SEQ // tm
    return jnp.where(i < ctx_tiles, 0, 1 + (i - ctx_tiles) // per_seq)


def _mod_spec(layer, which, tm, n_grid):
    if n_grid == 1:
        return pl.BlockSpec((1, 1, D_MODEL), lambda i: (layer * MOD_ROWS + _mod_row(i, tm), 0, which))
    return pl.BlockSpec((1, 1, D_MODEL), lambda i, j: (layer * MOD_ROWS + _mod_row(i, tm), 0, which))


def _dot(a, b):
    return jnp.dot(a, b, preferred_element_type=F32)


def _dot_nt(a, b):
    return lax.dot_general(a, b, (((1,), (1,)), ((), ())), preferred_element_type=F32)


def _dot_tn(a, b):
    return lax.dot_general(a, b, (((0,), (0,)), ((), ())), preferred_element_type=F32)


def _silu(x):
    return x * jax.nn.sigmoid(x)


def _mod_kernel(c_ref, w_ref, b_ref, o_ref):
    s = _silu(c_ref[...])
    hi = s.astype(BF16)
    lo = (s - hi.astype(F32)).astype(BF16)
    w = w_ref[...].astype(BF16)
    o_ref[...] = _dot(hi, w) + _dot(lo, w) + b_ref[...]


def _modulation(c_all, w_mod, b_mod):
    tn = 1024
    n = N_MOD * D_MODEL
    return pl.pallas_call(
        _mod_kernel,
        grid=(DEPTH, n // tn),
        in_specs=[pl.BlockSpec((MOD_ROWS, D_MODEL), lambda l, j: (0, 0)),
                  pl.BlockSpec((None, D_MODEL, tn), lambda l, j: (l, 0, j)),
                  pl.BlockSpec((None, 1, tn), lambda l, j: (l, 0, j))],
        out_specs=pl.BlockSpec((None, MOD_ROWS, tn), lambda l, j: (l, 0, j)),
        out_shape=jax.ShapeDtypeStruct((DEPTH, MOD_ROWS, n), F32),
        compiler_params=pltpu.CompilerParams(dimension_semantics=("arbitrary", "arbitrary"),
                                             vmem_limit_bytes=VMEM_LIMIT),
        name="modulation",
    )(c_all, w_mod, b_mod.reshape(DEPTH, 1, n))


def _norm_modulate(x, g, shift, scale):
    r = lax.rsqrt(jnp.mean(x * x, axis=-1, keepdims=True) + EPS)
    return (x * r * g) * (1.0 + scale) + shift


def _ffn_kernel(x_ref, g_ref, shift_ref, scale_ref, gate_ref, wg_ref, wu_ref, wd_ref, o_ref, h_ref):
    j = pl.program_id(1)

    @pl.when(j == 0)
    def _():
        h_ref[...] = _norm_modulate(x_ref[...], g_ref[...], shift_ref[0], scale_ref[0]).astype(BF16)

    h = h_ref[...]
    g = _dot(h, wg_ref[...])
    u = _dot(h, wu_ref[...])
    a = (_silu(g) * u).astype(BF16)
    y = _dot(a, wd_ref[...])

    @pl.when(j == 0)
    def _():
        o_ref[...] = y

    @pl.when(j > 0)
    def _():
        o_ref[...] += y

    @pl.when(j == pl.num_programs(1) - 1)
    def _():
        o_ref[...] = x_ref[...] + 0.5 * gate_ref[0] * o_ref[...]


def _ffn(x, mod, norm_g, wg, wu, wd, layer, mod_base, tm=TOK_TM, tf=512):
    row = pl.BlockSpec((tm, D_MODEL), lambda i, j: (i, 0))
    return pl.pallas_call(
        _ffn_kernel,
        grid=(ROWS // tm, D_FF // tf),
        in_specs=[row,
                  pl.BlockSpec((None, 1, D_MODEL), lambda i, j: (layer, 0, 0)),
                  _mod_spec(layer, mod_base + 0, tm, 2),
                  _mod_spec(layer, mod_base + 1, tm, 2),
                  _mod_spec(layer, mod_base + 2, tm, 2),
                  pl.BlockSpec((None, D_MODEL, tf), lambda i, j: (layer, 0, j)),
                  pl.BlockSpec((None, D_MODEL, tf), lambda i, j: (layer, 0, j)),
                  pl.BlockSpec((None, tf, D_MODEL), lambda i, j: (layer, j, 0))],
        out_specs=row,
        out_shape=jax.ShapeDtypeStruct((ROWS, D_MODEL), F32),
        scratch_shapes=[pltpu.VMEM((tm, D_MODEL), BF16)],
        compiler_params=pltpu.CompilerParams(dimension_semantics=("parallel", "arbitrary"),
                                             vmem_limit_bytes=VMEM_LIMIT),
        name="ffn",
    )(x, norm_g, mod, mod, mod, wg, wu, wd)


def _in_proj_kernel(x_ref, g_ref, shift_ref, scale_ref, w_ref, o_ref, h_ref):
    @pl.when(pl.program_id(1) == 0)
    def _():
        h_ref[...] = _norm_modulate(x_ref[...], g_ref[...], shift_ref[0], scale_ref[0]).astype(BF16)

    o_ref[...] = _dot(h_ref[...], w_ref[...]).astype(o_ref.dtype)


def _in_proj(x, mod, norm_g, w, layer, out_dtype, tn, tm=TOK_TM):
    n = w.shape[-1]
    return pl.pallas_call(
        _in_proj_kernel,
        grid=(ROWS // tm, n // tn),
        in_specs=[pl.BlockSpec((tm, D_MODEL), lambda i, j: (i, 0)),
                  pl.BlockSpec((None, 1, D_MODEL), lambda i, j: (layer, 0, 0)),
                  _mod_spec(layer, 3, tm, 2),
                  _mod_spec(layer, 4, tm, 2),
                  pl.BlockSpec((None, D_MODEL, tn), lambda i, j: (layer, 0, j))],
        out_specs=pl.BlockSpec((tm, tn), lambda i, j: (i, j)),
        out_shape=jax.ShapeDtypeStruct((ROWS, n), out_dtype),
        scratch_shapes=[pltpu.VMEM((tm, D_MODEL), BF16)],
        compiler_params=pltpu.CompilerParams(dimension_semantics=("parallel", "arbitrary"),
                                             vmem_limit_bytes=VMEM_LIMIT),
        name="in_proj",
    )(x, norm_g, mod, mod, w)


def _out_proj_kernel(x_ref, y0_ref, y1_ref, y2_ref, y3_ref, gate_ref, w_ref, o_ref):
    acc = _dot(y0_ref[...], w_ref[0 * GROUP_W:1 * GROUP_W, :])
    acc += _dot(y1_ref[...], w_ref[1 * GROUP_W:2 * GROUP_W, :])
    acc += _dot(y2_ref[...], w_ref[2 * GROUP_W:3 * GROUP_W, :])
    acc += _dot(y3_ref[...], w_ref[3 * GROUP_W:4 * GROUP_W, :])
    o_ref[...] = x_ref[...] + gate_ref[0] * acc


def _out_proj(x, ys, mod, w, layer, tm=TOK_TM):
    row = pl.BlockSpec((tm, D_MODEL), lambda i: (i, 0))
    yspec = pl.BlockSpec((tm, GROUP_W), lambda i: (i, 0))
    return pl.pallas_call(
        _out_proj_kernel,
        grid=(ROWS // tm,),
        in_specs=[row, yspec, yspec, yspec, yspec,
                  _mod_spec(layer, 5, tm, 1),
                  pl.BlockSpec((None, D_MODEL, D_MODEL), lambda i: (layer, 0, 0))],
        out_specs=row,
        out_shape=jax.ShapeDtypeStruct((ROWS, D_MODEL), F32),
        compiler_params=pltpu.CompilerParams(dimension_semantics=("parallel",),
                                             vmem_limit_bytes=VMEM_LIMIT),
        name="out_proj",
    )(x, *ys, mod, w)


def _rope_tables(T):
    n_rows = T // GRID_W
    row = jnp.repeat(jnp.arange(n_rows, dtype=F32), GRID_W)
    col = jnp.tile(jnp.arange(GRID_W, dtype=F32), n_rows)
    d_axis = MLA_ROPE // 2
    inv = ROPE_BASE ** (-jnp.arange(0, d_axis, 2, dtype=F32) / d_axis)
    ar, ac = row[:, None] * inv, col[:, None] * inv
    cos = jnp.concatenate([jnp.cos(ar), jnp.cos(ar), jnp.cos(ac), jnp.cos(ac)], axis=-1)
    sin = jnp.concatenate([-jnp.sin(ar), jnp.sin(ar), -jnp.sin(ac), jnp.sin(ac)], axis=-1)
    return jnp.tile(cos, (1, 2)), jnp.tile(sin, (1, 2))


def _swap16(x):
    lane = lax.broadcasted_iota(jnp.int32, x.shape, 1)
    up = pltpu.roll(x, LANE - 16, 1)
    down = pltpu.roll(x, 16, 1)
    return jnp.where((lane % 32) < 16, up, down)


def _rope(x, cos, sin):
    return x * cos + _swap16(x) * sin


def _mla_q_kernel(c0_ref, c1_ref, c2_ref, gl_ref, w_ref, gq_ref, cos_ref, sin_ref, o_ref):
    cs = [r[...].astype(F32) for r in (c0_ref, c1_ref, c2_ref)]
    ss = sum(jnp.sum(c * c, axis=-1, keepdims=True) for c in cs)
    r = lax.rsqrt(ss / MLA_Q_RANK + EPS)
    q = sum(_dot((cs[i] * r * gl_ref[:, i * LANE:(i + 1) * LANE]).astype(BF16),
                 w_ref[i * LANE:(i + 1) * LANE, :]) for i in range(3))
    cos, sin = cos_ref[...], sin_ref[...]
    scale = MLA_QK ** -0.5
    for h in range(MLA_HEADS):
        a = q[:, h * MLA_HEAD_W:h * MLA_HEAD_W + LANE]
        b = q[:, h * MLA_HEAD_W + LANE:(h + 1) * MLA_HEAD_W]
        ssq = jnp.sum(a * a, axis=-1, keepdims=True) + jnp.sum(b * b, axis=-1, keepdims=True)
        rh = lax.rsqrt(ssq / MLA_QK + EPS) * scale
        o_ref[:, h * MLA_HEAD_W:h * MLA_HEAD_W + LANE] = (a * rh * gq_ref[:, :LANE]).astype(BF16)
        o_ref[:, h * MLA_HEAD_W + LANE:(h + 1) * MLA_HEAD_W] = _rope(b * rh * gq_ref[:, LANE:], cos, sin).astype(BF16)


def _rope_tile_index(i, tm):
    ctx_tiles = CTX_ROWS // tm
    return jnp.where(i < ctx_tiles, 0, 1 + (i - ctx_tiles) % (DEC_SEQ // tm))


def _mla_q(um, gl, wq, gq, cos_all, sin_all, layer, tm=TOK_TM):
    cq = MAIN_MLA_CQ // LANE
    tab = pl.BlockSpec((tm, LANE), lambda i: (_rope_tile_index(i, tm), 0))
    return pl.pallas_call(
        _mla_q_kernel,
        grid=(ROWS // tm,),
        in_specs=[pl.BlockSpec((tm, LANE), lambda i: (i, cq)),
                  pl.BlockSpec((tm, LANE), lambda i: (i, cq + 1)),
                  pl.BlockSpec((tm, LANE), lambda i: (i, cq + 2)),
                  pl.BlockSpec((None, 1, MLA_Q_RANK), lambda i: (layer, 0, 0)),
                  pl.BlockSpec((None, MLA_Q_RANK, MLA_HEADS * MLA_HEAD_W), lambda i: (layer, 0, 0)),
                  pl.BlockSpec((None, 1, MLA_HEAD_W), lambda i: (layer, 0, 0)),
                  tab, tab],
        out_specs=pl.BlockSpec((tm, MLA_HEADS * MLA_HEAD_W), lambda i: (i, 0)),
        out_shape=jax.ShapeDtypeStruct((ROWS, MLA_HEADS * MLA_HEAD_W), BF16),
        compiler_params=pltpu.CompilerParams(dimension_semantics=("parallel",), vmem_limit_bytes=VMEM_LIMIT),
        name="mla_q",
    )(um, um, um, gl, wq, gq, cos_all, sin_all)


def _mla_kv_kernel(s_ref, gl_ref, w_ref, gk_ref, cos_ref, sin_ref, lat_ref, k_ref, v_ref, *, normalize):
    ckv = s_ref[:, :LANE]
    kpe = s_ref[:, LANE:]
    if normalize:
        ckv = ckv * lax.rsqrt(jnp.mean(ckv * ckv, axis=-1, keepdims=True) + EPS) * gl_ref[...]
    lat_ref[:, :LANE] = ckv
    lat_ref[:, LANE:] = kpe
    kv = _dot(ckv.astype(BF16), w_ref[...])
    ss_pe = jnp.sum(kpe * kpe, axis=-1, keepdims=True)
    cos, sin = cos_ref[...], sin_ref[...]
    for h in range(MLA_HEADS):
        a = kv[:, h * LANE:(h + 1) * LANE]
        rh = lax.rsqrt((jnp.sum(a * a, axis=-1, keepdims=True) + ss_pe) / MLA_QK + EPS)
        k_ref[:, h * MLA_HEAD_W:h * MLA_HEAD_W + LANE] = (a * rh * gk_ref[:, :LANE]).astype(BF16)
        k_ref[:, h * MLA_HEAD_W + LANE:(h + 1) * MLA_HEAD_W] = _rope(kpe * rh * gk_ref[:, LANE:], cos, sin).astype(BF16)
    v_ref[...] = kv[:, MLA_HEADS * LANE:].astype(BF16)


def _mla_kv(src, gl, wkv, gk, cos_all, sin_all, layer, *, normalize, tab_index, tm=TOK_TM):
    rows = src.shape[0]
    tab = pl.BlockSpec((tm, LANE), lambda i: (tab_index(i, tm), 0))
    return pl.pallas_call(
        functools.partial(_mla_kv_kernel, normalize=normalize),
        grid=(rows // tm,),
        in_specs=[pl.BlockSpec((tm, 2 * LANE), lambda i: (i, 0)),
                  pl.BlockSpec((None, 1, MLA_KV_RANK), lambda i: (layer, 0, 0)),
                  pl.BlockSpec((None, MLA_KV_RANK, 2 * MLA_HEADS * LANE), lambda i: (layer, 0, 0)),
                  pl.BlockSpec((None, 1, MLA_HEAD_W), lambda i: (layer, 0, 0)),
                  tab, tab],
        out_specs=[pl.BlockSpec((tm, 2 * LANE), lambda i: (i, 0)),
                   pl.BlockSpec((tm, MLA_HEADS * MLA_HEAD_W), lambda i: (i, 0)),
                   pl.BlockSpec((tm, GROUP_W), lambda i: (i, 0))],
        out_shape=[jax.ShapeDtypeStruct((rows, 2 * LANE), F32),
                   jax.ShapeDtypeStruct((rows, MLA_HEADS * MLA_HEAD_W), BF16),
                   jax.ShapeDtypeStruct((rows, GROUP_W), BF16)],
        compiler_params=pltpu.CompilerParams(dimension_semantics=("parallel",), vmem_limit_bytes=VMEM_LIMIT),
        name="mla_kv",
    )(src, gl, wkv, gk, cos_all, sin_all)


def _attn_kernel(q_ref, k_ref, v_ref, o_ref):
    s = _dot_nt(q_ref[...], k_ref[...])
    m = jnp.max(s, axis=-1, keepdims=True)
    p = jnp.exp(s - m)
    l = jnp.sum(p, axis=-1, keepdims=True)
    o_ref[...] = (_dot(p.astype(BF16), v_ref[...]) / l).astype(o_ref.dtype)


def _attention(q, k, v, n_seq, t, tq):
    s_len = k.shape[1]
    nq = t // tq
    return pl.pallas_call(
        _attn_kernel,
        grid=(n_seq, MLA_HEADS, nq),
        in_specs=[pl.BlockSpec((tq, MLA_HEAD_W), lambda b, h, i: (b * nq + i, h)),
                  pl.BlockSpec((None, s_len, MLA_HEAD_W), lambda b, h, i: (b, 0, h)),
                  pl.BlockSpec((None, s_len, MLA_V), lambda b, h, i: (b, 0, h))],
        out_specs=pl.BlockSpec((tq, MLA_V), lambda b, h, i: (b * nq + i, h)),
        out_shape=jax.ShapeDtypeStruct((n_seq * t, GROUP_W), BF16),
        compiler_params=pltpu.CompilerParams(dimension_semantics=("parallel", "parallel", "arbitrary"),
                                             vmem_limit_bytes=VMEM_LIMIT),
        name="mla_attention",
    )(q, k, v)


def _mla_weights(mla_q_lat_gain, mla_w_q_up, mla_q_gain, mla_kv_lat_gain, mla_w_kv_up, mla_k_gain):
    qcol = np.full((MLA_HEADS * MLA_HEAD_W,), -1, np.int64)
    kvcol = np.zeros((2 * MLA_HEADS * LANE,), np.int64)
    for h in range(MLA_HEADS):
        qcol[h * MLA_HEAD_W:h * MLA_HEAD_W + MLA_QK] = h * MLA_QK + np.arange(MLA_QK)
        kvcol[h * LANE:(h + 1) * LANE] = h * (MLA_NOPE + MLA_V) + np.arange(MLA_NOPE)
        kvcol[(MLA_HEADS + h) * LANE:(MLA_HEADS + h + 1) * LANE] = h * (MLA_NOPE + MLA_V) + MLA_NOPE + np.arange(MLA_V)
    pad = jnp.zeros((DEPTH, MLA_HEAD_W - MLA_QK), F32)
    return dict(
        gl=mla_q_lat_gain.reshape(DEPTH, 1, MLA_Q_RANK),
        wq=_gather_columns(mla_w_q_up, qcol).astype(BF16),
        gq=jnp.concatenate([mla_q_gain, pad], axis=-1).reshape(DEPTH, 1, MLA_HEAD_W),
        gkv=mla_kv_lat_gain.reshape(DEPTH, 1, MLA_KV_RANK),
        wkv=_gather_columns(mla_w_kv_up, kvcol).astype(BF16),
        gk=jnp.concatenate([mla_k_gain, pad], axis=-1).reshape(DEPTH, 1, MLA_HEAD_W))


def _mla_layer(um, us, cache_l, mw, cos_all, sin_all, layer):
    q = _mla_q(um, mw['gl'], mw['wq'], mw['gq'], cos_all, sin_all, layer)
    kv_lat, k, v = _mla_kv(us, mw['gkv'], mw['wkv'], mw['gk'], cos_all, sin_all, layer,
                           normalize=True, tab_index=_rope_tile_index)
    cache2 = jnp.pad(cache_l.reshape(DEC_BATCH * PAST_LEN, MLA_CACHE_W), ((0, 0), (0, 2 * LANE - MLA_CACHE_W)))
    _, k_c, v_c = _mla_kv(cache2, mw['gkv'], mw['wkv'], mw['gk'], cos_all, sin_all, layer,
                          normalize=False, tab_index=lambda i, tm: 0)
    hw = MLA_HEADS * MLA_HEAD_W
    y_ctx = _attention(q[:CTX_ROWS], k[:CTX_ROWS].reshape(BATCH, SEQ, hw),
                       v[:CTX_ROWS].reshape(BATCH, SEQ, GROUP_W), BATCH, SEQ, SEQ)
    k_l = jnp.concatenate([k[CTX_ROWS:].reshape(DEC_BATCH, DEC_SEQ, hw),
                           k_c.reshape(DEC_BATCH, PAST_LEN, hw)], axis=1)
    v_l = jnp.concatenate([v[CTX_ROWS:].reshape(DEC_BATCH, DEC_SEQ, GROUP_W),
                           v_c.reshape(DEC_BATCH, PAST_LEN, GROUP_W)], axis=1)
    y_lat = _attention(q[CTX_ROWS:], k_l, v_l, DEC_BATCH, DEC_SEQ, 256)
    return jnp.concatenate([y_ctx, y_lat], axis=0), kv_lat


def _ret_kernel(*refs, T, C, rope, has_h0, has_state):
    it = iter(refs)
    q_ref, k_ref, v_ref, g_ref = next(it), next(it), next(it), next(it)
    cos_ref, sin_ref = (next(it), next(it)) if rope else (None, None)
    lg_ref, gain_ref = next(it), next(it)
    h0_ref = next(it) if has_h0 else None
    y_ref = next(it)
    st_ref = next(it) if has_state else None
    qs_ref, ks_ref, oacc_ref, dm_ref, tab_ref, hst_ref = (next(it) for _ in range(6))

    nc = T // C
    ii = lax.broadcasted_iota(jnp.int32, (C, C), 0)
    jj = lax.broadcasted_iota(jnp.int32, (C, C), 1)
    dif = (ii - jj).astype(F32)
    lane = lax.broadcasted_iota(jnp.int32, (C, LANE), 1)
    rowi = lax.broadcasted_iota(jnp.int32, (C, LANE), 0).astype(F32)
    m_lo = lane < RET_QK
    hrow = lax.broadcasted_iota(jnp.int32, (LANE, LANE), 0) < RET_QK

    for h in range(RET_HEADS):
        lf, lb = lg_ref[0, h], lg_ref[1, h]
        dm_ref[h] = (jnp.where(dif >= 0, jnp.exp(lf * jnp.maximum(dif, 0.0)), 0.0)
                     + jnp.where(dif <= 0, jnp.exp(lb * jnp.maximum(-dif, 0.0)), 0.0))
    for p in range(2):
        lf = jnp.where(m_lo, lg_ref[0, 2 * p], lg_ref[0, 2 * p + 1])
        lb = jnp.where(m_lo, lg_ref[1, 2 * p], lg_ref[1, 2 * p + 1])
        tab_ref[p, 0] = jnp.exp(lf * (rowi + 1.0))
        tab_ref[p, 1] = jnp.exp(lf * (C - 1.0 - rowi))
        tab_ref[p, 2] = jnp.exp(lb * (C - rowi))
        tab_ref[p, 3] = jnp.exp(lb * rowi)
    if has_h0:
        hst_ref[...] = h0_ref[...]
    else:
        hst_ref[...] = jnp.zeros_like(hst_ref)

    def chunk_decay(d, p):
        return jnp.exp(jnp.where(hrow, lg_ref[d, 2 * p], lg_ref[d, 2 * p + 1]) * float(C))

    def forward(c, carry):
        r0 = pl.multiple_of(c * C, C)
        rows = pl.ds(r0, C)
        for p in range(2):
            cols = slice(p * LANE, (p + 1) * LANE)
            qp = q_ref[rows, cols].astype(F32)
            kp = k_ref[rows, cols].astype(F32) * (RET_QK ** -0.5)
            if rope:
                qp = _rope(qp, cos_ref[rows, :], sin_ref[rows, :])
                kp = _rope(kp, cos_ref[rows, :], sin_ref[rows, :])
            qs_ref[rows, cols] = qp.astype(BF16)
            ks_ref[rows, cols] = kp.astype(BF16)
            kpb = kp.astype(BF16)
            hp = hst_ref[0, p]
            hpb = hp.astype(BF16)
            upd = jnp.zeros((LANE, LANE), F32)
            for e in range(2):
                h = 2 * p + e
                mh = m_lo if e == 0 else jnp.logical_not(m_lo)
                hcols = slice(h * LANE, (h + 1) * LANE)
                qm = jnp.where(mh, qp, 0.0)
                s = _dot_nt(qm.astype(BF16), kpb)
                vh = v_ref[rows, hcols]
                o = _dot((s * dm_ref[h]).astype(BF16), vh)
                o += _dot((qm * tab_ref[p, 0]).astype(BF16), hpb)
                oacc_ref[rows, hcols] = o
                upd += _dot_tn(jnp.where(mh, kp * tab_ref[p, 1], 0.0).astype(BF16), vh)
            hst_ref[0, p] = hp * chunk_decay(0, p) + upd
        return carry

    lax.fori_loop(0, nc, forward, 0)

    def backward(t, carry):
        c = nc - 1 - t
        r0 = pl.multiple_of(c * C, C)
        rows = pl.ds(r0, C)
        for p in range(2):
            cols = slice(p * LANE, (p + 1) * LANE)
            qp = qs_ref[rows, cols].astype(F32)
            kp = ks_ref[rows, cols].astype(F32)
            hp = hst_ref[1, p]
            hpb = hp.astype(BF16)
            upd = jnp.zeros((LANE, LANE), F32)
            for e in range(2):
                h = 2 * p + e
                mh = m_lo if e == 0 else jnp.logical_not(m_lo)
                hcols = slice(h * LANE, (h + 1) * LANE)
                vh = v_ref[rows, hcols]
                o = oacc_ref[rows, hcols] + _dot(jnp.where(mh, qp * tab_ref[p, 2], 0.0).astype(BF16), hpb)
                upd += _dot_tn(jnp.where(mh, kp * tab_ref[p, 3], 0.0).astype(BF16), vh)
                oc = o - jnp.mean(o, axis=-1, keepdims=True)
                yn = oc * lax.rsqrt(jnp.mean(oc * oc, axis=-1, keepdims=True) + EPS) * gain_ref[:, hcols]
                y_ref[rows, hcols] = (_silu(g_ref[rows, hcols].astype(F32)) * yn).astype(y_ref.dtype)
            hst_ref[1, p] = hp * chunk_decay(1, p) + upd
        return carry

    lax.fori_loop(0, nc, backward, 0)
    if has_state:
        st_ref[...] = hst_ref[...]


def _retention(um, lg, gain, h0, rope_tabs, n_seq, T, row_block0):
    C = RET_CHUNK
    rope = rope_tabs is not None
    has_h0 = h0 is not None
    has_state = not has_h0

    def col(off, w):
        return pl.BlockSpec((T, w), lambda b: (row_block0 + b, off // w))
    in_specs = [col(MAIN_RET_Q, 2 * LANE), col(MAIN_RET_K, 2 * LANE), col(MAIN_RET_V, GROUP_W), col(MAIN_RET_G, GROUP_W)]
    args = [um, um, um, um]
    if rope:
        in_specs += [pl.BlockSpec((T, LANE), lambda b: (0, 0))] * 2
        args += list(rope_tabs)
    in_specs += [pl.BlockSpec(memory_space=pltpu.SMEM), pl.BlockSpec((1, GROUP_W), lambda b: (0, 0))]
    args += [lg, gain]
    st_spec = pl.BlockSpec((None, 2, 2, LANE, LANE), lambda b: (b, 0, 0, 0, 0))
    if has_h0:
        in_specs.append(st_spec)
        args.append(h0)
    out_specs = [pl.BlockSpec((T, GROUP_W), lambda b: (b, 0))]
    out_shape = [jax.ShapeDtypeStruct((n_seq * T, GROUP_W), BF16)]
    if has_state:
        out_specs.append(st_spec)
        out_shape.append(jax.ShapeDtypeStruct((n_seq, 2, 2, LANE, LANE), F32))
    res = pl.pallas_call(
        functools.partial(_ret_kernel, T=T, C=C, rope=rope, has_h0=has_h0, has_state=has_state),
        grid=(n_seq,),
        in_specs=in_specs,
        out_specs=out_specs,
        out_shape=out_shape,
        scratch_shapes=[pltpu.VMEM((T, 2 * LANE), BF16), pltpu.VMEM((T, 2 * LANE), BF16),
                        pltpu.VMEM((T, GROUP_W), F32), pltpu.VMEM((RET_HEADS, C, C), F32),
                        pltpu.VMEM((2, 4, C, LANE), F32), pltpu.VMEM((2, 2, LANE, LANE), F32)],
        compiler_params=pltpu.CompilerParams(dimension_semantics=("parallel",), vmem_limit_bytes=VMEM_LIMIT),
        name="retention",
    )(*args)
    return (res[0], res[1]) if has_state else (res[0], None)


def _split3(x):
    x1 = x.astype(BF16)
    r1 = x - x1.astype(F32)
    x2 = r1.astype(BF16)
    x3 = (r1 - x2.astype(F32)).astype(BF16)
    return x1, x2, x3


def _dot01_l(m, x):
    x1, x2, x3 = _split3(x)
    return _dot(m, x1) + _dot(m, x2) + _dot(m, x3)


def _dot01_r(x, m):
    x1, x2, x3 = _split3(x)
    return _dot(x1, m) + _dot(x2, m) + _dot(x3, m)


def _softplus(x):
    return jnp.maximum(x, 0.0) + jnp.log(1.0 + jnp.exp(-jnp.abs(x)))


def _head_expanders():
    e8 = np.zeros((LANE, SSD_HEADS * LANE), np.float32)
    e64 = np.zeros((LANE, GROUP_W), np.float32)
    for h in range(SSD_HEADS):
        e8[h, h * LANE:(h + 1) * LANE] = 1.0
        e64[h, h * SSD_HEAD_DIM:(h + 1) * SSD_HEAD_DIM] = 1.0
    return jnp.asarray(e8, BF16), jnp.asarray(e64, BF16)


def _ssd_kernel(*refs, T, has_h0, has_state):
    it = iter(refs)
    z_ref, xs_ref, bc_ref, dt_ref = next(it), next(it), next(it), next(it)
    wx_ref, wbc_ref, bx_ref, bbc_ref = next(it), next(it), next(it), next(it)
    a_ref, dtb_ref, d_ref, gain_ref, e8_ref, e64_ref = (next(it) for _ in range(6))
    h0_ref = next(it) if has_h0 else None
    y_ref = next(it)
    st_ref = next(it) if has_state else None
    xc_s, bcs_s, oacc_ref, hst_ref = next(it), next(it), next(it), next(it)

    C = SSD_CHUNK
    nc = T // C
    ii = lax.broadcasted_iota(jnp.int32, (C, C), 0)
    jj = lax.broadcasted_iota(jnp.int32, (C, C), 1)
    tril, triu = ii >= jj, ii <= jj
    tril_b, triu_b = tril.astype(BF16), triu.astype(BF16)
    lane = lax.broadcasted_iota(jnp.int32, (C, LANE), 1)
    m_lo = lane < SSD_STATE
    masks = (m_lo, jnp.logical_not(m_lo))
    hrow = lax.broadcasted_iota(jnp.int32, (LANE, LANE), 0) < SSD_STATE

    if has_h0:
        hst_ref[...] = h0_ref[...]
    else:
        hst_ref[...] = jnp.zeros_like(hst_ref)

    def decays(rows, d):
        dtv = _softplus(dt_ref[rows, :] + dtb_ref[d:d + 1, :])
        la = dtv * a_ref[d:d + 1, :]
        return dtv, _dot01_l(tril_b if d == 0 else triu_b, la)

    def conv(ref, w_ref, b_ref, c, r0):
        width = ref.shape[1]
        x = ref[pl.ds(r0, C), :].astype(F32)
        p0 = pl.multiple_of(jnp.maximum(r0 - 16, 0), 16)
        n0 = pl.multiple_of(jnp.minimum(r0 + C, T - 16), 16)
        prev_row = jnp.where(c > 0, ref[pl.ds(p0, 16), :].astype(F32)[15:16], 0.0)
        next_row = jnp.where(c < nc - 1, ref[pl.ds(n0, 16), :].astype(F32)[0:1], 0.0)
        rowi = lax.broadcasted_iota(jnp.int32, (C, width), 0)
        prev = jnp.where(rowi == 0, prev_row, pltpu.roll(x, 1, 0))
        nxt = jnp.where(rowi == C - 1, next_row, pltpu.roll(x, C - 1, 0))
        return _silu(prev * w_ref[0:1, :] + x * w_ref[1:2, :] + nxt * w_ref[2:3, :] + b_ref[...])

    def decay_matrix(qrow, pp):
        return jnp.where(hrow, qrow[:, pp * LANE:(pp + 1) * LANE],
                         qrow[:, (2 + pp) * LANE:(3 + pp) * LANE])

    def forward(c, carry):
        r0 = pl.multiple_of(c * C, C)
        rows = pl.ds(r0, C)
        xc = conv(xs_ref, wx_ref, bx_ref, c, r0)
        bcv = conv(bc_ref, wbc_ref, bbc_ref, c, r0)
        xc_s[rows, :] = xc.astype(BF16)
        bcs_s[rows, :] = bcv.astype(BF16)
        bmat, cmat = bcv[:, :LANE], bcv[:, LANE:]
        bmb = bmat.astype(BF16)
        dtf, bf = decays(rows, 0)
        dtb, bb = decays(rows, 1)
        bf_t, dtf_t, bb_t, dtb_t = bf.T, dtf.T, bb.T, dtb.T
        colf = _dot01_r(bf, e8_ref[...])
        colb = _dot01_r(bb, e8_ref[...])
        qdf = _dot01_r(jnp.exp(bf), e64_ref[...])
        kwf = _dot01_r(dtf * jnp.exp(bf[C - 1:C, :] - bf), e64_ref[...])
        cms = [jnp.where(masks[g], cmat, 0.0).astype(BF16) for g in range(SSD_GROUPS)]
        bms = [jnp.where(masks[g], bmat, 0.0).astype(BF16) for g in range(SSD_GROUPS)]
        scores = [_dot_nt(cms[g], bmb) for g in range(SSD_GROUPS)]
        for pp in range(2):
            hp = hst_ref[0, pp]
            hpb = hp.astype(BF16)
            upd = jnp.zeros((LANE, LANE), F32)
            for g in range(SSD_GROUPS):
                cols = slice((2 * g + pp) * LANE, (2 * g + pp + 1) * LANE)
                xs_pair = xc[:, cols]
                xsb = xs_pair.astype(BF16)
                outs = []
                for e in range(2):
                    h = 4 * g + 2 * pp + e
                    hc = slice(h * LANE, (h + 1) * LANE)
                    ef = jnp.where(tril, jnp.exp(jnp.minimum(colf[:, hc] - bf_t[h:h + 1, :], 0.0)), 0.0) * dtf_t[h:h + 1, :]
                    eb = jnp.where(triu, jnp.exp(jnp.minimum(colb[:, hc] - bb_t[h:h + 1, :], 0.0)), 0.0) * dtb_t[h:h + 1, :]
                    outs.append(_dot((scores[g] * (ef + eb)).astype(BF16), xsb))
                o = jnp.where(m_lo, outs[0], outs[1])
                o += _dot(cms[g], hpb) * qdf[:, cols]
                oacc_ref[rows, cols] = o
                upd += _dot_tn(bms[g], (xs_pair * kwf[:, cols]).astype(BF16))
            hst_ref[0, pp] = hp * decay_matrix(qdf[C - 1:C, :], pp) + upd
        return carry

    lax.fori_loop(0, nc, forward, 0)

    def backward(t, carry):
        c = nc - 1 - t
        r0 = pl.multiple_of(c * C, C)
        rows = pl.ds(r0, C)
        xc = xc_s[rows, :].astype(F32)
        bcv = bcs_s[rows, :]
        bmat, cmat = bcv[:, :LANE], bcv[:, LANE:]
        zero = jnp.zeros_like(bmat)
        dtb, bb = decays(rows, 1)
        qdb = _dot01_r(jnp.exp(bb), e64_ref[...])
        kwb = _dot01_r(dtb * jnp.exp(bb[0:1, :] - bb), e64_ref[...])
        blocks = {}
        for pp in range(2):
            hp = hst_ref[1, pp]
            hpb = hp.astype(BF16)
            upd = jnp.zeros((LANE, LANE), F32)
            for g in range(SSD_GROUPS):
                blk = 2 * g + pp
                cols = slice(blk * LANE, (blk + 1) * LANE)
                xs_pair = xc[:, cols]
                o = oacc_ref[rows, cols] + _dot(jnp.where(masks[g], cmat, zero), hpb) * qdb[:, cols]
                upd += _dot_tn(jnp.where(masks[g], bmat, zero), (xs_pair * kwb[:, cols]).astype(BF16))
                y = (o + d_ref[:, cols] * xs_pair) * _silu(z_ref[rows, cols].astype(F32))
                blocks[blk] = y
            hst_ref[1, pp] = hp * decay_matrix(qdb[0:1, :], pp) + upd
        for g in range(SSD_GROUPS):
            y0, y1 = blocks[2 * g], blocks[2 * g + 1]
            ss = jnp.sum(y0 * y0, axis=-1, keepdims=True) + jnp.sum(y1 * y1, axis=-1, keepdims=True)
            r = lax.rsqrt(ss / (2 * LANE) + EPS)
            for i, yb in enumerate((y0, y1)):
                cols = slice((2 * g + i) * LANE, (2 * g + i + 1) * LANE)
                y_ref[rows, cols] = (yb * r * gain_ref[:, cols]).astype(y_ref.dtype)
        return carry

    lax.fori_loop(0, nc, backward, 0)
    if has_state:
        st_ref[...] = hst_ref[...]


def _ssd_pack_state(st):
    n = st.shape[0]
    st = st.reshape(n, 2, SSD_GROUPS, 2, 2, SSD_STATE, SSD_HEAD_DIM)
    return st.transpose(0, 1, 3, 2, 5, 4, 6).reshape(n, 2, 2, LANE, LANE)


def _ssd_unpack_state(st):
    n = st.shape[0]
    st = st.reshape(n, 2, 2, SSD_GROUPS, SSD_STATE, 2, SSD_HEAD_DIM)
    return st.transpose(0, 1, 3, 2, 5, 4, 6).reshape(n, 2, SSD_HEADS, SSD_STATE, SSD_HEAD_DIM)


def _ssd_params(ssd_conv_w, ssd_conv_b, ssd_a_log, ssd_dt_bias, ssd_d, ssd_norm_g):
    def lanes8(v):
        return jnp.pad(v.astype(F32), ((0, 0), (0, 0), (0, LANE - SSD_HEADS)))
    e8, e64 = _head_expanders()
    return dict(wx=ssd_conv_w[:, :, :GROUP_W], wbc=ssd_conv_w[:, :, GROUP_W:],
                bx=ssd_conv_b[:, None, :GROUP_W], bbc=ssd_conv_b[:, None, GROUP_W:],
                a=lanes8(-jnp.exp(ssd_a_log.astype(F32))), dtb=lanes8(ssd_dt_bias),
                d=jnp.repeat(ssd_d, SSD_HEAD_DIM, axis=-1)[:, None, :], gain=ssd_norm_g[:, None, :],
                e8=e8, e64=e64)


def _ssd_scan(um, us, sp, layer, h0, n_seq, T, row_block0):
    has_h0 = h0 is not None
    has_state = not has_h0

    def col(off, w):
        return pl.BlockSpec((T, w), lambda b: (row_block0 + b, off // w))

    def per_layer(arr):
        return pl.BlockSpec((None,) + arr.shape[1:], lambda b: (layer,) + (0,) * (arr.ndim - 1))

    def const(arr):
        return pl.BlockSpec(arr.shape, lambda b: (0,) * arr.ndim)
    names = ('wx', 'wbc', 'bx', 'bbc', 'a', 'dtb', 'd', 'gain')
    in_specs = [col(MAIN_SSD_Z, GROUP_W), col(MAIN_SSD_XS, GROUP_W), col(MAIN_SSD_BC, 2 * LANE),
                pl.BlockSpec((T, LANE), lambda b: (row_block0 + b, SIDE_DT // LANE))]
    in_specs += [per_layer(sp[n]) for n in names] + [const(sp['e8']), const(sp['e64'])]
    args = [um, um, um, us] + [sp[n] for n in names] + [sp['e8'], sp['e64']]
    st_spec = pl.BlockSpec((None, 2, 2, LANE, LANE), lambda b: (b, 0, 0, 0, 0))
    if has_h0:
        in_specs.append(st_spec)
        args.append(h0)
    out_specs = [pl.BlockSpec((T, GROUP_W), lambda b: (b, 0))]
    out_shape = [jax.ShapeDtypeStruct((n_seq * T, GROUP_W), BF16)]
    if has_state:
        out_specs.append(st_spec)
        out_shape.append(jax.ShapeDtypeStruct((n_seq, 2, 2, LANE, LANE), F32))
    res = pl.pallas_call(
        functools.partial(_ssd_kernel, T=T, has_h0=has_h0, has_state=has_state),
        grid=(n_seq,),
        in_specs=in_specs,
        out_specs=out_specs,
        out_shape=out_shape,
        scratch_shapes=[pltpu.VMEM((T, GROUP_W), BF16), pltpu.VMEM((T, 2 * LANE), BF16),
                        pltpu.VMEM((T, GROUP_W), F32), pltpu.VMEM((2, 2, LANE, LANE), F32)],
        compiler_params=pltpu.CompilerParams(dimension_semantics=("parallel",), vmem_limit_bytes=VMEM_LIMIT),
        name="ssd",
    )(*args)
    return (res[0], res[1]) if has_state else (res[0], None)


GLA_MACRO = 128


def _log_sigmoid(x):
    return jnp.minimum(x, 0.0) - jnp.log(1.0 + jnp.exp(-jnp.abs(x)))


def _gla_kernel(*refs, T, has_h0, has_state):
    it = iter(refs)
    q_ref, k_ref, v_ref, r_ref, g1_ref = (next(it) for _ in range(5))
    wg_ref, bg_ref, gain_ref, ind_ref = (next(it) for _ in range(4))
    h0_ref = next(it) if has_h0 else None
    y_ref = next(it)
    st_ref = next(it) if has_state else None
    oacc_ref, hst_ref, qbuf, kbuf, bfbuf, bbbuf, vbuf, obuf = (next(it) for _ in range(8))

    C, L = GLA_MACRO, GLA_CHUNK
    nb = C // L
    nc = T // C
    ii = lax.broadcasted_iota(jnp.int32, (C, C), 0)
    jj = lax.broadcasted_iota(jnp.int32, (C, C), 1)
    same = (ii // L) == (jj // L)
    tri_l = jnp.logical_and(same, jj <= ii).astype(BF16)
    tri_u = jnp.logical_and(same, jj >= ii).astype(BF16)
    ones_b = same.astype(BF16)
    lane16 = lax.broadcasted_iota(jnp.int32, (L, LANE), 1)
    masks16 = (lane16 < GLA_QK, lane16 >= GLA_QK)
    ri = lax.broadcasted_iota(jnp.int32, (L, 2 * LANE), 0)

    if has_h0:
        hst_ref[...] = h0_ref[...]
    else:
        hst_ref[...] = jnp.zeros_like(hst_ref)

    def log_decay(rows, d):
        g1 = g1_ref[rows, :]
        hi = g1.astype(BF16)
        lo = (g1 - hi.astype(F32)).astype(BF16)
        logits = _dot(hi, wg_ref[d]) + _dot(lo, wg_ref[d]) + bg_ref[d]
        return _log_sigmoid(logits) / GLA_GATE_TEMP

    def recurrence(d, rows, r0, qt, kt, dec, blocks, first):
        vb = v_ref[rows, :]
        for blk in blocks:
            rs = slice(blk * L, (blk + 1) * L)
            orow = pl.ds(r0 + blk * L, L)
            for h in range(GLA_HEADS):
                p, e = divmod(h, 2)
                lanes = slice(p * LANE, (p + 1) * LANE)
                hcols = slice(h * LANE, (h + 1) * LANE)
                ht = hst_ref[d, h]
                qm = jnp.where(masks16[e], qt[rs, lanes], 0.0).astype(BF16)
                km = jnp.where(masks16[e], kt[rs, lanes], 0.0).astype(BF16)
                o = _dot_nt(qm, ht.astype(BF16))
                if first:
                    oacc_ref[orow, hcols] = obuf[rs, hcols] + o
                else:
                    oacc_ref[orow, hcols] += o
                hst_ref[d, h] = ht * dec[blk * L:blk * L + 1, lanes] + _dot_tn(vb[rs, hcols], km)

    def forward(c, carry):
        r0 = pl.multiple_of(c * C, C)
        rows = pl.ds(r0, C)
        q = q_ref[rows, :].astype(F32) * (GLA_QK ** -0.5)
        k = k_ref[rows, :].astype(F32)
        la_f, la_b = log_decay(rows, 0), log_decay(rows, 1)
        bf, tot_f = _dot01_l(tri_l, la_f), _dot01_l(ones_b, la_f)
        bb = _dot01_l(tri_u, la_b)
        qbuf[...] = q
        kbuf[...] = k
        bfbuf[...] = bf
        bbbuf[...] = bb
        vbuf[...] = v_ref[rows, :].astype(F32)

        def intra(blk, carry2):
            b0 = pl.multiple_of(blk * L, L)
            rs = pl.ds(b0, L)
            qi, bfi, bbi = qbuf[rs, :], bfbuf[rs, :], bbbuf[rs, :]
            pieces = []
            for j in range(L):
                rj = pl.ds(b0 + j, 1)
                expo = jnp.where(ri >= j, bfi - bfbuf[rj, :], bbi - bbbuf[rj, :])
                e = qi * kbuf[rj, :] * jnp.exp(jnp.minimum(expo, 0.0))
                pieces.append(jnp.where(ri == j, 2.0 * e, e).astype(BF16))
            spread = _dot(jnp.concatenate(pieces, axis=0), ind_ref[...])
            acc = jnp.zeros((L, GROUP_W), F32)
            for j in range(L):
                acc += spread[j * L:(j + 1) * L, :] * vbuf[pl.ds(b0 + j, 1), :]
            obuf[rs, :] = acc
            return carry2

        lax.fori_loop(0, nb, intra, 0)
        recurrence(0, rows, r0, q * jnp.exp(bf), k * jnp.exp(tot_f - bf), jnp.exp(tot_f), range(nb), True)
        return carry

    lax.fori_loop(0, nc, forward, 0)

    def backward(t, carry):
        c = nc - 1 - t
        r0 = pl.multiple_of(c * C, C)
        rows = pl.ds(r0, C)
        q = q_ref[rows, :].astype(F32) * (GLA_QK ** -0.5)
        k = k_ref[rows, :].astype(F32)
        la_b = log_decay(rows, 1)
        bb, tot_b = _dot01_l(tri_u, la_b), _dot01_l(ones_b, la_b)
        recurrence(1, rows, r0, q * jnp.exp(bb), k * jnp.exp(tot_b - bb), jnp.exp(tot_b),
                   range(nb - 1, -1, -1), False)
        for h in range(GLA_HEADS):
            hcols = slice(h * LANE, (h + 1) * LANE)
            o = oacc_ref[rows, hcols]
            yn = o * lax.rsqrt(jnp.mean(o * o, axis=-1, keepdims=True) + EPS) * gain_ref[:, hcols]
            y_ref[rows, hcols] = (_silu(r_ref[rows, hcols].astype(F32)) * yn).astype(y_ref.dtype)
        return carry

    lax.fori_loop(0, nc, backward, 0)
    if has_state:
        st_ref[...] = hst_ref[...]


def _gla_pack_state(st):
    n = st.shape[0]
    ht = jnp.swapaxes(st, -1, -2)
    z = jnp.zeros_like(ht)
    even = jnp.concatenate([ht, z], axis=-1)
    odd = jnp.concatenate([z, ht], axis=-1)
    sel = (jnp.arange(GLA_HEADS) % 2 == 0)[None, None, :, None, None]
    return jnp.where(sel, even, odd)


def _gla_unpack_state(st):
    even, odd = st[..., :GLA_QK], st[..., GLA_QK:]
    sel = (jnp.arange(GLA_HEADS) % 2 == 0)[None, None, :, None, None]
    return jnp.swapaxes(jnp.where(sel, even, odd), -1, -2)


def _gla_params(gla_w_g2, gla_b_g, gla_norm_g):
    wg = jnp.zeros((DEPTH, 2, LANE, 2 * LANE), F32)
    g1_lane = SIDE_G1 - SIDE_DT
    wg = wg.at[:, :, g1_lane:g1_lane + GLA_GATE_RANK, :].set(gla_w_g2)
    ind = np.zeros((2 * LANE, GROUP_W), np.float32)
    for h in range(GLA_HEADS):
        ind[h * GLA_QK:(h + 1) * GLA_QK, h * GLA_V:(h + 1) * GLA_V] = 1.0
    return dict(wg=wg.astype(BF16), bg=gla_b_g[:, :, None, :], gain=gla_norm_g[:, None, :],
                ind=jnp.asarray(ind, BF16))


def _gla_scan(um, us, gp, layer, h0, n_seq, T, row_block0):
    has_h0 = h0 is not None
    has_state = not has_h0
    C = GLA_MACRO

    def col(off, w):
        return pl.BlockSpec((T, w), lambda b: (row_block0 + b, off // w))

    def per_layer(arr):
        return pl.BlockSpec((None,) + arr.shape[1:], lambda b: (layer,) + (0,) * (arr.ndim - 1))
    in_specs = [col(MAIN_GLA_Q, 2 * LANE), col(MAIN_GLA_K, 2 * LANE), col(MAIN_GLA_V, GROUP_W), col(MAIN_GLA_R, GROUP_W),
                pl.BlockSpec((T, LANE), lambda b: (row_block0 + b, SIDE_DT // LANE)),
                per_layer(gp['wg']), per_layer(gp['bg']), per_layer(gp['gain']),
                pl.BlockSpec(gp['ind'].shape, lambda b: (0, 0))]
    args = [um, um, um, um, us, gp['wg'], gp['bg'], gp['gain'], gp['ind']]
    st_spec = pl.BlockSpec((None, 2, GLA_HEADS, LANE, LANE), lambda b: (b, 0, 0, 0, 0))
    if has_h0:
        in_specs.append(st_spec)
        args.append(h0)
    out_specs = [pl.BlockSpec((T, GROUP_W), lambda b: (b, 0))]
    out_shape = [jax.ShapeDtypeStruct((n_seq * T, GROUP_W), BF16)]
    if has_state:
        out_specs.append(st_spec)
        out_shape.append(jax.ShapeDtypeStruct((n_seq, 2, GLA_HEADS, LANE, LANE), F32))
    res = pl.pallas_call(
        functools.partial(_gla_kernel, T=T, has_h0=has_h0, has_state=has_state),
        grid=(n_seq,),
        in_specs=in_specs,
        out_specs=out_specs,
        out_shape=out_shape,
        scratch_shapes=[pltpu.VMEM((T, GROUP_W), F32), pltpu.VMEM((2, GLA_HEADS, LANE, LANE), F32),
                        pltpu.VMEM((C, 2 * LANE), F32), pltpu.VMEM((C, 2 * LANE), F32),
                        pltpu.VMEM((C, 2 * LANE), F32), pltpu.VMEM((C, 2 * LANE), F32),
                        pltpu.VMEM((C, GROUP_W), F32), pltpu.VMEM((C, GROUP_W), F32)],
        compiler_params=pltpu.CompilerParams(dimension_semantics=("parallel",), vmem_limit_bytes=VMEM_LIMIT),
        name="gla",
    )(*args)
    return (res[0], res[1]) if has_state else (res[0], None)


def kernel(x_prompt, x_sample, cache_mla_kv, state_ssd, state_ret, state_gla, c, c_ctx, w_mod, b_mod, norm_ffn1, ffn1_wg, ffn1_wu, ffn1_wd, norm_mix, w_in, ssd_conv_w, ssd_conv_b, ssd_a_log, ssd_dt_bias, ssd_d, ssd_norm_g, mla_q_lat_gain, mla_w_q_up, mla_q_gain, mla_kv_lat_gain, mla_w_kv_up, mla_k_gain, ret_decay_logit, ret_norm_g, gla_w_g2, gla_b_g, gla_norm_g, w_out, norm_ffn2, ffn2_wg, ffn2_wu, ffn2_wd):
    sp = _ssd_params(ssd_conv_w, ssd_conv_b, ssd_a_log, ssd_dt_bias, ssd_d, ssd_norm_g)
    ssd_h0 = jnp.stack([_ssd_pack_state(state_ssd[:, l]) for l in range(DEPTH)], axis=1)
    gp = _gla_params(gla_w_g2, gla_b_g, gla_norm_g)
    gla_h0 = jnp.stack([_gla_pack_state(state_gla[:, l]) for l in range(DEPTH)], axis=1)

    main_idx, side_idx = _in_proj_columns()
    w_main = _gather_columns(w_in, main_idx).astype(BF16)
    w_side = _gather_columns(w_in, side_idx).astype(BF16)
    w_out_b = w_out.astype(BF16)
    f1 = (ffn1_wg.astype(BF16), ffn1_wu.astype(BF16), ffn1_wd.astype(BF16))
    f2 = (ffn2_wg.astype(BF16), ffn2_wu.astype(BF16), ffn2_wd.astype(BF16))
    g_ffn1 = norm_ffn1.reshape(DEPTH, 1, D_MODEL)
    g_mix = norm_mix.reshape(DEPTH, 1, D_MODEL)
    g_ffn2 = norm_ffn2.reshape(DEPTH, 1, D_MODEL)
    mw = _mla_weights(mla_q_lat_gain, mla_w_q_up, mla_q_gain, mla_kv_lat_gain, mla_w_kv_up, mla_k_gain)
    cos_lat, sin_lat = _rope_tables(DEC_SEQ)
    cos_all = jnp.concatenate([jnp.ones((TOK_TM, LANE), F32), cos_lat], axis=0)
    sin_all = jnp.concatenate([jnp.zeros((TOK_TM, LANE), F32), sin_lat], axis=0)
    ret_lg = jax.nn.log_sigmoid(ret_decay_logit.astype(F32))
    ret_gain = ret_norm_g.reshape(DEPTH, 1, GROUP_W)
    ret_h0 = state_ret.reshape(DEC_BATCH, DEPTH, 2, 2, LANE, LANE)

    c_all = jnp.zeros((MOD_ROWS, D_MODEL), F32).at[0].set(c_ctx).at[1:1 + DEC_BATCH].set(c)
    mod = _modulation(c_all, w_mod, b_mod).reshape(DEPTH * MOD_ROWS, 1, N_MOD * D_MODEL)

    x = jnp.concatenate([x_prompt.reshape(CTX_ROWS, D_MODEL), x_sample.reshape(LAT_ROWS, D_MODEL)], axis=0)
    kv_list, ssd_list, ret_list, gla_list = [], [], [], []
    for l in range(DEPTH):
        x = _ffn(x, mod, g_ffn1, *f1, l, 0)
        um = _in_proj(x, mod, g_mix, w_main, l, BF16, MAIN_TN)
        us = _in_proj(x, mod, g_mix, w_side, l, F32, SIDE_W)

        y_mla, kv_lat = _mla_layer(um, us, cache_mla_kv[:, l], mw, cos_all, sin_all, l)
        yr_c, s_ret = _retention(um, ret_lg[l], ret_gain[l], None, None, BATCH, SEQ, 0)
        yr_l, _ = _retention(um, ret_lg[l], ret_gain[l], ret_h0[:, l], (cos_lat, sin_lat),
                             DEC_BATCH, DEC_SEQ, CTX_ROWS // DEC_SEQ)
        y_ret = jnp.concatenate([yr_c, yr_l], axis=0)

        ys_c, s_ssd = _ssd_scan(um, us, sp, l, None, BATCH, SEQ, 0)
        ys_l, _ = _ssd_scan(um, us, sp, l, ssd_h0[:, l], DEC_BATCH, DEC_SEQ, CTX_ROWS // DEC_SEQ)
        y_ssd = jnp.concatenate([ys_c, ys_l], axis=0)

        yg_c, s_gla = _gla_scan(um, us, gp, l, None, BATCH, SEQ, 0)
        yg_l, _ = _gla_scan(um, us, gp, l, gla_h0[:, l], DEC_BATCH, DEC_SEQ, CTX_ROWS // DEC_SEQ)
        y_gla = jnp.concatenate([yg_c, yg_l], axis=0)

        kv_list.append(kv_lat[:CTX_ROWS, :MLA_CACHE_W].reshape(BATCH, SEQ, MLA_CACHE_W))
        ssd_list.append(_ssd_unpack_state(s_ssd))
        ret_list.append(s_ret.reshape(BATCH, 2, RET_HEADS, RET_QK, RET_V))
        gla_list.append(_gla_unpack_state(s_gla))
        x = _out_proj(x, [y_ssd, y_mla, y_ret, y_gla], mod, w_out_b, l)
        x = _ffn(x, mod, g_ffn2, *f2, l, 6)
    y_p = x[:CTX_ROWS].reshape(BATCH, SEQ, D_MODEL)
    y_s = x[CTX_ROWS:].reshape(DEC_BATCH, DEC_SEQ, D_MODEL)
    return (y_p, y_s, jnp.stack(kv_list, axis=1), jnp.stack(ssd_list, axis=1),
            jnp.stack(ret_list, axis=1), jnp.stack(gla_list, axis=1))
```

```python
import functools

import jax
import jax.numpy as jnp
import numpy as np
from jax import lax
from jax.experimental import pallas as pl
from jax.experimental.pallas import tpu as pltpu

F32 = jnp.float32
BF16 = jnp.bfloat16

D_MODEL = 2048
BATCH = 32
SEQ = 256
DEPTH = 4
DEC_BATCH = 4
DEC_SEQ = 4096
PAST_LEN = 256
GRID_W = 64
ROPE_BASE = 10000.0
EPS = 1e-6
D_FF = 5632
N_MOD = 9
GROUP_W = D_MODEL // 4
ATTN_BLOCK = 128

SSD_HEAD_DIM = 64
SSD_HEADS = GROUP_W // SSD_HEAD_DIM
SSD_STATE = 64
SSD_GROUPS = 2
SSD_CONV_K = 3
SSD_CHUNK = 128
SSD_CONV_CH = GROUP_W + 2 * SSD_GROUPS * SSD_STATE
SSD_IN = GROUP_W + SSD_CONV_CH + SSD_HEADS

MLA_HEADS = 4
MLA_NOPE = 128
MLA_ROPE = 64
MLA_V = GROUP_W // MLA_HEADS
MLA_Q_RANK = 384
MLA_KV_RANK = 128
MLA_QK = MLA_NOPE + MLA_ROPE
MLA_IN = MLA_Q_RANK + MLA_KV_RANK + MLA_ROPE
MLA_CACHE_W = MLA_KV_RANK + MLA_ROPE

RET_HEADS = 4
RET_QK = 64
RET_V = GROUP_W // RET_HEADS
RET_CHUNK = 128
RET_IN = 2 * RET_HEADS * RET_QK + 2 * GROUP_W

GLA_HEADS = 4
GLA_QK = 64
GLA_V = GROUP_W // GLA_HEADS
GLA_GATE_RANK = 16
GLA_GATE_TEMP = 16.0
GLA_CHUNK = 16
GLA_IN = 2 * GLA_HEADS * GLA_QK + GROUP_W + GLA_GATE_RANK + GROUP_W

IN_W = SSD_IN + MLA_IN + RET_IN + GLA_IN

LANE = 128
CTX_ROWS = BATCH * SEQ
LAT_ROWS = DEC_BATCH * DEC_SEQ
ROWS = CTX_ROWS + LAT_ROWS
MOD_ROWS = 8
VMEM_LIMIT = 56 * 1024 * 1024
TOK_TM = 512
MLA_HEAD_W = 2 * LANE

MAIN_SSD_Z = 0
MAIN_SSD_XS = 512
MAIN_RET_V = 1024
MAIN_RET_G = 1536
MAIN_GLA_V = 2048
MAIN_GLA_R = 2560
MAIN_SSD_BC = 3072
MAIN_RET_Q = 3328
MAIN_RET_K = 3584
MAIN_GLA_Q = 3840
MAIN_GLA_K = 4096
MAIN_MLA_CQ = 4352
MAIN_W = 4864
MAIN_TN = MAIN_W // 2
SIDE_CKV = 0
SIDE_KPE = SIDE_CKV + MLA_KV_RANK
SIDE_DT = 2 * LANE
SIDE_G1 = SIDE_DT + SSD_HEADS
SIDE_W = 3 * LANE


def _in_proj_columns():
    o_ssd, o_mla, o_ret, o_gla = 0, SSD_IN, SSD_IN + MLA_IN, SSD_IN + MLA_IN + RET_IN
    main = np.full((MAIN_W,), -1, np.int64)

    def put(dst, src, n):
        main[dst:dst + n] = src + np.arange(n)
    qk = RET_HEADS * RET_QK
    put(MAIN_SSD_Z, o_ssd, GROUP_W)
    put(MAIN_SSD_XS, o_ssd + GROUP_W, GROUP_W)
    put(MAIN_SSD_BC, o_ssd + 2 * GROUP_W, 2 * SSD_GROUPS * SSD_STATE)
    put(MAIN_MLA_CQ, o_mla, MLA_Q_RANK)
    put(MAIN_RET_Q, o_ret, qk)
    put(MAIN_RET_K, o_ret + qk, qk)
    put(MAIN_RET_V, o_ret + 2 * qk, GROUP_W)
    put(MAIN_RET_G, o_ret + 2 * qk + GROUP_W, GROUP_W)
    put(MAIN_GLA_Q, o_gla, qk)
    put(MAIN_GLA_K, o_gla + qk, qk)
    put(MAIN_GLA_V, o_gla + 2 * qk, GROUP_W)
    put(MAIN_GLA_R, o_gla + 2 * qk + GROUP_W + GLA_GATE_RANK, GROUP_W)
    side = np.full((SIDE_W,), -1, np.int64)
    side[SIDE_CKV:SIDE_CKV + MLA_KV_RANK + MLA_ROPE] = o_mla + MLA_Q_RANK + np.arange(MLA_KV_RANK + MLA_ROPE)
    side[SIDE_DT:SIDE_DT + SSD_HEADS] = o_ssd + GROUP_W + SSD_CONV_CH + np.arange(SSD_HEADS)
    side[SIDE_G1:SIDE_G1 + GLA_GATE_RANK] = o_gla + 2 * qk + GROUP_W + np.arange(GLA_GATE_RANK)
    return main, side


def _gather_columns(w, idx):
    safe = np.where(idx < 0, 0, idx)
    out = jnp.take(w, jnp.asarray(safe, jnp.int32), axis=-1)
    return jnp.where(jnp.asarray(idx >= 0), out, 0.0)


def _mod_row(i, tm):
    ctx_tiles = CTX_ROWS // tm
    per_seq = DEC_SEQ // tm
    return jnp.where(i < ctx_tiles, 0, 1 + (i - ctx_tiles) // per_seq)


def _mod_spec(layer, which, tm, n_grid):
    if n_grid == 1:
        return pl.BlockSpec((1, 1, D_MODEL), lambda i: (layer * MOD_ROWS + _mod_row(i, tm), 0, which))
    return pl.BlockSpec((1, 1, D_MODEL), lambda i, j: (layer * MOD_ROWS + _mod_row(i, tm), 0, which))


def _dot(a, b):
    return jnp.dot(a, b, preferred_element_type=F32)


def _dot_nt(a, b):
    return lax.dot_general(a, b, (((1,), (1,)), ((), ())), preferred_element_type=F32)


def _dot_tn(a, b):
    return lax.dot_general(a, b, (((0,), (0,)), ((), ())), preferred_element_type=F32)


def _silu(x):
    return x * jax.nn.sigmoid(x)


def _mod_kernel(c_ref, w_ref, b_ref, o_ref):
    s = _silu(c_ref[...])
    hi = s.astype(BF16)
    lo = (s - hi.astype(F32)).astype(BF16)
    w = w_ref[...].astype(BF16)
    o_ref[...] = _dot(hi, w) + _dot(lo, w) + b_ref[...]


def _modulation(c_all, w_mod, b_mod):
    tn = 1024
    n = N_MOD * D_MODEL
    return pl.pallas_call(
        _mod_kernel,
        grid=(DEPTH, n // tn),
        in_specs=[pl.BlockSpec((MOD_ROWS, D_MODEL), lambda l, j: (0, 0)),
                  pl.BlockSpec((None, D_MODEL, tn), lambda l, j: (l, 0, j)),
                  pl.BlockSpec((None, 1, tn), lambda l, j: (l, 0, j))],
        out_specs=pl.BlockSpec((None, MOD_ROWS, tn), lambda l, j: (l, 0, j)),
        out_shape=jax.ShapeDtypeStruct((DEPTH, MOD_ROWS, n), F32),
        compiler_params=pltpu.CompilerParams(dimension_semantics=("arbitrary", "arbitrary"),
                                             vmem_limit_bytes=VMEM_LIMIT),
        name="modulation",
    )(c_all, w_mod, b_mod.reshape(DEPTH, 1, n))


def _norm_modulate(x, g, shift, scale):
    r = lax.rsqrt(jnp.mean(x * x, axis=-1, keepdims=True) + EPS)
    return (x * r * g) * (1.0 + scale) + shift


FFN_TF = 512
FFN_TN = 512
FFN_NF = D_FF // FFN_TF
FFN_NN = D_MODEL // FFN_TN
MXU_W = 256


def _ffn_kernel(x_ref, xt_ref, g_ref, shift_ref, scale_ref, gate_ref, wg_ref, wu_ref, wd_ref, o_ref, h_ref, a_ref):
    j = pl.program_id(1)

    @pl.when(j == 0)
    def _():
        h_ref[...] = _norm_modulate(x_ref[...], g_ref[...], shift_ref[0], scale_ref[0]).astype(BF16)

    @pl.when(j < FFN_NF)
    def _():
        h = h_ref[...]
        base = pl.multiple_of(j * FFN_TF, FFN_TF)
        for s in range(FFN_TF // MXU_W):
            cols = slice(s * MXU_W, (s + 1) * MXU_W)
            g = _dot(h, wg_ref[:, cols])
            u = _dot(h, wu_ref[:, cols])
            a_ref[:, pl.ds(base + s * MXU_W, MXU_W)] = (_silu(g) * u).astype(BF16)

    @pl.when(j >= FFN_NF)
    def _():
        o_ref[...] = xt_ref[...] + 0.5 * gate_ref[0] * _dot(a_ref[...], wd_ref[...])


def _ffn(x, mod, norm_g, wg, wu, wd, layer, mod_base, tm=TOK_TM):
    def up_tile(i, j):
        return (layer, 0, jnp.minimum(j, FFN_NF - 1))

    def down_tile(j):
        return jnp.maximum(j - FFN_NF, 0)
    gate_blocks = D_MODEL // FFN_TN
    return pl.pallas_call(
        _ffn_kernel,
        grid=(ROWS // tm, FFN_NF + FFN_NN),
        in_specs=[pl.BlockSpec((tm, D_MODEL), lambda i, j: (i, 0)),
                  pl.BlockSpec((tm, FFN_TN), lambda i, j: (i, down_tile(j))),
                  pl.BlockSpec((None, 1, D_MODEL), lambda i, j: (layer, 0, 0)),
                  _mod_spec(layer, mod_base + 0, tm, 2),
                  _mod_spec(layer, mod_base + 1, tm, 2),
                  pl.BlockSpec((1, 1, FFN_TN), lambda i, j: (layer * MOD_ROWS + _mod_row(i, tm), 0,
                                                             (mod_base + 2) * gate_blocks + down_tile(j))),
                  pl.BlockSpec((None, D_MODEL, FFN_TF), up_tile),
                  pl.BlockSpec((None, D_MODEL, FFN_TF), up_tile),
                  pl.BlockSpec((None, D_FF, FFN_TN), lambda i, j: (layer, 0, down_tile(j)))],
        out_specs=pl.BlockSpec((tm, FFN_TN), lambda i, j: (i, down_tile(j))),
        out_shape=jax.ShapeDtypeStruct((ROWS, D_MODEL), F32),
        scratch_shapes=[pltpu.VMEM((tm, D_MODEL), BF16), pltpu.VMEM((tm, D_FF), BF16)],
        compiler_params=pltpu.CompilerParams(dimension_semantics=("parallel", "arbitrary"),
                                             vmem_limit_bytes=VMEM_LIMIT),
        name="ffn",
    )(x, x, norm_g, mod, mod, mod, wg, wu, wd)


def _in_proj_kernel(x_ref, g_ref, shift_ref, scale_ref, w_ref, ws_ref, o_ref, os_ref, h_ref):
    @pl.when(pl.program_id(1) == 0)
    def _():
        h_ref[...] = _norm_modulate(x_ref[...], g_ref[...], shift_ref[0], scale_ref[0]).astype(BF16)
        os_ref[...] = _dot(h_ref[...], ws_ref[...])

    o_ref[...] = _dot(h_ref[...], w_ref[...]).astype(o_ref.dtype)


def _in_proj(x, mod, norm_g, w_main, w_side, layer, tm=TOK_TM):
    return pl.pallas_call(
        _in_proj_kernel,
        grid=(ROWS // tm, MAIN_W // MAIN_TN),
        in_specs=[pl.BlockSpec((tm, D_MODEL), lambda i, j: (i, 0)),
                  pl.BlockSpec((None, 1, D_MODEL), lambda i, j: (layer, 0, 0)),
                  _mod_spec(layer, 3, tm, 2),
                  _mod_spec(layer, 4, tm, 2),
                  pl.BlockSpec((None, D_MODEL, MAIN_TN), lambda i, j: (layer, 0, j)),
                  pl.BlockSpec((None, D_MODEL, SIDE_W), lambda i, j: (layer, 0, 0))],
        out_specs=[pl.BlockSpec((tm, MAIN_TN), lambda i, j: (i, j)),
                   pl.BlockSpec((tm, SIDE_W), lambda i, j: (i, 0))],
        out_shape=[jax.ShapeDtypeStruct((ROWS, MAIN_W), BF16),
                   jax.ShapeDtypeStruct((ROWS, SIDE_W), F32)],
        scratch_shapes=[pltpu.VMEM((tm, D_MODEL), BF16)],
        compiler_params=pltpu.CompilerParams(dimension_semantics=("parallel", "arbitrary"),
                                             vmem_limit_bytes=VMEM_LIMIT),
        name="in_proj",
    )(x, norm_g, mod, mod, w_main, w_side)


def _out_proj_kernel(x_ref, y0_ref, y1_ref, y2_ref, y3_ref, gate_ref, w_ref, o_ref):
    acc = _dot(y0_ref[...], w_ref[0 * GROUP_W:1 * GROUP_W, :])
    acc += _dot(y1_ref[...], w_ref[1 * GROUP_W:2 * GROUP_W, :])
    acc += _dot(y2_ref[...], w_ref[2 * GROUP_W:3 * GROUP_W, :])
    acc += _dot(y3_ref[...], w_ref[3 * GROUP_W:4 * GROUP_W, :])
    o_ref[...] = x_ref[...] + gate_ref[0] * acc


def _out_proj(x, ys, mod, w, layer, tm=TOK_TM):
    row = pl.BlockSpec((tm, D_MODEL), lambda i: (i, 0))
    yspec = pl.BlockSpec((tm, GROUP_W), lambda i: (i, 0))
    return pl.pallas_call(
        _out_proj_kernel,
        grid=(ROWS // tm,),
        in_specs=[row, yspec, yspec, yspec, yspec,
                  _mod_spec(layer, 5, tm, 1),
                  pl.BlockSpec((None, D_MODEL, D_MODEL), lambda i: (layer, 0, 0))],
        out_specs=row,
        out_shape=jax.ShapeDtypeStruct((ROWS, D_MODEL), F32),
        compiler_params=pltpu.CompilerParams(dimension_semantics=("parallel",),
                                             vmem_limit_bytes=VMEM_LIMIT),
        name="out_proj",
    )(x, *ys, mod, w)


def _rope_tables(T):
    n_rows = T // GRID_W
    row = jnp.repeat(jnp.arange(n_rows, dtype=F32), GRID_W)
    col = jnp.tile(jnp.arange(GRID_W, dtype=F32), n_rows)
    d_axis = MLA_ROPE // 2
    inv = ROPE_BASE ** (-jnp.arange(0, d_axis, 2, dtype=F32) / d_axis)
    ar, ac = row[:, None] * inv, col[:, None] * inv
    cos = jnp.concatenate([jnp.cos(ar), jnp.cos(ar), jnp.cos(ac), jnp.cos(ac)], axis=-1)
    sin = jnp.concatenate([-jnp.sin(ar), jnp.sin(ar), -jnp.sin(ac), jnp.sin(ac)], axis=-1)
    return jnp.tile(cos, (1, 2)), jnp.tile(sin, (1, 2))


def _swap16(x):
    lane = lax.broadcasted_iota(jnp.int32, x.shape, 1)
    up = pltpu.roll(x, LANE - 16, 1)
    down = pltpu.roll(x, 16, 1)
    return jnp.where((lane % 32) < 16, up, down)


def _rope(x, cos, sin):
    return x * cos + _swap16(x) * sin


def _mla_q_kernel(c0_ref, c1_ref, c2_ref, gl_ref, w_ref, gq_ref, cos_ref, sin_ref, o_ref):
    cs = [r[...].astype(F32) for r in (c0_ref, c1_ref, c2_ref)]
    ss = sum(jnp.sum(c * c, axis=-1, keepdims=True) for c in cs)
    r = lax.rsqrt(ss / MLA_Q_RANK + EPS)
    q = sum(_dot((cs[i] * r * gl_ref[:, i * LANE:(i + 1) * LANE]).astype(BF16),
                 w_ref[i * LANE:(i + 1) * LANE, :]) for i in range(3))
    cos, sin = cos_ref[...], sin_ref[...]
    scale = MLA_QK ** -0.5
    for h in range(MLA_HEADS):
        a = q[:, h * MLA_HEAD_W:h * MLA_HEAD_W + LANE]
        b = q[:, h * MLA_HEAD_W + LANE:(h + 1) * MLA_HEAD_W]
        ssq = jnp.sum(a * a, axis=-1, keepdims=True) + jnp.sum(b * b, axis=-1, keepdims=True)
        rh = lax.rsqrt(ssq / MLA_QK + EPS) * scale
        o_ref[:, h * MLA_HEAD_W:h * MLA_HEAD_W + LANE] = (a * rh * gq_ref[:, :LANE]).astype(BF16)
        o_ref[:, h * MLA_HEAD_W + LANE:(h + 1) * MLA_HEAD_W] = _rope(b * rh * gq_ref[:, LANE:], cos, sin).astype(BF16)


def _rope_tile_index(i, tm):
    ctx_tiles = CTX_ROWS // tm
    return jnp.where(i < ctx_tiles, 0, 1 + (i - ctx_tiles) % (DEC_SEQ // tm))


def _mla_q(um, gl, wq, gq, cos_all, sin_all, layer, tm=TOK_TM):
    cq = MAIN_MLA_CQ // LANE
    tab = pl.BlockSpec((tm, LANE), lambda i: (_rope_tile_index(i, tm), 0))
    return pl.pallas_call(
        _mla_q_kernel,
        grid=(ROWS // tm,),
        in_specs=[pl.BlockSpec((tm, LANE), lambda i: (i, cq)),
                  pl.BlockSpec((tm, LANE), lambda i: (i, cq + 1)),
                  pl.BlockSpec((tm, LANE), lambda i: (i, cq + 2)),
                  pl.BlockSpec((None, 1, MLA_Q_RANK), lambda i: (layer, 0, 0)),
                  pl.BlockSpec((None, MLA_Q_RANK, MLA_HEADS * MLA_HEAD_W), lambda i: (layer, 0, 0)),
                  pl.BlockSpec((None, 1, MLA_HEAD_W), lambda i: (layer, 0, 0)),
                  tab, tab],
        out_specs=pl.BlockSpec((tm, MLA_HEADS * MLA_HEAD_W), lambda i: (i, 0)),
        out_shape=jax.ShapeDtypeStruct((ROWS, MLA_HEADS * MLA_HEAD_W), BF16),
        compiler_params=pltpu.CompilerParams(dimension_semantics=("parallel",), vmem_limit_bytes=VMEM_LIMIT),
        name="mla_q",
    )(um, um, um, gl, wq, gq, cos_all, sin_all)


def _mla_kv_kernel(s_ref, gl_ref, w_ref, gk_ref, cos_ref, sin_ref, lat_ref, k_ref, v_ref, *, normalize):
    ckv = s_ref[:, :LANE]
    kpe = s_ref[:, LANE:]
    if normalize:
        ckv = ckv * lax.rsqrt(jnp.mean(ckv * ckv, axis=-1, keepdims=True) + EPS) * gl_ref[...]
    lat_ref[:, :LANE] = ckv
    lat_ref[:, LANE:] = kpe
    kv = _dot(ckv.astype(BF16), w_ref[...])
    ss_pe = jnp.sum(kpe * kpe, axis=-1, keepdims=True)
    cos, sin = cos_ref[...], sin_ref[...]
    for h in range(MLA_HEADS):
        a = kv[:, h * LANE:(h + 1) * LANE]
        rh = lax.rsqrt((jnp.sum(a * a, axis=-1, keepdims=True) + ss_pe) / MLA_QK + EPS)
        k_ref[:, h * MLA_HEAD_W:h * MLA_HEAD_W + LANE] = (a * rh * gk_ref[:, :LANE]).astype(BF16)
        k_ref[:, h * MLA_HEAD_W + LANE:(h + 1) * MLA_HEAD_W] = _rope(kpe * rh * gk_ref[:, LANE:], cos, sin).astype(BF16)
    v_ref[...] = kv[:, MLA_HEADS * LANE:].astype(BF16)


def _mla_kv(src, gl, wkv, gk, cos_all, sin_all, layer, *, normalize, tab_index, tm=TOK_TM):
    rows = src.shape[0]
    tab = pl.BlockSpec((tm, LANE), lambda i: (tab_index(i, tm), 0))
    return pl.pallas_call(
        functools.partial(_mla_kv_kernel, normalize=normalize),
        grid=(rows // tm,),
        in_specs=[pl.BlockSpec((tm, 2 * LANE), lambda i: (i, 0)),
                  pl.BlockSpec((None, 1, MLA_KV_RANK), lambda i: (layer, 0, 0)),
                  pl.BlockSpec((None, MLA_KV_RANK, 2 * MLA_HEADS * LANE), lambda i: (layer, 0, 0)),
                  pl.BlockSpec((None, 1, MLA_HEAD_W), lambda i: (layer, 0, 0)),
                  tab, tab],
        out_specs=[pl.BlockSpec((tm, 2 * LANE), lambda i: (i, 0)),
                   pl.BlockSpec((tm, MLA_HEADS * MLA_HEAD_W), lambda i: (i, 0)),
                   pl.BlockSpec((tm, GROUP_W), lambda i: (i, 0))],
        out_shape=[jax.ShapeDtypeStruct((rows, 2 * LANE), F32),
                   jax.ShapeDtypeStruct((rows, MLA_HEADS * MLA_HEAD_W), BF16),
                   jax.ShapeDtypeStruct((rows, GROUP_W), BF16)],
        compiler_params=pltpu.CompilerParams(dimension_semantics=("parallel",), vmem_limit_bytes=VMEM_LIMIT),
        name="mla_kv",
    )(src, gl, wkv, gk, cos_all, sin_all)


def _attn_kernel(q_ref, k_ref, v_ref, o_ref):
    s = _dot_nt(q_ref[...], k_ref[...])
    m = jnp.max(s, axis=-1, keepdims=True)
    p = jnp.exp(s - m)
    l = jnp.sum(p, axis=-1, keepdims=True)
    o_ref[...] = (_dot(p.astype(BF16), v_ref[...]) / l).astype(o_ref.dtype)


def _attention(q, k, v, n_seq, t, tq):
    s_len = k.shape[1]
    nq = t // tq
    return pl.pallas_call(
        _attn_kernel,
        grid=(n_seq, MLA_HEADS, nq),
        in_specs=[pl.BlockSpec((tq, MLA_HEAD_W), lambda b, h, i: (b * nq + i, h)),
                  pl.BlockSpec((None, s_len, MLA_HEAD_W), lambda b, h, i: (b, 0, h)),
                  pl.BlockSpec((None, s_len, MLA_V), lambda b, h, i: (b, 0, h))],
        out_specs=pl.BlockSpec((tq, MLA_V), lambda b, h, i: (b * nq + i, h)),
        out_shape=jax.ShapeDtypeStruct((n_seq * t, GROUP_W), BF16),
        compiler_params=pltpu.CompilerParams(dimension_semantics=("parallel", "parallel", "arbitrary"),
                                             vmem_limit_bytes=VMEM_LIMIT),
        name="mla_attention",
    )(q, k, v)


def _mla_weights(mla_q_lat_gain, mla_w_q_up, mla_q_gain, mla_kv_lat_gain, mla_w_kv_up, mla_k_gain):
    qcol = np.full((MLA_HEADS * MLA_HEAD_W,), -1, np.int64)
    kvcol = np.zeros((2 * MLA_HEADS * LANE,), np.int64)
    for h in range(MLA_HEADS):
        qcol[h * MLA_HEAD_W:h * MLA_HEAD_W + MLA_QK] = h * MLA_QK + np.arange(MLA_QK)
        kvcol[h * LANE:(h + 1) * LANE] = h * (MLA_NOPE + MLA_V) + np.arange(MLA_NOPE)
        kvcol[(MLA_HEADS + h) * LANE:(MLA_HEADS + h + 1) * LANE] = h * (MLA_NOPE + MLA_V) + MLA_NOPE + np.arange(MLA_V)
    pad = jnp.zeros((DEPTH, MLA_HEAD_W - MLA_QK), F32)
    return dict(
        gl=mla_q_lat_gain.reshape(DEPTH, 1, MLA_Q_RANK),
        wq=_gather_columns(mla_w_q_up, qcol).astype(BF16),
        gq=jnp.concatenate([mla_q_gain, pad], axis=-1).reshape(DEPTH, 1, MLA_HEAD_W),
        gkv=mla_kv_lat_gain.reshape(DEPTH, 1, MLA_KV_RANK),
        wkv=_gather_columns(mla_w_kv_up, kvcol).astype(BF16),
        gk=jnp.concatenate([mla_k_gain, pad], axis=-1).reshape(DEPTH, 1, MLA_HEAD_W))


def _mla_layer(um, us, cache_l, mw, cos_all, sin_all, layer):
    q = _mla_q(um, mw['gl'], mw['wq'], mw['gq'], cos_all, sin_all, layer)
    kv_lat, k, v = _mla_kv(us, mw['gkv'], mw['wkv'], mw['gk'], cos_all, sin_all, layer,
                           normalize=True, tab_index=_rope_tile_index)
    cache2 = jnp.pad(cache_l.reshape(DEC_BATCH * PAST_LEN, MLA_CACHE_W), ((0, 0), (0, 2 * LANE - MLA_CACHE_W)))
    _, k_c, v_c = _mla_kv(cache2, mw['gkv'], mw['wkv'], mw['gk'], cos_all, sin_all, layer,
                          normalize=False, tab_index=lambda i, tm: 0)
    hw = MLA_HEADS * MLA_HEAD_W
    y_ctx = _attention(q[:CTX_ROWS], k[:CTX_ROWS].reshape(BATCH, SEQ, hw),
                       v[:CTX_ROWS].reshape(BATCH, SEQ, GROUP_W), BATCH, SEQ, SEQ)
    k_l = jnp.concatenate([k[CTX_ROWS:].reshape(DEC_BATCH, DEC_SEQ, hw),
                           k_c.reshape(DEC_BATCH, PAST_LEN, hw)], axis=1)
    v_l = jnp.concatenate([v[CTX_ROWS:].reshape(DEC_BATCH, DEC_SEQ, GROUP_W),
                           v_c.reshape(DEC_BATCH, PAST_LEN, GROUP_W)], axis=1)
    y_lat = _attention(q[CTX_ROWS:], k_l, v_l, DEC_BATCH, DEC_SEQ, 256)
    return jnp.concatenate([y_ctx, y_lat], axis=0), kv_lat


def _ret_kernel(*refs, T, C, rope, has_h0, has_state):
    it = iter(refs)
    q_ref, k_ref, v_ref, g_ref = next(it), next(it), next(it), next(it)
    cos_ref, sin_ref = (next(it), next(it)) if rope else (None, None)
    lg_ref, gain_ref = next(it), next(it)
    h0_ref = next(it) if has_h0 else None
    y_ref = next(it)
    st_ref = next(it) if has_state else None
    qs_ref, ks_ref, oacc_ref, dm_ref, tab_ref, hst_ref = (next(it) for _ in range(6))

    nc = T // C
    ii = lax.broadcasted_iota(jnp.int32, (C, C), 0)
    jj = lax.broadcasted_iota(jnp.int32, (C, C), 1)
    dif = (ii - jj).astype(F32)
    lane = lax.broadcasted_iota(jnp.int32, (C, LANE), 1)
    rowi = lax.broadcasted_iota(jnp.int32, (C, LANE), 0).astype(F32)
    m_lo = lane < RET_QK
    hrow = lax.broadcasted_iota(jnp.int32, (LANE, LANE), 0) < RET_QK

    for h in range(RET_HEADS):
        lf, lb = lg_ref[0, h], lg_ref[1, h]
        dm_ref[h] = (jnp.where(dif >= 0, jnp.exp(lf * jnp.maximum(dif, 0.0)), 0.0)
                     + jnp.where(dif <= 0, jnp.exp(lb * jnp.maximum(-dif, 0.0)), 0.0))
    for p in range(2):
        lf = jnp.where(m_lo, lg_ref[0, 2 * p], lg_ref[0, 2 * p + 1])
        lb = jnp.where(m_lo, lg_ref[1, 2 * p], lg_ref[1, 2 * p + 1])
        tab_ref[p, 0] = jnp.exp(lf * (rowi + 1.0))
        tab_ref[p, 1] = jnp.exp(lf * (C - 1.0 - rowi))
        tab_ref[p, 2] = jnp.exp(lb * (C - rowi))
        tab_ref[p, 3] = jnp.exp(lb * rowi)
    if has_h0:
        hst_ref[...] = h0_ref[...]
    else:
        hst_ref[...] = jnp.zeros_like(hst_ref)

    def chunk_decay(d, p):
        return jnp.exp(jnp.where(hrow, lg_ref[d, 2 * p], lg_ref[d, 2 * p + 1]) * float(C))

    def forward(c, carry):
        r0 = pl.multiple_of(c * C, C)
        rows = pl.ds(r0, C)
        for p in range(2):
            cols = slice(p * LANE, (p + 1) * LANE)
            qp = q_ref[rows, cols].astype(F32)
            kp = k_ref[rows, cols].astype(F32) * (RET_QK ** -0.5)
            if rope:
                qp = _rope(qp, cos_ref[rows, :], sin_ref[rows, :])
                kp = _rope(kp, cos_ref[rows, :], sin_ref[rows, :])
            qs_ref[rows, cols] = qp.astype(BF16)
            ks_ref[rows, cols] = kp.astype(BF16)
            kpb = kp.astype(BF16)
            hp = hst_ref[0, p]
            hpb = hp.astype(BF16)
            upd = jnp.zeros((LANE, LANE), F32)
            for e in range(2):
                h = 2 * p + e
                mh = m_lo if e == 0 else jnp.logical_not(m_lo)
                hcols = slice(h * LANE, (h + 1) * LANE)
                qm = jnp.where(mh, qp, 0.0)
                s = _dot_nt(qm.astype(BF16), kpb)
                vh = v_ref[rows, hcols]
                o = _dot((s * dm_ref[h]).astype(BF16), vh)
                o += _dot((qm * tab_ref[p, 0]).astype(BF16), hpb)
                oacc_ref[rows, hcols] = o
                upd += _dot_tn(jnp.where(mh, kp * tab_ref[p, 1], 0.0).astype(BF16), vh)
            hst_ref[0, p] = hp * chunk_decay(0, p) + upd
        return carry

    lax.fori_loop(0, nc, forward, 0)

    def backward(t, carry):
        c = nc - 1 - t
        r0 = pl.multiple_of(c * C, C)
        rows = pl.ds(r0, C)
        for p in range(2):
            cols = slice(p * LANE, (p + 1) * LANE)
            qp = qs_ref[rows, cols].astype(F32)
            kp = ks_ref[rows, cols].astype(F32)
            hp = hst_ref[1, p]
            hpb = hp.astype(BF16)
            upd = jnp.zeros((LANE, LANE), F32)
            for e in range(2):
                h = 2 * p + e
                mh = m_lo if e == 0 else jnp.logical_not(m_lo)
                hcols = slice(h * LANE, (h + 1) * LANE)
                vh = v_ref[rows, hcols]
                o = oacc_ref[rows, hcols] + _dot(jnp.where(mh, qp * tab_ref[p, 2], 0.0).astype(BF16), hpb)
                upd += _dot_tn(jnp.where(mh, kp * tab_ref[p, 3], 0.0).astype(BF16), vh)
                oc = o - jnp.mean(o, axis=-1, keepdims=True)
                yn = oc * lax.rsqrt(jnp.mean(oc * oc, axis=-1, keepdims=True) + EPS) * gain_ref[:, hcols]
                y_ref[rows, hcols] = (_silu(g_ref[rows, hcols].astype(F32)) * yn).astype(y_ref.dtype)
            hst_ref[1, p] = hp * chunk_decay(1, p) + upd
        return carry

    lax.fori_loop(0, nc, backward, 0)
    if has_state:
        st_ref[...] = hst_ref[...]


def _retention(um, lg, gain, h0, rope_tabs, n_seq, T, row_block0):
    C = RET_CHUNK
    rope = rope_tabs is not None
    has_h0 = h0 is not None
    has_state = not has_h0

    def col(off, w):
        return pl.BlockSpec((T, w), lambda b: (row_block0 + b, off // w))
    in_specs = [col(MAIN_RET_Q, 2 * LANE), col(MAIN_RET_K, 2 * LANE), col(MAIN_RET_V, GROUP_W), col(MAIN_RET_G, GROUP_W)]
    args = [um, um, um, um]
    if rope:
        in_specs += [pl.BlockSpec((T, LANE), lambda b: (0, 0))] * 2
        args += list(rope_tabs)
    in_specs += [pl.BlockSpec(memory_space=pltpu.SMEM), pl.BlockSpec((1, GROUP_W), lambda b: (0, 0))]
    args += [lg, gain]
    st_spec = pl.BlockSpec((None, 2, 2, LANE, LANE), lambda b: (b, 0, 0, 0, 0))
    if has_h0:
        in_specs.append(st_spec)
        args.append(h0)
    out_specs = [pl.BlockSpec((T, GROUP_W), lambda b: (b, 0))]
    out_shape = [jax.ShapeDtypeStruct((n_seq * T, GROUP_W), BF16)]
    if has_state:
        out_specs.append(st_spec)
        out_shape.append(jax.ShapeDtypeStruct((n_seq, 2, 2, LANE, LANE), F32))
    res = pl.pallas_call(
        functools.partial(_ret_kernel, T=T, C=C, rope=rope, has_h0=has_h0, has_state=has_state),
        grid=(n_seq,),
        in_specs=in_specs,
        out_specs=out_specs,
        out_shape=out_shape,
        scratch_shapes=[pltpu.VMEM((T, 2 * LANE), BF16), pltpu.VMEM((T, 2 * LANE), BF16),
                        pltpu.VMEM((T, GROUP_W), F32), pltpu.VMEM((RET_HEADS, C, C), F32),
                        pltpu.VMEM((2, 4, C, LANE), F32), pltpu.VMEM((2, 2, LANE, LANE), F32)],
        compiler_params=pltpu.CompilerParams(dimension_semantics=("parallel",), vmem_limit_bytes=VMEM_LIMIT),
        name="retention",
    )(*args)
    return (res[0], res[1]) if has_state else (res[0], None)


def _split3(x):
    x1 = x.astype(BF16)
    r1 = x - x1.astype(F32)
    x2 = r1.astype(BF16)
    x3 = (r1 - x2.astype(F32)).astype(BF16)
    return x1, x2, x3


def _dot01_l(m, x):
    x1, x2, x3 = _split3(x)
    return _dot(m, x1) + _dot(m, x2) + _dot(m, x3)


def _dot01_r(x, m):
    x1, x2, x3 = _split3(x)
    return _dot(x1, m) + _dot(x2, m) + _dot(x3, m)


def _softplus(x):
    return jnp.maximum(x, 0.0) + jnp.log(1.0 + jnp.exp(-jnp.abs(x)))


def _head_expanders():
    e8 = np.zeros((LANE, SSD_HEADS * LANE), np.float32)
    e64 = np.zeros((LANE, GROUP_W), np.float32)
    for h in range(SSD_HEADS):
        e8[h, h * LANE:(h + 1) * LANE] = 1.0
        e64[h, h * SSD_HEAD_DIM:(h + 1) * SSD_HEAD_DIM] = 1.0
    return jnp.asarray(e8, BF16), jnp.asarray(e64, BF16)


def _ssd_kernel(*refs, T, has_h0, has_state):
    it = iter(refs)
    z_ref, xs_ref, bc_ref, dt_ref = next(it), next(it), next(it), next(it)
    wx_ref, wbc_ref, bx_ref, bbc_ref = next(it), next(it), next(it), next(it)
    a_ref, dtb_ref, d_ref, gain_ref, e8_ref, e64_ref = (next(it) for _ in range(6))
    h0_ref = next(it) if has_h0 else None
    y_ref = next(it)
    st_ref = next(it) if has_state else None
    xc_s, bcs_s, oacc_ref, hst_ref = next(it), next(it), next(it), next(it)

    C = SSD_CHUNK
    nc = T // C
    ii = lax.broadcasted_iota(jnp.int32, (C, C), 0)
    jj = lax.broadcasted_iota(jnp.int32, (C, C), 1)
    tril, triu = ii >= jj, ii <= jj
    tril_b, triu_b = tril.astype(BF16), triu.astype(BF16)
    lane = lax.broadcasted_iota(jnp.int32, (C, LANE), 1)
    m_lo = lane < SSD_STATE
    masks = (m_lo, jnp.logical_not(m_lo))
    hrow = lax.broadcasted_iota(jnp.int32, (LANE, LANE), 0) < SSD_STATE

    if has_h0:
        hst_ref[...] = h0_ref[...]
    else:
        hst_ref[...] = jnp.zeros_like(hst_ref)

    def decays(rows, d):
        dtv = _softplus(dt_ref[rows, :] + dtb_ref[d:d + 1, :])
        la = dtv * a_ref[d:d + 1, :]
        return dtv, _dot01_l(tril_b if d == 0 else triu_b, la)

    def conv(ref, w_ref, b_ref, c, r0):
        width = ref.shape[1]
        x = ref[pl.ds(r0, C), :].astype(F32)
        p0 = pl.multiple_of(jnp.maximum(r0 - 16, 0), 16)
        n0 = pl.multiple_of(jnp.minimum(r0 + C, T - 16), 16)
        prev_row = jnp.where(c > 0, ref[pl.ds(p0, 16), :].astype(F32)[15:16], 0.0)
        next_row = jnp.where(c < nc - 1, ref[pl.ds(n0, 16), :].astype(F32)[0:1], 0.0)
        rowi = lax.broadcasted_iota(jnp.int32, (C, width), 0)
        prev = jnp.where(rowi == 0, prev_row, pltpu.roll(x, 1, 0))
        nxt = jnp.where(rowi == C - 1, next_row, pltpu.roll(x, C - 1, 0))
        return _silu(prev * w_ref[0:1, :] + x * w_ref[1:2, :] + nxt * w_ref[2:3, :] + b_ref[...])

    def decay_matrix(qrow, pp):
        return jnp.where(hrow, qrow[:, pp * LANE:(pp + 1) * LANE],
                         qrow[:, (2 + pp) * LANE:(3 + pp) * LANE])

    def forward(c, carry):
        r0 = pl.multiple_of(c * C, C)
        rows = pl.ds(r0, C)
        xc = conv(xs_ref, wx_ref, bx_ref, c, r0)
        bcv = conv(bc_ref, wbc_ref, bbc_ref, c, r0)
        xc_s[rows, :] = xc.astype(BF16)
        bcs_s[rows, :] = bcv.astype(BF16)
        bmat, cmat = bcv[:, :LANE], bcv[:, LANE:]
        bmb = bmat.astype(BF16)
        dtf, bf = decays(rows, 0)
        dtb, bb = decays(rows, 1)
        bf_t, dtf_t, bb_t, dtb_t = bf.T, dtf.T, bb.T, dtb.T
        colf = _dot01_r(bf, e8_ref[...])
        colb = _dot01_r(bb, e8_ref[...])
        qdf = _dot01_r(jnp.exp(bf), e64_ref[...])
        kwf = _dot01_r(dtf * jnp.exp(bf[C - 1:C, :] - bf), e64_ref[...])
        cms = [jnp.where(masks[g], cmat, 0.0).astype(BF16) for g in range(SSD_GROUPS)]
        bms = [jnp.where(masks[g], bmat, 0.0).astype(BF16) for g in range(SSD_GROUPS)]
        scores = [_dot_nt(cms[g], bmb) for g in range(SSD_GROUPS)]
        for pp in range(2):
            hp = hst_ref[0, pp]
            hpb = hp.astype(BF16)
            upd = jnp.zeros((LANE, LANE), F32)
            for g in range(SSD_GROUPS):
                cols = slice((2 * g + pp) * LANE, (2 * g + pp + 1) * LANE)
                xs_pair = xc[:, cols]
                xsb = xs_pair.astype(BF16)
                outs = []
                for e in range(2):
                    h = 4 * g + 2 * pp + e
                    hc = slice(h * LANE, (h + 1) * LANE)
                    ef = jnp.where(tril, jnp.exp(jnp.minimum(colf[:, hc] - bf_t[h:h + 1, :], 0.0)), 0.0) * dtf_t[h:h + 1, :]
                    eb = jnp.where(triu, jnp.exp(jnp.minimum(colb[:, hc] - bb_t[h:h + 1, :], 0.0)), 0.0) * dtb_t[h:h + 1, :]
                    outs.append(_dot((scores[g] * (ef + eb)).astype(BF16), xsb))
                o = jnp.where(m_lo, outs[0], outs[1])
                o += _dot(cms[g], hpb) * qdf[:, cols]
                oacc_ref[rows, cols] = o
                upd += _dot_tn(bms[g], (xs_pair * kwf[:, cols]).astype(BF16))
            hst_ref[0, pp] = hp * decay_matrix(qdf[C - 1:C, :], pp) + upd
        return carry

    lax.fori_loop(0, nc, forward, 0)

    def backward(t, carry):
        c = nc - 1 - t
        r0 = pl.multiple_of(c * C, C)
        rows = pl.ds(r0, C)
        xc = xc_s[rows, :].astype(F32)
        bcv = bcs_s[rows, :]
        bmat, cmat = bcv[:, :LANE], bcv[:, LANE:]
        zero = jnp.zeros_like(bmat)
        dtb, bb = decays(rows, 1)
        qdb = _dot01_r(jnp.exp(bb), e64_ref[...])
        kwb = _dot01_r(dtb * jnp.exp(bb[0:1, :] - bb), e64_ref[...])
        blocks = {}
        for pp in range(2):
            hp = hst_ref[1, pp]
            hpb = hp.astype(BF16)
            upd = jnp.zeros((LANE, LANE), F32)
            for g in range(SSD_GROUPS):
                blk = 2 * g + pp
                cols = slice(blk * LANE, (blk + 1) * LANE)
                xs_pair = xc[:, cols]
                o = oacc_ref[rows, cols] + _dot(jnp.where(masks[g], cmat, zero), hpb) * qdb[:, cols]
                upd += _dot_tn(jnp.where(masks[g], bmat, zero), (xs_pair * kwb[:, cols]).astype(BF16))
                y = (o + d_ref[:, cols] * xs_pair) * _silu(z_ref[rows, cols].astype(F32))
                blocks[blk] = y
            hst_ref[1, pp] = hp * decay_matrix(qdb[0:1, :], pp) + upd
        for g in range(SSD_GROUPS):
            y0, y1 = blocks[2 * g], blocks[2 * g + 1]
            ss = jnp.sum(y0 * y0, axis=-1, keepdims=True) + jnp.sum(y1 * y1, axis=-1, keepdims=True)
            r = lax.rsqrt(ss / (2 * LANE) + EPS)
            for i, yb in enumerate((y0, y1)):
                cols = slice((2 * g + i) * LANE, (2 * g + i + 1) * LANE)
                y_ref[rows, cols] = (yb * r * gain_ref[:, cols]).astype(y_ref.dtype)
        return carry

    lax.fori_loop(0, nc, backward, 0)
    if has_state:
        st_ref[...] = hst_ref[...]


def _ssd_pack_state(st):
    n = st.shape[0]
    st = st.reshape(n, 2, SSD_GROUPS, 2, 2, SSD_STATE, SSD_HEAD_DIM)
    return st.transpose(0, 1, 3, 2, 5, 4, 6).reshape(n, 2, 2, LANE, LANE)


def _ssd_unpack_state(st):
    n = st.shape[0]
    st = st.reshape(n, 2, 2, SSD_GROUPS, SSD_STATE, 2, SSD_HEAD_DIM)
    return st.transpose(0, 1, 3, 2, 5, 4, 6).reshape(n, 2, SSD_HEADS, SSD_STATE, SSD_HEAD_DIM)


def _ssd_params(ssd_conv_w, ssd_conv_b, ssd_a_log, ssd_dt_bias, ssd_d, ssd_norm_g):
    def lanes8(v):
        return jnp.pad(v.astype(F32), ((0, 0), (0, 0), (0, LANE - SSD_HEADS)))
    e8, e64 = _head_expanders()
    return dict(wx=ssd_conv_w[:, :, :GROUP_W], wbc=ssd_conv_w[:, :, GROUP_W:],
                bx=ssd_conv_b[:, None, :GROUP_W], bbc=ssd_conv_b[:, None, GROUP_W:],
                a=lanes8(-jnp.exp(ssd_a_log.astype(F32))), dtb=lanes8(ssd_dt_bias),
                d=jnp.repeat(ssd_d, SSD_HEAD_DIM, axis=-1)[:, None, :], gain=ssd_norm_g[:, None, :],
                e8=e8, e64=e64)


def _ssd_scan(um, us, sp, layer, h0, n_seq, T, row_block0):
    has_h0 = h0 is not None
    has_state = not has_h0

    def col(off, w):
        return pl.BlockSpec((T, w), lambda b: (row_block0 + b, off // w))

    def per_layer(arr):
        return pl.BlockSpec((None,) + arr.shape[1:], lambda b: (layer,) + (0,) * (arr.ndim - 1))

    def const(arr):
        return pl.BlockSpec(arr.shape, lambda b: (0,) * arr.ndim)
    names = ('wx', 'wbc', 'bx', 'bbc', 'a', 'dtb', 'd', 'gain')
    in_specs = [col(MAIN_SSD_Z, GROUP_W), col(MAIN_SSD_XS, GROUP_W), col(MAIN_SSD_BC, 2 * LANE),
                pl.BlockSpec((T, LANE), lambda b: (row_block0 + b, SIDE_DT // LANE))]
    in_specs += [per_layer(sp[n]) for n in names] + [const(sp['e8']), const(sp['e64'])]
    args = [um, um, um, us] + [sp[n] for n in names] + [sp['e8'], sp['e64']]
    st_spec = pl.BlockSpec((None, 2, 2, LANE, LANE), lambda b: (b, 0, 0, 0, 0))
    if has_h0:
        in_specs.append(st_spec)
        args.append(h0)
    out_specs = [pl.BlockSpec((T, GROUP_W), lambda b: (b, 0))]
    out_shape = [jax.ShapeDtypeStruct((n_seq * T, GROUP_W), BF16)]
    if has_state:
        out_specs.append(st_spec)
        out_shape.append(jax.ShapeDtypeStruct((n_seq, 2, 2, LANE, LANE), F32))
    res = pl.pallas_call(
        functools.partial(_ssd_kernel, T=T, has_h0=has_h0, has_state=has_state),
        grid=(n_seq,),
        in_specs=in_specs,
        out_specs=out_specs,
        out_shape=out_shape,
        scratch_shapes=[pltpu.VMEM((T, GROUP_W), BF16), pltpu.VMEM((T, 2 * LANE), BF16),
                        pltpu.VMEM((T, GROUP_W), F32), pltpu.VMEM((2, 2, LANE, LANE), F32)],
        compiler_params=pltpu.CompilerParams(dimension_semantics=("parallel",), vmem_limit_bytes=VMEM_LIMIT),
        name="ssd",
    )(*args)
    return (res[0], res[1]) if has_state else (res[0], None)


GLA_MACRO = 128


def _log_sigmoid(x):
    return jnp.minimum(x, 0.0) - jnp.log(1.0 + jnp.exp(-jnp.abs(x)))


def _gla_kernel(*refs, T, has_h0, has_state):
    it = iter(refs)
    q_ref, k_ref, v_ref, r_ref, g1_ref = (next(it) for _ in range(5))
    wg_ref, bg_ref, gain_ref, ind_ref = (next(it) for _ in range(4))
    h0_ref = next(it) if has_h0 else None
    y_ref = next(it)
    st_ref = next(it) if has_state else None
    oacc_ref, hst_ref, qbuf, kbuf, bfbuf, bbbuf, vbuf, obuf = (next(it) for _ in range(8))

    C, L = GLA_MACRO, GLA_CHUNK
    nb = C // L
    nc = T // C
    ii = lax.broadcasted_iota(jnp.int32, (C, C), 0)
    jj = lax.broadcasted_iota(jnp.int32, (C, C), 1)
    same = (ii // L) == (jj // L)
    tri_l = jnp.logical_and(same, jj <= ii).astype(BF16)
    tri_u = jnp.logical_and(same, jj >= ii).astype(BF16)
    ones_b = same.astype(BF16)
    lane = lax.broadcasted_iota(jnp.int32, (C, LANE), 1)
    masks = (lane < GLA_QK, lane >= GLA_QK)
    HALF = 8
    ri8 = lax.broadcasted_iota(jnp.int32, (HALF, 2 * LANE), 0)
    bdmask = (lax.broadcasted_iota(jnp.int32, (C, nb * LANE), 0) // L
              == lax.broadcasted_iota(jnp.int32, (C, nb * LANE), 1) // LANE)

    if has_h0:
        hst_ref[...] = h0_ref[...]
    else:
        hst_ref[...] = jnp.zeros_like(hst_ref)

    def log_decay(rows, d):
        g1 = g1_ref[rows, :]
        hi = g1.astype(BF16)
        lo = (g1 - hi.astype(F32)).astype(BF16)
        logits = _dot(hi, wg_ref[d]) + _dot(lo, wg_ref[d]) + bg_ref[d]
        return _log_sigmoid(logits) / GLA_GATE_TEMP

    def recurrence(d, rows, qt, kt, dec, blocks):
        zero = jnp.zeros((), BF16)

        def block_diag(x):
            return jnp.where(bdmask, jnp.concatenate([x] * nb, axis=1), zero)
        upds = []
        for h in range(GLA_HEADS):
            p, e = divmod(h, 2)
            km = jnp.where(masks[e], kt[:, p * LANE:(p + 1) * LANE], 0.0).astype(BF16)
            upds.append(_dot_tn(v_ref[rows, h * LANE:(h + 1) * LANE], block_diag(km)))
        snaps = []
        for h in range(GLA_HEADS):
            lanes = slice((h // 2) * LANE, (h // 2 + 1) * LANE)
            ht = hst_ref[d, h]
            snap = [None] * nb
            for blk in blocks:
                snap[blk] = ht.astype(BF16)
                ht = ht * dec[blk * L:blk * L + 1, lanes] + upds[h][:, blk * LANE:(blk + 1) * LANE]
            hst_ref[d, h] = ht
            snaps.append(jnp.concatenate(snap, axis=1))
        outs = []
        for h in range(GLA_HEADS):
            p, e = divmod(h, 2)
            qm = jnp.where(masks[e], qt[:, p * LANE:(p + 1) * LANE], 0.0).astype(BF16)
            outs.append(_dot_nt(block_diag(qm), snaps[h]))
        return outs

    def intra_block(b0):
        pieces = []
        for j in range(L):
            kj, bfj, bbj = kbuf[b0 + j:b0 + j + 1, :], bfbuf[b0 + j:b0 + j + 1, :], bbbuf[b0 + j:b0 + j + 1, :]
            halves = []
            for s in range(L // HALF):
                rs = slice(b0 + s * HALF, b0 + (s + 1) * HALF)
                qk = qbuf[rs, :] * kj
                if j < s * HALF:
                    e = qk * jnp.exp(bfbuf[rs, :] - bfj)
                elif j >= (s + 1) * HALF:
                    e = qk * jnp.exp(bbbuf[rs, :] - bbj)
                else:
                    rel = ri8 + (s * HALF - j)
                    e = qk * jnp.exp(jnp.where(rel >= 0, bfbuf[rs, :] - bfj, bbbuf[rs, :] - bbj))
                    e = jnp.where(rel == 0, 2.0 * e, e)
                halves.append(e)
            pieces.append(jnp.concatenate(halves, axis=0).astype(BF16))
        spread = _dot(jnp.concatenate(pieces, axis=0), ind_ref[...])
        acc = spread[0:L, :] * vbuf[b0:b0 + 1, :]
        for j in range(1, L):
            acc += spread[j * L:(j + 1) * L, :] * vbuf[b0 + j:b0 + j + 1, :]
        return acc

    def forward(c, carry):
        r0 = pl.multiple_of(c * C, C)
        rows = pl.ds(r0, C)
        q = q_ref[rows, :].astype(F32) * (GLA_QK ** -0.5)
        k = k_ref[rows, :].astype(F32)
        la_f, la_b = log_decay(rows, 0), log_decay(rows, 1)
        bf, tot_f = _dot01_l(tri_l, la_f), _dot01_l(ones_b, la_f)
        bb = _dot01_l(tri_u, la_b)
        qbuf[...] = q
        kbuf[...] = k
        bfbuf[...] = bf
        bbbuf[...] = bb
        vbuf[...] = v_ref[rows, :].astype(F32)
        for blk in range(nb):
            obuf[blk * L:(blk + 1) * L, :] = intra_block(blk * L)
        qt, kt, dec = q * jnp.exp(bf), k * jnp.exp(tot_f - bf), jnp.exp(tot_f)
        inter = recurrence(0, rows, qt, kt, dec, range(nb))
        for h in range(GLA_HEADS):
            hcols = slice(h * LANE, (h + 1) * LANE)
            oacc_ref[rows, hcols] = obuf[:, hcols] + inter[h]
        return carry

    lax.fori_loop(0, nc, forward, 0)

    def backward(t, carry):
        c = nc - 1 - t
        r0 = pl.multiple_of(c * C, C)
        rows = pl.ds(r0, C)
        q = q_ref[rows, :].astype(F32) * (GLA_QK ** -0.5)
        k = k_ref[rows, :].astype(F32)
        la_b = log_decay(rows, 1)
        bb, tot_b = _dot01_l(tri_u, la_b), _dot01_l(ones_b, la_b)
        qt, kt, dec = q * jnp.exp(bb), k * jnp.exp(tot_b - bb), jnp.exp(tot_b)
        inter = recurrence(1, rows, qt, kt, dec, range(nb - 1, -1, -1))
        for h in range(GLA_HEADS):
            hcols = slice(h * LANE, (h + 1) * LANE)
            o = oacc_ref[rows, hcols] + inter[h]
            yn = o * lax.rsqrt(jnp.mean(o * o, axis=-1, keepdims=True) + EPS) * gain_ref[:, hcols]
            y_ref[rows, hcols] = (_silu(r_ref[rows, hcols].astype(F32)) * yn).astype(y_ref.dtype)
        return carry

    lax.fori_loop(0, nc, backward, 0)
    if has_state:
        st_ref[...] = hst_ref[...]


def _gla_pack_state(st):
    n = st.shape[0]
    ht = jnp.swapaxes(st, -1, -2)
    z = jnp.zeros_like(ht)
    even = jnp.concatenate([ht, z], axis=-1)
    odd = jnp.concatenate([z, ht], axis=-1)
    sel = (jnp.arange(GLA_HEADS) % 2 == 0)[None, None, :, None, None]
    return jnp.where(sel, even, odd)


def _gla_unpack_state(st):
    even, odd = st[..., :GLA_QK], st[..., GLA_QK:]
    sel = (jnp.arange(GLA_HEADS) % 2 == 0)[None, None, :, None, None]
    return jnp.swapaxes(jnp.where(sel, even, odd), -1, -2)


def _gla_params(gla_w_g2, gla_b_g, gla_norm_g):
    wg = jnp.zeros((DEPTH, 2, LANE, 2 * LANE), F32)
    g1_lane = SIDE_G1 - SIDE_DT
    wg = wg.at[:, :, g1_lane:g1_lane + GLA_GATE_RANK, :].set(gla_w_g2)
    ind = np.zeros((2 * LANE, GROUP_W), np.float32)
    for h in range(GLA_HEADS):
        ind[h * GLA_QK:(h + 1) * GLA_QK, h * GLA_V:(h + 1) * GLA_V] = 1.0
    return dict(wg=wg.astype(BF16), bg=gla_b_g[:, :, None, :], gain=gla_norm_g[:, None, :],
                ind=jnp.asarray(ind, BF16))


def _gla_scan(um, us, gp, layer, h0, n_seq, T, row_block0):
    has_h0 = h0 is not None
    has_state = not has_h0
    C = GLA_MACRO

    def col(off, w):
        return pl.BlockSpec((T, w), lambda b: (row_block0 + b, off // w))

    def per_layer(arr):
        return pl.BlockSpec((None,) + arr.shape[1:], lambda b: (layer,) + (0,) * (arr.ndim - 1))
    in_specs = [col(MAIN_GLA_Q, 2 * LANE), col(MAIN_GLA_K, 2 * LANE), col(MAIN_GLA_V, GROUP_W), col(MAIN_GLA_R, GROUP_W),
                pl.BlockSpec((T, LANE), lambda b: (row_block0 + b, SIDE_DT // LANE)),
                per_layer(gp['wg']), per_layer(gp['bg']), per_layer(gp['gain']),
                pl.BlockSpec(gp['ind'].shape, lambda b: (0, 0))]
    args = [um, um, um, um, us, gp['wg'], gp['bg'], gp['gain'], gp['ind']]
    st_spec = pl.BlockSpec((None, 2, GLA_HEADS, LANE, LANE), lambda b: (b, 0, 0, 0, 0))
    if has_h0:
        in_specs.append(st_spec)
        args.append(h0)
    out_specs = [pl.BlockSpec((T, GROUP_W), lambda b: (b, 0))]
    out_shape = [jax.ShapeDtypeStruct((n_seq * T, GROUP_W), BF16)]
    if has_state:
        out_specs.append(st_spec)
        out_shape.append(jax.ShapeDtypeStruct((n_seq, 2, GLA_HEADS, LANE, LANE), F32))
    res = pl.pallas_call(
        functools.partial(_gla_kernel, T=T, has_h0=has_h0, has_state=has_state),
        grid=(n_seq,),
        in_specs=in_specs,
        out_specs=out_specs,
        out_shape=out_shape,
        scratch_shapes=[pltpu.VMEM((T, GROUP_W), F32), pltpu.VMEM((2, GLA_HEADS, LANE, LANE), F32),
                        pltpu.VMEM((C, 2 * LANE), F32), pltpu.VMEM((C, 2 * LANE), F32),
                        pltpu.VMEM((C, 2 * LANE), F32), pltpu.VMEM((C, 2 * LANE), F32),
                        pltpu.VMEM((C, GROUP_W), F32), pltpu.VMEM((C, GROUP_W), F32)],
        compiler_params=pltpu.CompilerParams(dimension_semantics=("parallel",), vmem_limit_bytes=VMEM_LIMIT),
        name="gla",
    )(*args)
    return (res[0], res[1]) if has_state else (res[0], None)


def kernel(x_prompt, x_sample, cache_mla_kv, state_ssd, state_ret, state_gla, c, c_ctx, w_mod, b_mod, norm_ffn1, ffn1_wg, ffn1_wu, ffn1_wd, norm_mix, w_in, ssd_conv_w, ssd_conv_b, ssd_a_log, ssd_dt_bias, ssd_d, ssd_norm_g, mla_q_lat_gain, mla_w_q_up, mla_q_gain, mla_kv_lat_gain, mla_w_kv_up, mla_k_gain, ret_decay_logit, ret_norm_g, gla_w_g2, gla_b_g, gla_norm_g, w_out, norm_ffn2, ffn2_wg, ffn2_wu, ffn2_wd):
    sp = _ssd_params(ssd_conv_w, ssd_conv_b, ssd_a_log, ssd_dt_bias, ssd_d, ssd_norm_g)
    ssd_h0 = jnp.stack([_ssd_pack_state(state_ssd[:, l]) for l in range(DEPTH)], axis=1)
    gp = _gla_params(gla_w_g2, gla_b_g, gla_norm_g)
    gla_h0 = jnp.stack([_gla_pack_state(state_gla[:, l]) for l in range(DEPTH)], axis=1)

    main_idx, side_idx = _in_proj_columns()
    w_main = _gather_columns(w_in, main_idx).astype(BF16)
    w_side = _gather_columns(w_in, side_idx).astype(BF16)
    w_out_b = w_out.astype(BF16)
    f1 = (ffn1_wg.astype(BF16), ffn1_wu.astype(BF16), ffn1_wd.astype(BF16))
    f2 = (ffn2_wg.astype(BF16), ffn2_wu.astype(BF16), ffn2_wd.astype(BF16))
    g_ffn1 = norm_ffn1.reshape(DEPTH, 1, D_MODEL)
    g_mix = norm_mix.reshape(DEPTH, 1, D_MODEL)
    g_ffn2 = norm_ffn2.reshape(DEPTH, 1, D_MODEL)
    mw = _mla_weights(mla_q_lat_gain, mla_w_q_up, mla_q_gain, mla_kv_lat_gain, mla_w_kv_up, mla_k_gain)
    cos_lat, sin_lat = _rope_tables(DEC_SEQ)
    cos_all = jnp.concatenate([jnp.ones((TOK_TM, LANE), F32), cos_lat], axis=0)
    sin_all = jnp.concatenate([jnp.zeros((TOK_TM, LANE), F32), sin_lat], axis=0)
    ret_lg = jax.nn.log_sigmoid(ret_decay_logit.astype(F32))
    ret_gain = ret_norm_g.reshape(DEPTH, 1, GROUP_W)
    ret_h0 = state_ret.reshape(DEC_BATCH, DEPTH, 2, 2, LANE, LANE)

    c_all = jnp.zeros((MOD_ROWS, D_MODEL), F32).at[0].set(c_ctx).at[1:1 + DEC_BATCH].set(c)
    mod = _modulation(c_all, w_mod, b_mod).reshape(DEPTH * MOD_ROWS, 1, N_MOD * D_MODEL)

    x = jnp.concatenate([x_prompt.reshape(CTX_ROWS, D_MODEL), x_sample.reshape(LAT_ROWS, D_MODEL)], axis=0)
    kv_list, ssd_list, ret_list, gla_list = [], [], [], []
    for l in range(DEPTH):
        x = _ffn(x, mod, g_ffn1, *f1, l, 0)
        um, us = _in_proj(x, mod, g_mix, w_main, w_side, l)

        y_mla, kv_lat = _mla_layer(um, us, cache_mla_kv[:, l], mw, cos_all, sin_all, l)
        yr_c, s_ret = _retention(um, ret_lg[l], ret_gain[l], None, None, BATCH, SEQ, 0)
        yr_l, _ = _retention(um, ret_lg[l], ret_gain[l], ret_h0[:, l], (cos_lat, sin_lat),
                             DEC_BATCH, DEC_SEQ, CTX_ROWS // DEC_SEQ)
        y_ret = jnp.concatenate([yr_c, yr_l], axis=0)

        ys_c, s_ssd = _ssd_scan(um, us, sp, l, None, BATCH, SEQ, 0)
        ys_l, _ = _ssd_scan(um, us, sp, l, ssd_h0[:, l], DEC_BATCH, DEC_SEQ, CTX_ROWS // DEC_SEQ)
        y_ssd = jnp.concatenate([ys_c, ys_l], axis=0)

        yg_c, s_gla = _gla_scan(um, us, gp, l, None, BATCH, SEQ, 0)
        yg_l, _ = _gla_scan(um, us, gp, l, gla_h0[:, l], DEC_BATCH, DEC_SEQ, CTX_ROWS // DEC_SEQ)
        y_gla = jnp.concatenate([yg_c, yg_l], axis=0)

        kv_list.append(kv_lat[:CTX_ROWS, :MLA_CACHE_W].reshape(BATCH, SEQ, MLA_CACHE_W))
        ssd_list.append(_ssd_unpack_state(s_ssd))
        ret_list.append(s_ret.reshape(BATCH, 2, RET_HEADS, RET_QK, RET_V))
        gla_list.append(_gla_unpack_state(s_gla))
        x = _out_proj(x, [y_ssd, y_mla, y_ret, y_gla], mod, w_out_b, l)
        x = _ffn(x, mod, g_ffn2, *f2, l, 6)
    y_p = x[:CTX_ROWS].reshape(BATCH, SEQ, D_MODEL)
    y_s = x[CTX_ROWS:].reshape(DEC_BATCH, DEC_SEQ, D_MODEL)
    return (y_p, y_s, jnp.stack(kv_list, axis=1), jnp.stack(ssd_list, axis=1),
            jnp.stack(ret_list, axis=1), jnp.stack(gla_list, axis=1))
```

```python
import functools

import jax
import jax.numpy as jnp
import numpy as np
from jax import lax
from jax.experimental import pallas as pl
from jax.experimental.pallas import tpu as pltpu

F32 = jnp.float32
BF16 = jnp.bfloat16

D_MODEL = 2048
BATCH = 32
SEQ = 256
DEPTH = 4
DEC_BATCH = 4
DEC_SEQ = 4096
PAST_LEN = 256
GRID_W = 64
ROPE_BASE = 10000.0
EPS = 1e-6
D_FF = 5632
N_MOD = 9
GROUP_W = D_MODEL // 4
ATTN_BLOCK = 128

SSD_HEAD_DIM = 64
SSD_HEADS = GROUP_W // SSD_HEAD_DIM
SSD_STATE = 64
SSD_GROUPS = 2
SSD_CONV_K = 3
SSD_CHUNK = 128
SSD_CONV_CH = GROUP_W + 2 * SSD_GROUPS * SSD_STATE
SSD_IN = GROUP_W + SSD_CONV_CH + SSD_HEADS

MLA_HEADS = 4
MLA_NOPE = 128
MLA_ROPE = 64
MLA_V = GROUP_W // MLA_HEADS
MLA_Q_RANK = 384
MLA_KV_RANK = 128
MLA_QK = MLA_NOPE + MLA_ROPE
MLA_IN = MLA_Q_RANK + MLA_KV_RANK + MLA_ROPE
MLA_CACHE_W = MLA_KV_RANK + MLA_ROPE

RET_HEADS = 4
RET_QK = 64
RET_V = GROUP_W // RET_HEADS
RET_CHUNK = 128
RET_IN = 2 * RET_HEADS * RET_QK + 2 * GROUP_W

GLA_HEADS = 4
GLA_QK = 64
GLA_V = GROUP_W // GLA_HEADS
GLA_GATE_RANK = 16
GLA_GATE_TEMP = 16.0
GLA_CHUNK = 16
GLA_IN = 2 * GLA_HEADS * GLA_QK + GROUP_W + GLA_GATE_RANK + GROUP_W

IN_W = SSD_IN + MLA_IN + RET_IN + GLA_IN

LANE = 128
CTX_ROWS = BATCH * SEQ
LAT_ROWS = DEC_BATCH * DEC_SEQ
ROWS = CTX_ROWS + LAT_ROWS
MOD_ROWS = 8
VMEM_LIMIT = 56 * 1024 * 1024
TOK_TM = 512
MLA_HEAD_W = 2 * LANE

MAIN_SSD_Z = 0
MAIN_SSD_XS = 512
MAIN_RET_V = 1024
MAIN_RET_G = 1536
MAIN_GLA_V = 2048
MAIN_GLA_R = 2560
MAIN_SSD_BC = 3072
MAIN_RET_Q = 3328
MAIN_RET_K = 3584
MAIN_GLA_Q = 3840
MAIN_GLA_K = 4096
MAIN_MLA_CQ = 4352
MAIN_W = 4864
MAIN_TN = MAIN_W // 2
SIDE_CKV = 0
SIDE_KPE = SIDE_CKV + MLA_KV_RANK
SIDE_DT = 2 * LANE
SIDE_G1 = SIDE_DT + SSD_HEADS
SIDE_W = 3 * LANE


def _in_proj_columns():
    o_ssd, o_mla, o_ret, o_gla = 0, SSD_IN, SSD_IN + MLA_IN, SSD_IN + MLA_IN + RET_IN
    main = np.full((MAIN_W,), -1, np.int64)

    def put(dst, src, n):
        main[dst:dst + n] = src + np.arange(n)
    qk = RET_HEADS * RET_QK
    put(MAIN_SSD_Z, o_ssd, GROUP_W)
    put(MAIN_SSD_XS, o_ssd + GROUP_W, GROUP_W)
    put(MAIN_SSD_BC, o_ssd + 2 * GROUP_W, 2 * SSD_GROUPS * SSD_STATE)
    put(MAIN_MLA_CQ, o_mla, MLA_Q_RANK)
    put(MAIN_RET_Q, o_ret, qk)
    put(MAIN_RET_K, o_ret + qk, qk)
    put(MAIN_RET_V, o_ret + 2 * qk, GROUP_W)
    put(MAIN_RET_G, o_ret + 2 * qk + GROUP_W, GROUP_W)
    put(MAIN_GLA_Q, o_gla, qk)
    put(MAIN_GLA_K, o_gla + qk, qk)
    put(MAIN_GLA_V, o_gla + 2 * qk, GROUP_W)
    put(MAIN_GLA_R, o_gla + 2 * qk + GROUP_W + GLA_GATE_RANK, GROUP_W)
    side = np.full((SIDE_W,), -1, np.int64)
    side[SIDE_CKV:SIDE_CKV + MLA_KV_RANK + MLA_ROPE] = o_mla + MLA_Q_RANK + np.arange(MLA_KV_RANK + MLA_ROPE)
    side[SIDE_DT:SIDE_DT + SSD_HEADS] = o_ssd + GROUP_W + SSD_CONV_CH + np.arange(SSD_HEADS)
    side[SIDE_G1:SIDE_G1 + GLA_GATE_RANK] = o_gla + 2 * qk + GROUP_W + np.arange(GLA_GATE_RANK)
    return main, side


def _gather_columns(w, idx):
    safe = np.where(idx < 0, 0, idx)
    out = jnp.take(w, jnp.asarray(safe, jnp.int32), axis=-1)
    return jnp.where(jnp.asarray(idx >= 0), out, 0.0)


def _column_tiles(w, tn):
    n_layers, k, n = w.shape
    return w.astype(BF16).reshape(n_layers, k, n // tn, tn).transpose(0, 2, 1, 3)


def _mod_row(i, tm):
    ctx_tiles = CTX_ROWS // tm
    per_seq = DEC_SEQ // tm
    return jnp.where(i < ctx_tiles, 0, 1 + (i - ctx_tiles) // per_seq)


def _mod_spec(layer, which, tm, n_grid):
    if n_grid == 1:
        return pl.BlockSpec((1, 1, D_MODEL), lambda i: (layer * MOD_ROWS + _mod_row(i, tm), 0, which))
    return pl.BlockSpec((1, 1, D_MODEL), lambda i, j: (layer * MOD_ROWS + _mod_row(i, tm), 0, which))


def _dot(a, b):
    return jnp.dot(a, b, preferred_element_type=F32)


def _dot_nt(a, b):
    return lax.dot_general(a, b, (((1,), (1,)), ((), ())), preferred_element_type=F32)


def _dot_tn(a, b):
    return lax.dot_general(a, b, (((0,), (0,)), ((), ())), preferred_element_type=F32)


def _silu(x):
    return x * jax.nn.sigmoid(x)


def _mod_kernel(c_ref, w_ref, b_ref, o_ref):
    s = _silu(c_ref[...])
    hi = s.astype(BF16)
    lo = (s - hi.astype(F32)).astype(BF16)
    w = w_ref[...].astype(BF16)
    o_ref[...] = _dot(hi, w) + _dot(lo, w) + b_ref[...]


def _modulation(c_all, w_mod, b_mod):
    tn = 1024
    n = N_MOD * D_MODEL
    return pl.pallas_call(
        _mod_kernel,
        grid=(DEPTH, n // tn),
        in_specs=[pl.BlockSpec((MOD_ROWS, D_MODEL), lambda l, j: (0, 0)),
                  pl.BlockSpec((None, D_MODEL, tn), lambda l, j: (l, 0, j)),
                  pl.BlockSpec((None, 1, tn), lambda l, j: (l, 0, j))],
        out_specs=pl.BlockSpec((None, MOD_ROWS, tn), lambda l, j: (l, 0, j)),
        out_shape=jax.ShapeDtypeStruct((DEPTH, MOD_ROWS, n), F32),
        compiler_params=pltpu.CompilerParams(dimension_semantics=("arbitrary", "arbitrary"),
                                             vmem_limit_bytes=VMEM_LIMIT),
        name="modulation",
    )(c_all, w_mod, b_mod.reshape(DEPTH, 1, n))


def _norm_modulate(x, g, shift, scale):
    r = lax.rsqrt(jnp.mean(x * x, axis=-1, keepdims=True) + EPS)
    return (x * r * g) * (1.0 + scale) + shift


FFN_TF = 512
FFN_TN = 512
FFN_NF = D_FF // FFN_TF
FFN_NN = D_MODEL // FFN_TN
MXU_W = 256


def _ffn_kernel(x_ref, xt_ref, g_ref, shift_ref, scale_ref, gate_ref, wg_ref, wu_ref, wd_ref, o_ref, h_ref, a_ref):
    j = pl.program_id(1)

    @pl.when(j == 0)
    def _():
        h_ref[...] = _norm_modulate(x_ref[...], g_ref[...], shift_ref[0], scale_ref[0]).astype(BF16)

    @pl.when(j < FFN_NF)
    def _():
        h = h_ref[...]
        base = pl.multiple_of(j * FFN_TF, FFN_TF)
        for s in range(FFN_TF // MXU_W):
            cols = slice(s * MXU_W, (s + 1) * MXU_W)
            g = _dot(h, wg_ref[:, cols])
            u = _dot(h, wu_ref[:, cols])
            a_ref[:, pl.ds(base + s * MXU_W, MXU_W)] = (_silu(g) * u).astype(BF16)

    @pl.when(j >= FFN_NF)
    def _():
        o_ref[...] = xt_ref[...] + 0.5 * gate_ref[0] * _dot(a_ref[...], wd_ref[...])


def _ffn(x, mod, norm_g, wg, wu, wd, layer, mod_base, tm=TOK_TM):
    def up_tile(i, j):
        return (layer, jnp.minimum(j, FFN_NF - 1), 0, 0)

    def down_tile(j):
        return jnp.maximum(j - FFN_NF, 0)
    gate_blocks = D_MODEL // FFN_TN
    return pl.pallas_call(
        _ffn_kernel,
        grid=(ROWS // tm, FFN_NF + FFN_NN),
        in_specs=[pl.BlockSpec((tm, D_MODEL), lambda i, j: (i, 0)),
                  pl.BlockSpec((tm, FFN_TN), lambda i, j: (i, down_tile(j))),
                  pl.BlockSpec((None, 1, D_MODEL), lambda i, j: (layer, 0, 0)),
                  _mod_spec(layer, mod_base + 0, tm, 2),
                  _mod_spec(layer, mod_base + 1, tm, 2),
                  pl.BlockSpec((1, 1, FFN_TN), lambda i, j: (layer * MOD_ROWS + _mod_row(i, tm), 0,
                                                             (mod_base + 2) * gate_blocks + down_tile(j))),
                  pl.BlockSpec((None, None, D_MODEL, FFN_TF), up_tile),
                  pl.BlockSpec((None, None, D_MODEL, FFN_TF), up_tile),
                  pl.BlockSpec((None, None, D_FF, FFN_TN), lambda i, j: (layer, down_tile(j), 0, 0))],
        out_specs=pl.BlockSpec((tm, FFN_TN), lambda i, j: (i, down_tile(j))),
        out_shape=jax.ShapeDtypeStruct((ROWS, D_MODEL), F32),
        scratch_shapes=[pltpu.VMEM((tm, D_MODEL), BF16), pltpu.VMEM((tm, D_FF), BF16)],
        compiler_params=pltpu.CompilerParams(dimension_semantics=("parallel", "arbitrary"),
                                             vmem_limit_bytes=VMEM_LIMIT),
        name="ffn",
    )(x, x, norm_g, mod, mod, mod, wg, wu, wd)


def _in_proj_kernel(x_ref, g_ref, shift_ref, scale_ref, w_ref, ws_ref, o_ref, os_ref, h_ref):
    @pl.when(pl.program_id(1) == 0)
    def _():
        h_ref[...] = _norm_modulate(x_ref[...], g_ref[...], shift_ref[0], scale_ref[0]).astype(BF16)
        os_ref[...] = _dot(h_ref[...], ws_ref[...])

    o_ref[...] = _dot(h_ref[...], w_ref[...]).astype(o_ref.dtype)


def _in_proj(x, mod, norm_g, w_main, w_side, layer, tm=TOK_TM):
    return pl.pallas_call(
        _in_proj_kernel,
        grid=(ROWS // tm, MAIN_W // MAIN_TN),
        in_specs=[pl.BlockSpec((tm, D_MODEL), lambda i, j: (i, 0)),
                  pl.BlockSpec((None, 1, D_MODEL), lambda i, j: (layer, 0, 0)),
                  _mod_spec(layer, 3, tm, 2),
                  _mod_spec(layer, 4, tm, 2),
                  pl.BlockSpec((None, None, D_MODEL, MAIN_TN), lambda i, j: (layer, j, 0, 0)),
                  pl.BlockSpec((None, D_MODEL, SIDE_W), lambda i, j: (layer, 0, 0))],
        out_specs=[pl.BlockSpec((tm, MAIN_TN), lambda i, j: (i, j)),
                   pl.BlockSpec((tm, SIDE_W), lambda i, j: (i, 0))],
        out_shape=[jax.ShapeDtypeStruct((ROWS, MAIN_W), BF16),
                   jax.ShapeDtypeStruct((ROWS, SIDE_W), F32)],
        scratch_shapes=[pltpu.VMEM((tm, D_MODEL), BF16)],
        compiler_params=pltpu.CompilerParams(dimension_semantics=("parallel", "arbitrary"),
                                             vmem_limit_bytes=VMEM_LIMIT),
        name="in_proj",
    )(x, norm_g, mod, mod, w_main, w_side)


def _out_proj_kernel(x_ref, *refs, ctx_tiles):
    ctx_refs, lat_refs, (gate_ref, w_ref, o_ref) = refs[0:4], refs[4:8], refs[8:]

    def run(y_refs):
        acc = _dot(y_refs[0][...], w_ref[0:GROUP_W, :])
        for g in range(1, 4):
            acc += _dot(y_refs[g][...], w_ref[g * GROUP_W:(g + 1) * GROUP_W, :])
        o_ref[...] = x_ref[...] + gate_ref[0] * acc

    @pl.when(pl.program_id(0) < ctx_tiles)
    def _():
        run(ctx_refs)

    @pl.when(pl.program_id(0) >= ctx_tiles)
    def _():
        run(lat_refs)


def _out_proj(x, ys_ctx, ys_lat, mod, w, layer, tm=TOK_TM):
    ctx_tiles = CTX_ROWS // tm
    row = pl.BlockSpec((tm, D_MODEL), lambda i: (i, 0))
    cspec = pl.BlockSpec((tm, GROUP_W), lambda i: (jnp.minimum(i, ctx_tiles - 1), 0))
    lspec = pl.BlockSpec((tm, GROUP_W), lambda i: (jnp.maximum(i - ctx_tiles, 0), 0))
    return pl.pallas_call(
        functools.partial(_out_proj_kernel, ctx_tiles=ctx_tiles),
        grid=(ROWS // tm,),
        in_specs=[row] + [cspec] * 4 + [lspec] * 4 + [
            _mod_spec(layer, 5, tm, 1),
            pl.BlockSpec((None, D_MODEL, D_MODEL), lambda i: (layer, 0, 0))],
        out_specs=row,
        out_shape=jax.ShapeDtypeStruct((ROWS, D_MODEL), F32),
        compiler_params=pltpu.CompilerParams(dimension_semantics=("parallel",),
                                             vmem_limit_bytes=VMEM_LIMIT),
        name="out_proj",
    )(x, *ys_ctx, *ys_lat, mod, w)


def _rope_tables(T):
    n_rows = T // GRID_W
    row = jnp.repeat(jnp.arange(n_rows, dtype=F32), GRID_W)
    col = jnp.tile(jnp.arange(GRID_W, dtype=F32), n_rows)
    d_axis = MLA_ROPE // 2
    inv = ROPE_BASE ** (-jnp.arange(0, d_axis, 2, dtype=F32) / d_axis)
    ar, ac = row[:, None] * inv, col[:, None] * inv
    cos = jnp.concatenate([jnp.cos(ar), jnp.cos(ar), jnp.cos(ac), jnp.cos(ac)], axis=-1)
    sin = jnp.concatenate([-jnp.sin(ar), jnp.sin(ar), -jnp.sin(ac), jnp.sin(ac)], axis=-1)
    return jnp.tile(cos, (1, 2)), jnp.tile(sin, (1, 2))


def _swap16(x):
    lane = lax.broadcasted_iota(jnp.int32, x.shape, 1)
    up = pltpu.roll(x, LANE - 16, 1)
    down = pltpu.roll(x, 16, 1)
    return jnp.where((lane % 32) < 16, up, down)


def _rope(x, cos, sin):
    return x * cos + _swap16(x) * sin


def _mla_q_kernel(c0_ref, c1_ref, c2_ref, gl_ref, w_ref, gq_ref, cos_ref, sin_ref, o_ref):
    cs = [r[...].astype(F32) for r in (c0_ref, c1_ref, c2_ref)]
    ss = sum(jnp.sum(c * c, axis=-1, keepdims=True) for c in cs)
    r = lax.rsqrt(ss / MLA_Q_RANK + EPS)
    q = sum(_dot((cs[i] * r * gl_ref[:, i * LANE:(i + 1) * LANE]).astype(BF16),
                 w_ref[i * LANE:(i + 1) * LANE, :]) for i in range(3))
    cos, sin = cos_ref[...], sin_ref[...]
    scale = MLA_QK ** -0.5
    for h in range(MLA_HEADS):
        a = q[:, h * MLA_HEAD_W:h * MLA_HEAD_W + LANE]
        b = q[:, h * MLA_HEAD_W + LANE:(h + 1) * MLA_HEAD_W]
        ssq = jnp.sum(a * a, axis=-1, keepdims=True) + jnp.sum(b * b, axis=-1, keepdims=True)
        rh = lax.rsqrt(ssq / MLA_QK + EPS) * scale
        o_ref[:, h * MLA_HEAD_W:h * MLA_HEAD_W + LANE] = (a * rh * gq_ref[:, :LANE]).astype(BF16)
        o_ref[:, h * MLA_HEAD_W + LANE:(h + 1) * MLA_HEAD_W] = _rope(b * rh * gq_ref[:, LANE:], cos, sin).astype(BF16)


def _rope_tile_index(i, tm):
    ctx_tiles = CTX_ROWS // tm
    return jnp.where(i < ctx_tiles, 0, 1 + (i - ctx_tiles) % (DEC_SEQ // tm))


def _mla_q(um, gl, wq, gq, cos_all, sin_all, layer, tm=TOK_TM):
    cq = MAIN_MLA_CQ // LANE
    tab = pl.BlockSpec((tm, LANE), lambda i: (_rope_tile_index(i, tm), 0))
    return pl.pallas_call(
        _mla_q_kernel,
        grid=(ROWS // tm,),
        in_specs=[pl.BlockSpec((tm, LANE), lambda i: (i, cq)),
                  pl.BlockSpec((tm, LANE), lambda i: (i, cq + 1)),
                  pl.BlockSpec((tm, LANE), lambda i: (i, cq + 2)),
                  pl.BlockSpec((None, 1, MLA_Q_RANK), lambda i: (layer, 0, 0)),
                  pl.BlockSpec((None, MLA_Q_RANK, MLA_HEADS * MLA_HEAD_W), lambda i: (layer, 0, 0)),
                  pl.BlockSpec((None, 1, MLA_HEAD_W), lambda i: (layer, 0, 0)),
                  tab, tab],
        out_specs=pl.BlockSpec((tm, MLA_HEADS * MLA_HEAD_W), lambda i: (i, 0)),
        out_shape=jax.ShapeDtypeStruct((ROWS, MLA_HEADS * MLA_HEAD_W), BF16),
        compiler_params=pltpu.CompilerParams(dimension_semantics=("parallel",), vmem_limit_bytes=VMEM_LIMIT),
        name="mla_q",
    )(um, um, um, gl, wq, gq, cos_all, sin_all)


def _mla_kv_kernel(s_ref, gl_ref, w_ref, gk_ref, cos_ref, sin_ref, lat_ref, k_ref, v_ref, *, normalize):
    ckv = s_ref[:, :LANE]
    kpe = s_ref[:, LANE:]
    if normalize:
        ckv = ckv * lax.rsqrt(jnp.mean(ckv * ckv, axis=-1, keepdims=True) + EPS) * gl_ref[...]
    lat_ref[:, :LANE] = ckv
    lat_ref[:, LANE:] = kpe
    kv = _dot(ckv.astype(BF16), w_ref[...])
    ss_pe = jnp.sum(kpe * kpe, axis=-1, keepdims=True)
    cos, sin = cos_ref[...], sin_ref[...]
    for h in range(MLA_HEADS):
        a = kv[:, h * LANE:(h + 1) * LANE]
        rh = lax.rsqrt((jnp.sum(a * a, axis=-1, keepdims=True) + ss_pe) / MLA_QK + EPS)
        k_ref[:, h * MLA_HEAD_W:h * MLA_HEAD_W + LANE] = (a * rh * gk_ref[:, :LANE]).astype(BF16)
        k_ref[:, h * MLA_HEAD_W + LANE:(h + 1) * MLA_HEAD_W] = _rope(kpe * rh * gk_ref[:, LANE:], cos, sin).astype(BF16)
    v_ref[...] = kv[:, MLA_HEADS * LANE:].astype(BF16)


def _mla_kv(src, gl, wkv, gk, cos_all, sin_all, layer, *, normalize, tab_index, tm=TOK_TM):
    rows = src.shape[0]
    tab = pl.BlockSpec((tm, LANE), lambda i: (tab_index(i, tm), 0))
    return pl.pallas_call(
        functools.partial(_mla_kv_kernel, normalize=normalize),
        grid=(rows // tm,),
        in_specs=[pl.BlockSpec((tm, 2 * LANE), lambda i: (i, 0)),
                  pl.BlockSpec((None, 1, MLA_KV_RANK), lambda i: (layer, 0, 0)),
                  pl.BlockSpec((None, MLA_KV_RANK, 2 * MLA_HEADS * LANE), lambda i: (layer, 0, 0)),
                  pl.BlockSpec((None, 1, MLA_HEAD_W), lambda i: (layer, 0, 0)),
                  tab, tab],
        out_specs=[pl.BlockSpec((tm, 2 * LANE), lambda i: (i, 0)),
                   pl.BlockSpec((tm, MLA_HEADS * MLA_HEAD_W), lambda i: (i, 0)),
                   pl.BlockSpec((tm, GROUP_W), lambda i: (i, 0))],
        out_shape=[jax.ShapeDtypeStruct((rows, 2 * LANE), F32),
                   jax.ShapeDtypeStruct((rows, MLA_HEADS * MLA_HEAD_W), BF16),
                   jax.ShapeDtypeStruct((rows, GROUP_W), BF16)],
        compiler_params=pltpu.CompilerParams(dimension_semantics=("parallel",), vmem_limit_bytes=VMEM_LIMIT),
        name="mla_kv",
    )(src, gl, wkv, gk, cos_all, sin_all)


def _attn_kernel(q_ref, *refs):
    o_ref = refs[-1]
    kv = [(refs[i], refs[i + 1]) for i in range(0, len(refs) - 1, 2)]
    q = q_ref[...]
    scores = [_dot_nt(q, k_ref[...]) for k_ref, _ in kv]
    m = functools.reduce(jnp.maximum, [jnp.max(s, axis=-1, keepdims=True) for s in scores])
    ps = [jnp.exp(s - m) for s in scores]
    l = sum(jnp.sum(p, axis=-1, keepdims=True) for p in ps)
    o = sum(_dot(p.astype(BF16), v_ref[...]) for p, (_, v_ref) in zip(ps, kv))
    o_ref[...] = (o / l).astype(o_ref.dtype)


def _attention(q, sources, n_seq, t, tq, q_row0):
    nq = t // tq
    in_specs = [pl.BlockSpec((tq, MLA_HEAD_W), lambda b, h, i: (q_row0 // tq + b * nq + i, h))]
    args = [q]
    for k, v, s_len, row0 in sources:
        in_specs += [pl.BlockSpec((s_len, MLA_HEAD_W), lambda b, h, i, o=row0 // s_len: (o + b, h)),
                     pl.BlockSpec((s_len, MLA_V), lambda b, h, i, o=row0 // s_len: (o + b, h))]
        args += [k, v]
    return pl.pallas_call(
        _attn_kernel,
        grid=(n_seq, MLA_HEADS, nq),
        in_specs=in_specs,
        out_specs=pl.BlockSpec((tq, MLA_V), lambda b, h, i: (b * nq + i, h)),
        out_shape=jax.ShapeDtypeStruct((n_seq * t, GROUP_W), BF16),
        compiler_params=pltpu.CompilerParams(dimension_semantics=("parallel", "parallel", "arbitrary"),
                                             vmem_limit_bytes=VMEM_LIMIT),
        name="mla_attention",
    )(*args)


def _mla_weights(mla_q_lat_gain, mla_w_q_up, mla_q_gain, mla_kv_lat_gain, mla_w_kv_up, mla_k_gain):
    qcol = np.full((MLA_HEADS * MLA_HEAD_W,), -1, np.int64)
    kvcol = np.zeros((2 * MLA_HEADS * LANE,), np.int64)
    for h in range(MLA_HEADS):
        qcol[h * MLA_HEAD_W:h * MLA_HEAD_W + MLA_QK] = h * MLA_QK + np.arange(MLA_QK)
        kvcol[h * LANE:(h + 1) * LANE] = h * (MLA_NOPE + MLA_V) + np.arange(MLA_NOPE)
        kvcol[(MLA_HEADS + h) * LANE:(MLA_HEADS + h + 1) * LANE] = h * (MLA_NOPE + MLA_V) + MLA_NOPE + np.arange(MLA_V)
    pad = jnp.zeros((DEPTH, MLA_HEAD_W - MLA_QK), F32)
    return dict(
        gl=mla_q_lat_gain.reshape(DEPTH, 1, MLA_Q_RANK),
        wq=_gather_columns(mla_w_q_up, qcol).astype(BF16),
        gq=jnp.concatenate([mla_q_gain, pad], axis=-1).reshape(DEPTH, 1, MLA_HEAD_W),
        gkv=mla_kv_lat_gain.reshape(DEPTH, 1, MLA_KV_RANK),
        wkv=_gather_columns(mla_w_kv_up, kvcol).astype(BF16),
        gk=jnp.concatenate([mla_k_gain, pad], axis=-1).reshape(DEPTH, 1, MLA_HEAD_W))


def _mla_layer(um, us, cache_l, mw, cos_all, sin_all, layer):
    q = _mla_q(um, mw['gl'], mw['wq'], mw['gq'], cos_all, sin_all, layer)
    kv_lat, k, v = _mla_kv(us, mw['gkv'], mw['wkv'], mw['gk'], cos_all, sin_all, layer,
                           normalize=True, tab_index=_rope_tile_index)
    cache2 = jnp.pad(cache_l.reshape(DEC_BATCH * PAST_LEN, MLA_CACHE_W), ((0, 0), (0, 2 * LANE - MLA_CACHE_W)))
    _, k_c, v_c = _mla_kv(cache2, mw['gkv'], mw['wkv'], mw['gk'], cos_all, sin_all, layer,
                          normalize=False, tab_index=lambda i, tm: 0)
    y_ctx = _attention(q, [(k, v, SEQ, 0)], BATCH, SEQ, SEQ, 0)
    y_lat = _attention(q, [(k, v, DEC_SEQ, CTX_ROWS), (k_c, v_c, PAST_LEN, 0)], DEC_BATCH, DEC_SEQ, 256, CTX_ROWS)
    return y_ctx, y_lat, kv_lat


def _ret_kernel(*refs, T, C, rope, has_h0, has_state):
    it = iter(refs)
    q_ref, k_ref, v_ref, g_ref = next(it), next(it), next(it), next(it)
    cos_ref, sin_ref = (next(it), next(it)) if rope else (None, None)
    lg_ref, gain_ref = next(it), next(it)
    h0_ref = next(it) if has_h0 else None
    y_ref = next(it)
    st_ref = next(it) if has_state else None
    qs_ref, ks_ref, oacc_ref, dm_ref, tab_ref, hst_ref = (next(it) for _ in range(6))

    nc = T // C
    ii = lax.broadcasted_iota(jnp.int32, (C, C), 0)
    jj = lax.broadcasted_iota(jnp.int32, (C, C), 1)
    dif = (ii - jj).astype(F32)
    lane = lax.broadcasted_iota(jnp.int32, (C, LANE), 1)
    rowi = lax.broadcasted_iota(jnp.int32, (C, LANE), 0).astype(F32)
    m_lo = lane < RET_QK
    hrow = lax.broadcasted_iota(jnp.int32, (LANE, LANE), 0) < RET_QK

    for h in range(RET_HEADS):
        lf, lb = lg_ref[0, h], lg_ref[1, h]
        dm_ref[h] = (jnp.where(dif >= 0, jnp.exp(lf * jnp.maximum(dif, 0.0)), 0.0)
                     + jnp.where(dif <= 0, jnp.exp(lb * jnp.maximum(-dif, 0.0)), 0.0))
    for p in range(2):
        lf = jnp.where(m_lo, lg_ref[0, 2 * p], lg_ref[0, 2 * p + 1])
        lb = jnp.where(m_lo, lg_ref[1, 2 * p], lg_ref[1, 2 * p + 1])
        tab_ref[p, 0] = jnp.exp(lf * (rowi + 1.0))
        tab_ref[p, 1] = jnp.exp(lf * (C - 1.0 - rowi))
        tab_ref[p, 2] = jnp.exp(lb * (C - rowi))
        tab_ref[p, 3] = jnp.exp(lb * rowi)
    if has_h0:
        hst_ref[...] = h0_ref[...]
    else:
        hst_ref[...] = jnp.zeros_like(hst_ref)

    def chunk_decay(d, p):
        return jnp.exp(jnp.where(hrow, lg_ref[d, 2 * p], lg_ref[d, 2 * p + 1]) * float(C))

    def forward(c, carry):
        r0 = pl.multiple_of(c * C, C)
        rows = pl.ds(r0, C)
        for p in range(2):
            cols = slice(p * LANE, (p + 1) * LANE)
            qp = q_ref[rows, cols].astype(F32)
            kp = k_ref[rows, cols].astype(F32) * (RET_QK ** -0.5)
            if rope:
                qp = _rope(qp, cos_ref[rows, :], sin_ref[rows, :])
                kp = _rope(kp, cos_ref[rows, :], sin_ref[rows, :])
            qs_ref[rows, cols] = qp.astype(BF16)
            ks_ref[rows, cols] = kp.astype(BF16)
            kpb = kp.astype(BF16)
            hp = hst_ref[0, p]
            hpb = hp.astype(BF16)
            upd = jnp.zeros((LANE, LANE), F32)
            for e in range(2):
                h = 2 * p + e
                mh = m_lo if e == 0 else jnp.logical_not(m_lo)
                hcols = slice(h * LANE, (h + 1) * LANE)
                qm = jnp.where(mh, qp, 0.0)
                s = _dot_nt(qm.astype(BF16), kpb)
                vh = v_ref[rows, hcols]
                o = _dot((s * dm_ref[h]).astype(BF16), vh)
                o += _dot((qm * tab_ref[p, 0]).astype(BF16), hpb)
                oacc_ref[rows, hcols] = o
                upd += _dot_tn(jnp.where(mh, kp * tab_ref[p, 1], 0.0).astype(BF16), vh)
            hst_ref[0, p] = hp * chunk_decay(0, p) + upd
        return carry

    lax.fori_loop(0, nc, forward, 0)

    def backward(t, carry):
        c = nc - 1 - t
        r0 = pl.multiple_of(c * C, C)
        rows = pl.ds(r0, C)
        for p in range(2):
            cols = slice(p * LANE, (p + 1) * LANE)
            qp = qs_ref[rows, cols].astype(F32)
            kp = ks_ref[rows, cols].astype(F32)
            hp = hst_ref[1, p]
            hpb = hp.astype(BF16)
            upd = jnp.zeros((LANE, LANE), F32)
            for e in range(2):
                h = 2 * p + e
                mh = m_lo if e == 0 else jnp.logical_not(m_lo)
                hcols = slice(h * LANE, (h + 1) * LANE)
                vh = v_ref[rows, hcols]
                o = oacc_ref[rows, hcols] + _dot(jnp.where(mh, qp * tab_ref[p, 2], 0.0).astype(BF16), hpb)
                upd += _dot_tn(jnp.where(mh, kp * tab_ref[p, 3], 0.0).astype(BF16), vh)
                oc = o - jnp.mean(o, axis=-1, keepdims=True)
                yn = oc * lax.rsqrt(jnp.mean(oc * oc, axis=-1, keepdims=True) + EPS) * gain_ref[:, hcols]
                y_ref[rows, hcols] = (_silu(g_ref[rows, hcols].astype(F32)) * yn).astype(y_ref.dtype)
            hst_ref[1, p] = hp * chunk_decay(1, p) + upd
        return carry

    lax.fori_loop(0, nc, backward, 0)
    if has_state:
        st_ref[...] = hst_ref[...]


def _retention(um, lg, gain, h0, rope_tabs, n_seq, T, row_block0):
    C = RET_CHUNK
    rope = rope_tabs is not None
    has_h0 = h0 is not None
    has_state = not has_h0

    def col(off, w):
        return pl.BlockSpec((T, w), lambda b: (row_block0 + b, off // w))
    in_specs = [col(MAIN_RET_Q, 2 * LANE), col(MAIN_RET_K, 2 * LANE), col(MAIN_RET_V, GROUP_W), col(MAIN_RET_G, GROUP_W)]
    args = [um, um, um, um]
    if rope:
        in_specs += [pl.BlockSpec((T, LANE), lambda b: (0, 0))] * 2
        args += list(rope_tabs)
    in_specs += [pl.BlockSpec(memory_space=pltpu.SMEM), pl.BlockSpec((1, GROUP_W), lambda b: (0, 0))]
    args += [lg, gain]
    st_spec = pl.BlockSpec((None, 2, 2, LANE, LANE), lambda b: (b, 0, 0, 0, 0))
    if has_h0:
        in_specs.append(st_spec)
        args.append(h0)
    out_specs = [pl.BlockSpec((T, GROUP_W), lambda b: (b, 0))]
    out_shape = [jax.ShapeDtypeStruct((n_seq * T, GROUP_W), BF16)]
    if has_state:
        out_specs.append(st_spec)
        out_shape.append(jax.ShapeDtypeStruct((n_seq, 2, 2, LANE, LANE), F32))
    res = pl.pallas_call(
        functools.partial(_ret_kernel, T=T, C=C, rope=rope, has_h0=has_h0, has_state=has_state),
        grid=(n_seq,),
        in_specs=in_specs,
        out_specs=out_specs,
        out_shape=out_shape,
        scratch_shapes=[pltpu.VMEM((T, 2 * LANE), BF16), pltpu.VMEM((T, 2 * LANE), BF16),
                        pltpu.VMEM((T, GROUP_W), F32), pltpu.VMEM((RET_HEADS, C, C), F32),
                        pltpu.VMEM((2, 4, C, LANE), F32), pltpu.VMEM((2, 2, LANE, LANE), F32)],
        compiler_params=pltpu.CompilerParams(dimension_semantics=("parallel",), vmem_limit_bytes=VMEM_LIMIT),
        name="retention",
    )(*args)
    return (res[0], res[1]) if has_state else (res[0], None)


def _split3(x):
    x1 = x.astype(BF16)
    r1 = x - x1.astype(F32)
    x2 = r1.astype(BF16)
    x3 = (r1 - x2.astype(F32)).astype(BF16)
    return x1, x2, x3


def _dot01_l(m, x):
    x1, x2, x3 = _split3(x)
    return _dot(m, x1) + _dot(m, x2) + _dot(m, x3)


def _dot01_r(x, m):
    x1, x2, x3 = _split3(x)
    return _dot(x1, m) + _dot(x2, m) + _dot(x3, m)


def _softplus(x):
    return jnp.maximum(x, 0.0) + jnp.log(1.0 + jnp.exp(-jnp.abs(x)))


def _head_expanders():
    e8 = np.zeros((LANE, SSD_HEADS * LANE), np.float32)
    e64 = np.zeros((LANE, GROUP_W), np.float32)
    for h in range(SSD_HEADS):
        e8[h, h * LANE:(h + 1) * LANE] = 1.0
        e64[h, h * SSD_HEAD_DIM:(h + 1) * SSD_HEAD_DIM] = 1.0
    return jnp.asarray(e8, BF16), jnp.asarray(e64, BF16)


def _ssd_kernel(*refs, T, has_h0, has_state):
    it = iter(refs)
    z_ref, xs_ref, bc_ref, dt_ref = next(it), next(it), next(it), next(it)
    wx_ref, wbc_ref, bx_ref, bbc_ref = next(it), next(it), next(it), next(it)
    a_ref, dtb_ref, d_ref, gain_ref, e8_ref, e64_ref = (next(it) for _ in range(6))
    h0_ref = next(it) if has_h0 else None
    y_ref = next(it)
    st_ref = next(it) if has_state else None
    xc_s, bcs_s, oacc_ref, hst_ref = next(it), next(it), next(it), next(it)

    C = SSD_CHUNK
    nc = T // C
    ii = lax.broadcasted_iota(jnp.int32, (C, C), 0)
    jj = lax.broadcasted_iota(jnp.int32, (C, C), 1)
    tril, triu = ii >= jj, ii <= jj
    tril_b, triu_b = tril.astype(BF16), triu.astype(BF16)
    lane = lax.broadcasted_iota(jnp.int32, (C, LANE), 1)
    m_lo = lane < SSD_STATE
    masks = (m_lo, jnp.logical_not(m_lo))
    hrow = lax.broadcasted_iota(jnp.int32, (LANE, LANE), 0) < SSD_STATE

    if has_h0:
        hst_ref[...] = h0_ref[...]
    else:
        hst_ref[...] = jnp.zeros_like(hst_ref)

    def decays(rows, d):
        dtv = _softplus(dt_ref[rows, :] + dtb_ref[d:d + 1, :])
        la = dtv * a_ref[d:d + 1, :]
        return dtv, _dot01_l(tril_b if d == 0 else triu_b, la)

    def conv(ref, w_ref, b_ref, c, r0):
        width = ref.shape[1]
        x = ref[pl.ds(r0, C), :].astype(F32)
        p0 = pl.multiple_of(jnp.maximum(r0 - 16, 0), 16)
        n0 = pl.multiple_of(jnp.minimum(r0 + C, T - 16), 16)
        prev_row = jnp.where(c > 0, ref[pl.ds(p0, 16), :].astype(F32)[15:16], 0.0)
        next_row = jnp.where(c < nc - 1, ref[pl.ds(n0, 16), :].astype(F32)[0:1], 0.0)
        rowi = lax.broadcasted_iota(jnp.int32, (C, width), 0)
        prev = jnp.where(rowi == 0, prev_row, pltpu.roll(x, 1, 0))
        nxt = jnp.where(rowi == C - 1, next_row, pltpu.roll(x, C - 1, 0))
        return _silu(prev * w_ref[0:1, :] + x * w_ref[1:2, :] + nxt * w_ref[2:3, :] + b_ref[...])

    def decay_matrix(qrow, pp):
        return jnp.where(hrow, qrow[:, pp * LANE:(pp + 1) * LANE],
                         qrow[:, (2 + pp) * LANE:(3 + pp) * LANE])

    def forward(c, carry):
        r0 = pl.multiple_of(c * C, C)
        rows = pl.ds(r0, C)
        xc = conv(xs_ref, wx_ref, bx_ref, c, r0)
        bcv = conv(bc_ref, wbc_ref, bbc_ref, c, r0)
        xc_s[rows, :] = xc.astype(BF16)
        bcs_s[rows, :] = bcv.astype(BF16)
        bmat, cmat = bcv[:, :LANE], bcv[:, LANE:]
        bmb = bmat.astype(BF16)
        dtf, bf = decays(rows, 0)
        dtb, bb = decays(rows, 1)
        bf_t, dtf_t, bb_t, dtb_t = bf.T, dtf.T, bb.T, dtb.T
        colf = _dot01_r(bf, e8_ref[...])
        colb = _dot01_r(bb, e8_ref[...])
        qdf = _dot01_r(jnp.exp(bf), e64_ref[...])
        kwf = _dot01_r(dtf * jnp.exp(bf[C - 1:C, :] - bf), e64_ref[...])
        cms = [jnp.where(masks[g], cmat, 0.0).astype(BF16) for g in range(SSD_GROUPS)]
        bms = [jnp.where(masks[g], bmat, 0.0).astype(BF16) for g in range(SSD_GROUPS)]
        scores = [_dot_nt(cms[g], bmb) for g in range(SSD_GROUPS)]
        for pp in range(2):
            hp = hst_ref[0, pp]
            hpb = hp.astype(BF16)
            upd = jnp.zeros((LANE, LANE), F32)
            for g in range(SSD_GROUPS):
                cols = slice((2 * g + pp) * LANE, (2 * g + pp + 1) * LANE)
                xs_pair = xc[:, cols]
                xsb = xs_pair.astype(BF16)
                outs = []
                for e in range(2):
                    h = 4 * g + 2 * pp + e
                    hc = slice(h * LANE, (h + 1) * LANE)
                    ef = jnp.where(tril, jnp.exp(jnp.minimum(colf[:, hc] - bf_t[h:h + 1, :], 0.0)), 0.0) * dtf_t[h:h + 1, :]
                    eb = jnp.where(triu, jnp.exp(jnp.minimum(colb[:, hc] - bb_t[h:h + 1, :], 0.0)), 0.0) * dtb_t[h:h + 1, :]
                    outs.append(_dot((scores[g] * (ef + eb)).astype(BF16), xsb))
                o = jnp.where(m_lo, outs[0], outs[1])
                o += _dot(cms[g], hpb) * qdf[:, cols]
                oacc_ref[rows, cols] = o
                upd += _dot_tn(bms[g], (xs_pair * kwf[:, cols]).astype(BF16))
            hst_ref[0, pp] = hp * decay_matrix(qdf[C - 1:C, :], pp) + upd
        return carry

    lax.fori_loop(0, nc, forward, 0)

    def backward(t, carry):
        c = nc - 1 - t
        r0 = pl.multiple_of(c * C, C)
        rows = pl.ds(r0, C)
        xc = xc_s[rows, :].astype(F32)
        bcv = bcs_s[rows, :]
        bmat, cmat = bcv[:, :LANE], bcv[:, LANE:]
        zero = jnp.zeros_like(bmat)
        dtb, bb = decays(rows, 1)
        qdb = _dot01_r(jnp.exp(bb), e64_ref[...])
        kwb = _dot01_r(dtb * jnp.exp(bb[0:1, :] - bb), e64_ref[...])
        blocks = {}
        for pp in range(2):
            hp = hst_ref[1, pp]
            hpb = hp.astype(BF16)
            upd = jnp.zeros((LANE, LANE), F32)
            for g in range(SSD_GROUPS):
                blk = 2 * g + pp
                cols = slice(blk * LANE, (blk + 1) * LANE)
                xs_pair = xc[:, cols]
                o = oacc_ref[rows, cols] + _dot(jnp.where(masks[g], cmat, zero), hpb) * qdb[:, cols]
                upd += _dot_tn(jnp.where(masks[g], bmat, zero), (xs_pair * kwb[:, cols]).astype(BF16))
                y = (o + d_ref[:, cols] * xs_pair) * _silu(z_ref[rows, cols].astype(F32))
                blocks[blk] = y
            hst_ref[1, pp] = hp * decay_matrix(qdb[0:1, :], pp) + upd
        for g in range(SSD_GROUPS):
            y0, y1 = blocks[2 * g], blocks[2 * g + 1]
            ss = jnp.sum(y0 * y0, axis=-1, keepdims=True) + jnp.sum(y1 * y1, axis=-1, keepdims=True)
            r = lax.rsqrt(ss / (2 * LANE) + EPS)
            for i, yb in enumerate((y0, y1)):
                cols = slice((2 * g + i) * LANE, (2 * g + i + 1) * LANE)
                y_ref[rows, cols] = (yb * r * gain_ref[:, cols]).astype(y_ref.dtype)
        return carry

    lax.fori_loop(0, nc, backward, 0)
    if has_state:
        st_ref[...] = hst_ref[...]


def _ssd_pack_state(st):
    n = st.shape[0]
    st = st.reshape(n, 2, SSD_GROUPS, 2, 2, SSD_STATE, SSD_HEAD_DIM)
    return st.transpose(0, 1, 3, 2, 5, 4, 6).reshape(n, 2, 2, LANE, LANE)


def _ssd_unpack_state(st):
    n = st.shape[0]
    st = st.reshape(n, 2, 2, SSD_GROUPS, SSD_STATE, 2, SSD_HEAD_DIM)
    return st.transpose(0, 1, 3, 2, 5, 4, 6).reshape(n, 2, SSD_HEADS, SSD_STATE, SSD_HEAD_DIM)


def _ssd_params(ssd_conv_w, ssd_conv_b, ssd_a_log, ssd_dt_bias, ssd_d, ssd_norm_g):
    def lanes8(v):
        return jnp.pad(v.astype(F32), ((0, 0), (0, 0), (0, LANE - SSD_HEADS)))
    e8, e64 = _head_expanders()
    return dict(wx=ssd_conv_w[:, :, :GROUP_W], wbc=ssd_conv_w[:, :, GROUP_W:],
                bx=ssd_conv_b[:, None, :GROUP_W], bbc=ssd_conv_b[:, None, GROUP_W:],
                a=lanes8(-jnp.exp(ssd_a_log.astype(F32))), dtb=lanes8(ssd_dt_bias),
                d=jnp.repeat(ssd_d, SSD_HEAD_DIM, axis=-1)[:, None, :], gain=ssd_norm_g[:, None, :],
                e8=e8, e64=e64)


def _ssd_scan(um, us, sp, layer, h0, n_seq, T, row_block0):
    has_h0 = h0 is not None
    has_state = not has_h0

    def col(off, w):
        return pl.BlockSpec((T, w), lambda b: (row_block0 + b, off // w))

    def per_layer(arr):
        return pl.BlockSpec((None,) + arr.shape[1:], lambda b: (layer,) + (0,) * (arr.ndim - 1))

    def const(arr):
        return pl.BlockSpec(arr.shape, lambda b: (0,) * arr.ndim)
    names = ('wx', 'wbc', 'bx', 'bbc', 'a', 'dtb', 'd', 'gain')
    in_specs = [col(MAIN_SSD_Z, GROUP_W), col(MAIN_SSD_XS, GROUP_W), col(MAIN_SSD_BC, 2 * LANE),
                pl.BlockSpec((T, LANE), lambda b: (row_block0 + b, SIDE_DT // LANE))]
    in_specs += [per_layer(sp[n]) for n in names] + [const(sp['e8']), const(sp['e64'])]
    args = [um, um, um, us] + [sp[n] for n in names] + [sp['e8'], sp['e64']]
    st_spec = pl.BlockSpec((None, 2, 2, LANE, LANE), lambda b: (b, 0, 0, 0, 0))
    if has_h0:
        in_specs.append(st_spec)
        args.append(h0)
    out_specs = [pl.BlockSpec((T, GROUP_W), lambda b: (b, 0))]
    out_shape = [jax.ShapeDtypeStruct((n_seq * T, GROUP_W), BF16)]
    if has_state:
        out_specs.append(st_spec)
        out_shape.append(jax.ShapeDtypeStruct((n_seq, 2, 2, LANE, LANE), F32))
    res = pl.pallas_call(
        functools.partial(_ssd_kernel, T=T, has_h0=has_h0, has_state=has_state),
        grid=(n_seq,),
        in_specs=in_specs,
        out_specs=out_specs,
        out_shape=out_shape,
        scratch_shapes=[pltpu.VMEM((T, GROUP_W), BF16), pltpu.VMEM((T, 2 * LANE), BF16),
                        pltpu.VMEM((T, GROUP_W), F32), pltpu.VMEM((2, 2, LANE, LANE), F32)],
        compiler_params=pltpu.CompilerParams(dimension_semantics=("parallel",), vmem_limit_bytes=VMEM_LIMIT),
        name="ssd",
    )(*args)
    return (res[0], res[1]) if has_state else (res[0], None)


GLA_MACRO = 128


def _log_sigmoid(x):
    return jnp.minimum(x, 0.0) - jnp.log(1.0 + jnp.exp(-jnp.abs(x)))


def _gla_kernel(*refs, T, has_h0, has_state):
    it = iter(refs)
    q_ref, k_ref, v_ref, r_ref, g1_ref = (next(it) for _ in range(5))
    wg_ref, bg_ref, gain_ref, ind_ref = (next(it) for _ in range(4))
    h0_ref = next(it) if has_h0 else None
    y_ref = next(it)
    st_ref = next(it) if has_state else None
    oacc_ref, hst_ref, qbuf, kbuf, bfbuf, bbbuf, vbuf, obuf = (next(it) for _ in range(8))

    C, L = GLA_MACRO, GLA_CHUNK
    nb = C // L
    nc = T // C
    ii = lax.broadcasted_iota(jnp.int32, (C, C), 0)
    jj = lax.broadcasted_iota(jnp.int32, (C, C), 1)
    same = (ii // L) == (jj // L)
    tri_l = jnp.logical_and(same, jj <= ii).astype(BF16)
    tri_u = jnp.logical_and(same, jj >= ii).astype(BF16)
    ones_b = same.astype(BF16)
    lane = lax.broadcasted_iota(jnp.int32, (C, LANE), 1)
    masks = (lane < GLA_QK, lane >= GLA_QK)
    HALF = 8
    ri8 = lax.broadcasted_iota(jnp.int32, (HALF, 2 * LANE), 0)
    bdmask = (lax.broadcasted_iota(jnp.int32, (C, nb * LANE), 0) // L
              == lax.broadcasted_iota(jnp.int32, (C, nb * LANE), 1) // LANE)

    if has_h0:
        hst_ref[...] = h0_ref[...]
    else:
        hst_ref[...] = jnp.zeros_like(hst_ref)

    def log_decay(rows, d):
        g1 = g1_ref[rows, :]
        hi = g1.astype(BF16)
        lo = (g1 - hi.astype(F32)).astype(BF16)
        logits = _dot(hi, wg_ref[d]) + _dot(lo, wg_ref[d]) + bg_ref[d]
        return _log_sigmoid(logits) / GLA_GATE_TEMP

    def recurrence(d, rows, qt, kt, dec, blocks):
        zero = jnp.zeros((), BF16)

        def block_diag(x):
            return jnp.where(bdmask, jnp.concatenate([x] * nb, axis=1), zero)
        upds = []
        for h in range(GLA_HEADS):
            p, e = divmod(h, 2)
            km = jnp.where(masks[e], kt[:, p * LANE:(p + 1) * LANE], 0.0).astype(BF16)
            upds.append(_dot_tn(v_ref[rows, h * LANE:(h + 1) * LANE], block_diag(km)))
        snaps = []
        for h in range(GLA_HEADS):
            lanes = slice((h // 2) * LANE, (h // 2 + 1) * LANE)
            ht = hst_ref[d, h]
            snap = [None] * nb
            for blk in blocks:
                snap[blk] = ht.astype(BF16)
                ht = ht * dec[blk * L:blk * L + 1, lanes] + upds[h][:, blk * LANE:(blk + 1) * LANE]
            hst_ref[d, h] = ht
            snaps.append(jnp.concatenate(snap, axis=1))
        outs = []
        for h in range(GLA_HEADS):
            p, e = divmod(h, 2)
            qm = jnp.where(masks[e], qt[:, p * LANE:(p + 1) * LANE], 0.0).astype(BF16)
            outs.append(_dot_nt(block_diag(qm), snaps[h]))
        return outs

    def intra_block(b0):
        pieces = []
        for j in range(L):
            kj, bfj, bbj = kbuf[b0 + j:b0 + j + 1, :], bfbuf[b0 + j:b0 + j + 1, :], bbbuf[b0 + j:b0 + j + 1, :]
            halves = []
            for s in range(L // HALF):
                rs = slice(b0 + s * HALF, b0 + (s + 1) * HALF)
                qk = qbuf[rs, :] * kj
                if j < s * HALF:
                    e = qk * jnp.exp(bfbuf[rs, :] - bfj)
                elif j >= (s + 1) * HALF:
                    e = qk * jnp.exp(bbbuf[rs, :] - bbj)
                else:
                    rel = ri8 + (s * HALF - j)
                    e = qk * jnp.exp(jnp.where(rel >= 0, bfbuf[rs, :] - bfj, bbbuf[rs, :] - bbj))
                    e = jnp.where(rel == 0, 2.0 * e, e)
                halves.append(e)
            pieces.append(jnp.concatenate(halves, axis=0).astype(BF16))
        spread = _dot(jnp.concatenate(pieces, axis=0), ind_ref[...])
        acc = spread[0:L, :] * vbuf[b0:b0 + 1, :]
        for j in range(1, L):
            acc += spread[j * L:(j + 1) * L, :] * vbuf[b0 + j:b0 + j + 1, :]
        return acc

    def forward(c, carry):
        r0 = pl.multiple_of(c * C, C)
        rows = pl.ds(r0, C)
        q = q_ref[rows, :].astype(F32) * (GLA_QK ** -0.5)
        k = k_ref[rows, :].astype(F32)
        la_f, la_b = log_decay(rows, 0), log_decay(rows, 1)
        bf, tot_f = _dot01_l(tri_l, la_f), _dot01_l(ones_b, la_f)
        bb = _dot01_l(tri_u, la_b)
        qbuf[...] = q
        kbuf[...] = k
        bfbuf[...] = bf
        bbbuf[...] = bb
        vbuf[...] = v_ref[rows, :].astype(F32)
        for blk in range(nb):
            obuf[blk * L:(blk + 1) * L, :] = intra_block(blk * L)
        qt, kt, dec = q * jnp.exp(bf), k * jnp.exp(tot_f - bf), jnp.exp(tot_f)
        inter = recurrence(0, rows, qt, kt, dec, range(nb))
        for h in range(GLA_HEADS):
            hcols = slice(h * LANE, (h + 1) * LANE)
            oacc_ref[rows, hcols] = obuf[:, hcols] + inter[h]
        return carry

    lax.fori_loop(0, nc, forward, 0)

    def backward(t, carry):
        c = nc - 1 - t
        r0 = pl.multiple_of(c * C, C)
        rows = pl.ds(r0, C)
        q = q_ref[rows, :].astype(F32) * (GLA_QK ** -0.5)
        k = k_ref[rows, :].astype(F32)
        la_b = log_decay(rows, 1)
        bb, tot_b = _dot01_l(tri_u, la_b), _dot01_l(ones_b, la_b)
        qt, kt, dec = q * jnp.exp(bb), k * jnp.exp(tot_b - bb), jnp.exp(tot_b)
        inter = recurrence(1, rows, qt, kt, dec, range(nb - 1, -1, -1))
        for h in range(GLA_HEADS):
            hcols = slice(h * LANE, (h + 1) * LANE)
            o = oacc_ref[rows, hcols] + inter[h]
            yn = o * lax.rsqrt(jnp.mean(o * o, axis=-1, keepdims=True) + EPS) * gain_ref[:, hcols]
            y_ref[rows, hcols] = (_silu(r_ref[rows, hcols].astype(F32)) * yn).astype(y_ref.dtype)
        return carry

    lax.fori_loop(0, nc, backward, 0)
    if has_state:
        st_ref[...] = hst_ref[...]


def _gla_pack_state(st):
    n = st.shape[0]
    ht = jnp.swapaxes(st, -1, -2)
    z = jnp.zeros_like(ht)
    even = jnp.concatenate([ht, z], axis=-1)
    odd = jnp.concatenate([z, ht], axis=-1)
    sel = (jnp.arange(GLA_HEADS) % 2 == 0)[None, None, :, None, None]
    return jnp.where(sel, even, odd)


def _gla_unpack_state(st):
    even, odd = st[..., :GLA_QK], st[..., GLA_QK:]
    sel = (jnp.arange(GLA_HEADS) % 2 == 0)[None, None, :, None, None]
    return jnp.swapaxes(jnp.where(sel, even, odd), -1, -2)


def _gla_params(gla_w_g2, gla_b_g, gla_norm_g):
    wg = jnp.zeros((DEPTH, 2, LANE, 2 * LANE), F32)
    g1_lane = SIDE_G1 - SIDE_DT
    wg = wg.at[:, :, g1_lane:g1_lane + GLA_GATE_RANK, :].set(gla_w_g2)
    ind = np.zeros((2 * LANE, GROUP_W), np.float32)
    for h in range(GLA_HEADS):
        ind[h * GLA_QK:(h + 1) * GLA_QK, h * GLA_V:(h + 1) * GLA_V] = 1.0
    return dict(wg=wg.astype(BF16), bg=gla_b_g[:, :, None, :], gain=gla_norm_g[:, None, :],
                ind=jnp.asarray(ind, BF16))


def _gla_scan(um, us, gp, layer, h0, n_seq, T, row_block0):
    has_h0 = h0 is not None
    has_state = not has_h0
    C = GLA_MACRO

    def col(off, w):
        return pl.BlockSpec((T, w), lambda b: (row_block0 + b, off // w))

    def per_layer(arr):
        return pl.BlockSpec((None,) + arr.shape[1:], lambda b: (layer,) + (0,) * (arr.ndim - 1))
    in_specs = [col(MAIN_GLA_Q, 2 * LANE), col(MAIN_GLA_K, 2 * LANE), col(MAIN_GLA_V, GROUP_W), col(MAIN_GLA_R, GROUP_W),
                pl.BlockSpec((T, LANE), lambda b: (row_block0 + b, SIDE_DT // LANE)),
                per_layer(gp['wg']), per_layer(gp['bg']), per_layer(gp['gain']),
                pl.BlockSpec(gp['ind'].shape, lambda b: (0, 0))]
    args = [um, um, um, um, us, gp['wg'], gp['bg'], gp['gain'], gp['ind']]
    st_spec = pl.BlockSpec((None, 2, GLA_HEADS, LANE, LANE), lambda b: (b, 0, 0, 0, 0))
    if has_h0:
        in_specs.append(st_spec)
        args.append(h0)
    out_specs = [pl.BlockSpec((T, GROUP_W), lambda b: (b, 0))]
    out_shape = [jax.ShapeDtypeStruct((n_seq * T, GROUP_W), BF16)]
    if has_state:
        out_specs.append(st_spec)
        out_shape.append(jax.ShapeDtypeStruct((n_seq, 2, GLA_HEADS, LANE, LANE), F32))
    res = pl.pallas_call(
        functools.partial(_gla_kernel, T=T, has_h0=has_h0, has_state=has_state),
        grid=(n_seq,),
        in_specs=in_specs,
        out_specs=out_specs,
        out_shape=out_shape,
        scratch_shapes=[pltpu.VMEM((T, GROUP_W), F32), pltpu.VMEM((2, GLA_HEADS, LANE, LANE), F32),
                        pltpu.VMEM((C, 2 * LANE), F32), pltpu.VMEM((C, 2 * LANE), F32),
                        pltpu.VMEM((C, 2 * LANE), F32), pltpu.VMEM((C, 2 * LANE), F32),
                        pltpu.VMEM((C, GROUP_W), F32), pltpu.VMEM((C, GROUP_W), F32)],
        compiler_params=pltpu.CompilerParams(dimension_semantics=("parallel",), vmem_limit_bytes=VMEM_LIMIT),
        name="gla",
    )(*args)
    return (res[0], res[1]) if has_state else (res[0], None)


def kernel(x_prompt, x_sample, cache_mla_kv, state_ssd, state_ret, state_gla, c, c_ctx, w_mod, b_mod, norm_ffn1, ffn1_wg, ffn1_wu, ffn1_wd, norm_mix, w_in, ssd_conv_w, ssd_conv_b, ssd_a_log, ssd_dt_bias, ssd_d, ssd_norm_g, mla_q_lat_gain, mla_w_q_up, mla_q_gain, mla_kv_lat_gain, mla_w_kv_up, mla_k_gain, ret_decay_logit, ret_norm_g, gla_w_g2, gla_b_g, gla_norm_g, w_out, norm_ffn2, ffn2_wg, ffn2_wu, ffn2_wd):
    sp = _ssd_params(ssd_conv_w, ssd_conv_b, ssd_a_log, ssd_dt_bias, ssd_d, ssd_norm_g)
    ssd_h0 = jnp.stack([_ssd_pack_state(state_ssd[:, l]) for l in range(DEPTH)], axis=1)
    gp = _gla_params(gla_w_g2, gla_b_g, gla_norm_g)
    gla_h0 = jnp.stack([_gla_pack_state(state_gla[:, l]) for l in range(DEPTH)], axis=1)

    main_idx, side_idx = _in_proj_columns()
    w_main = _column_tiles(_gather_columns(w_in, main_idx), MAIN_TN)
    w_side = _gather_columns(w_in, side_idx).astype(BF16)
    w_out_b = w_out.astype(BF16)
    f1 = (_column_tiles(ffn1_wg, FFN_TF), _column_tiles(ffn1_wu, FFN_TF), _column_tiles(ffn1_wd, FFN_TN))
    f2 = (_column_tiles(ffn2_wg, FFN_TF), _column_tiles(ffn2_wu, FFN_TF), _column_tiles(ffn2_wd, FFN_TN))
    g_ffn1 = norm_ffn1.reshape(DEPTH, 1, D_MODEL)
    g_mix = norm_mix.reshape(DEPTH, 1, D_MODEL)
    g_ffn2 = norm_ffn2.reshape(DEPTH, 1, D_MODEL)
    mw = _mla_weights(mla_q_lat_gain, mla_w_q_up, mla_q_gain, mla_kv_lat_gain, mla_w_kv_up, mla_k_gain)
    cos_lat, sin_lat = _rope_tables(DEC_SEQ)
    cos_all = jnp.concatenate([jnp.ones((TOK_TM, LANE), F32), cos_lat], axis=0)
    sin_all = jnp.concatenate([jnp.zeros((TOK_TM, LANE), F32), sin_lat], axis=0)
    ret_lg = jax.nn.log_sigmoid(ret_decay_logit.astype(F32))
    ret_gain = ret_norm_g.reshape(DEPTH, 1, GROUP_W)
    ret_h0 = state_ret.reshape(DEC_BATCH, DEPTH, 2, 2, LANE, LANE)

    c_all = jnp.zeros((MOD_ROWS, D_MODEL), F32).at[0].set(c_ctx).at[1:1 + DEC_BATCH].set(c)
    mod = _modulation(c_all, w_mod, b_mod).reshape(DEPTH * MOD_ROWS, 1, N_MOD * D_MODEL)

    x = jnp.concatenate([x_prompt.reshape(CTX_ROWS, D_MODEL), x_sample.reshape(LAT_ROWS, D_MODEL)], axis=0)
    kv_list, ssd_list, ret_list, gla_list = [], [], [], []
    for l in range(DEPTH):
        x = _ffn(x, mod, g_ffn1, *f1, l, 0)
        um, us = _in_proj(x, mod, g_mix, w_main, w_side, l)

        ym_c, ym_l, kv_lat = _mla_layer(um, us, cache_mla_kv[:, l], mw, cos_all, sin_all, l)
        yr_c, s_ret = _retention(um, ret_lg[l], ret_gain[l], None, None, BATCH, SEQ, 0)
        yr_l, _ = _retention(um, ret_lg[l], ret_gain[l], ret_h0[:, l], (cos_lat, sin_lat),
                             DEC_BATCH, DEC_SEQ, CTX_ROWS // DEC_SEQ)
        ys_c, s_ssd = _ssd_scan(um, us, sp, l, None, BATCH, SEQ, 0)
        ys_l, _ = _ssd_scan(um, us, sp, l, ssd_h0[:, l], DEC_BATCH, DEC_SEQ, CTX_ROWS // DEC_SEQ)
        yg_c, s_gla = _gla_scan(um, us, gp, l, None, BATCH, SEQ, 0)
        yg_l, _ = _gla_scan(um, us, gp, l, gla_h0[:, l], DEC_BATCH, DEC_SEQ, CTX_ROWS // DEC_SEQ)

        kv_list.append(kv_lat[:CTX_ROWS, :MLA_CACHE_W].reshape(BATCH, SEQ, MLA_CACHE_W))
        ssd_list.append(_ssd_unpack_state(s_ssd))
        ret_list.append(s_ret.reshape(BATCH, 2, RET_HEADS, RET_QK, RET_V))
        gla_list.append(_gla_unpack_state(s_gla))
        x = _out_proj(x, [ys_c, ym_c, yr_c, yg_c], [ys_l, ym_l, yr_l, yg_l], mod, w_out_b, l)
        x = _ffn(x, mod, g_ffn2, *f2, l, 6)
    y_p = x[:CTX_ROWS].reshape(BATCH, SEQ, D_MODEL)
    y_s = x[CTX_ROWS:].reshape(DEC_BATCH, DEC_SEQ, D_MODEL)
    return (y_p, y_s, jnp.stack(kv_list, axis=1), jnp.stack(ssd_list, axis=1),
            jnp.stack(ret_list, axis=1), jnp.stack(gla_list, axis=1))
```

```python
import functools

import jax
import jax.numpy as jnp
import numpy as np
from jax import lax
from jax.experimental import pallas as pl
from jax.experimental.pallas import tpu as pltpu

F32 = jnp.float32
BF16 = jnp.bfloat16

D_MODEL = 2048
BATCH = 32
SEQ = 256
DEPTH = 4
DEC_BATCH = 4
DEC_SEQ = 4096
PAST_LEN = 256
GRID_W = 64
ROPE_BASE = 10000.0
EPS = 1e-6
D_FF = 5632
N_MOD = 9
GROUP_W = D_MODEL // 4
ATTN_BLOCK = 128

SSD_HEAD_DIM = 64
SSD_HEADS = GROUP_W // SSD_HEAD_DIM
SSD_STATE = 64
SSD_GROUPS = 2
SSD_CONV_K = 3
SSD_CHUNK = 128
SSD_CONV_CH = GROUP_W + 2 * SSD_GROUPS * SSD_STATE
SSD_IN = GROUP_W + SSD_CONV_CH + SSD_HEADS

MLA_HEADS = 4
MLA_NOPE = 128
MLA_ROPE = 64
MLA_V = GROUP_W // MLA_HEADS
MLA_Q_RANK = 384
MLA_KV_RANK = 128
MLA_QK = MLA_NOPE + MLA_ROPE
MLA_IN = MLA_Q_RANK + MLA_KV_RANK + MLA_ROPE
MLA_CACHE_W = MLA_KV_RANK + MLA_ROPE

RET_HEADS = 4
RET_QK = 64
RET_V = GROUP_W // RET_HEADS
RET_CHUNK = 128
RET_IN = 2 * RET_HEADS * RET_QK + 2 * GROUP_W

GLA_HEADS = 4
GLA_QK = 64
GLA_V = GROUP_W // GLA_HEADS
GLA_GATE_RANK = 16
GLA_GATE_TEMP = 16.0
GLA_CHUNK = 16
GLA_IN = 2 * GLA_HEADS * GLA_QK + GROUP_W + GLA_GATE_RANK + GROUP_W

IN_W = SSD_IN + MLA_IN + RET_IN + GLA_IN

LANE = 128
CTX_ROWS = BATCH * SEQ
LAT_ROWS = DEC_BATCH * DEC_SEQ
ROWS = CTX_ROWS + LAT_ROWS
MOD_ROWS = 8
VMEM_LIMIT = 56 * 1024 * 1024
TOK_TM = 512
MLA_HEAD_W = 2 * LANE

MAIN_SSD_Z = 0
MAIN_SSD_XS = 512
MAIN_RET_V = 1024
MAIN_RET_G = 1536
MAIN_GLA_V = 2048
MAIN_GLA_R = 2560
MAIN_SSD_BC = 3072
MAIN_RET_Q = 3328
MAIN_RET_K = 3584
MAIN_GLA_Q = 3840
MAIN_GLA_K = 4096
MAIN_MLA_CQ = 4352
MAIN_W = 4864
MAIN_TN = MAIN_W // 2
SIDE_CKV = 0
SIDE_KPE = SIDE_CKV + MLA_KV_RANK
SIDE_DT = 2 * LANE
SIDE_G1 = SIDE_DT + SSD_HEADS
SIDE_W = 3 * LANE


def _in_proj_columns():
    o_ssd, o_mla, o_ret, o_gla = 0, SSD_IN, SSD_IN + MLA_IN, SSD_IN + MLA_IN + RET_IN
    main = np.full((MAIN_W,), -1, np.int64)

    def put(dst, src, n):
        main[dst:dst + n] = src + np.arange(n)
    qk = RET_HEADS * RET_QK
    put(MAIN_SSD_Z, o_ssd, GROUP_W)
    put(MAIN_SSD_XS, o_ssd + GROUP_W, GROUP_W)
    put(MAIN_SSD_BC, o_ssd + 2 * GROUP_W, 2 * SSD_GROUPS * SSD_STATE)
    put(MAIN_MLA_CQ, o_mla, MLA_Q_RANK)
    put(MAIN_RET_Q, o_ret, qk)
    put(MAIN_RET_K, o_ret + qk, qk)
    put(MAIN_RET_V, o_ret + 2 * qk, GROUP_W)
    put(MAIN_RET_G, o_ret + 2 * qk + GROUP_W, GROUP_W)
    put(MAIN_GLA_Q, o_gla, qk)
    put(MAIN_GLA_K, o_gla + qk, qk)
    put(MAIN_GLA_V, o_gla + 2 * qk, GROUP_W)
    put(MAIN_GLA_R, o_gla + 2 * qk + GROUP_W + GLA_GATE_RANK, GROUP_W)
    side = np.full((SIDE_W,), -1, np.int64)
    side[SIDE_CKV:SIDE_CKV + MLA_KV_RANK + MLA_ROPE] = o_mla + MLA_Q_RANK + np.arange(MLA_KV_RANK + MLA_ROPE)
    side[SIDE_DT:SIDE_DT + SSD_HEADS] = o_ssd + GROUP_W + SSD_CONV_CH + np.arange(SSD_HEADS)
    side[SIDE_G1:SIDE_G1 + GLA_GATE_RANK] = o_gla + 2 * qk + GROUP_W + np.arange(GLA_GATE_RANK)
    return main, side


def _gather_columns(w, idx):
    safe = np.where(idx < 0, 0, idx)
    out = jnp.take(w, jnp.asarray(safe, jnp.int32), axis=-1)
    return jnp.where(jnp.asarray(idx >= 0), out, 0.0)


def _mod_row(i, tm):
    ctx_tiles = CTX_ROWS // tm
    per_seq = DEC_SEQ // tm
    return jnp.where(i < ctx_tiles, 0, 1 + (i - ctx_tiles) // per_seq)


def _mod_spec(layer, which, tm, n_grid):
    if n_grid == 1:
        return pl.BlockSpec((1, 1, D_MODEL), lambda i: (layer * MOD_ROWS + _mod_row(i, tm), 0, which))
    return pl.BlockSpec((1, 1, D_MODEL), lambda i, j: (layer * MOD_ROWS + _mod_row(i, tm), 0, which))


def _dot(a, b):
    return jnp.dot(a, b, preferred_element_type=F32)


def _dot_nt(a, b):
    return lax.dot_general(a, b, (((1,), (1,)), ((), ())), preferred_element_type=F32)


def _dot_tn(a, b):
    return lax.dot_general(a, b, (((0,), (0,)), ((), ())), preferred_element_type=F32)


def _silu(x):
    return x * jax.nn.sigmoid(x)


def _mod_kernel(c_ref, w_ref, b_ref, o_ref):
    s = _silu(c_ref[...])
    hi = s.astype(BF16)
    lo = (s - hi.astype(F32)).astype(BF16)
    w = w_ref[...].astype(BF16)
    o_ref[...] = _dot(hi, w) + _dot(lo, w) + b_ref[...]


def _modulation(c_all, w_mod, b_mod):
    tn = 1024
    n = N_MOD * D_MODEL
    return pl.pallas_call(
        _mod_kernel,
        grid=(DEPTH, n // tn),
        in_specs=[pl.BlockSpec((MOD_ROWS, D_MODEL), lambda l, j: (0, 0)),
                  pl.BlockSpec((None, D_MODEL, tn), lambda l, j: (l, 0, j)),
                  pl.BlockSpec((None, 1, tn), lambda l, j: (l, 0, j))],
        out_specs=pl.BlockSpec((None, MOD_ROWS, tn), lambda l, j: (l, 0, j)),
        out_shape=jax.ShapeDtypeStruct((DEPTH, MOD_ROWS, n), F32),
        compiler_params=pltpu.CompilerParams(dimension_semantics=("arbitrary", "arbitrary"),
                                             vmem_limit_bytes=VMEM_LIMIT),
        name="modulation",
    )(c_all, w_mod, b_mod.reshape(DEPTH, 1, n))


def _norm_modulate(x, g, shift, scale):
    r = lax.rsqrt(jnp.mean(x * x, axis=-1, keepdims=True) + EPS)
    return (x * r * g) * (1.0 + scale) + shift


FFN_TM = 1024
FFN_VMEM_LIMIT = 60 * 1024 * 1024
FFN_TF = 512
FFN_TN = 256
FFN_NF = D_FF // FFN_TF
FFN_NN = D_MODEL // FFN_TN
MXU_W = 256


def _ffn_kernel(x_ref, xt_ref, g_ref, shift_ref, scale_ref, gate_ref, wg_ref, wu_ref, wd_ref, o_ref, h_ref, a_ref):
    j = pl.program_id(1)

    @pl.when(j == 0)
    def _():
        h_ref[...] = _norm_modulate(x_ref[...], g_ref[...], shift_ref[0], scale_ref[0]).astype(BF16)

    @pl.when(j < FFN_NF)
    def _():
        h = h_ref[...]
        base = pl.multiple_of(j * FFN_TF, FFN_TF)
        for s in range(FFN_TF // MXU_W):
            cols = slice(s * MXU_W, (s + 1) * MXU_W)
            g = _dot(h, wg_ref[:, cols])
            u = _dot(h, wu_ref[:, cols])
            a_ref[:, pl.ds(base + s * MXU_W, MXU_W)] = (_silu(g) * u).astype(BF16)

    @pl.when(j >= FFN_NF)
    def _():
        o_ref[...] = xt_ref[...] + 0.5 * gate_ref[0] * _dot(a_ref[...], wd_ref[...])


def _ffn(x, mod, norm_g, wg, wu, wd, layer, mod_base, tm=FFN_TM):
    def up_tile(i, j):
        return (layer, 0, jnp.minimum(j, FFN_NF - 1))

    def down_tile(j):
        return jnp.maximum(j - FFN_NF, 0)
    gate_blocks = D_MODEL // FFN_TN
    return pl.pallas_call(
        _ffn_kernel,
        grid=(ROWS // tm, FFN_NF + FFN_NN),
        in_specs=[pl.BlockSpec((tm, D_MODEL), lambda i, j: (i, 0)),
                  pl.BlockSpec((tm, FFN_TN), lambda i, j: (i, down_tile(j))),
                  pl.BlockSpec((None, 1, D_MODEL), lambda i, j: (layer, 0, 0)),
                  _mod_spec(layer, mod_base + 0, tm, 2),
                  _mod_spec(layer, mod_base + 1, tm, 2),
                  pl.BlockSpec((1, 1, FFN_TN), lambda i, j: (layer * MOD_ROWS + _mod_row(i, tm), 0,
                                                             (mod_base + 2) * gate_blocks + down_tile(j))),
                  pl.BlockSpec((None, D_MODEL, FFN_TF), up_tile),
                  pl.BlockSpec((None, D_MODEL, FFN_TF), up_tile),
                  pl.BlockSpec((None, D_FF, FFN_TN), lambda i, j: (layer, 0, down_tile(j)))],
        out_specs=pl.BlockSpec((tm, FFN_TN), lambda i, j: (i, down_tile(j))),
        out_shape=jax.ShapeDtypeStruct((ROWS, D_MODEL), F32),
        scratch_shapes=[pltpu.VMEM((tm, D_MODEL), BF16), pltpu.VMEM((tm, D_FF), BF16)],
        compiler_params=pltpu.CompilerParams(dimension_semantics=("parallel", "arbitrary"),
                                             vmem_limit_bytes=FFN_VMEM_LIMIT),
        name="ffn",
    )(x, x, norm_g, mod, mod, mod, wg, wu, wd)


def _in_proj_kernel(x_ref, g_ref, shift_ref, scale_ref, w_ref, ws_ref, o_ref, os_ref, h_ref):
    @pl.when(pl.program_id(1) == 0)
    def _():
        h_ref[...] = _norm_modulate(x_ref[...], g_ref[...], shift_ref[0], scale_ref[0]).astype(BF16)
        os_ref[...] = _dot(h_ref[...], ws_ref[...])

    o_ref[...] = _dot(h_ref[...], w_ref[...]).astype(o_ref.dtype)


def _in_proj(x, mod, norm_g, w_main, w_side, layer, tm=TOK_TM):
    return pl.pallas_call(
        _in_proj_kernel,
        grid=(ROWS // tm, MAIN_W // MAIN_TN),
        in_specs=[pl.BlockSpec((tm, D_MODEL), lambda i, j: (i, 0)),
                  pl.BlockSpec((None, 1, D_MODEL), lambda i, j: (layer, 0, 0)),
                  _mod_spec(layer, 3, tm, 2),
                  _mod_spec(layer, 4, tm, 2),
                  pl.BlockSpec((None, D_MODEL, MAIN_TN), lambda i, j: (layer, 0, j)),
                  pl.BlockSpec((None, D_MODEL, SIDE_W), lambda i, j: (layer, 0, 0))],
        out_specs=[pl.BlockSpec((tm, MAIN_TN), lambda i, j: (i, j)),
                   pl.BlockSpec((tm, SIDE_W), lambda i, j: (i, 0))],
        out_shape=[jax.ShapeDtypeStruct((ROWS, MAIN_W), BF16),
                   jax.ShapeDtypeStruct((ROWS, SIDE_W), F32)],
        scratch_shapes=[pltpu.VMEM((tm, D_MODEL), BF16)],
        compiler_params=pltpu.CompilerParams(dimension_semantics=("parallel", "arbitrary"),
                                             vmem_limit_bytes=VMEM_LIMIT),
        name="in_proj",
    )(x, norm_g, mod, mod, w_main, w_side)


def _out_proj_kernel(x_ref, *refs, ctx_tiles):
    ctx_refs, lat_refs, (gate_ref, w_ref, o_ref) = refs[0:4], refs[4:8], refs[8:]

    def run(y_refs):
        acc = _dot(y_refs[0][...], w_ref[0:GROUP_W, :])
        for g in range(1, 4):
            acc += _dot(y_refs[g][...], w_ref[g * GROUP_W:(g + 1) * GROUP_W, :])
        o_ref[...] = x_ref[...] + gate_ref[0] * acc

    @pl.when(pl.program_id(0) < ctx_tiles)
    def _():
        run(ctx_refs)

    @pl.when(pl.program_id(0) >= ctx_tiles)
    def _():
        run(lat_refs)


def _out_proj(x, ys_ctx, ys_lat, mod, w, layer, tm=TOK_TM):
    ctx_tiles = CTX_ROWS // tm
    row = pl.BlockSpec((tm, D_MODEL), lambda i: (i, 0))
    cspec = pl.BlockSpec((tm, GROUP_W), lambda i: (jnp.minimum(i, ctx_tiles - 1), 0))
    lspec = pl.BlockSpec((tm, GROUP_W), lambda i: (jnp.maximum(i - ctx_tiles, 0), 0))
    return pl.pallas_call(
        functools.partial(_out_proj_kernel, ctx_tiles=ctx_tiles),
        grid=(ROWS // tm,),
        in_specs=[row] + [cspec] * 4 + [lspec] * 4 + [
            _mod_spec(layer, 5, tm, 1),
            pl.BlockSpec((None, D_MODEL, D_MODEL), lambda i: (layer, 0, 0))],
        out_specs=row,
        out_shape=jax.ShapeDtypeStruct((ROWS, D_MODEL), F32),
        compiler_params=pltpu.CompilerParams(dimension_semantics=("parallel",),
                                             vmem_limit_bytes=VMEM_LIMIT),
        name="out_proj",
    )(x, *ys_ctx, *ys_lat, mod, w)


def _rope_tables(T):
    n_rows = T // GRID_W
    row = jnp.repeat(jnp.arange(n_rows, dtype=F32), GRID_W)
    col = jnp.tile(jnp.arange(GRID_W, dtype=F32), n_rows)
    d_axis = MLA_ROPE // 2
    inv = ROPE_BASE ** (-jnp.arange(0, d_axis, 2, dtype=F32) / d_axis)
    ar, ac = row[:, None] * inv, col[:, None] * inv
    cos = jnp.concatenate([jnp.cos(ar), jnp.cos(ar), jnp.cos(ac), jnp.cos(ac)], axis=-1)
    sin = jnp.concatenate([-jnp.sin(ar), jnp.sin(ar), -jnp.sin(ac), jnp.sin(ac)], axis=-1)
    return jnp.tile(cos, (1, 2)), jnp.tile(sin, (1, 2))


def _swap16(x):
    lane = lax.broadcasted_iota(jnp.int32, x.shape, 1)
    up = pltpu.roll(x, LANE - 16, 1)
    down = pltpu.roll(x, 16, 1)
    return jnp.where((lane % 32) < 16, up, down)


def _rope(x, cos, sin):
    return x * cos + _swap16(x) * sin


def _mla_q_kernel(c0_ref, c1_ref, c2_ref, gl_ref, w_ref, gq_ref, cos_ref, sin_ref, o_ref):
    cs = [r[...].astype(F32) for r in (c0_ref, c1_ref, c2_ref)]
    ss = sum(jnp.sum(c * c, axis=-1, keepdims=True) for c in cs)
    r = lax.rsqrt(ss / MLA_Q_RANK + EPS)
    q = sum(_dot((cs[i] * r * gl_ref[:, i * LANE:(i + 1) * LANE]).astype(BF16),
                 w_ref[i * LANE:(i + 1) * LANE, :]) for i in range(3))
    cos, sin = cos_ref[...], sin_ref[...]
    scale = MLA_QK ** -0.5
    for h in range(MLA_HEADS):
        a = q[:, h * MLA_HEAD_W:h * MLA_HEAD_W + LANE]
        b = q[:, h * MLA_HEAD_W + LANE:(h + 1) * MLA_HEAD_W]
        ssq = jnp.sum(a * a, axis=-1, keepdims=True) + jnp.sum(b * b, axis=-1, keepdims=True)
        rh = lax.rsqrt(ssq / MLA_QK + EPS) * scale
        o_ref[:, h * MLA_HEAD_W:h * MLA_HEAD_W + LANE] = (a * rh * gq_ref[:, :LANE]).astype(BF16)
        o_ref[:, h * MLA_HEAD_W + LANE:(h + 1) * MLA_HEAD_W] = _rope(b * rh * gq_ref[:, LANE:], cos, sin).astype(BF16)


def _rope_tile_index(i, tm):
    ctx_tiles = CTX_ROWS // tm
    return jnp.where(i < ctx_tiles, 0, 1 + (i - ctx_tiles) % (DEC_SEQ // tm))


def _mla_q(um, gl, wq, gq, cos_all, sin_all, layer, tm=TOK_TM):
    cq = MAIN_MLA_CQ // LANE
    tab = pl.BlockSpec((tm, LANE), lambda i: (_rope_tile_index(i, tm), 0))
    return pl.pallas_call(
        _mla_q_kernel,
        grid=(ROWS // tm,),
        in_specs=[pl.BlockSpec((tm, LANE), lambda i: (i, cq)),
                  pl.BlockSpec((tm, LANE), lambda i: (i, cq + 1)),
                  pl.BlockSpec((tm, LANE), lambda i: (i, cq + 2)),
                  pl.BlockSpec((None, 1, MLA_Q_RANK), lambda i: (layer, 0, 0)),
                  pl.BlockSpec((None, MLA_Q_RANK, MLA_HEADS * MLA_HEAD_W), lambda i: (layer, 0, 0)),
                  pl.BlockSpec((None, 1, MLA_HEAD_W), lambda i: (layer, 0, 0)),
                  tab, tab],
        out_specs=pl.BlockSpec((tm, MLA_HEADS * MLA_HEAD_W), lambda i: (i, 0)),
        out_shape=jax.ShapeDtypeStruct((ROWS, MLA_HEADS * MLA_HEAD_W), BF16),
        compiler_params=pltpu.CompilerParams(dimension_semantics=("parallel",), vmem_limit_bytes=VMEM_LIMIT),
        name="mla_q",
    )(um, um, um, gl, wq, gq, cos_all, sin_all)


def _mla_kv_kernel(s_ref, gl_ref, w_ref, gk_ref, cos_ref, sin_ref, lat_ref, k_ref, v_ref, *, normalize):
    ckv = s_ref[:, :LANE]
    kpe = s_ref[:, LANE:]
    if normalize:
        ckv = ckv * lax.rsqrt(jnp.mean(ckv * ckv, axis=-1, keepdims=True) + EPS) * gl_ref[...]
    lat_ref[:, :LANE] = ckv
    lat_ref[:, LANE:] = kpe
    kv = _dot(ckv.astype(BF16), w_ref[...])
    ss_pe = jnp.sum(kpe * kpe, axis=-1, keepdims=True)
    cos, sin = cos_ref[...], sin_ref[...]
    for h in range(MLA_HEADS):
        a = kv[:, h * LANE:(h + 1) * LANE]
        rh = lax.rsqrt((jnp.sum(a * a, axis=-1, keepdims=True) + ss_pe) / MLA_QK + EPS)
        k_ref[:, h * MLA_HEAD_W:h * MLA_HEAD_W + LANE] = (a * rh * gk_ref[:, :LANE]).astype(BF16)
        k_ref[:, h * MLA_HEAD_W + LANE:(h + 1) * MLA_HEAD_W] = _rope(kpe * rh * gk_ref[:, LANE:], cos, sin).astype(BF16)
    v_ref[...] = kv[:, MLA_HEADS * LANE:].astype(BF16)


def _mla_kv(src, gl, wkv, gk, cos_all, sin_all, layer, *, normalize, tab_index, tm=TOK_TM):
    rows = src.shape[0]
    tab = pl.BlockSpec((tm, LANE), lambda i: (tab_index(i, tm), 0))
    return pl.pallas_call(
        functools.partial(_mla_kv_kernel, normalize=normalize),
        grid=(rows // tm,),
        in_specs=[pl.BlockSpec((tm, 2 * LANE), lambda i: (i, 0)),
                  pl.BlockSpec((None, 1, MLA_KV_RANK), lambda i: (layer, 0, 0)),
                  pl.BlockSpec((None, MLA_KV_RANK, 2 * MLA_HEADS * LANE), lambda i: (layer, 0, 0)),
                  pl.BlockSpec((None, 1, MLA_HEAD_W), lambda i: (layer, 0, 0)),
                  tab, tab],
        out_specs=[pl.BlockSpec((tm, 2 * LANE), lambda i: (i, 0)),
                   pl.BlockSpec((tm, MLA_HEADS * MLA_HEAD_W), lambda i: (i, 0)),
                   pl.BlockSpec((tm, GROUP_W), lambda i: (i, 0))],
        out_shape=[jax.ShapeDtypeStruct((rows, 2 * LANE), F32),
                   jax.ShapeDtypeStruct((rows, MLA_HEADS * MLA_HEAD_W), BF16),
                   jax.ShapeDtypeStruct((rows, GROUP_W), BF16)],
        compiler_params=pltpu.CompilerParams(dimension_semantics=("parallel",), vmem_limit_bytes=VMEM_LIMIT),
        name="mla_kv",
    )(src, gl, wkv, gk, cos_all, sin_all)


ATTN_TQ_LAT = 256


def _attn_kernel(q_ref, *refs):
    o_ref = refs[-1]
    kv = [(refs[i], refs[i + 1]) for i in range(0, len(refs) - 1, 2)]
    q = q_ref[...]
    scores = [_dot_nt(q, k_ref[...]) for k_ref, _ in kv]
    m = functools.reduce(jnp.maximum, [jnp.max(s, axis=-1, keepdims=True) for s in scores])
    ps = [jnp.exp(s - m) for s in scores]
    l = sum(jnp.sum(p, axis=-1, keepdims=True) for p in ps)
    o = sum(_dot(p.astype(BF16), v_ref[...]) for p, (_, v_ref) in zip(ps, kv))
    o_ref[...] = (o / l).astype(o_ref.dtype)


def _attention(q, sources, n_seq, t, tq, q_row0):
    nq = t // tq
    in_specs = [pl.BlockSpec((tq, MLA_HEAD_W), lambda b, h, i: (q_row0 // tq + b * nq + i, h))]
    args = [q]
    for k, v, s_len, row0 in sources:
        in_specs += [pl.BlockSpec((s_len, MLA_HEAD_W), lambda b, h, i, o=row0 // s_len: (o + b, h)),
                     pl.BlockSpec((s_len, MLA_V), lambda b, h, i, o=row0 // s_len: (o + b, h))]
        args += [k, v]
    return pl.pallas_call(
        _attn_kernel,
        grid=(n_seq, MLA_HEADS, nq),
        in_specs=in_specs,
        out_specs=pl.BlockSpec((tq, MLA_V), lambda b, h, i: (b * nq + i, h)),
        out_shape=jax.ShapeDtypeStruct((n_seq * t, GROUP_W), BF16),
        compiler_params=pltpu.CompilerParams(dimension_semantics=("parallel", "parallel", "arbitrary"),
                                             vmem_limit_bytes=VMEM_LIMIT),
        name="mla_attention",
    )(*args)


def _mla_weights(mla_q_lat_gain, mla_w_q_up, mla_q_gain, mla_kv_lat_gain, mla_w_kv_up, mla_k_gain):
    qcol = np.full((MLA_HEADS * MLA_HEAD_W,), -1, np.int64)
    kvcol = np.zeros((2 * MLA_HEADS * LANE,), np.int64)
    for h in range(MLA_HEADS):
        qcol[h * MLA_HEAD_W:h * MLA_HEAD_W + MLA_QK] = h * MLA_QK + np.arange(MLA_QK)
        kvcol[h * LANE:(h + 1) * LANE] = h * (MLA_NOPE + MLA_V) + np.arange(MLA_NOPE)
        kvcol[(MLA_HEADS + h) * LANE:(MLA_HEADS + h + 1) * LANE] = h * (MLA_NOPE + MLA_V) + MLA_NOPE + np.arange(MLA_V)
    pad = jnp.zeros((DEPTH, MLA_HEAD_W - MLA_QK), F32)
    return dict(
        gl=mla_q_lat_gain.reshape(DEPTH, 1, MLA_Q_RANK),
        wq=_gather_columns(mla_w_q_up, qcol).astype(BF16),
        gq=jnp.concatenate([mla_q_gain, pad], axis=-1).reshape(DEPTH, 1, MLA_HEAD_W),
        gkv=mla_kv_lat_gain.reshape(DEPTH, 1, MLA_KV_RANK),
        wkv=_gather_columns(mla_w_kv_up, kvcol).astype(BF16),
        gk=jnp.concatenate([mla_k_gain, pad], axis=-1).reshape(DEPTH, 1, MLA_HEAD_W))


def _mla_layer(um, us, cache_l, mw, cos_all, sin_all, layer):
    q = _mla_q(um, mw['gl'], mw['wq'], mw['gq'], cos_all, sin_all, layer)
    kv_lat, k, v = _mla_kv(us, mw['gkv'], mw['wkv'], mw['gk'], cos_all, sin_all, layer,
                           normalize=True, tab_index=_rope_tile_index)
    cache2 = jnp.pad(cache_l.reshape(DEC_BATCH * PAST_LEN, MLA_CACHE_W), ((0, 0), (0, 2 * LANE - MLA_CACHE_W)))
    _, k_c, v_c = _mla_kv(cache2, mw['gkv'], mw['wkv'], mw['gk'], cos_all, sin_all, layer,
                          normalize=False, tab_index=lambda i, tm: 0)
    y_ctx = _attention(q, [(k, v, SEQ, 0)], BATCH, SEQ, SEQ, 0)
    y_lat = _attention(q, [(k, v, DEC_SEQ, CTX_ROWS), (k_c, v_c, PAST_LEN, 0)], DEC_BATCH, DEC_SEQ, ATTN_TQ_LAT,
                       CTX_ROWS)
    return y_ctx, y_lat, kv_lat


def _ret_kernel(*refs, T, C, rope, has_h0, has_state):
    it = iter(refs)
    q_ref, k_ref, v_ref, g_ref = next(it), next(it), next(it), next(it)
    cos_ref, sin_ref = (next(it), next(it)) if rope else (None, None)
    lg_ref, gain_ref = next(it), next(it)
    h0_ref = next(it) if has_h0 else None
    y_ref = next(it)
    st_ref = next(it) if has_state else None
    qs_ref, ks_ref, oacc_ref, dm_ref, tab_ref, hst_ref = (next(it) for _ in range(6))

    nc = T // C
    ii = lax.broadcasted_iota(jnp.int32, (C, C), 0)
    jj = lax.broadcasted_iota(jnp.int32, (C, C), 1)
    dif = (ii - jj).astype(F32)
    lane = lax.broadcasted_iota(jnp.int32, (C, LANE), 1)
    rowi = lax.broadcasted_iota(jnp.int32, (C, LANE), 0).astype(F32)
    m_lo = lane < RET_QK
    hrow = lax.broadcasted_iota(jnp.int32, (LANE, LANE), 0) < RET_QK

    for h in range(RET_HEADS):
        lf, lb = lg_ref[0, h], lg_ref[1, h]
        dm_ref[h] = (jnp.where(dif >= 0, jnp.exp(lf * jnp.maximum(dif, 0.0)), 0.0)
                     + jnp.where(dif <= 0, jnp.exp(lb * jnp.maximum(-dif, 0.0)), 0.0))
    for p in range(2):
        lf = jnp.where(m_lo, lg_ref[0, 2 * p], lg_ref[0, 2 * p + 1])
        lb = jnp.where(m_lo, lg_ref[1, 2 * p], lg_ref[1, 2 * p + 1])
        tab_ref[p, 0] = jnp.exp(lf * (rowi + 1.0))
        tab_ref[p, 1] = jnp.exp(lf * (C - 1.0 - rowi))
        tab_ref[p, 2] = jnp.exp(lb * (C - rowi))
        tab_ref[p, 3] = jnp.exp(lb * rowi)
    if has_h0:
        hst_ref[...] = h0_ref[...]
    else:
        hst_ref[...] = jnp.zeros_like(hst_ref)

    def chunk_decay(d, p):
        return jnp.exp(jnp.where(hrow, lg_ref[d, 2 * p], lg_ref[d, 2 * p + 1]) * float(C))

    def forward(c, carry):
        r0 = pl.multiple_of(c * C, C)
        rows = pl.ds(r0, C)
        for p in range(2):
            cols = slice(p * LANE, (p + 1) * LANE)
            qp = q_ref[rows, cols].astype(F32)
            kp = k_ref[rows, cols].astype(F32) * (RET_QK ** -0.5)
            if rope:
                qp = _rope(qp, cos_ref[rows, :], sin_ref[rows, :])
                kp = _rope(kp, cos_ref[rows, :], sin_ref[rows, :])
            qs_ref[rows, cols] = qp.astype(BF16)
            ks_ref[rows, cols] = kp.astype(BF16)
            kpb = kp.astype(BF16)
            hp = hst_ref[0, p]
            hpb = hp.astype(BF16)
            upd = jnp.zeros((LANE, LANE), F32)
            for e in range(2):
                h = 2 * p + e
                mh = m_lo if e == 0 else jnp.logical_not(m_lo)
                hcols = slice(h * LANE, (h + 1) * LANE)
                qm = jnp.where(mh, qp, 0.0)
                s = _dot_nt(qm.astype(BF16), kpb)
                vh = v_ref[rows, hcols]
                o = _dot((s * dm_ref[h]).astype(BF16), vh)
                o += _dot((qm * tab_ref[p, 0]).astype(BF16), hpb)
                oacc_ref[rows, hcols] = o
                upd += _dot_tn(jnp.where(mh, kp * tab_ref[p, 1], 0.0).astype(BF16), vh)
            hst_ref[0, p] = hp * chunk_decay(0, p) + upd
        return carry

    lax.fori_loop(0, nc, forward, 0)

    def backward(t, carry):
        c = nc - 1 - t
        r0 = pl.multiple_of(c * C, C)
        rows = pl.ds(r0, C)
        for p in range(2):
            cols = slice(p * LANE, (p + 1) * LANE)
            qp = qs_ref[rows, cols].astype(F32)
            kp = ks_ref[rows, cols].astype(F32)
            hp = hst_ref[1, p]
            hpb = hp.astype(BF16)
            upd = jnp.zeros((LANE, LANE), F32)
            for e in range(2):
                h = 2 * p + e
                mh = m_lo if e == 0 else jnp.logical_not(m_lo)
                hcols = slice(h * LANE, (h + 1) * LANE)
                vh = v_ref[rows, hcols]
                o = oacc_ref[rows, hcols] + _dot(jnp.where(mh, qp * tab_ref[p, 2], 0.0).astype(BF16), hpb)
                upd += _dot_tn(jnp.where(mh, kp * tab_ref[p, 3], 0.0).astype(BF16), vh)
                oc = o - jnp.mean(o, axis=-1, keepdims=True)
                yn = oc * lax.rsqrt(jnp.mean(oc * oc, axis=-1, keepdims=True) + EPS) * gain_ref[:, hcols]
                y_ref[rows, hcols] = (_silu(g_ref[rows, hcols].astype(F32)) * yn).astype(y_ref.dtype)
            hst_ref[1, p] = hp * chunk_decay(1, p) + upd
        return carry

    lax.fori_loop(0, nc, backward, 0)
    if has_state:
        st_ref[...] = hst_ref[...]


def _retention(um, lg, gain, h0, rope_tabs, n_seq, T, row_block0):
    C = RET_CHUNK
    rope = rope_tabs is not None
    has_h0 = h0 is not None
    has_state = not has_h0

    def col(off, w):
        return pl.BlockSpec((T, w), lambda b: (row_block0 + b, off // w))
    in_specs = [col(MAIN_RET_Q, 2 * LANE), col(MAIN_RET_K, 2 * LANE), col(MAIN_RET_V, GROUP_W), col(MAIN_RET_G, GROUP_W)]
    args = [um, um, um, um]
    if rope:
        in_specs += [pl.BlockSpec((T, LANE), lambda b: (0, 0))] * 2
        args += list(rope_tabs)
    in_specs += [pl.BlockSpec(memory_space=pltpu.SMEM), pl.BlockSpec((1, GROUP_W), lambda b: (0, 0))]
    args += [lg, gain]
    st_spec = pl.BlockSpec((None, 2, 2, LANE, LANE), lambda b: (b, 0, 0, 0, 0))
    if has_h0:
        in_specs.append(st_spec)
        args.append(h0)
    out_specs = [pl.BlockSpec((T, GROUP_W), lambda b: (b, 0))]
    out_shape = [jax.ShapeDtypeStruct((n_seq * T, GROUP_W), BF16)]
    if has_state:
        out_specs.append(st_spec)
        out_shape.append(jax.ShapeDtypeStruct((n_seq, 2, 2, LANE, LANE), F32))
    res = pl.pallas_call(
        functools.partial(_ret_kernel, T=T, C=C, rope=rope, has_h0=has_h0, has_state=has_state),
        grid=(n_seq,),
        in_specs=in_specs,
        out_specs=out_specs,
        out_shape=out_shape,
        scratch_shapes=[pltpu.VMEM((T, 2 * LANE), BF16), pltpu.VMEM((T, 2 * LANE), BF16),
                        pltpu.VMEM((T, GROUP_W), F32), pltpu.VMEM((RET_HEADS, C, C), F32),
                        pltpu.VMEM((2, 4, C, LANE), F32), pltpu.VMEM((2, 2, LANE, LANE), F32)],
        compiler_params=pltpu.CompilerParams(dimension_semantics=("parallel",), vmem_limit_bytes=VMEM_LIMIT),
        name="retention",
    )(*args)
    return (res[0], res[1]) if has_state else (res[0], None)


def _split3(x):
    x1 = x.astype(BF16)
    r1 = x - x1.astype(F32)
    x2 = r1.astype(BF16)
    x3 = (r1 - x2.astype(F32)).astype(BF16)
    return x1, x2, x3


def _dot01_l(m, x):
    x1, x2, x3 = _split3(x)
    return _dot(m, x1) + _dot(m, x2) + _dot(m, x3)


def _dot01_r(x, m):
    x1, x2, x3 = _split3(x)
    return _dot(x1, m) + _dot(x2, m) + _dot(x3, m)


def _softplus(x):
    return jnp.maximum(x, 0.0) + jnp.log(1.0 + jnp.exp(-jnp.abs(x)))


def _head_expanders():
    e8 = np.zeros((LANE, SSD_HEADS * LANE), np.float32)
    e64 = np.zeros((LANE, GROUP_W), np.float32)
    for h in range(SSD_HEADS):
        e8[h, h * LANE:(h + 1) * LANE] = 1.0
        e64[h, h * SSD_HEAD_DIM:(h + 1) * SSD_HEAD_DIM] = 1.0
    return jnp.asarray(e8, BF16), jnp.asarray(e64, BF16)


def _ssd_kernel(*refs, T, has_h0, has_state):
    it = iter(refs)
    z_ref, xs_ref, bc_ref, dt_ref = next(it), next(it), next(it), next(it)
    wx_ref, wbc_ref, bx_ref, bbc_ref = next(it), next(it), next(it), next(it)
    a_ref, dtb_ref, d_ref, gain_ref, e8_ref, e64_ref = (next(it) for _ in range(6))
    h0_ref = next(it) if has_h0 else None
    y_ref = next(it)
    st_ref = next(it) if has_state else None
    xc_s, bcs_s, oacc_ref, hst_ref = next(it), next(it), next(it), next(it)

    C = SSD_CHUNK
    nc = T // C
    ii = lax.broadcasted_iota(jnp.int32, (C, C), 0)
    jj = lax.broadcasted_iota(jnp.int32, (C, C), 1)
    tril, triu = ii >= jj, ii <= jj
    tril_b, triu_b = tril.astype(BF16), triu.astype(BF16)
    lane = lax.broadcasted_iota(jnp.int32, (C, LANE), 1)
    m_lo = lane < SSD_STATE
    masks = (m_lo, jnp.logical_not(m_lo))
    hrow = lax.broadcasted_iota(jnp.int32, (LANE, LANE), 0) < SSD_STATE

    if has_h0:
        hst_ref[...] = h0_ref[...]
    else:
        hst_ref[...] = jnp.zeros_like(hst_ref)

    def decays(rows, d):
        dtv = _softplus(dt_ref[rows, :] + dtb_ref[d:d + 1, :])
        la = dtv * a_ref[d:d + 1, :]
        return dtv, _dot01_l(tril_b if d == 0 else triu_b, la)

    def conv(ref, w_ref, b_ref, c, r0):
        width = ref.shape[1]
        x = ref[pl.ds(r0, C), :].astype(F32)
        p0 = pl.multiple_of(jnp.maximum(r0 - 16, 0), 16)
        n0 = pl.multiple_of(jnp.minimum(r0 + C, T - 16), 16)
        prev_row = jnp.where(c > 0, ref[pl.ds(p0, 16), :].astype(F32)[15:16], 0.0)
        next_row = jnp.where(c < nc - 1, ref[pl.ds(n0, 16), :].astype(F32)[0:1], 0.0)
        rowi = lax.broadcasted_iota(jnp.int32, (C, width), 0)
        prev = jnp.where(rowi == 0, prev_row, pltpu.roll(x, 1, 0))
        nxt = jnp.where(rowi == C - 1, next_row, pltpu.roll(x, C - 1, 0))
        return _silu(prev * w_ref[0:1, :] + x * w_ref[1:2, :] + nxt * w_ref[2:3, :] + b_ref[...])

    def decay_matrix(qrow, pp):
        return jnp.where(hrow, qrow[:, pp * LANE:(pp + 1) * LANE],
                         qrow[:, (2 + pp) * LANE:(3 + pp) * LANE])

    def forward(c, carry):
        r0 = pl.multiple_of(c * C, C)
        rows = pl.ds(r0, C)
        xc = conv(xs_ref, wx_ref, bx_ref, c, r0)
        bcv = conv(bc_ref, wbc_ref, bbc_ref, c, r0)
        xc_s[rows, :] = xc.astype(BF16)
        bcs_s[rows, :] = bcv.astype(BF16)
        bmat, cmat = bcv[:, :LANE], bcv[:, LANE:]
        bmb = bmat.astype(BF16)
        dtf, bf = decays(rows, 0)
        dtb, bb = decays(rows, 1)
        bf_t, dtf_t, bb_t, dtb_t = bf.T, dtf.T, bb.T, dtb.T
        colf = _dot01_r(bf, e8_ref[...])
        colb = _dot01_r(bb, e8_ref[...])
        qdf = _dot01_r(jnp.exp(bf), e64_ref[...])
        kwf = _dot01_r(dtf * jnp.exp(bf[C - 1:C, :] - bf), e64_ref[...])
        cms = [jnp.where(masks[g], cmat, 0.0).astype(BF16) for g in range(SSD_GROUPS)]
        bms = [jnp.where(masks[g], bmat, 0.0).astype(BF16) for g in range(SSD_GROUPS)]
        scores = [_dot_nt(cms[g], bmb) for g in range(SSD_GROUPS)]
        for pp in range(2):
            hp = hst_ref[0, pp]
            hpb = hp.astype(BF16)
            upd = jnp.zeros((LANE, LANE), F32)
            for g in range(SSD_GROUPS):
                cols = slice((2 * g + pp) * LANE, (2 * g + pp + 1) * LANE)
                xs_pair = xc[:, cols]
                xsb = xs_pair.astype(BF16)
                outs = []
                for e in range(2):
                    h = 4 * g + 2 * pp + e
                    hc = slice(h * LANE, (h + 1) * LANE)
                    ef = jnp.where(tril, jnp.exp(jnp.minimum(colf[:, hc] - bf_t[h:h + 1, :], 0.0)), 0.0) * dtf_t[h:h + 1, :]
                    eb = jnp.where(triu, jnp.exp(jnp.minimum(colb[:, hc] - bb_t[h:h + 1, :], 0.0)), 0.0) * dtb_t[h:h + 1, :]
                    outs.append(_dot((scores[g] * (ef + eb)).astype(BF16), xsb))
                o = jnp.where(m_lo, outs[0], outs[1])
                o += _dot(cms[g], hpb) * qdf[:, cols]
                oacc_ref[rows, cols] = o
                upd += _dot_tn(bms[g], (xs_pair * kwf[:, cols]).astype(BF16))
            hst_ref[0, pp] = hp * decay_matrix(qdf[C - 1:C, :], pp) + upd
        return carry

    lax.fori_loop(0, nc, forward, 0)

    def backward(t, carry):
        c = nc - 1 - t
        r0 = pl.multiple_of(c * C, C)
        rows = pl.ds(r0, C)
        xc = xc_s[rows, :].astype(F32)
        bcv = bcs_s[rows, :]
        bmat, cmat = bcv[:, :LANE], bcv[:, LANE:]
        zero = jnp.zeros_like(bmat)
        dtb, bb = decays(rows, 1)
        qdb = _dot01_r(jnp.exp(bb), e64_ref[...])
        kwb = _dot01_r(dtb * jnp.exp(bb[0:1, :] - bb), e64_ref[...])
        blocks = {}
        for pp in range(2):
            hp = hst_ref[1, pp]
            hpb = hp.astype(BF16)
            upd = jnp.zeros((LANE, LANE), F32)
            for g in range(SSD_GROUPS):
                blk = 2 * g + pp
                cols = slice(blk * LANE, (blk + 1) * LANE)
                xs_pair = xc[:, cols]
                o = oacc_ref[rows, cols] + _dot(jnp.where(masks[g], cmat, zero), hpb) * qdb[:, cols]
                upd += _dot_tn(jnp.where(masks[g], bmat, zero), (xs_pair * kwb[:, cols]).astype(BF16))
                y = (o + d_ref[:, cols] * xs_pair) * _silu(z_ref[rows, cols].astype(F32))
                blocks[blk] = y
            hst_ref[1, pp] = hp * decay_matrix(qdb[0:1, :], pp) + upd
        for g in range(SSD_GROUPS):
            y0, y1 = blocks[2 * g], blocks[2 * g + 1]
            ss = jnp.sum(y0 * y0, axis=-1, keepdims=True) + jnp.sum(y1 * y1, axis=-1, keepdims=True)
            r = lax.rsqrt(ss / (2 * LANE) + EPS)
            for i, yb in enumerate((y0, y1)):
                cols = slice((2 * g + i) * LANE, (2 * g + i + 1) * LANE)
                y_ref[rows, cols] = (yb * r * gain_ref[:, cols]).astype(y_ref.dtype)
        return carry

    lax.fori_loop(0, nc, backward, 0)
    if has_state:
        st_ref[...] = hst_ref[...]


def _ssd_pack_state(st):
    n = st.shape[0]
    st = st.reshape(n, 2, SSD_GROUPS, 2, 2, SSD_STATE, SSD_HEAD_DIM)
    return st.transpose(0, 1, 3, 2, 5, 4, 6).reshape(n, 2, 2, LANE, LANE)


def _ssd_unpack_state(st):
    n = st.shape[0]
    st = st.reshape(n, 2, 2, SSD_GROUPS, SSD_STATE, 2, SSD_HEAD_DIM)
    return st.transpose(0, 1, 3, 2, 5, 4, 6).reshape(n, 2, SSD_HEADS, SSD_STATE, SSD_HEAD_DIM)


def _ssd_params(ssd_conv_w, ssd_conv_b, ssd_a_log, ssd_dt_bias, ssd_d, ssd_norm_g):
    def lanes8(v):
        return jnp.pad(v.astype(F32), ((0, 0), (0, 0), (0, LANE - SSD_HEADS)))
    e8, e64 = _head_expanders()
    return dict(wx=ssd_conv_w[:, :, :GROUP_W], wbc=ssd_conv_w[:, :, GROUP_W:],
                bx=ssd_conv_b[:, None, :GROUP_W], bbc=ssd_conv_b[:, None, GROUP_W:],
                a=lanes8(-jnp.exp(ssd_a_log.astype(F32))), dtb=lanes8(ssd_dt_bias),
                d=jnp.repeat(ssd_d, SSD_HEAD_DIM, axis=-1)[:, None, :], gain=ssd_norm_g[:, None, :],
                e8=e8, e64=e64)


def _ssd_scan(um, us, sp, layer, h0, n_seq, T, row_block0):
    has_h0 = h0 is not None
    has_state = not has_h0

    def col(off, w):
        return pl.BlockSpec((T, w), lambda b: (row_block0 + b, off // w))

    def per_layer(arr):
        return pl.BlockSpec((None,) + arr.shape[1:], lambda b: (layer,) + (0,) * (arr.ndim - 1))

    def const(arr):
        return pl.BlockSpec(arr.shape, lambda b: (0,) * arr.ndim)
    names = ('wx', 'wbc', 'bx', 'bbc', 'a', 'dtb', 'd', 'gain')
    in_specs = [col(MAIN_SSD_Z, GROUP_W), col(MAIN_SSD_XS, GROUP_W), col(MAIN_SSD_BC, 2 * LANE),
                pl.BlockSpec((T, LANE), lambda b: (row_block0 + b, SIDE_DT // LANE))]
    in_specs += [per_layer(sp[n]) for n in names] + [const(sp['e8']), const(sp['e64'])]
    args = [um, um, um, us] + [sp[n] for n in names] + [sp['e8'], sp['e64']]
    st_spec = pl.BlockSpec((None, 2, 2, LANE, LANE), lambda b: (b, 0, 0, 0, 0))
    if has_h0:
        in_specs.append(st_spec)
        args.append(h0)
    out_specs = [pl.BlockSpec((T, GROUP_W), lambda b: (b, 0))]
    out_shape = [jax.ShapeDtypeStruct((n_seq * T, GROUP_W), BF16)]
    if has_state:
        out_specs.append(st_spec)
        out_shape.append(jax.ShapeDtypeStruct((n_seq, 2, 2, LANE, LANE), F32))
    res = pl.pallas_call(
        functools.partial(_ssd_kernel, T=T, has_h0=has_h0, has_state=has_state),
        grid=(n_seq,),
        in_specs=in_specs,
        out_specs=out_specs,
        out_shape=out_shape,
        scratch_shapes=[pltpu.VMEM((T, GROUP_W), BF16), pltpu.VMEM((T, 2 * LANE), BF16),
                        pltpu.VMEM((T, GROUP_W), F32), pltpu.VMEM((2, 2, LANE, LANE), F32)],
        compiler_params=pltpu.CompilerParams(dimension_semantics=("parallel",), vmem_limit_bytes=VMEM_LIMIT),
        name="ssd",
    )(*args)
    return (res[0], res[1]) if has_state else (res[0], None)


GLA_MACRO = 128


def _log_sigmoid(x):
    return jnp.minimum(x, 0.0) - jnp.log(1.0 + jnp.exp(-jnp.abs(x)))


def _gla_kernel(*refs, T, has_h0, has_state):
    it = iter(refs)
    q_ref, k_ref, v_ref, r_ref, g1_ref = (next(it) for _ in range(5))
    wg_ref, bg_ref, gain_ref, ind_ref = (next(it) for _ in range(4))
    h0_ref = next(it) if has_h0 else None
    y_ref = next(it)
    st_ref = next(it) if has_state else None
    oacc_ref, hst_ref, qbuf, kbuf, bfbuf, bbbuf, vbuf, obuf = (next(it) for _ in range(8))

    C, L = GLA_MACRO, GLA_CHUNK
    nb = C // L
    nc = T // C
    ii = lax.broadcasted_iota(jnp.int32, (C, C), 0)
    jj = lax.broadcasted_iota(jnp.int32, (C, C), 1)
    same = (ii // L) == (jj // L)
    tri_l = jnp.logical_and(same, jj <= ii).astype(BF16)
    tri_u = jnp.logical_and(same, jj >= ii).astype(BF16)
    ones_b = same.astype(BF16)
    lane = lax.broadcasted_iota(jnp.int32, (C, LANE), 1)
    masks = (lane < GLA_QK, lane >= GLA_QK)
    HALF = 8
    ri8 = lax.broadcasted_iota(jnp.int32, (HALF, 2 * LANE), 0)
    bdmask = (lax.broadcasted_iota(jnp.int32, (C, nb * LANE), 0) // L
              == lax.broadcasted_iota(jnp.int32, (C, nb * LANE), 1) // LANE)

    if has_h0:
        hst_ref[...] = h0_ref[...]
    else:
        hst_ref[...] = jnp.zeros_like(hst_ref)

    def log_decay(rows, d):
        g1 = g1_ref[rows, :]
        hi = g1.astype(BF16)
        lo = (g1 - hi.astype(F32)).astype(BF16)
        logits = _dot(hi, wg_ref[d]) + _dot(lo, wg_ref[d]) + bg_ref[d]
        return _log_sigmoid(logits) / GLA_GATE_TEMP

    def recurrence(d, rows, qt, kt, dec, blocks):
        zero = jnp.zeros((), BF16)

        def block_diag(x):
            return jnp.where(bdmask, jnp.concatenate([x] * nb, axis=1), zero)
        upds = []
        for h in range(GLA_HEADS):
            p, e = divmod(h, 2)
            km = jnp.where(masks[e], kt[:, p * LANE:(p + 1) * LANE], 0.0).astype(BF16)
            upds.append(_dot_tn(v_ref[rows, h * LANE:(h + 1) * LANE], block_diag(km)))
        snaps = []
        for h in range(GLA_HEADS):
            lanes = slice((h // 2) * LANE, (h // 2 + 1) * LANE)
            ht = hst_ref[d, h]
            snap = [None] * nb
            for blk in blocks:
                snap[blk] = ht.astype(BF16)
                ht = ht * dec[blk * L:blk * L + 1, lanes] + upds[h][:, blk * LANE:(blk + 1) * LANE]
            hst_ref[d, h] = ht
            snaps.append(jnp.concatenate(snap, axis=1))
        outs = []
        for h in range(GLA_HEADS):
            p, e = divmod(h, 2)
            qm = jnp.where(masks[e], qt[:, p * LANE:(p + 1) * LANE], 0.0).astype(BF16)
            outs.append(_dot_nt(block_diag(qm), snaps[h]))
        return outs

    def intra_block(b0):
        pieces = []
        for j in range(L):
            kj, bfj, bbj = kbuf[b0 + j:b0 + j + 1, :], bfbuf[b0 + j:b0 + j + 1, :], bbbuf[b0 + j:b0 + j + 1, :]
            halves = []
            for s in range(L // HALF):
                rs = slice(b0 + s * HALF, b0 + (s + 1) * HALF)
                qk = qbuf[rs, :] * kj
                if j < s * HALF:
                    e = qk * jnp.exp(bfbuf[rs, :] - bfj)
                elif j >= (s + 1) * HALF:
                    e = qk * jnp.exp(bbbuf[rs, :] - bbj)
                else:
                    rel = ri8 + (s * HALF - j)
                    e = qk * jnp.exp(jnp.where(rel >= 0, bfbuf[rs, :] - bfj, bbbuf[rs, :] - bbj))
                    e = jnp.where(rel == 0, 2.0 * e, e)
                halves.append(e)
            pieces.append(jnp.concatenate(halves, axis=0).astype(BF16))
        spread = _dot(jnp.concatenate(pieces, axis=0), ind_ref[...])
        acc = spread[0:L, :] * vbuf[b0:b0 + 1, :]
        for j in range(1, L):
            acc += spread[j * L:(j + 1) * L, :] * vbuf[b0 + j:b0 + j + 1, :]
        return acc

    def forward(c, carry):
        r0 = pl.multiple_of(c * C, C)
        rows = pl.ds(r0, C)
        q = q_ref[rows, :].astype(F32) * (GLA_QK ** -0.5)
        k = k_ref[rows, :].astype(F32)
        la_f, la_b = log_decay(rows, 0), log_decay(rows, 1)
        bf, tot_f = _dot01_l(tri_l, la_f), _dot01_l(ones_b, la_f)
        bb = _dot01_l(tri_u, la_b)
        qbuf[...] = q
        kbuf[...] = k
        bfbuf[...] = bf
        bbbuf[...] = bb
        vbuf[...] = v_ref[rows, :].astype(F32)
        for blk in range(nb):
            obuf[blk * L:(blk + 1) * L, :] = intra_block(blk * L)
        qt, kt, dec = q * jnp.exp(bf), k * jnp.exp(tot_f - bf), jnp.exp(tot_f)
        inter = recurrence(0, rows, qt, kt, dec, range(nb))
        for h in range(GLA_HEADS):
            hcols = slice(h * LANE, (h + 1) * LANE)
            oacc_ref[rows, hcols] = obuf[:, hcols] + inter[h]
        return carry

    lax.fori_loop(0, nc, forward, 0)

    def backward(t, carry):
        c = nc - 1 - t
        r0 = pl.multiple_of(c * C, C)
        rows = pl.ds(r0, C)
        q = q_ref[rows, :].astype(F32) * (GLA_QK ** -0.5)
        k = k_ref[rows, :].astype(F32)
        la_b = log_decay(rows, 1)
        bb, tot_b = _dot01_l(tri_u, la_b), _dot01_l(ones_b, la_b)
        qt, kt, dec = q * jnp.exp(bb), k * jnp.exp(tot_b - bb), jnp.exp(tot_b)
        inter = recurrence(1, rows, qt, kt, dec, range(nb - 1, -1, -1))
        for h in range(GLA_HEADS):
            hcols = slice(h * LANE, (h + 1) * LANE)
            o = oacc_ref[rows, hcols] + inter[h]
            yn = o * lax.rsqrt(jnp.mean(o * o, axis=-1, keepdims=True) + EPS) * gain_ref[:, hcols]
            y_ref[rows, hcols] = (_silu(r_ref[rows, hcols].astype(F32)) * yn).astype(y_ref.dtype)
        return carry

    lax.fori_loop(0, nc, backward, 0)
    if has_state:
        st_ref[...] = hst_ref[...]


def _gla_pack_state(st):
    n = st.shape[0]
    ht = jnp.swapaxes(st, -1, -2)
    z = jnp.zeros_like(ht)
    even = jnp.concatenate([ht, z], axis=-1)
    odd = jnp.concatenate([z, ht], axis=-1)
    sel = (jnp.arange(GLA_HEADS) % 2 == 0)[None, None, :, None, None]
    return jnp.where(sel, even, odd)


def _gla_unpack_state(st):
    even, odd = st[..., :GLA_QK], st[..., GLA_QK:]
    sel = (jnp.arange(GLA_HEADS) % 2 == 0)[None, None, :, None, None]
    return jnp.swapaxes(jnp.where(sel, even, odd), -1, -2)


def _gla_params(gla_w_g2, gla_b_g, gla_norm_g):
    wg = jnp.zeros((DEPTH, 2, LANE, 2 * LANE), F32)
    g1_lane = SIDE_G1 - SIDE_DT
    wg = wg.at[:, :, g1_lane:g1_lane + GLA_GATE_RANK, :].set(gla_w_g2)
    ind = np.zeros((2 * LANE, GROUP_W), np.float32)
    for h in range(GLA_HEADS):
        ind[h * GLA_QK:(h + 1) * GLA_QK, h * GLA_V:(h + 1) * GLA_V] = 1.0
    return dict(wg=wg.astype(BF16), bg=gla_b_g[:, :, None, :], gain=gla_norm_g[:, None, :],
                ind=jnp.asarray(ind, BF16))


def _gla_scan(um, us, gp, layer, h0, n_seq, T, row_block0):
    has_h0 = h0 is not None
    has_state = not has_h0
    C = GLA_MACRO

    def col(off, w):
        return pl.BlockSpec((T, w), lambda b: (row_block0 + b, off // w))

    def per_layer(arr):
        return pl.BlockSpec((None,) + arr.shape[1:], lambda b: (layer,) + (0,) * (arr.ndim - 1))
    in_specs = [col(MAIN_GLA_Q, 2 * LANE), col(MAIN_GLA_K, 2 * LANE), col(MAIN_GLA_V, GROUP_W), col(MAIN_GLA_R, GROUP_W),
                pl.BlockSpec((T, LANE), lambda b: (row_block0 + b, SIDE_DT // LANE)),
                per_layer(gp['wg']), per_layer(gp['bg']), per_layer(gp['gain']),
                pl.BlockSpec(gp['ind'].shape, lambda b: (0, 0))]
    args = [um, um, um, um, us, gp['wg'], gp['bg'], gp['gain'], gp['ind']]
    st_spec = pl.BlockSpec((None, 2, GLA_HEADS, LANE, LANE), lambda b: (b, 0, 0, 0, 0))
    if has_h0:
        in_specs.append(st_spec)
        args.append(h0)
    out_specs = [pl.BlockSpec((T, GROUP_W), lambda b: (b, 0))]
    out_shape = [jax.ShapeDtypeStruct((n_seq * T, GROUP_W), BF16)]
    if has_state:
        out_specs.append(st_spec)
        out_shape.append(jax.ShapeDtypeStruct((n_seq, 2, GLA_HEADS, LANE, LANE), F32))
    res = pl.pallas_call(
        functools.partial(_gla_kernel, T=T, has_h0=has_h0, has_state=has_state),
        grid=(n_seq,),
        in_specs=in_specs,
        out_specs=out_specs,
        out_shape=out_shape,
        scratch_shapes=[pltpu.VMEM((T, GROUP_W), F32), pltpu.VMEM((2, GLA_HEADS, LANE, LANE), F32),
                        pltpu.VMEM((C, 2 * LANE), F32), pltpu.VMEM((C, 2 * LANE), F32),
                        pltpu.VMEM((C, 2 * LANE), F32), pltpu.VMEM((C, 2 * LANE), F32),
                        pltpu.VMEM((C, GROUP_W), F32), pltpu.VMEM((C, GROUP_W), F32)],
        compiler_params=pltpu.CompilerParams(dimension_semantics=("parallel",), vmem_limit_bytes=VMEM_LIMIT),
        name="gla",
    )(*args)
    return (res[0], res[1]) if has_state else (res[0], None)


def kernel(x_prompt, x_sample, cache_mla_kv, state_ssd, state_ret, state_gla, c, c_ctx, w_mod, b_mod, norm_ffn1, ffn1_wg, ffn1_wu, ffn1_wd, norm_mix, w_in, ssd_conv_w, ssd_conv_b, ssd_a_log, ssd_dt_bias, ssd_d, ssd_norm_g, mla_q_lat_gain, mla_w_q_up, mla_q_gain, mla_kv_lat_gain, mla_w_kv_up, mla_k_gain, ret_decay_logit, ret_norm_g, gla_w_g2, gla_b_g, gla_norm_g, w_out, norm_ffn2, ffn2_wg, ffn2_wu, ffn2_wd):
    sp = _ssd_params(ssd_conv_w, ssd_conv_b, ssd_a_log, ssd_dt_bias, ssd_d, ssd_norm_g)
    ssd_h0 = jnp.stack([_ssd_pack_state(state_ssd[:, l]) for l in range(DEPTH)], axis=1)
    gp = _gla_params(gla_w_g2, gla_b_g, gla_norm_g)
    gla_h0 = jnp.stack([_gla_pack_state(state_gla[:, l]) for l in range(DEPTH)], axis=1)

    main_idx, side_idx = _in_proj_columns()
    w_main = _gather_columns(w_in, main_idx).astype(BF16)
    w_side = _gather_columns(w_in, side_idx).astype(BF16)
    w_out_b = w_out.astype(BF16)
    f1 = (ffn1_wg.astype(BF16), ffn1_wu.astype(BF16), ffn1_wd.astype(BF16))
    f2 = (ffn2_wg.astype(BF16), ffn2_wu.astype(BF16), ffn2_wd.astype(BF16))
    g_ffn1 = norm_ffn1.reshape(DEPTH, 1, D_MODEL)
    g_mix = norm_mix.reshape(DEPTH, 1, D_MODEL)
    g_ffn2 = norm_ffn2.reshape(DEPTH, 1, D_MODEL)
    mw = _mla_weights(mla_q_lat_gain, mla_w_q_up, mla_q_gain, mla_kv_lat_gain, mla_w_kv_up, mla_k_gain)
    cos_lat, sin_lat = _rope_tables(DEC_SEQ)
    cos_all = jnp.concatenate([jnp.ones((TOK_TM, LANE), F32), cos_lat], axis=0)
    sin_all = jnp.concatenate([jnp.zeros((TOK_TM, LANE), F32), sin_lat], axis=0)
    ret_lg = jax.nn.log_sigmoid(ret_decay_logit.astype(F32))
    ret_gain = ret_norm_g.reshape(DEPTH, 1, GROUP_W)
    ret_h0 = state_ret.reshape(DEC_BATCH, DEPTH, 2, 2, LANE, LANE)

    c_all = jnp.zeros((MOD_ROWS, D_MODEL), F32).at[0].set(c_ctx).at[1:1 + DEC_BATCH].set(c)
    mod = _modulation(c_all, w_mod, b_mod).reshape(DEPTH * MOD_ROWS, 1, N_MOD * D_MODEL)

    x = jnp.concatenate([x_prompt.reshape(CTX_ROWS, D_MODEL), x_sample.reshape(LAT_ROWS, D_MODEL)], axis=0)
    kv_list, ssd_list, ret_list, gla_list = [], [], [], []
    for l in range(DEPTH):
        x = _ffn(x, mod, g_ffn1, *f1, l, 0)
        um, us = _in_proj(x, mod, g_mix, w_main, w_side, l)

        ym_c, ym_l, kv_lat = _mla_layer(um, us, cache_mla_kv[:, l], mw, cos_all, sin_all, l)
        yr_c, s_ret = _retention(um, ret_lg[l], ret_gain[l], None, None, BATCH, SEQ, 0)
        yr_l, _ = _retention(um, ret_lg[l], ret_gain[l], ret_h0[:, l], (cos_lat, sin_lat),
                             DEC_BATCH, DEC_SEQ, CTX_ROWS // DEC_SEQ)
        ys_c, s_ssd = _ssd_scan(um, us, sp, l, None, BATCH, SEQ, 0)
        ys_l, _ = _ssd_scan(um, us, sp, l, ssd_h0[:, l], DEC_BATCH, DEC_SEQ, CTX_ROWS // DEC_SEQ)
        yg_c, s_gla = _gla_scan(um, us, gp, l, None, BATCH, SEQ, 0)
        yg_l, _ = _gla_scan(um, us, gp, l, gla_h0[:, l], DEC_BATCH, DEC_SEQ, CTX_ROWS // DEC_SEQ)

        kv_list.append(kv_lat[:CTX_ROWS, :MLA_CACHE_W].reshape(BATCH, SEQ, MLA_CACHE_W))
        ssd_list.append(_ssd_unpack_state(s_ssd))
        ret_list.append(s_ret.reshape(BATCH, 2, RET_HEADS, RET_QK, RET_V))
        gla_list.append(_gla_unpack_state(s_gla))
        x = _out_proj(x, [ys_c, ym_c, yr_c, yg_c], [ys_l, ym_l, yr_l, yg_l], mod, w_out_b, l)
        x = _ffn(x, mod, g_ffn2, *f2, l, 6)
    y_p = x[:CTX_ROWS].reshape(BATCH, SEQ, D_MODEL)
    y_s = x[CTX_ROWS:].reshape(DEC_BATCH, DEC_SEQ, D_MODEL)
    return (y_p, y_s, jnp.stack(kv_list, axis=1), jnp.stack(ssd_list, axis=1),
            jnp.stack(ret_list, axis=1), jnp.stack(gla_list, axis=1))
```

```python
import functools

import jax
import jax.numpy as jnp
import numpy as np
from jax import lax
from jax.experimental import pallas as pl
from jax.experimental.pallas import tpu as pltpu

F32 = jnp.float32
BF16 = jnp.bfloat16

D_MODEL = 2048
BATCH = 32
SEQ = 256
DEPTH = 4
DEC_BATCH = 4
DEC_SEQ = 4096
PAST_LEN = 256
GRID_W = 64
ROPE_BASE = 10000.0
EPS = 1e-6
D_FF = 5632
N_MOD = 9
GROUP_W = D_MODEL // 4
ATTN_BLOCK = 128

SSD_HEAD_DIM = 64
SSD_HEADS = GROUP_W // SSD_HEAD_DIM
SSD_STATE = 64
SSD_GROUPS = 2
SSD_CONV_K = 3
SSD_CHUNK = 128
SSD_CONV_CH = GROUP_W + 2 * SSD_GROUPS * SSD_STATE
SSD_IN = GROUP_W + SSD_CONV_CH + SSD_HEADS

MLA_HEADS = 4
MLA_NOPE = 128
MLA_ROPE = 64
MLA_V = GROUP_W // MLA_HEADS
MLA_Q_RANK = 384
MLA_KV_RANK = 128
MLA_QK = MLA_NOPE + MLA_ROPE
MLA_IN = MLA_Q_RANK + MLA_KV_RANK + MLA_ROPE
MLA_CACHE_W = MLA_KV_RANK + MLA_ROPE

RET_HEADS = 4
RET_QK = 64
RET_V = GROUP_W // RET_HEADS
RET_CHUNK = 128
RET_IN = 2 * RET_HEADS * RET_QK + 2 * GROUP_W

GLA_HEADS = 4
GLA_QK = 64
GLA_V = GROUP_W // GLA_HEADS
GLA_GATE_RANK = 16
GLA_GATE_TEMP = 16.0
GLA_CHUNK = 16
GLA_IN = 2 * GLA_HEADS * GLA_QK + GROUP_W + GLA_GATE_RANK + GROUP_W

IN_W = SSD_IN + MLA_IN + RET_IN + GLA_IN

LANE = 128
CTX_ROWS = BATCH * SEQ
LAT_ROWS = DEC_BATCH * DEC_SEQ
ROWS = CTX_ROWS + LAT_ROWS
MOD_ROWS = 8
VMEM_LIMIT = 56 * 1024 * 1024
TOK_TM = 512
MLA_HEAD_W = 2 * LANE

MAIN_SSD_Z = 0
MAIN_SSD_XS = 512
MAIN_RET_V = 1024
MAIN_RET_G = 1536
MAIN_GLA_V = 2048
MAIN_GLA_R = 2560
MAIN_SSD_BC = 3072
MAIN_RET_Q = 3328
MAIN_RET_K = 3584
MAIN_GLA_Q = 3840
MAIN_GLA_K = 4096
MAIN_MLA_CQ = 4352
MAIN_W = 4864
MAIN_TN = MAIN_W // 2
SIDE_CKV = 0
SIDE_KPE = SIDE_CKV + MLA_KV_RANK
SIDE_DT = 2 * LANE
SIDE_G1 = SIDE_DT + SSD_HEADS
SIDE_W = 3 * LANE


def _in_proj_columns():
    o_ssd, o_mla, o_ret, o_gla = 0, SSD_IN, SSD_IN + MLA_IN, SSD_IN + MLA_IN + RET_IN
    main = np.full((MAIN_W,), -1, np.int64)

    def put(dst, src, n):
        main[dst:dst + n] = src + np.arange(n)
    qk = RET_HEADS * RET_QK
    put(MAIN_SSD_Z, o_ssd, GROUP_W)
    put(MAIN_SSD_XS, o_ssd + GROUP_W, GROUP_W)
    put(MAIN_SSD_BC, o_ssd + 2 * GROUP_W, 2 * SSD_GROUPS * SSD_STATE)
    put(MAIN_MLA_CQ, o_mla, MLA_Q_RANK)
    put(MAIN_RET_Q, o_ret, qk)
    put(MAIN_RET_K, o_ret + qk, qk)
    put(MAIN_RET_V, o_ret + 2 * qk, GROUP_W)
    put(MAIN_RET_G, o_ret + 2 * qk + GROUP_W, GROUP_W)
    put(MAIN_GLA_Q, o_gla, qk)
    put(MAIN_GLA_K, o_gla + qk, qk)
    put(MAIN_GLA_V, o_gla + 2 * qk, GROUP_W)
    put(MAIN_GLA_R, o_gla + 2 * qk + GROUP_W + GLA_GATE_RANK, GROUP_W)
    side = np.full((SIDE_W,), -1, np.int64)
    side[SIDE_CKV:SIDE_CKV + MLA_KV_RANK + MLA_ROPE] = o_mla + MLA_Q_RANK + np.arange(MLA_KV_RANK + MLA_ROPE)
    side[SIDE_DT:SIDE_DT + SSD_HEADS] = o_ssd + GROUP_W + SSD_CONV_CH + np.arange(SSD_HEADS)
    side[SIDE_G1:SIDE_G1 + GLA_GATE_RANK] = o_gla + 2 * qk + GROUP_W + np.arange(GLA_GATE_RANK)
    return main, side


def _gather_columns(w, idx):
    safe = np.where(idx < 0, 0, idx)
    out = jnp.take(w, jnp.asarray(safe, jnp.int32), axis=-1)
    return jnp.where(jnp.asarray(idx >= 0), out, 0.0)


def _mod_row(i, tm):
    ctx_tiles = CTX_ROWS // tm
    per_seq = DEC_SEQ // tm
    return jnp.where(i < ctx_tiles, 0, 1 + (i - ctx_tiles) // per_seq)


def _mod_spec(layer, which, tm, n_grid):
    if n_grid == 1:
        return pl.BlockSpec((1, 1, D_MODEL), lambda i: (layer * MOD_ROWS + _mod_row(i, tm), 0, which))
    return pl.BlockSpec((1, 1, D_MODEL), lambda i, j: (layer * MOD_ROWS + _mod_row(i, tm), 0, which))


def _dot(a, b):
    return jnp.dot(a, b, preferred_element_type=F32)


def _dot_nt(a, b):
    return lax.dot_general(a, b, (((1,), (1,)), ((), ())), preferred_element_type=F32)


def _dot_tn(a, b):
    return lax.dot_general(a, b, (((0,), (0,)), ((), ())), preferred_element_type=F32)


def _silu(x):
    return x * jax.nn.sigmoid(x)


def _mod_kernel(c_ref, w_ref, b_ref, o_ref):
    s = _silu(c_ref[...])
    hi = s.astype(BF16)
    lo = (s - hi.astype(F32)).astype(BF16)
    w = w_ref[...].astype(BF16)
    o_ref[...] = _dot(hi, w) + _dot(lo, w) + b_ref[...]


def _modulation(c_all, w_mod, b_mod):
    tn = 1024
    n = N_MOD * D_MODEL
    return pl.pallas_call(
        _mod_kernel,
        grid=(DEPTH, n // tn),
        in_specs=[pl.BlockSpec((MOD_ROWS, D_MODEL), lambda l, j: (0, 0)),
                  pl.BlockSpec((None, D_MODEL, tn), lambda l, j: (l, 0, j)),
                  pl.BlockSpec((None, 1, tn), lambda l, j: (l, 0, j))],
        out_specs=pl.BlockSpec((None, MOD_ROWS, tn), lambda l, j: (l, 0, j)),
        out_shape=jax.ShapeDtypeStruct((DEPTH, MOD_ROWS, n), F32),
        compiler_params=pltpu.CompilerParams(dimension_semantics=("arbitrary", "arbitrary"),
                                             vmem_limit_bytes=VMEM_LIMIT),
        name="modulation",
    )(c_all, w_mod, b_mod.reshape(DEPTH, 1, n))


NORM_ROWS = 16


def _norm_modulate(x_ref, g_ref, shift_ref, scale_ref, h_ref):
    gs = g_ref[...] * (1.0 + scale_ref[0])
    shift = shift_ref[0]

    def body(r, carry):
        rows = pl.ds(pl.multiple_of(r * NORM_ROWS, NORM_ROWS), NORM_ROWS)
        x = x_ref[rows, :]
        inv = lax.rsqrt(jnp.mean(x * x, axis=-1, keepdims=True) + EPS)
        h_ref[rows, :] = ((x * inv) * gs + shift).astype(BF16)
        return carry

    lax.fori_loop(0, x_ref.shape[0] // NORM_ROWS, body, 0, unroll=8)


FFN_TM = 1024
FFN_VMEM_LIMIT = 60 * 1024 * 1024
FFN_TF = 512
FFN_TN = 256
FFN_NF = D_FF // FFN_TF
FFN_NN = D_MODEL // FFN_TN
MXU_W = 256


def _ffn_kernel(*refs, n_src, n_out, ctx_tiles):
    it = iter(refs)
    srcs = [(next(it), next(it)) for _ in range(n_src)]
    g_ref, shift_ref, scale_ref, gate_ref, wg_ref, wu_ref, wd_ref = (next(it) for _ in range(7))
    outs = [next(it) for _ in range(n_out)]
    h_ref, a_ref = next(it), next(it)
    j = pl.program_id(1)
    is_ctx = pl.program_id(0) < ctx_tiles

    def for_group(k, n_groups, fn):
        if n_groups == 1:
            fn()
        else:
            pl.when(is_ctx if k == 0 else jnp.logical_not(is_ctx))(fn)

    @pl.when(j == 0)
    def _():
        for k, (x_ref, _) in enumerate(srcs):
            for_group(k, n_src, functools.partial(_norm_modulate, x_ref, g_ref, shift_ref, scale_ref, h_ref))

    @pl.when(j < FFN_NF)
    def _():
        h = h_ref[...]
        base = pl.multiple_of(j * FFN_TF, FFN_TF)
        for s in range(FFN_TF // MXU_W):
            cols = slice(s * MXU_W, (s + 1) * MXU_W)
            g = _dot(h, wg_ref[:, cols])
            u = _dot(h, wu_ref[:, cols])
            a_ref[:, pl.ds(base + s * MXU_W, MXU_W)] = (_silu(g) * u).astype(BF16)

    @pl.when(j >= FFN_NF)
    def _():
        y = 0.5 * gate_ref[0] * _dot(a_ref[...], wd_ref[...])
        n_groups = max(n_src, n_out)
        for k in range(n_groups):
            xt_ref, o_ref = srcs[min(k, n_src - 1)][1], outs[min(k, n_out - 1)]

            def store(xt_ref=xt_ref, o_ref=o_ref):
                o_ref[...] = xt_ref[...] + y
            for_group(k, n_groups, store)


def _ffn(xs, mod, norm_g, wg, wu, wd, layer, mod_base, tm=FFN_TM, split_out=False):
    n_src, n_out = len(xs), 2 if split_out else 1
    ctx_tiles = CTX_ROWS // tm

    def up_tile(i, j):
        return (layer, 0, jnp.minimum(j, FFN_NF - 1))

    def down_tile(j):
        return jnp.maximum(j - FFN_NF, 0)

    def row_of(k, n_groups, i):
        if n_groups == 1:
            return i
        return jnp.minimum(i, ctx_tiles - 1) if k == 0 else jnp.maximum(i - ctx_tiles, 0)

    def col_of(k, n_groups, i, j):
        if n_groups == 1:
            return down_tile(j)
        if k == 0:
            return jnp.where(i < ctx_tiles, down_tile(j), FFN_NN - 1)
        return jnp.where(i >= ctx_tiles, down_tile(j), 0)
    gate_blocks = D_MODEL // FFN_TN
    in_specs, args = [], []
    for k, x in enumerate(xs):
        in_specs += [pl.BlockSpec((tm, D_MODEL), lambda i, j, k=k: (row_of(k, n_src, i), 0)),
                     pl.BlockSpec((tm, FFN_TN), lambda i, j, k=k: (row_of(k, n_src, i), col_of(k, n_src, i, j)))]
        args += [x, x]
    in_specs += [pl.BlockSpec((None, 1, D_MODEL), lambda i, j: (layer, 0, 0)),
                 _mod_spec(layer, mod_base + 0, tm, 2),
                 _mod_spec(layer, mod_base + 1, tm, 2),
                 pl.BlockSpec((1, 1, FFN_TN), lambda i, j: (layer * MOD_ROWS + _mod_row(i, tm), 0,
                                                            (mod_base + 2) * gate_blocks + down_tile(j))),
                 pl.BlockSpec((None, D_MODEL, FFN_TF), up_tile),
                 pl.BlockSpec((None, D_MODEL, FFN_TF), up_tile),
                 pl.BlockSpec((None, D_FF, FFN_TN), lambda i, j: (layer, 0, down_tile(j)))]
    args += [norm_g, mod, mod, mod, wg, wu, wd]
    out_rows = (CTX_ROWS, LAT_ROWS) if split_out else (ROWS,)
    res = pl.pallas_call(
        functools.partial(_ffn_kernel, n_src=n_src, n_out=n_out, ctx_tiles=ctx_tiles),
        grid=(ROWS // tm, FFN_NF + FFN_NN),
        in_specs=in_specs,
        out_specs=[pl.BlockSpec((tm, FFN_TN), lambda i, j, k=k: (row_of(k, n_out, i), col_of(k, n_out, i, j)))
                   for k in range(n_out)],
        out_shape=[jax.ShapeDtypeStruct((r, D_MODEL), F32) for r in out_rows],
        scratch_shapes=[pltpu.VMEM((tm, D_MODEL), BF16), pltpu.VMEM((tm, D_FF), BF16)],
        compiler_params=pltpu.CompilerParams(dimension_semantics=("parallel", "arbitrary"),
                                             vmem_limit_bytes=FFN_VMEM_LIMIT),
        name="ffn",
    )(*args)
    return res if split_out else res[0]


def _in_proj_kernel(x_ref, g_ref, shift_ref, scale_ref, w_ref, ws_ref, o_ref, os_ref, h_ref):
    @pl.when(pl.program_id(1) == 0)
    def _():
        _norm_modulate(x_ref, g_ref, shift_ref, scale_ref, h_ref)
        os_ref[...] = _dot(h_ref[...], ws_ref[...])

    o_ref[...] = _dot(h_ref[...], w_ref[...]).astype(o_ref.dtype)


def _in_proj(x, mod, norm_g, w_main, w_side, layer, tm=TOK_TM):
    return pl.pallas_call(
        _in_proj_kernel,
        grid=(ROWS // tm, MAIN_W // MAIN_TN),
        in_specs=[pl.BlockSpec((tm, D_MODEL), lambda i, j: (i, 0)),
                  pl.BlockSpec((None, 1, D_MODEL), lambda i, j: (layer, 0, 0)),
                  _mod_spec(layer, 3, tm, 2),
                  _mod_spec(layer, 4, tm, 2),
                  pl.BlockSpec((None, D_MODEL, MAIN_TN), lambda i, j: (layer, 0, j)),
                  pl.BlockSpec((None, D_MODEL, SIDE_W), lambda i, j: (layer, 0, 0))],
        out_specs=[pl.BlockSpec((tm, MAIN_TN), lambda i, j: (i, j)),
                   pl.BlockSpec((tm, SIDE_W), lambda i, j: (i, 0))],
        out_shape=[jax.ShapeDtypeStruct((ROWS, MAIN_W), BF16),
                   jax.ShapeDtypeStruct((ROWS, SIDE_W), F32)],
        scratch_shapes=[pltpu.VMEM((tm, D_MODEL), BF16)],
        compiler_params=pltpu.CompilerParams(dimension_semantics=("parallel", "arbitrary"),
                                             vmem_limit_bytes=VMEM_LIMIT),
        name="in_proj",
    )(x, norm_g, mod, mod, w_main, w_side)


def _out_proj_kernel(x_ref, *refs, ctx_tiles):
    ctx_refs, lat_refs, (gate_ref, w_ref, o_ref) = refs[0:4], refs[4:8], refs[8:]

    def run(y_refs):
        acc = _dot(y_refs[0][...], w_ref[0:GROUP_W, :])
        for g in range(1, 4):
            acc += _dot(y_refs[g][...], w_ref[g * GROUP_W:(g + 1) * GROUP_W, :])
        o_ref[...] = x_ref[...] + gate_ref[0] * acc

    @pl.when(pl.program_id(0) < ctx_tiles)
    def _():
        run(ctx_refs)

    @pl.when(pl.program_id(0) >= ctx_tiles)
    def _():
        run(lat_refs)


def _out_proj(x, ys_ctx, ys_lat, mod, w, layer, tm=TOK_TM):
    ctx_tiles = CTX_ROWS // tm
    row = pl.BlockSpec((tm, D_MODEL), lambda i: (i, 0))
    cspec = pl.BlockSpec((tm, GROUP_W), lambda i: (jnp.minimum(i, ctx_tiles - 1), 0))
    lspec = pl.BlockSpec((tm, GROUP_W), lambda i: (jnp.maximum(i - ctx_tiles, 0), 0))
    return pl.pallas_call(
        functools.partial(_out_proj_kernel, ctx_tiles=ctx_tiles),
        grid=(ROWS // tm,),
        in_specs=[row] + [cspec] * 4 + [lspec] * 4 + [
            _mod_spec(layer, 5, tm, 1),
            pl.BlockSpec((None, D_MODEL, D_MODEL), lambda i: (layer, 0, 0))],
        out_specs=row,
        out_shape=jax.ShapeDtypeStruct((ROWS, D_MODEL), F32),
        compiler_params=pltpu.CompilerParams(dimension_semantics=("parallel",),
                                             vmem_limit_bytes=VMEM_LIMIT),
        name="out_proj",
    )(x, *ys_ctx, *ys_lat, mod, w)


def _rope_tables(T):
    n_rows = T // GRID_W
    row = jnp.repeat(jnp.arange(n_rows, dtype=F32), GRID_W)
    col = jnp.tile(jnp.arange(GRID_W, dtype=F32), n_rows)
    d_axis = MLA_ROPE // 2
    inv = ROPE_BASE ** (-jnp.arange(0, d_axis, 2, dtype=F32) / d_axis)
    ar, ac = row[:, None] * inv, col[:, None] * inv
    cos = jnp.concatenate([jnp.cos(ar), jnp.cos(ar), jnp.cos(ac), jnp.cos(ac)], axis=-1)
    sin = jnp.concatenate([-jnp.sin(ar), jnp.sin(ar), -jnp.sin(ac), jnp.sin(ac)], axis=-1)
    return jnp.tile(cos, (1, 2)), jnp.tile(sin, (1, 2))


def _swap16(x):
    lane = lax.broadcasted_iota(jnp.int32, x.shape, 1)
    up = pltpu.roll(x, LANE - 16, 1)
    down = pltpu.roll(x, 16, 1)
    return jnp.where((lane % 32) < 16, up, down)


def _rope(x, cos, sin):
    return x * cos + _swap16(x) * sin


def _mla_q_kernel(c0_ref, c1_ref, c2_ref, gl_ref, w_ref, gq_ref, cos_ref, sin_ref, o_ref):
    cs = [r[...].astype(F32) for r in (c0_ref, c1_ref, c2_ref)]
    ss = sum(jnp.sum(c * c, axis=-1, keepdims=True) for c in cs)
    r = lax.rsqrt(ss / MLA_Q_RANK + EPS)
    q = sum(_dot((cs[i] * r * gl_ref[:, i * LANE:(i + 1) * LANE]).astype(BF16),
                 w_ref[i * LANE:(i + 1) * LANE, :]) for i in range(3))
    cos, sin = cos_ref[...], sin_ref[...]
    scale = MLA_QK ** -0.5
    for h in range(MLA_HEADS):
        a = q[:, h * MLA_HEAD_W:h * MLA_HEAD_W + LANE]
        b = q[:, h * MLA_HEAD_W + LANE:(h + 1) * MLA_HEAD_W]
        ssq = jnp.sum(a * a, axis=-1, keepdims=True) + jnp.sum(b * b, axis=-1, keepdims=True)
        rh = lax.rsqrt(ssq / MLA_QK + EPS) * scale
        o_ref[:, h * MLA_HEAD_W:h * MLA_HEAD_W + LANE] = (a * rh * gq_ref[:, :LANE]).astype(BF16)
        o_ref[:, h * MLA_HEAD_W + LANE:(h + 1) * MLA_HEAD_W] = _rope(b * rh * gq_ref[:, LANE:], cos, sin).astype(BF16)


def _rope_tile_index(i, tm):
    ctx_tiles = CTX_ROWS // tm
    return jnp.where(i < ctx_tiles, 0, 1 + (i - ctx_tiles) % (DEC_SEQ // tm))


def _mla_q(um, gl, wq, gq, cos_all, sin_all, layer, tm=TOK_TM):
    cq = MAIN_MLA_CQ // LANE
    tab = pl.BlockSpec((tm, LANE), lambda i: (_rope_tile_index(i, tm), 0))
    return pl.pallas_call(
        _mla_q_kernel,
        grid=(ROWS // tm,),
        in_specs=[pl.BlockSpec((tm, LANE), lambda i: (i, cq)),
                  pl.BlockSpec((tm, LANE), lambda i: (i, cq + 1)),
                  pl.BlockSpec((tm, LANE), lambda i: (i, cq + 2)),
                  pl.BlockSpec((None, 1, MLA_Q_RANK), lambda i: (layer, 0, 0)),
                  pl.BlockSpec((None, MLA_Q_RANK, MLA_HEADS * MLA_HEAD_W), lambda i: (layer, 0, 0)),
                  pl.BlockSpec((None, 1, MLA_HEAD_W), lambda i: (layer, 0, 0)),
                  tab, tab],
        out_specs=pl.BlockSpec((tm, MLA_HEADS * MLA_HEAD_W), lambda i: (i, 0)),
        out_shape=jax.ShapeDtypeStruct((ROWS, MLA_HEADS * MLA_HEAD_W), BF16),
        compiler_params=pltpu.CompilerParams(dimension_semantics=("parallel",), vmem_limit_bytes=VMEM_LIMIT),
        name="mla_q",
    )(um, um, um, gl, wq, gq, cos_all, sin_all)


def _mla_kv_kernel(s_ref, gl_ref, w_ref, gk_ref, cos_ref, sin_ref, lat_ref, k_ref, v_ref, *, normalize):
    ckv = s_ref[:, :LANE]
    kpe = s_ref[:, LANE:]
    if normalize:
        ckv = ckv * lax.rsqrt(jnp.mean(ckv * ckv, axis=-1, keepdims=True) + EPS) * gl_ref[...]
    lat_ref[:, :LANE] = ckv
    lat_ref[:, LANE:] = kpe
    kv = _dot(ckv.astype(BF16), w_ref[...])
    ss_pe = jnp.sum(kpe * kpe, axis=-1, keepdims=True)
    cos, sin = cos_ref[...], sin_ref[...]
    for h in range(MLA_HEADS):
        a = kv[:, h * LANE:(h + 1) * LANE]
        rh = lax.rsqrt((jnp.sum(a * a, axis=-1, keepdims=True) + ss_pe) / MLA_QK + EPS)
        k_ref[:, h * MLA_HEAD_W:h * MLA_HEAD_W + LANE] = (a * rh * gk_ref[:, :LANE]).astype(BF16)
        k_ref[:, h * MLA_HEAD_W + LANE:(h + 1) * MLA_HEAD_W] = _rope(kpe * rh * gk_ref[:, LANE:], cos, sin).astype(BF16)
    v_ref[...] = kv[:, MLA_HEADS * LANE:].astype(BF16)


def _mla_kv(src, gl, wkv, gk, cos_all, sin_all, layer, *, normalize, tab_index, tm=TOK_TM):
    rows = src.shape[0]
    tab = pl.BlockSpec((tm, LANE), lambda i: (tab_index(i, tm), 0))
    return pl.pallas_call(
        functools.partial(_mla_kv_kernel, normalize=normalize),
        grid=(rows // tm,),
        in_specs=[pl.BlockSpec((tm, 2 * LANE), lambda i: (i, 0)),
                  pl.BlockSpec((None, 1, MLA_KV_RANK), lambda i: (layer, 0, 0)),
                  pl.BlockSpec((None, MLA_KV_RANK, 2 * MLA_HEADS * LANE), lambda i: (layer, 0, 0)),
                  pl.BlockSpec((None, 1, MLA_HEAD_W), lambda i: (layer, 0, 0)),
                  tab, tab],
        out_specs=[pl.BlockSpec((tm, 2 * LANE), lambda i: (i, 0)),
                   pl.BlockSpec((tm, MLA_HEADS * MLA_HEAD_W), lambda i: (i, 0)),
                   pl.BlockSpec((tm, GROUP_W), lambda i: (i, 0))],
        out_shape=[jax.ShapeDtypeStruct((rows, 2 * LANE), F32),
                   jax.ShapeDtypeStruct((rows, MLA_HEADS * MLA_HEAD_W), BF16),
                   jax.ShapeDtypeStruct((rows, GROUP_W), BF16)],
        compiler_params=pltpu.CompilerParams(dimension_semantics=("parallel",), vmem_limit_bytes=VMEM_LIMIT),
        name="mla_kv",
    )(src, gl, wkv, gk, cos_all, sin_all)


ATTN_TQ_LAT = 256


def _attn_kernel(q_ref, *refs):
    o_ref = refs[-1]
    kv = [(refs[i], refs[i + 1]) for i in range(0, len(refs) - 1, 2)]
    q = q_ref[...]
    scores = [_dot_nt(q, k_ref[...]) for k_ref, _ in kv]
    m = functools.reduce(jnp.maximum, [jnp.max(s, axis=-1, keepdims=True) for s in scores])
    ps = [jnp.exp(s - m) for s in scores]
    l = sum(jnp.sum(p, axis=-1, keepdims=True) for p in ps)
    o = sum(_dot(p.astype(BF16), v_ref[...]) for p, (_, v_ref) in zip(ps, kv))
    o_ref[...] = (o / l).astype(o_ref.dtype)


def _attention(q, sources, n_seq, t, tq, q_row0):
    nq = t // tq
    in_specs = [pl.BlockSpec((tq, MLA_HEAD_W), lambda b, h, i: (q_row0 // tq + b * nq + i, h))]
    args = [q]
    for k, v, s_len, row0 in sources:
        in_specs += [pl.BlockSpec((s_len, MLA_HEAD_W), lambda b, h, i, o=row0 // s_len: (o + b, h)),
                     pl.BlockSpec((s_len, MLA_V), lambda b, h, i, o=row0 // s_len: (o + b, h))]
        args += [k, v]
    return pl.pallas_call(
        _attn_kernel,
        grid=(n_seq, MLA_HEADS, nq),
        in_specs=in_specs,
        out_specs=pl.BlockSpec((tq, MLA_V), lambda b, h, i: (b * nq + i, h)),
        out_shape=jax.ShapeDtypeStruct((n_seq * t, GROUP_W), BF16),
        compiler_params=pltpu.CompilerParams(dimension_semantics=("parallel", "parallel", "arbitrary"),
                                             vmem_limit_bytes=VMEM_LIMIT),
        name="mla_attention",
    )(*args)


def _mla_weights(mla_q_lat_gain, mla_w_q_up, mla_q_gain, mla_kv_lat_gain, mla_w_kv_up, mla_k_gain):
    qcol = np.full((MLA_HEADS * MLA_HEAD_W,), -1, np.int64)
    kvcol = np.zeros((2 * MLA_HEADS * LANE,), np.int64)
    for h in range(MLA_HEADS):
        qcol[h * MLA_HEAD_W:h * MLA_HEAD_W + MLA_QK] = h * MLA_QK + np.arange(MLA_QK)
        kvcol[h * LANE:(h + 1) * LANE] = h * (MLA_NOPE + MLA_V) + np.arange(MLA_NOPE)
        kvcol[(MLA_HEADS + h) * LANE:(MLA_HEADS + h + 1) * LANE] = h * (MLA_NOPE + MLA_V) + MLA_NOPE + np.arange(MLA_V)
    pad = jnp.zeros((DEPTH, MLA_HEAD_W - MLA_QK), F32)
    return dict(
        gl=mla_q_lat_gain.reshape(DEPTH, 1, MLA_Q_RANK),
        wq=_gather_columns(mla_w_q_up, qcol).astype(BF16),
        gq=jnp.concatenate([mla_q_gain, pad], axis=-1).reshape(DEPTH, 1, MLA_HEAD_W),
        gkv=mla_kv_lat_gain.reshape(DEPTH, 1, MLA_KV_RANK),
        wkv=_gather_columns(mla_w_kv_up, kvcol).astype(BF16),
        gk=jnp.concatenate([mla_k_gain, pad], axis=-1).reshape(DEPTH, 1, MLA_HEAD_W))


def _mla_layer(um, us, cache_l, mw, cos_all, sin_all, layer):
    q = _mla_q(um, mw['gl'], mw['wq'], mw['gq'], cos_all, sin_all, layer)
    kv_lat, k, v = _mla_kv(us, mw['gkv'], mw['wkv'], mw['gk'], cos_all, sin_all, layer,
                           normalize=True, tab_index=_rope_tile_index)
    cache2 = jnp.pad(cache_l.reshape(DEC_BATCH * PAST_LEN, MLA_CACHE_W), ((0, 0), (0, 2 * LANE - MLA_CACHE_W)))
    _, k_c, v_c = _mla_kv(cache2, mw['gkv'], mw['wkv'], mw['gk'], cos_all, sin_all, layer,
                          normalize=False, tab_index=lambda i, tm: 0)
    y_ctx = _attention(q, [(k, v, SEQ, 0)], BATCH, SEQ, SEQ, 0)
    y_lat = _attention(q, [(k, v, DEC_SEQ, CTX_ROWS), (k_c, v_c, PAST_LEN, 0)], DEC_BATCH, DEC_SEQ, ATTN_TQ_LAT,
                       CTX_ROWS)
    return y_ctx, y_lat, kv_lat


def _ret_kernel(*refs, T, C, rope, has_h0, has_state):
    it = iter(refs)
    q_ref, k_ref, v_ref, g_ref = next(it), next(it), next(it), next(it)
    cos_ref, sin_ref = (next(it), next(it)) if rope else (None, None)
    lg_ref, gain_ref = next(it), next(it)
    h0_ref = next(it) if has_h0 else None
    y_ref = next(it)
    st_ref = next(it) if has_state else None
    qs_ref, ks_ref, oacc_ref, dm_ref, tab_ref, hst_ref = (next(it) for _ in range(6))

    nc = T // C
    ii = lax.broadcasted_iota(jnp.int32, (C, C), 0)
    jj = lax.broadcasted_iota(jnp.int32, (C, C), 1)
    dif = (ii - jj).astype(F32)
    lane = lax.broadcasted_iota(jnp.int32, (C, LANE), 1)
    rowi = lax.broadcasted_iota(jnp.int32, (C, LANE), 0).astype(F32)
    m_lo = lane < RET_QK
    hrow = lax.broadcasted_iota(jnp.int32, (LANE, LANE), 0) < RET_QK

    for h in range(RET_HEADS):
        lf, lb = lg_ref[0, h], lg_ref[1, h]
        dm_ref[h] = (jnp.where(dif >= 0, jnp.exp(lf * jnp.maximum(dif, 0.0)), 0.0)
                     + jnp.where(dif <= 0, jnp.exp(lb * jnp.maximum(-dif, 0.0)), 0.0))
    for p in range(2):
        lf = jnp.where(m_lo, lg_ref[0, 2 * p], lg_ref[0, 2 * p + 1])
        lb = jnp.where(m_lo, lg_ref[1, 2 * p], lg_ref[1, 2 * p + 1])
        tab_ref[p, 0] = jnp.exp(lf * (rowi + 1.0))
        tab_ref[p, 1] = jnp.exp(lf * (C - 1.0 - rowi))
        tab_ref[p, 2] = jnp.exp(lb * (C - rowi))
        tab_ref[p, 3] = jnp.exp(lb * rowi)
    if has_h0:
        hst_ref[...] = h0_ref[...]
    else:
        hst_ref[...] = jnp.zeros_like(hst_ref)

    def chunk_decay(d, p):
        return jnp.exp(jnp.where(hrow, lg_ref[d, 2 * p], lg_ref[d, 2 * p + 1]) * float(C))

    def forward(c, carry):
        r0 = pl.multiple_of(c * C, C)
        rows = pl.ds(r0, C)
        for p in range(2):
            cols = slice(p * LANE, (p + 1) * LANE)
            qp = q_ref[rows, cols].astype(F32)
            kp = k_ref[rows, cols].astype(F32) * (RET_QK ** -0.5)
            if rope:
                qp = _rope(qp, cos_ref[rows, :], sin_ref[rows, :])
                kp = _rope(kp, cos_ref[rows, :], sin_ref[rows, :])
            qs_ref[rows, cols] = qp.astype(BF16)
            ks_ref[rows, cols] = kp.astype(BF16)
            kpb = kp.astype(BF16)
            hp = hst_ref[0, p]
            hpb = hp.astype(BF16)
            upd = jnp.zeros((LANE, LANE), F32)
            for e in range(2):
                h = 2 * p + e
                mh = m_lo if e == 0 else jnp.logical_not(m_lo)
                hcols = slice(h * LANE, (h + 1) * LANE)
                qm = jnp.where(mh, qp, 0.0)
                s = _dot_nt(qm.astype(BF16), kpb)
                vh = v_ref[rows, hcols]
                o = _dot((s * dm_ref[h]).astype(BF16), vh)
                o += _dot((qm * tab_ref[p, 0]).astype(BF16), hpb)
                oacc_ref[rows, hcols] = o
                upd += _dot_tn(jnp.where(mh, kp * tab_ref[p, 1], 0.0).astype(BF16), vh)
            hst_ref[0, p] = hp * chunk_decay(0, p) + upd
        return carry

    lax.fori_loop(0, nc, forward, 0)

    def backward(t, carry):
        c = nc - 1 - t
        r0 = pl.multiple_of(c * C, C)
        rows = pl.ds(r0, C)
        for p in range(2):
            cols = slice(p * LANE, (p + 1) * LANE)
            qp = qs_ref[rows, cols].astype(F32)
            kp = ks_ref[rows, cols].astype(F32)
            hp = hst_ref[1, p]
            hpb = hp.astype(BF16)
            upd = jnp.zeros((LANE, LANE), F32)
            for e in range(2):
                h = 2 * p + e
                mh = m_lo if e == 0 else jnp.logical_not(m_lo)
                hcols = slice(h * LANE, (h + 1) * LANE)
                vh = v_ref[rows, hcols]
                o = oacc_ref[rows, hcols] + _dot(jnp.where(mh, qp * tab_ref[p, 2], 0.0).astype(BF16), hpb)
                upd += _dot_tn(jnp.where(mh, kp * tab_ref[p, 3], 0.0).astype(BF16), vh)
                oc = o - jnp.mean(o, axis=-1, keepdims=True)
                yn = oc * lax.rsqrt(jnp.mean(oc * oc, axis=-1, keepdims=True) + EPS) * gain_ref[:, hcols]
                y_ref[rows, hcols] = (_silu(g_ref[rows, hcols].astype(F32)) * yn).astype(y_ref.dtype)
            hst_ref[1, p] = hp * chunk_decay(1, p) + upd
        return carry

    lax.fori_loop(0, nc, backward, 0)
    if has_state:
        st_ref[...] = hst_ref[...]


def _retention(um, lg, gain, h0, rope_tabs, n_seq, T, row_block0):
    C = RET_CHUNK
    rope = rope_tabs is not None
    has_h0 = h0 is not None
    has_state = not has_h0

    def col(off, w):
        return pl.BlockSpec((T, w), lambda b: (row_block0 + b, off // w))
    in_specs = [col(MAIN_RET_Q, 2 * LANE), col(MAIN_RET_K, 2 * LANE), col(MAIN_RET_V, GROUP_W), col(MAIN_RET_G, GROUP_W)]
    args = [um, um, um, um]
    if rope:
        in_specs += [pl.BlockSpec((T, LANE), lambda b: (0, 0))] * 2
        args += list(rope_tabs)
    in_specs += [pl.BlockSpec(memory_space=pltpu.SMEM), pl.BlockSpec((1, GROUP_W), lambda b: (0, 0))]
    args += [lg, gain]
    st_spec = pl.BlockSpec((None, 2, 2, LANE, LANE), lambda b: (b, 0, 0, 0, 0))
    if has_h0:
        in_specs.append(st_spec)
        args.append(h0)
    out_specs = [pl.BlockSpec((T, GROUP_W), lambda b: (b, 0))]
    out_shape = [jax.ShapeDtypeStruct((n_seq * T, GROUP_W), BF16)]
    if has_state:
        out_specs.append(st_spec)
        out_shape.append(jax.ShapeDtypeStruct((n_seq, 2, 2, LANE, LANE), F32))
    res = pl.pallas_call(
        functools.partial(_ret_kernel, T=T, C=C, rope=rope, has_h0=has_h0, has_state=has_state),
        grid=(n_seq,),
        in_specs=in_specs,
        out_specs=out_specs,
        out_shape=out_shape,
        scratch_shapes=[pltpu.VMEM((T, 2 * LANE), BF16), pltpu.VMEM((T, 2 * LANE), BF16),
                        pltpu.VMEM((T, GROUP_W), F32), pltpu.VMEM((RET_HEADS, C, C), F32),
                        pltpu.VMEM((2, 4, C, LANE), F32), pltpu.VMEM((2, 2, LANE, LANE), F32)],
        compiler_params=pltpu.CompilerParams(dimension_semantics=("parallel",), vmem_limit_bytes=VMEM_LIMIT),
        name="retention",
    )(*args)
    return (res[0], res[1]) if has_state else (res[0], None)


def _split3(x):
    x1 = x.astype(BF16)
    r1 = x - x1.astype(F32)
    x2 = r1.astype(BF16)
    x3 = (r1 - x2.astype(F32)).astype(BF16)
    return x1, x2, x3


def _dot01_l(m, x):
    x1, x2, x3 = _split3(x)
    return _dot(m, x1) + _dot(m, x2) + _dot(m, x3)


def _dot01_r(x, m):
    x1, x2, x3 = _split3(x)
    return _dot(x1, m) + _dot(x2, m) + _dot(x3, m)


def _softplus(x):
    return jnp.maximum(x, 0.0) + jnp.log(1.0 + jnp.exp(-jnp.abs(x)))


def _head_expanders():
    e8 = np.zeros((LANE, SSD_HEADS * LANE), np.float32)
    e64 = np.zeros((LANE, GROUP_W), np.float32)
    for h in range(SSD_HEADS):
        e8[h, h * LANE:(h + 1) * LANE] = 1.0
        e64[h, h * SSD_HEAD_DIM:(h + 1) * SSD_HEAD_DIM] = 1.0
    return jnp.asarray(e8, BF16), jnp.asarray(e64, BF16)


def _ssd_kernel(*refs, T, has_h0, has_state):
    it = iter(refs)
    z_ref, xs_ref, bc_ref, dt_ref = next(it), next(it), next(it), next(it)
    wx_ref, wbc_ref, bx_ref, bbc_ref = next(it), next(it), next(it), next(it)
    a_ref, dtb_ref, d_ref, gain_ref, e8_ref, e64_ref = (next(it) for _ in range(6))
    h0_ref = next(it) if has_h0 else None
    y_ref = next(it)
    st_ref = next(it) if has_state else None
    xc_s, bcs_s, oacc_ref, hst_ref = next(it), next(it), next(it), next(it)

    C = SSD_CHUNK
    nc = T // C
    ii = lax.broadcasted_iota(jnp.int32, (C, C), 0)
    jj = lax.broadcasted_iota(jnp.int32, (C, C), 1)
    tril, triu = ii >= jj, ii <= jj
    tril_b, triu_b = tril.astype(BF16), triu.astype(BF16)
    lane = lax.broadcasted_iota(jnp.int32, (C, LANE), 1)
    m_lo = lane < SSD_STATE
    masks = (m_lo, jnp.logical_not(m_lo))
    hrow = lax.broadcasted_iota(jnp.int32, (LANE, LANE), 0) < SSD_STATE

    if has_h0:
        hst_ref[...] = h0_ref[...]
    else:
        hst_ref[...] = jnp.zeros_like(hst_ref)

    def decays(rows, d):
        dtv = _softplus(dt_ref[rows, :] + dtb_ref[d:d + 1, :])
        la = dtv * a_ref[d:d + 1, :]
        return dtv, _dot01_l(tril_b if d == 0 else triu_b, la)

    def conv(ref, w_ref, b_ref, c, r0):
        width = ref.shape[1]
        x = ref[pl.ds(r0, C), :].astype(F32)
        p0 = pl.multiple_of(jnp.maximum(r0 - 16, 0), 16)
        n0 = pl.multiple_of(jnp.minimum(r0 + C, T - 16), 16)
        prev_row = jnp.where(c > 0, ref[pl.ds(p0, 16), :].astype(F32)[15:16], 0.0)
        next_row = jnp.where(c < nc - 1, ref[pl.ds(n0, 16), :].astype(F32)[0:1], 0.0)
        rowi = lax.broadcasted_iota(jnp.int32, (C, width), 0)
        prev = jnp.where(rowi == 0, prev_row, pltpu.roll(x, 1, 0))
        nxt = jnp.where(rowi == C - 1, next_row, pltpu.roll(x, C - 1, 0))
        return _silu(prev * w_ref[0:1, :] + x * w_ref[1:2, :] + nxt * w_ref[2:3, :] + b_ref[...])

    def decay_matrix(qrow, pp):
        return jnp.where(hrow, qrow[:, pp * LANE:(pp + 1) * LANE],
                         qrow[:, (2 + pp) * LANE:(3 + pp) * LANE])

    def forward(c, carry):
        r0 = pl.multiple_of(c * C, C)
        rows = pl.ds(r0, C)
        xc = conv(xs_ref, wx_ref, bx_ref, c, r0)
        bcv = conv(bc_ref, wbc_ref, bbc_ref, c, r0)
        xc_s[rows, :] = xc.astype(BF16)
        bcs_s[rows, :] = bcv.astype(BF16)
        bmat, cmat = bcv[:, :LANE], bcv[:, LANE:]
        bmb = bmat.astype(BF16)
        dtf, bf = decays(rows, 0)
        dtb, bb = decays(rows, 1)
        bf_t, dtf_t, bb_t, dtb_t = bf.T, dtf.T, bb.T, dtb.T
        colf = _dot01_r(bf, e8_ref[...])
        colb = _dot01_r(bb, e8_ref[...])
        qdf = _dot01_r(jnp.exp(bf), e64_ref[...])
        kwf = _dot01_r(dtf * jnp.exp(bf[C - 1:C, :] - bf), e64_ref[...])
        cms = [jnp.where(masks[g], cmat, 0.0).astype(BF16) for g in range(SSD_GROUPS)]
        bms = [jnp.where(masks[g], bmat, 0.0).astype(BF16) for g in range(SSD_GROUPS)]
        scores = [_dot_nt(cms[g], bmb) for g in range(SSD_GROUPS)]
        for pp in range(2):
            hp = hst_ref[0, pp]
            hpb = hp.astype(BF16)
            upd = jnp.zeros((LANE, LANE), F32)
            for g in range(SSD_GROUPS):
                cols = slice((2 * g + pp) * LANE, (2 * g + pp + 1) * LANE)
                xs_pair = xc[:, cols]
                xsb = xs_pair.astype(BF16)
                outs = []
                for e in range(2):
                    h = 4 * g + 2 * pp + e
                    hc = slice(h * LANE, (h + 1) * LANE)
                    ef = jnp.where(tril, jnp.exp(jnp.minimum(colf[:, hc] - bf_t[h:h + 1, :], 0.0)), 0.0) * dtf_t[h:h + 1, :]
                    eb = jnp.where(triu, jnp.exp(jnp.minimum(colb[:, hc] - bb_t[h:h + 1, :], 0.0)), 0.0) * dtb_t[h:h + 1, :]
                    outs.append(_dot((scores[g] * (ef + eb)).astype(BF16), xsb))
                o = jnp.where(m_lo, outs[0], outs[1])
                o += _dot(cms[g], hpb) * qdf[:, cols]
                oacc_ref[rows, cols] = o
                upd += _dot_tn(bms[g], (xs_pair * kwf[:, cols]).astype(BF16))
            hst_ref[0, pp] = hp * decay_matrix(qdf[C - 1:C, :], pp) + upd
        return carry

    lax.fori_loop(0, nc, forward, 0)

    def backward(t, carry):
        c = nc - 1 - t
        r0 = pl.multiple_of(c * C, C)
        rows = pl.ds(r0, C)
        xc = xc_s[rows, :].astype(F32)
        bcv = bcs_s[rows, :]
        bmat, cmat = bcv[:, :LANE], bcv[:, LANE:]
        zero = jnp.zeros_like(bmat)
        dtb, bb = decays(rows, 1)
        qdb = _dot01_r(jnp.exp(bb), e64_ref[...])
        kwb = _dot01_r(dtb * jnp.exp(bb[0:1, :] - bb), e64_ref[...])
        blocks = {}
        for pp in range(2):
            hp = hst_ref[1, pp]
            hpb = hp.astype(BF16)
            upd = jnp.zeros((LANE, LANE), F32)
            for g in range(SSD_GROUPS):
                blk = 2 * g + pp
                cols = slice(blk * LANE, (blk + 1) * LANE)
                xs_pair = xc[:, cols]
                o = oacc_ref[rows, cols] + _dot(jnp.where(masks[g], cmat, zero), hpb) * qdb[:, cols]
                upd += _dot_tn(jnp.where(masks[g], bmat, zero), (xs_pair * kwb[:, cols]).astype(BF16))
                y = (o + d_ref[:, cols] * xs_pair) * _silu(z_ref[rows, cols].astype(F32))
                blocks[blk] = y
            hst_ref[1, pp] = hp * decay_matrix(qdb[0:1, :], pp) + upd
        for g in range(SSD_GROUPS):
            y0, y1 = blocks[2 * g], blocks[2 * g + 1]
            ss = jnp.sum(y0 * y0, axis=-1, keepdims=True) + jnp.sum(y1 * y1, axis=-1, keepdims=True)
            r = lax.rsqrt(ss / (2 * LANE) + EPS)
            for i, yb in enumerate((y0, y1)):
                cols = slice((2 * g + i) * LANE, (2 * g + i + 1) * LANE)
                y_ref[rows, cols] = (yb * r * gain_ref[:, cols]).astype(y_ref.dtype)
        return carry

    lax.fori_loop(0, nc, backward, 0)
    if has_state:
        st_ref[...] = hst_ref[...]


def _ssd_pack_state(st):
    n = st.shape[0]
    st = st.reshape(n, 2, SSD_GROUPS, 2, 2, SSD_STATE, SSD_HEAD_DIM)
    return st.transpose(0, 1, 3, 2, 5, 4, 6).reshape(n, 2, 2, LANE, LANE)


def _ssd_unpack_state(st):
    n = st.shape[0]
    st = st.reshape(n, 2, 2, SSD_GROUPS, SSD_STATE, 2, SSD_HEAD_DIM)
    return st.transpose(0, 1, 3, 2, 5, 4, 6).reshape(n, 2, SSD_HEADS, SSD_STATE, SSD_HEAD_DIM)


def _ssd_params(ssd_conv_w, ssd_conv_b, ssd_a_log, ssd_dt_bias, ssd_d, ssd_norm_g):
    def lanes8(v):
        return jnp.pad(v.astype(F32), ((0, 0), (0, 0), (0, LANE - SSD_HEADS)))
    e8, e64 = _head_expanders()
    return dict(wx=ssd_conv_w[:, :, :GROUP_W], wbc=ssd_conv_w[:, :, GROUP_W:],
                bx=ssd_conv_b[:, None, :GROUP_W], bbc=ssd_conv_b[:, None, GROUP_W:],
                a=lanes8(-jnp.exp(ssd_a_log.astype(F32))), dtb=lanes8(ssd_dt_bias),
                d=jnp.repeat(ssd_d, SSD_HEAD_DIM, axis=-1)[:, None, :], gain=ssd_norm_g[:, None, :],
                e8=e8, e64=e64)


def _ssd_scan(um, us, sp, layer, h0, n_seq, T, row_block0):
    has_h0 = h0 is not None
    has_state = not has_h0

    def col(off, w):
        return pl.BlockSpec((T, w), lambda b: (row_block0 + b, off // w))

    def per_layer(arr):
        return pl.BlockSpec((None,) + arr.shape[1:], lambda b: (layer,) + (0,) * (arr.ndim - 1))

    def const(arr):
        return pl.BlockSpec(arr.shape, lambda b: (0,) * arr.ndim)
    names = ('wx', 'wbc', 'bx', 'bbc', 'a', 'dtb', 'd', 'gain')
    in_specs = [col(MAIN_SSD_Z, GROUP_W), col(MAIN_SSD_XS, GROUP_W), col(MAIN_SSD_BC, 2 * LANE),
                pl.BlockSpec((T, LANE), lambda b: (row_block0 + b, SIDE_DT // LANE))]
    in_specs += [per_layer(sp[n]) for n in names] + [const(sp['e8']), const(sp['e64'])]
    args = [um, um, um, us] + [sp[n] for n in names] + [sp['e8'], sp['e64']]
    st_spec = pl.BlockSpec((None, 2, 2, LANE, LANE), lambda b: (b, 0, 0, 0, 0))
    if has_h0:
        in_specs.append(st_spec)
        args.append(h0)
    out_specs = [pl.BlockSpec((T, GROUP_W), lambda b: (b, 0))]
    out_shape = [jax.ShapeDtypeStruct((n_seq * T, GROUP_W), BF16)]
    if has_state:
        out_specs.append(st_spec)
        out_shape.append(jax.ShapeDtypeStruct((n_seq, 2, 2, LANE, LANE), F32))
    res = pl.pallas_call(
        functools.partial(_ssd_kernel, T=T, has_h0=has_h0, has_state=has_state),
        grid=(n_seq,),
        in_specs=in_specs,
        out_specs=out_specs,
        out_shape=out_shape,
        scratch_shapes=[pltpu.VMEM((T, GROUP_W), BF16), pltpu.VMEM((T, 2 * LANE), BF16),
                        pltpu.VMEM((T, GROUP_W), F32), pltpu.VMEM((2, 2, LANE, LANE), F32)],
        compiler_params=pltpu.CompilerParams(dimension_semantics=("parallel",), vmem_limit_bytes=VMEM_LIMIT),
        name="ssd",
    )(*args)
    return (res[0], res[1]) if has_state else (res[0], None)


GLA_MACRO = 128


def _log_sigmoid(x):
    return jnp.minimum(x, 0.0) - jnp.log(1.0 + jnp.exp(-jnp.abs(x)))


def _gla_kernel(*refs, T, has_h0, has_state):
    it = iter(refs)
    q_ref, k_ref, v_ref, r_ref, g1_ref = (next(it) for _ in range(5))
    wg_ref, bg_ref, gain_ref, ind_ref = (next(it) for _ in range(4))
    h0_ref = next(it) if has_h0 else None
    y_ref = next(it)
    st_ref = next(it) if has_state else None
    oacc_ref, hst_ref, qbuf, kbuf, bfbuf, bbbuf, vbuf, obuf = (next(it) for _ in range(8))

    C, L = GLA_MACRO, GLA_CHUNK
    nb = C // L
    nc = T // C
    ii = lax.broadcasted_iota(jnp.int32, (C, C), 0)
    jj = lax.broadcasted_iota(jnp.int32, (C, C), 1)
    same = (ii // L) == (jj // L)
    tri_l = jnp.logical_and(same, jj <= ii).astype(BF16)
    tri_u = jnp.logical_and(same, jj >= ii).astype(BF16)
    ones_b = same.astype(BF16)
    lane = lax.broadcasted_iota(jnp.int32, (C, LANE), 1)
    masks = (lane < GLA_QK, lane >= GLA_QK)
    HALF = 8
    ri8 = lax.broadcasted_iota(jnp.int32, (HALF, 2 * LANE), 0)
    bdmask = (lax.broadcasted_iota(jnp.int32, (C, nb * LANE), 0) // L
              == lax.broadcasted_iota(jnp.int32, (C, nb * LANE), 1) // LANE)

    if has_h0:
        hst_ref[...] = h0_ref[...]
    else:
        hst_ref[...] = jnp.zeros_like(hst_ref)

    def log_decay(rows, d):
        g1 = g1_ref[rows, :]
        hi = g1.astype(BF16)
        lo = (g1 - hi.astype(F32)).astype(BF16)
        logits = _dot(hi, wg_ref[d]) + _dot(lo, wg_ref[d]) + bg_ref[d]
        return _log_sigmoid(logits) / GLA_GATE_TEMP

    def recurrence(d, rows, qt, kt, dec, blocks):
        zero = jnp.zeros((), BF16)

        def block_diag(x):
            return jnp.where(bdmask, jnp.concatenate([x] * nb, axis=1), zero)
        upds = []
        for h in range(GLA_HEADS):
            p, e = divmod(h, 2)
            km = jnp.where(masks[e], kt[:, p * LANE:(p + 1) * LANE], 0.0).astype(BF16)
            upds.append(_dot_tn(v_ref[rows, h * LANE:(h + 1) * LANE], block_diag(km)))
        snaps = []
        for h in range(GLA_HEADS):
            lanes = slice((h // 2) * LANE, (h // 2 + 1) * LANE)
            ht = hst_ref[d, h]
            snap = [None] * nb
            for blk in blocks:
                snap[blk] = ht.astype(BF16)
                ht = ht * dec[blk * L:blk * L + 1, lanes] + upds[h][:, blk * LANE:(blk + 1) * LANE]
            hst_ref[d, h] = ht
            snaps.append(jnp.concatenate(snap, axis=1))
        outs = []
        for h in range(GLA_HEADS):
            p, e = divmod(h, 2)
            qm = jnp.where(masks[e], qt[:, p * LANE:(p + 1) * LANE], 0.0).astype(BF16)
            outs.append(_dot_nt(block_diag(qm), snaps[h]))
        return outs

    def intra_block(b0):
        pieces = []
        for j in range(L):
            kj, bfj, bbj = kbuf[b0 + j:b0 + j + 1, :], bfbuf[b0 + j:b0 + j + 1, :], bbbuf[b0 + j:b0 + j + 1, :]
            halves = []
            for s in range(L // HALF):
                rs = slice(b0 + s * HALF, b0 + (s + 1) * HALF)
                qk = qbuf[rs, :] * kj
                if j < s * HALF:
                    e = qk * jnp.exp(bfbuf[rs, :] - bfj)
                elif j >= (s + 1) * HALF:
                    e = qk * jnp.exp(bbbuf[rs, :] - bbj)
                else:
                    rel = ri8 + (s * HALF - j)
                    e = qk * jnp.exp(jnp.where(rel >= 0, bfbuf[rs, :] - bfj, bbbuf[rs, :] - bbj))
                    e = jnp.where(rel == 0, 2.0 * e, e)
                halves.append(e)
            pieces.append(jnp.concatenate(halves, axis=0).astype(BF16))
        spread = _dot(jnp.concatenate(pieces, axis=0), ind_ref[...])
        acc = spread[0:L, :] * vbuf[b0:b0 + 1, :]
        for j in range(1, L):
            acc += spread[j * L:(j + 1) * L, :] * vbuf[b0 + j:b0 + j + 1, :]
        return acc

    def forward(c, carry):
        r0 = pl.multiple_of(c * C, C)
        rows = pl.ds(r0, C)
        q = q_ref[rows, :].astype(F32) * (GLA_QK ** -0.5)
        k = k_ref[rows, :].astype(F32)
        la_f, la_b = log_decay(rows, 0), log_decay(rows, 1)
        bf, tot_f = _dot01_l(tri_l, la_f), _dot01_l(ones_b, la_f)
        bb = _dot01_l(tri_u, la_b)
        qbuf[...] = q
        kbuf[...] = k
        bfbuf[...] = bf
        bbbuf[...] = bb
        vbuf[...] = v_ref[rows, :].astype(F32)
        for blk in range(nb):
            obuf[blk * L:(blk + 1) * L, :] = intra_block(blk * L)
        qt, kt, dec = q * jnp.exp(bf), k * jnp.exp(tot_f - bf), jnp.exp(tot_f)
        inter = recurrence(0, rows, qt, kt, dec, range(nb))
        for h in range(GLA_HEADS):
            hcols = slice(h * LANE, (h + 1) * LANE)
            oacc_ref[rows, hcols] = obuf[:, hcols] + inter[h]
        return carry

    lax.fori_loop(0, nc, forward, 0)

    def backward(t, carry):
        c = nc - 1 - t
        r0 = pl.multiple_of(c * C, C)
        rows = pl.ds(r0, C)
        q = q_ref[rows, :].astype(F32) * (GLA_QK ** -0.5)
        k = k_ref[rows, :].astype(F32)
        la_b = log_decay(rows, 1)
        bb, tot_b = _dot01_l(tri_u, la_b), _dot01_l(ones_b, la_b)
        qt, kt, dec = q * jnp.exp(bb), k * jnp.exp(tot_b - bb), jnp.exp(tot_b)
        inter = recurrence(1, rows, qt, kt, dec, range(nb - 1, -1, -1))
        for h in range(GLA_HEADS):
            hcols = slice(h * LANE, (h + 1) * LANE)
            o = oacc_ref[rows, hcols] + inter[h]
            yn = o * lax.rsqrt(jnp.mean(o * o, axis=-1, keepdims=True) + EPS) * gain_ref[:, hcols]
            y_ref[rows, hcols] = (_silu(r_ref[rows, hcols].astype(F32)) * yn).astype(y_ref.dtype)
        return carry

    lax.fori_loop(0, nc, backward, 0)
    if has_state:
        st_ref[...] = hst_ref[...]


def _gla_pack_state(st):
    n = st.shape[0]
    ht = jnp.swapaxes(st, -1, -2)
    z = jnp.zeros_like(ht)
    even = jnp.concatenate([ht, z], axis=-1)
    odd = jnp.concatenate([z, ht], axis=-1)
    sel = (jnp.arange(GLA_HEADS) % 2 == 0)[None, None, :, None, None]
    return jnp.where(sel, even, odd)


def _gla_unpack_state(st):
    even, odd = st[..., :GLA_QK], st[..., GLA_QK:]
    sel = (jnp.arange(GLA_HEADS) % 2 == 0)[None, None, :, None, None]
    return jnp.swapaxes(jnp.where(sel, even, odd), -1, -2)


def _gla_params(gla_w_g2, gla_b_g, gla_norm_g):
    wg = jnp.zeros((DEPTH, 2, LANE, 2 * LANE), F32)
    g1_lane = SIDE_G1 - SIDE_DT
    wg = wg.at[:, :, g1_lane:g1_lane + GLA_GATE_RANK, :].set(gla_w_g2)
    ind = np.zeros((2 * LANE, GROUP_W), np.float32)
    for h in range(GLA_HEADS):
        ind[h * GLA_QK:(h + 1) * GLA_QK, h * GLA_V:(h + 1) * GLA_V] = 1.0
    return dict(wg=wg.astype(BF16), bg=gla_b_g[:, :, None, :], gain=gla_norm_g[:, None, :],
                ind=jnp.asarray(ind, BF16))


def _gla_scan(um, us, gp, layer, h0, n_seq, T, row_block0):
    has_h0 = h0 is not None
    has_state = not has_h0
    C = GLA_MACRO

    def col(off, w):
        return pl.BlockSpec((T, w), lambda b: (row_block0 + b, off // w))

    def per_layer(arr):
        return pl.BlockSpec((None,) + arr.shape[1:], lambda b: (layer,) + (0,) * (arr.ndim - 1))
    in_specs = [col(MAIN_GLA_Q, 2 * LANE), col(MAIN_GLA_K, 2 * LANE), col(MAIN_GLA_V, GROUP_W), col(MAIN_GLA_R, GROUP_W),
                pl.BlockSpec((T, LANE), lambda b: (row_block0 + b, SIDE_DT // LANE)),
                per_layer(gp['wg']), per_layer(gp['bg']), per_layer(gp['gain']),
                pl.BlockSpec(gp['ind'].shape, lambda b: (0, 0))]
    args = [um, um, um, um, us, gp['wg'], gp['bg'], gp['gain'], gp['ind']]
    st_spec = pl.BlockSpec((None, 2, GLA_HEADS, LANE, LANE), lambda b: (b, 0, 0, 0, 0))
    if has_h0:
        in_specs.append(st_spec)
        args.append(h0)
    out_specs = [pl.BlockSpec((T, GROUP_W), lambda b: (b, 0))]
    out_shape = [jax.ShapeDtypeStruct((n_seq * T, GROUP_W), BF16)]
    if has_state:
        out_specs.append(st_spec)
        out_shape.append(jax.ShapeDtypeStruct((n_seq, 2, GLA_HEADS, LANE, LANE), F32))
    res = pl.pallas_call(
        functools.partial(_gla_kernel, T=T, has_h0=has_h0, has_state=has_state),
        grid=(n_seq,),
        in_specs=in_specs,
        out_specs=out_specs,
        out_shape=out_shape,
        scratch_shapes=[pltpu.VMEM((T, GROUP_W), F32), pltpu.VMEM((2, GLA_HEADS, LANE, LANE), F32),
                        pltpu.VMEM((C, 2 * LANE), F32), pltpu.VMEM((C, 2 * LANE), F32),
                        pltpu.VMEM((C, 2 * LANE), F32), pltpu.VMEM((C, 2 * LANE), F32),
                        pltpu.VMEM((C, GROUP_W), F32), pltpu.VMEM((C, GROUP_W), F32)],
        compiler_params=pltpu.CompilerParams(dimension_semantics=("parallel",), vmem_limit_bytes=VMEM_LIMIT),
        name="gla",
    )(*args)
    return (res[0], res[1]) if has_state else (res[0], None)


def kernel(x_prompt, x_sample, cache_mla_kv, state_ssd, state_ret, state_gla, c, c_ctx, w_mod, b_mod, norm_ffn1, ffn1_wg, ffn1_wu, ffn1_wd, norm_mix, w_in, ssd_conv_w, ssd_conv_b, ssd_a_log, ssd_dt_bias, ssd_d, ssd_norm_g, mla_q_lat_gain, mla_w_q_up, mla_q_gain, mla_kv_lat_gain, mla_w_kv_up, mla_k_gain, ret_decay_logit, ret_norm_g, gla_w_g2, gla_b_g, gla_norm_g, w_out, norm_ffn2, ffn2_wg, ffn2_wu, ffn2_wd):
    sp = _ssd_params(ssd_conv_w, ssd_conv_b, ssd_a_log, ssd_dt_bias, ssd_d, ssd_norm_g)
    ssd_h0 = jnp.stack([_ssd_pack_state(state_ssd[:, l]) for l in range(DEPTH)], axis=1)
    gp = _gla_params(gla_w_g2, gla_b_g, gla_norm_g)
    gla_h0 = jnp.stack([_gla_pack_state(state_gla[:, l]) for l in range(DEPTH)], axis=1)

    main_idx, side_idx = _in_proj_columns()
    w_main = _gather_columns(w_in, main_idx).astype(BF16)
    w_side = _gather_columns(w_in, side_idx).astype(BF16)
    w_out_b = w_out.astype(BF16)
    f1 = (ffn1_wg.astype(BF16), ffn1_wu.astype(BF16), ffn1_wd.astype(BF16))
    f2 = (ffn2_wg.astype(BF16), ffn2_wu.astype(BF16), ffn2_wd.astype(BF16))
    g_ffn1 = norm_ffn1.reshape(DEPTH, 1, D_MODEL)
    g_mix = norm_mix.reshape(DEPTH, 1, D_MODEL)
    g_ffn2 = norm_ffn2.reshape(DEPTH, 1, D_MODEL)
    mw = _mla_weights(mla_q_lat_gain, mla_w_q_up, mla_q_gain, mla_kv_lat_gain, mla_w_kv_up, mla_k_gain)
    cos_lat, sin_lat = _rope_tables(DEC_SEQ)
    cos_all = jnp.concatenate([jnp.ones((TOK_TM, LANE), F32), cos_lat], axis=0)
    sin_all = jnp.concatenate([jnp.zeros((TOK_TM, LANE), F32), sin_lat], axis=0)
    ret_lg = jax.nn.log_sigmoid(ret_decay_logit.astype(F32))
    ret_gain = ret_norm_g.reshape(DEPTH, 1, GROUP_W)
    ret_h0 = state_ret.reshape(DEC_BATCH, DEPTH, 2, 2, LANE, LANE)

    c_all = jnp.zeros((MOD_ROWS, D_MODEL), F32).at[0].set(c_ctx).at[1:1 + DEC_BATCH].set(c)
    mod = _modulation(c_all, w_mod, b_mod).reshape(DEPTH * MOD_ROWS, 1, N_MOD * D_MODEL)

    x = None
    kv_list, ssd_list, ret_list, gla_list = [], [], [], []
    for l in range(DEPTH):
        if l == 0:
            x = _ffn((x_prompt.reshape(CTX_ROWS, D_MODEL), x_sample.reshape(LAT_ROWS, D_MODEL)),
                     mod, g_ffn1, *f1, l, 0, tm=TOK_TM)
        else:
            x = _ffn((x,), mod, g_ffn1, *f1, l, 0)
        um, us = _in_proj(x, mod, g_mix, w_main, w_side, l)

        ym_c, ym_l, kv_lat = _mla_layer(um, us, cache_mla_kv[:, l], mw, cos_all, sin_all, l)
        yr_c, s_ret = _retention(um, ret_lg[l], ret_gain[l], None, None, BATCH, SEQ, 0)
        yr_l, _ = _retention(um, ret_lg[l], ret_gain[l], ret_h0[:, l], (cos_lat, sin_lat),
                             DEC_BATCH, DEC_SEQ, CTX_ROWS // DEC_SEQ)
        ys_c, s_ssd = _ssd_scan(um, us, sp, l, None, BATCH, SEQ, 0)
        ys_l, _ = _ssd_scan(um, us, sp, l, ssd_h0[:, l], DEC_BATCH, DEC_SEQ, CTX_ROWS // DEC_SEQ)
        yg_c, s_gla = _gla_scan(um, us, gp, l, None, BATCH, SEQ, 0)
        yg_l, _ = _gla_scan(um, us, gp, l, gla_h0[:, l], DEC_BATCH, DEC_SEQ, CTX_ROWS // DEC_SEQ)

        kv_list.append(kv_lat[:CTX_ROWS, :MLA_CACHE_W].reshape(BATCH, SEQ, MLA_CACHE_W))
        ssd_list.append(_ssd_unpack_state(s_ssd))
        ret_list.append(s_ret.reshape(BATCH, 2, RET_HEADS, RET_QK, RET_V))
        gla_list.append(_gla_unpack_state(s_gla))
        x = _out_proj(x, [ys_c, ym_c, yr_c, yg_c], [ys_l, ym_l, yr_l, yg_l], mod, w_out_b, l)
        x = _ffn((x,), mod, g_ffn2, *f2, l, 6, split_out=(l == DEPTH - 1))
    y_p = x[0].reshape(BATCH, SEQ, D_MODEL)
    y_s = x[1].reshape(DEC_BATCH, DEC_SEQ, D_MODEL)
    return (y_p, y_s, jnp.stack(kv_list, axis=1), jnp.stack(ssd_list, axis=1),
            jnp.stack(ret_list, axis=1), jnp.stack(gla_list, axis=1))
```

```python
import functools

import jax
import jax.numpy as jnp
import numpy as np
from jax import lax
from jax.experimental import pallas as pl
from jax.experimental.pallas import tpu as pltpu

F32 = jnp.float32
BF16 = jnp.bfloat16

D_MODEL = 2048
BATCH = 32
SEQ = 256
DEPTH = 4
DEC_BATCH = 4
DEC_SEQ = 4096
PAST_LEN = 256
GRID_W = 64
ROPE_BASE = 10000.0
EPS = 1e-6
D_FF = 5632
N_MOD = 9
GROUP_W = D_MODEL // 4
ATTN_BLOCK = 128

SSD_HEAD_DIM = 64
SSD_HEADS = GROUP_W // SSD_HEAD_DIM
SSD_STATE = 64
SSD_GROUPS = 2
SSD_CONV_K = 3
SSD_CHUNK = 128
SSD_CONV_CH = GROUP_W + 2 * SSD_GROUPS * SSD_STATE
SSD_IN = GROUP_W + SSD_CONV_CH + SSD_HEADS

MLA_HEADS = 4
MLA_NOPE = 128
MLA_ROPE = 64
MLA_V = GROUP_W // MLA_HEADS
MLA_Q_RANK = 384
MLA_KV_RANK = 128
MLA_QK = MLA_NOPE + MLA_ROPE
MLA_IN = MLA_Q_RANK + MLA_KV_RANK + MLA_ROPE
MLA_CACHE_W = MLA_KV_RANK + MLA_ROPE

RET_HEADS = 4
RET_QK = 64
RET_V = GROUP_W // RET_HEADS
RET_CHUNK = 128
RET_IN = 2 * RET_HEADS * RET_QK + 2 * GROUP_W

GLA_HEADS = 4
GLA_QK = 64
GLA_V = GROUP_W // GLA_HEADS
GLA_GATE_RANK = 16
GLA_GATE_TEMP = 16.0
GLA_CHUNK = 16
GLA_IN = 2 * GLA_HEADS * GLA_QK + GROUP_W + GLA_GATE_RANK + GROUP_W

IN_W = SSD_IN + MLA_IN + RET_IN + GLA_IN

LANE = 128
CTX_ROWS = BATCH * SEQ
LAT_ROWS = DEC_BATCH * DEC_SEQ
ROWS = CTX_ROWS + LAT_ROWS
MOD_ROWS = 8
VMEM_LIMIT = 56 * 1024 * 1024
TOK_TM = 512
MLA_HEAD_W = 2 * LANE

MAIN_SSD_Z = 0
MAIN_SSD_XS = 512
MAIN_RET_V = 1024
MAIN_RET_G = 1536
MAIN_GLA_V = 2048
MAIN_GLA_R = 2560
MAIN_SSD_BC = 3072
MAIN_RET_Q = 3328
MAIN_RET_K = 3584
MAIN_GLA_Q = 3840
MAIN_GLA_K = 4096
MAIN_MLA_CQ = 4352
MAIN_W = 4864
MAIN_TN = MAIN_W // 2
SIDE_CKV = 0
SIDE_KPE = SIDE_CKV + MLA_KV_RANK
SIDE_DT = 2 * LANE
SIDE_G1 = SIDE_DT + SSD_HEADS
SIDE_W = 3 * LANE


def _in_proj_columns():
    o_ssd, o_mla, o_ret, o_gla = 0, SSD_IN, SSD_IN + MLA_IN, SSD_IN + MLA_IN + RET_IN
    main = np.full((MAIN_W,), -1, np.int64)

    def put(dst, src, n):
        main[dst:dst + n] = src + np.arange(n)
    qk = RET_HEADS * RET_QK
    put(MAIN_SSD_Z, o_ssd, GROUP_W)
    put(MAIN_SSD_XS, o_ssd + GROUP_W, GROUP_W)
    put(MAIN_SSD_BC, o_ssd + 2 * GROUP_W, 2 * SSD_GROUPS * SSD_STATE)
    put(MAIN_MLA_CQ, o_mla, MLA_Q_RANK)
    put(MAIN_RET_Q, o_ret, qk)
    put(MAIN_RET_K, o_ret + qk, qk)
    put(MAIN_RET_V, o_ret + 2 * qk, GROUP_W)
    put(MAIN_RET_G, o_ret + 2 * qk + GROUP_W, GROUP_W)
    put(MAIN_GLA_Q, o_gla, qk)
    put(MAIN_GLA_K, o_gla + qk, qk)
    put(MAIN_GLA_V, o_gla + 2 * qk, GROUP_W)
    put(MAIN_GLA_R, o_gla + 2 * qk + GROUP_W + GLA_GATE_RANK, GROUP_W)
    side = np.full((SIDE_W,), -1, np.int64)
    side[SIDE_CKV:SIDE_CKV + MLA_KV_RANK + MLA_ROPE] = o_mla + MLA_Q_RANK + np.arange(MLA_KV_RANK + MLA_ROPE)
    side[SIDE_DT:SIDE_DT + SSD_HEADS] = o_ssd + GROUP_W + SSD_CONV_CH + np.arange(SSD_HEADS)
    side[SIDE_G1:SIDE_G1 + GLA_GATE_RANK] = o_gla + 2 * qk + GROUP_W + np.arange(GLA_GATE_RANK)
    return main, side


def _gather_columns(w, idx):
    safe = np.where(idx < 0, 0, idx)
    out = jnp.take(w, jnp.asarray(safe, jnp.int32), axis=-1)
    return jnp.where(jnp.asarray(idx >= 0), out, 0.0)


def _mod_row(i, tm):
    ctx_tiles = CTX_ROWS // tm
    per_seq = DEC_SEQ // tm
    return jnp.where(i < ctx_tiles, 0, 1 + (i - ctx_tiles) // per_seq)


def _mod_spec(layer, which, tm, n_grid):
    if n_grid == 1:
        return pl.BlockSpec((1, 1, D_MODEL), lambda i: (layer * MOD_ROWS + _mod_row(i, tm), 0, which))
    return pl.BlockSpec((1, 1, D_MODEL), lambda i, j: (layer * MOD_ROWS + _mod_row(i, tm), 0, which))


def _dot(a, b):
    return jnp.dot(a, b, preferred_element_type=F32)


def _dot_nt(a, b):
    return lax.dot_general(a, b, (((1,), (1,)), ((), ())), preferred_element_type=F32)


def _dot_tn(a, b):
    return lax.dot_general(a, b, (((0,), (0,)), ((), ())), preferred_element_type=F32)


def _silu(x):
    return x * jax.nn.sigmoid(x)


def _mod_kernel(c_ref, w_ref, b_ref, o_ref):
    s = _silu(c_ref[...])
    hi = s.astype(BF16)
    lo = (s - hi.astype(F32)).astype(BF16)
    w = w_ref[...].astype(BF16)
    o_ref[...] = _dot(hi, w) + _dot(lo, w) + b_ref[...]


def _modulation(c_all, w_mod, b_mod):
    tn = 1024
    n = N_MOD * D_MODEL
    return pl.pallas_call(
        _mod_kernel,
        grid=(DEPTH, n // tn),
        in_specs=[pl.BlockSpec((MOD_ROWS, D_MODEL), lambda l, j: (0, 0)),
                  pl.BlockSpec((None, D_MODEL, tn), lambda l, j: (l, 0, j)),
                  pl.BlockSpec((None, 1, tn), lambda l, j: (l, 0, j))],
        out_specs=pl.BlockSpec((None, MOD_ROWS, tn), lambda l, j: (l, 0, j)),
        out_shape=jax.ShapeDtypeStruct((DEPTH, MOD_ROWS, n), F32),
        compiler_params=pltpu.CompilerParams(dimension_semantics=("arbitrary", "arbitrary"),
                                             vmem_limit_bytes=VMEM_LIMIT),
        name="modulation",
    )(c_all, w_mod, b_mod.reshape(DEPTH, 1, n))


NORM_ROWS = 16


def _norm_modulate(x_ref, g_ref, shift_ref, scale_ref, h_ref):
    gs = g_ref[...] * (1.0 + scale_ref[0])
    shift = shift_ref[0]

    def body(r, carry):
        rows = pl.ds(pl.multiple_of(r * NORM_ROWS, NORM_ROWS), NORM_ROWS)
        x = x_ref[rows, :]
        inv = lax.rsqrt(jnp.mean(x * x, axis=-1, keepdims=True) + EPS)
        h_ref[rows, :] = ((x * inv) * gs + shift).astype(BF16)
        return carry

    lax.fori_loop(0, x_ref.shape[0] // NORM_ROWS, body, 0, unroll=8)


FFN_TM = 1024
FFN_VMEM_LIMIT = 60 * 1024 * 1024
FFN_TF = 512
FFN_TN = 256
FFN_NF = D_FF // FFN_TF
FFN_NN = D_MODEL // FFN_TN
MXU_W = 256


def _ffn_kernel(*refs, n_src, n_out, ctx_tiles):
    it = iter(refs)
    srcs = [(next(it), next(it)) for _ in range(n_src)]
    g_ref, shift_ref, scale_ref, gate_ref, wg_ref, wu_ref, wd_ref = (next(it) for _ in range(7))
    outs = [next(it) for _ in range(n_out)]
    h_ref, a_ref = next(it), next(it)
    j = pl.program_id(1)
    is_ctx = pl.program_id(0) < ctx_tiles

    def for_group(k, n_groups, fn):
        if n_groups == 1:
            fn()
        else:
            pl.when(is_ctx if k == 0 else jnp.logical_not(is_ctx))(fn)

    @pl.when(j == 0)
    def _():
        for k, (x_ref, _) in enumerate(srcs):
            for_group(k, n_src, functools.partial(_norm_modulate, x_ref, g_ref, shift_ref, scale_ref, h_ref))

    @pl.when(j < FFN_NF)
    def _():
        h = h_ref[...]
        base = pl.multiple_of(j * FFN_TF, FFN_TF)
        for s in range(FFN_TF // MXU_W):
            cols = slice(s * MXU_W, (s + 1) * MXU_W)
            g = _dot(h, wg_ref[:, cols])
            u = _dot(h, wu_ref[:, cols])
            a_ref[:, pl.ds(base + s * MXU_W, MXU_W)] = (_silu(g) * u).astype(BF16)

    @pl.when(j >= FFN_NF)
    def _():
        y = 0.5 * gate_ref[0] * _dot(a_ref[...], wd_ref[...])
        n_groups = max(n_src, n_out)
        for k in range(n_groups):
            xt_ref, o_ref = srcs[min(k, n_src - 1)][1], outs[min(k, n_out - 1)]

            def store(xt_ref=xt_ref, o_ref=o_ref):
                o_ref[...] = xt_ref[...] + y
            for_group(k, n_groups, store)


def _ffn(xs, mod, norm_g, wg, wu, wd, layer, mod_base, tm=FFN_TM, split_out=False):
    n_src, n_out = len(xs), 2 if split_out else 1
    ctx_tiles = CTX_ROWS // tm

    def up_tile(i, j):
        return (layer, 0, jnp.minimum(j, FFN_NF - 1))

    def down_tile(j):
        return jnp.maximum(j - FFN_NF, 0)

    def row_of(k, n_groups, i):
        if n_groups == 1:
            return i
        return jnp.minimum(i, ctx_tiles - 1) if k == 0 else jnp.maximum(i - ctx_tiles, 0)

    def col_of(k, n_groups, i, j):
        if n_groups == 1:
            return down_tile(j)
        if k == 0:
            return jnp.where(i < ctx_tiles, down_tile(j), FFN_NN - 1)
        return jnp.where(i >= ctx_tiles, down_tile(j), 0)
    gate_blocks = D_MODEL // FFN_TN
    in_specs, args = [], []
    for k, x in enumerate(xs):
        in_specs += [pl.BlockSpec((tm, D_MODEL), lambda i, j, k=k: (row_of(k, n_src, i), 0)),
                     pl.BlockSpec((tm, FFN_TN), lambda i, j, k=k: (row_of(k, n_src, i), col_of(k, n_src, i, j)))]
        args += [x, x]
    in_specs += [pl.BlockSpec((None, 1, D_MODEL), lambda i, j: (layer, 0, 0)),
                 _mod_spec(layer, mod_base + 0, tm, 2),
                 _mod_spec(layer, mod_base + 1, tm, 2),
                 pl.BlockSpec((1, 1, FFN_TN), lambda i, j: (layer * MOD_ROWS + _mod_row(i, tm), 0,
                                                            (mod_base + 2) * gate_blocks + down_tile(j))),
                 pl.BlockSpec((None, D_MODEL, FFN_TF), up_tile),
                 pl.BlockSpec((None, D_MODEL, FFN_TF), up_tile),
                 pl.BlockSpec((None, D_FF, FFN_TN), lambda i, j: (layer, 0, down_tile(j)))]
    args += [norm_g, mod, mod, mod, wg, wu, wd]
    out_rows = (CTX_ROWS, LAT_ROWS) if split_out else (ROWS,)
    res = pl.pallas_call(
        functools.partial(_ffn_kernel, n_src=n_src, n_out=n_out, ctx_tiles=ctx_tiles),
        grid=(ROWS // tm, FFN_NF + FFN_NN),
        in_specs=in_specs,
        out_specs=[pl.BlockSpec((tm, FFN_TN), lambda i, j, k=k: (row_of(k, n_out, i), col_of(k, n_out, i, j)))
                   for k in range(n_out)],
        out_shape=[jax.ShapeDtypeStruct((r, D_MODEL), F32) for r in out_rows],
        scratch_shapes=[pltpu.VMEM((tm, D_MODEL), BF16), pltpu.VMEM((tm, D_FF), BF16)],
        compiler_params=pltpu.CompilerParams(dimension_semantics=("parallel", "arbitrary"),
                                             vmem_limit_bytes=FFN_VMEM_LIMIT),
        name="ffn",
    )(*args)
    return res if split_out else res[0]


def _in_proj_kernel(x_ref, g_ref, shift_ref, scale_ref, w_ref, ws_ref, o_ref, os_ref, h_ref):
    @pl.when(pl.program_id(1) == 0)
    def _():
        x = x_ref[...]
        inv = lax.rsqrt(jnp.mean(x * x, axis=-1, keepdims=True) + EPS)
        h_ref[...] = ((x * inv) * (g_ref[...] * (1.0 + scale_ref[0])) + shift_ref[0]).astype(BF16)
        os_ref[...] = _dot(h_ref[...], ws_ref[...])

    o_ref[...] = _dot(h_ref[...], w_ref[...]).astype(o_ref.dtype)


def _in_proj(x, mod, norm_g, w_main, w_side, layer, tm=TOK_TM):
    return pl.pallas_call(
        _in_proj_kernel,
        grid=(ROWS // tm, MAIN_W // MAIN_TN),
        in_specs=[pl.BlockSpec((tm, D_MODEL), lambda i, j: (i, 0)),
                  pl.BlockSpec((None, 1, D_MODEL), lambda i, j: (layer, 0, 0)),
                  _mod_spec(layer, 3, tm, 2),
                  _mod_spec(layer, 4, tm, 2),
                  pl.BlockSpec((None, D_MODEL, MAIN_TN), lambda i, j: (layer, 0, j)),
                  pl.BlockSpec((None, D_MODEL, SIDE_W), lambda i, j: (layer, 0, 0))],
        out_specs=[pl.BlockSpec((tm, MAIN_TN), lambda i, j: (i, j)),
                   pl.BlockSpec((tm, SIDE_W), lambda i, j: (i, 0))],
        out_shape=[jax.ShapeDtypeStruct((ROWS, MAIN_W), BF16),
                   jax.ShapeDtypeStruct((ROWS, SIDE_W), F32)],
        scratch_shapes=[pltpu.VMEM((tm, D_MODEL), BF16)],
        compiler_params=pltpu.CompilerParams(dimension_semantics=("parallel", "arbitrary"),
                                             vmem_limit_bytes=VMEM_LIMIT),
        name="in_proj",
    )(x, norm_g, mod, mod, w_main, w_side)


def _out_proj_kernel(x_ref, *refs, ctx_tiles):
    ctx_refs, lat_refs, (gate_ref, w_ref, o_ref) = refs[0:4], refs[4:8], refs[8:]

    def run(y_refs):
        acc = _dot(y_refs[0][...], w_ref[0:GROUP_W, :])
        for g in range(1, 4):
            acc += _dot(y_refs[g][...], w_ref[g * GROUP_W:(g + 1) * GROUP_W, :])
        o_ref[...] = x_ref[...] + gate_ref[0] * acc

    @pl.when(pl.program_id(0) < ctx_tiles)
    def _():
        run(ctx_refs)

    @pl.when(pl.program_id(0) >= ctx_tiles)
    def _():
        run(lat_refs)


def _out_proj(x, ys_ctx, ys_lat, mod, w, layer, tm=TOK_TM):
    ctx_tiles = CTX_ROWS // tm
    row = pl.BlockSpec((tm, D_MODEL), lambda i: (i, 0))
    cspec = pl.BlockSpec((tm, GROUP_W), lambda i: (jnp.minimum(i, ctx_tiles - 1), 0))
    lspec = pl.BlockSpec((tm, GROUP_W), lambda i: (jnp.maximum(i - ctx_tiles, 0), 0))
    return pl.pallas_call(
        functools.partial(_out_proj_kernel, ctx_tiles=ctx_tiles),
        grid=(ROWS // tm,),
        in_specs=[row] + [cspec] * 4 + [lspec] * 4 + [
            _mod_spec(layer, 5, tm, 1),
            pl.BlockSpec((None, D_MODEL, D_MODEL), lambda i: (layer, 0, 0))],
        out_specs=row,
        out_shape=jax.ShapeDtypeStruct((ROWS, D_MODEL), F32),
        compiler_params=pltpu.CompilerParams(dimension_semantics=("parallel",),
                                             vmem_limit_bytes=VMEM_LIMIT),
        name="out_proj",
    )(x, *ys_ctx, *ys_lat, mod, w)


def _rope_tables(T):
    n_rows = T // GRID_W
    row = jnp.repeat(jnp.arange(n_rows, dtype=F32), GRID_W)
    col = jnp.tile(jnp.arange(GRID_W, dtype=F32), n_rows)
    d_axis = MLA_ROPE // 2
    inv = ROPE_BASE ** (-jnp.arange(0, d_axis, 2, dtype=F32) / d_axis)
    ar, ac = row[:, None] * inv, col[:, None] * inv
    cos = jnp.concatenate([jnp.cos(ar), jnp.cos(ar), jnp.cos(ac), jnp.cos(ac)], axis=-1)
    sin = jnp.concatenate([-jnp.sin(ar), jnp.sin(ar), -jnp.sin(ac), jnp.sin(ac)], axis=-1)
    return jnp.tile(cos, (1, 2)), jnp.tile(sin, (1, 2))


def _swap16(x):
    lane = lax.broadcasted_iota(jnp.int32, x.shape, 1)
    up = pltpu.roll(x, LANE - 16, 1)
    down = pltpu.roll(x, 16, 1)
    return jnp.where((lane % 32) < 16, up, down)


def _rope(x, cos, sin):
    return x * cos + _swap16(x) * sin


def _mla_q_kernel(c0_ref, c1_ref, c2_ref, gl_ref, w_ref, gq_ref, cos_ref, sin_ref, o_ref):
    cs = [r[...].astype(F32) for r in (c0_ref, c1_ref, c2_ref)]
    ss = sum(jnp.sum(c * c, axis=-1, keepdims=True) for c in cs)
    r = lax.rsqrt(ss / MLA_Q_RANK + EPS)
    q = sum(_dot((cs[i] * r * gl_ref[:, i * LANE:(i + 1) * LANE]).astype(BF16),
                 w_ref[i * LANE:(i + 1) * LANE, :]) for i in range(3))
    cos, sin = cos_ref[...], sin_ref[...]
    scale = MLA_QK ** -0.5
    for h in range(MLA_HEADS):
        a = q[:, h * MLA_HEAD_W:h * MLA_HEAD_W + LANE]
        b = q[:, h * MLA_HEAD_W + LANE:(h + 1) * MLA_HEAD_W]
        ssq = jnp.sum(a * a, axis=-1, keepdims=True) + jnp.sum(b * b, axis=-1, keepdims=True)
        rh = lax.rsqrt(ssq / MLA_QK + EPS) * scale
        o_ref[:, h * MLA_HEAD_W:h * MLA_HEAD_W + LANE] = (a * rh * gq_ref[:, :LANE]).astype(BF16)
        o_ref[:, h * MLA_HEAD_W + LANE:(h + 1) * MLA_HEAD_W] = _rope(b * rh * gq_ref[:, LANE:], cos, sin).astype(BF16)


def _rope_tile_index(i, tm):
    ctx_tiles = CTX_ROWS // tm
    return jnp.where(i < ctx_tiles, 0, 1 + (i - ctx_tiles) % (DEC_SEQ // tm))


def _mla_q(um, gl, wq, gq, cos_all, sin_all, layer, tm=TOK_TM):
    cq = MAIN_MLA_CQ // LANE
    tab = pl.BlockSpec((tm, LANE), lambda i: (_rope_tile_index(i, tm), 0))
    return pl.pallas_call(
        _mla_q_kernel,
        grid=(ROWS // tm,),
        in_specs=[pl.BlockSpec((tm, LANE), lambda i: (i, cq)),
                  pl.BlockSpec((tm, LANE), lambda i: (i, cq + 1)),
                  pl.BlockSpec((tm, LANE), lambda i: (i, cq + 2)),
                  pl.BlockSpec((None, 1, MLA_Q_RANK), lambda i: (layer, 0, 0)),
                  pl.BlockSpec((None, MLA_Q_RANK, MLA_HEADS * MLA_HEAD_W), lambda i: (layer, 0, 0)),
                  pl.BlockSpec((None, 1, MLA_HEAD_W), lambda i: (layer, 0, 0)),
                  tab, tab],
        out_specs=pl.BlockSpec((tm, MLA_HEADS * MLA_HEAD_W), lambda i: (i, 0)),
        out_shape=jax.ShapeDtypeStruct((ROWS, MLA_HEADS * MLA_HEAD_W), BF16),
        compiler_params=pltpu.CompilerParams(dimension_semantics=("parallel",), vmem_limit_bytes=VMEM_LIMIT),
        name="mla_q",
    )(um, um, um, gl, wq, gq, cos_all, sin_all)


def _mla_kv_kernel(s_ref, gl_ref, w_ref, gk_ref, cos_ref, sin_ref, lat_ref, k_ref, v_ref, *, normalize):
    ckv = s_ref[:, :LANE]
    kpe = s_ref[:, LANE:]
    if normalize:
        ckv = ckv * lax.rsqrt(jnp.mean(ckv * ckv, axis=-1, keepdims=True) + EPS) * gl_ref[...]
    lat_ref[:, :LANE] = ckv
    lat_ref[:, LANE:] = kpe
    kv = _dot(ckv.astype(BF16), w_ref[...])
    ss_pe = jnp.sum(kpe * kpe, axis=-1, keepdims=True)
    cos, sin = cos_ref[...], sin_ref[...]
    for h in range(MLA_HEADS):
        a = kv[:, h * LANE:(h + 1) * LANE]
        rh = lax.rsqrt((jnp.sum(a * a, axis=-1, keepdims=True) + ss_pe) / MLA_QK + EPS)
        k_ref[:, h * MLA_HEAD_W:h * MLA_HEAD_W + LANE] = (a * rh * gk_ref[:, :LANE]).astype(BF16)
        k_ref[:, h * MLA_HEAD_W + LANE:(h + 1) * MLA_HEAD_W] = _rope(kpe * rh * gk_ref[:, LANE:], cos, sin).astype(BF16)
    v_ref[...] = kv[:, MLA_HEADS * LANE:].astype(BF16)


def _mla_kv(src, gl, wkv, gk, cos_all, sin_all, layer, *, normalize, tab_index, tm=TOK_TM):
    rows = src.shape[0]
    tab = pl.BlockSpec((tm, LANE), lambda i: (tab_index(i, tm), 0))
    return pl.pallas_call(
        functools.partial(_mla_kv_kernel, normalize=normalize),
        grid=(rows // tm,),
        in_specs=[pl.BlockSpec((tm, 2 * LANE), lambda i: (i, 0)),
                  pl.BlockSpec((None, 1, MLA_KV_RANK), lambda i: (layer, 0, 0)),
                  pl.BlockSpec((None, MLA_KV_RANK, 2 * MLA_HEADS * LANE), lambda i: (layer, 0, 0)),
                  pl.BlockSpec((None, 1, MLA_HEAD_W), lambda i: (layer, 0, 0)),
                  tab, tab],
        out_specs=[pl.BlockSpec((tm, 2 * LANE), lambda i: (i, 0)),
                   pl.BlockSpec((tm, MLA_HEADS * MLA_HEAD_W), lambda i: (i, 0)),
                   pl.BlockSpec((tm, GROUP_W), lambda i: (i, 0))],
        out_shape=[jax.ShapeDtypeStruct((rows, 2 * LANE), F32),
                   jax.ShapeDtypeStruct((rows, MLA_HEADS * MLA_HEAD_W), BF16),
                   jax.ShapeDtypeStruct((rows, GROUP_W), BF16)],
        compiler_params=pltpu.CompilerParams(dimension_semantics=("parallel",), vmem_limit_bytes=VMEM_LIMIT),
        name="mla_kv",
    )(src, gl, wkv, gk, cos_all, sin_all)


ATTN_TQ_LAT = 256
ATTN_HEADS_PER_STEP = 4


def _attn_kernel(q_ref, *refs):
    o_ref = refs[-1]
    kv = [(refs[i], refs[i + 1]) for i in range(0, len(refs) - 1, 2)]
    heads = range(ATTN_HEADS_PER_STEP)
    scores = [[_dot_nt(q_ref[:, h * MLA_HEAD_W:(h + 1) * MLA_HEAD_W], k_ref[:, h * MLA_HEAD_W:(h + 1) * MLA_HEAD_W])
               for k_ref, _ in kv] for h in heads]
    for h in heads:
        m = functools.reduce(jnp.maximum, [jnp.max(s, axis=-1, keepdims=True) for s in scores[h]])
        ps = [jnp.exp(s - m) for s in scores[h]]
        l = sum(jnp.sum(p, axis=-1, keepdims=True) for p in ps)
        o = sum(_dot(p.astype(BF16), v_ref[:, h * MLA_V:(h + 1) * MLA_V]) for p, (_, v_ref) in zip(ps, kv))
        o_ref[:, h * MLA_V:(h + 1) * MLA_V] = (o / l).astype(o_ref.dtype)


def _attention(q, sources, n_seq, t, tq, q_row0):
    nq = t // tq
    hs = ATTN_HEADS_PER_STEP
    in_specs = [pl.BlockSpec((tq, hs * MLA_HEAD_W), lambda b, h, i: (q_row0 // tq + b * nq + i, h))]
    args = [q]
    for k, v, s_len, row0 in sources:
        in_specs += [pl.BlockSpec((s_len, hs * MLA_HEAD_W), lambda b, h, i, o=row0 // s_len: (o + b, h)),
                     pl.BlockSpec((s_len, hs * MLA_V), lambda b, h, i, o=row0 // s_len: (o + b, h))]
        args += [k, v]
    return pl.pallas_call(
        _attn_kernel,
        grid=(n_seq, MLA_HEADS // hs, nq),
        in_specs=in_specs,
        out_specs=pl.BlockSpec((tq, hs * MLA_V), lambda b, h, i: (b * nq + i, h)),
        out_shape=jax.ShapeDtypeStruct((n_seq * t, GROUP_W), BF16),
        compiler_params=pltpu.CompilerParams(dimension_semantics=("parallel", "parallel", "arbitrary"),
                                             vmem_limit_bytes=VMEM_LIMIT),
        name="mla_attention",
    )(*args)


def _mla_weights(mla_q_lat_gain, mla_w_q_up, mla_q_gain, mla_kv_lat_gain, mla_w_kv_up, mla_k_gain):
    qcol = np.full((MLA_HEADS * MLA_HEAD_W,), -1, np.int64)
    kvcol = np.zeros((2 * MLA_HEADS * LANE,), np.int64)
    for h in range(MLA_HEADS):
        qcol[h * MLA_HEAD_W:h * MLA_HEAD_W + MLA_QK] = h * MLA_QK + np.arange(MLA_QK)
        kvcol[h * LANE:(h + 1) * LANE] = h * (MLA_NOPE + MLA_V) + np.arange(MLA_NOPE)
        kvcol[(MLA_HEADS + h) * LANE:(MLA_HEADS + h + 1) * LANE] = h * (MLA_NOPE + MLA_V) + MLA_NOPE + np.arange(MLA_V)
    pad = jnp.zeros((DEPTH, MLA_HEAD_W - MLA_QK), F32)
    return dict(
        gl=mla_q_lat_gain.reshape(DEPTH, 1, MLA_Q_RANK),
        wq=_gather_columns(mla_w_q_up, qcol).astype(BF16),
        gq=jnp.concatenate([mla_q_gain, pad], axis=-1).reshape(DEPTH, 1, MLA_HEAD_W),
        gkv=mla_kv_lat_gain.reshape(DEPTH, 1, MLA_KV_RANK),
        wkv=_gather_columns(mla_w_kv_up, kvcol).astype(BF16),
        gk=jnp.concatenate([mla_k_gain, pad], axis=-1).reshape(DEPTH, 1, MLA_HEAD_W))


def _mla_layer(um, us, cache_l, mw, cos_all, sin_all, layer):
    q = _mla_q(um, mw['gl'], mw['wq'], mw['gq'], cos_all, sin_all, layer)
    kv_lat, k, v = _mla_kv(us, mw['gkv'], mw['wkv'], mw['gk'], cos_all, sin_all, layer,
                           normalize=True, tab_index=_rope_tile_index)
    cache2 = jnp.pad(cache_l.reshape(DEC_BATCH * PAST_LEN, MLA_CACHE_W), ((0, 0), (0, 2 * LANE - MLA_CACHE_W)))
    _, k_c, v_c = _mla_kv(cache2, mw['gkv'], mw['wkv'], mw['gk'], cos_all, sin_all, layer,
                          normalize=False, tab_index=lambda i, tm: 0)
    y_ctx = _attention(q, [(k, v, SEQ, 0)], BATCH, SEQ, SEQ, 0)
    y_lat = _attention(q, [(k, v, DEC_SEQ, CTX_ROWS), (k_c, v_c, PAST_LEN, 0)], DEC_BATCH, DEC_SEQ, ATTN_TQ_LAT,
                       CTX_ROWS)
    return y_ctx, y_lat, kv_lat


def _ret_kernel(*refs, T, C, rope, has_h0, has_state):
    it = iter(refs)
    q_ref, k_ref, v_ref, g_ref = next(it), next(it), next(it), next(it)
    cos_ref, sin_ref = (next(it), next(it)) if rope else (None, None)
    lg_ref, gain_ref = next(it), next(it)
    h0_ref = next(it) if has_h0 else None
    y_ref = next(it)
    st_ref = next(it) if has_state else None
    qs_ref, ks_ref, oacc_ref, dm_ref, tab_ref, hst_ref = (next(it) for _ in range(6))

    nc = T // C
    ii = lax.broadcasted_iota(jnp.int32, (C, C), 0)
    jj = lax.broadcasted_iota(jnp.int32, (C, C), 1)
    dif = (ii - jj).astype(F32)
    lane = lax.broadcasted_iota(jnp.int32, (C, LANE), 1)
    rowi = lax.broadcasted_iota(jnp.int32, (C, LANE), 0).astype(F32)
    m_lo = lane < RET_QK
    hrow = lax.broadcasted_iota(jnp.int32, (LANE, LANE), 0) < RET_QK

    for h in range(RET_HEADS):
        lf, lb = lg_ref[0, h], lg_ref[1, h]
        dm_ref[h] = (jnp.where(dif >= 0, jnp.exp(lf * jnp.maximum(dif, 0.0)), 0.0)
                     + jnp.where(dif <= 0, jnp.exp(lb * jnp.maximum(-dif, 0.0)), 0.0))
    for p in range(2):
        lf = jnp.where(m_lo, lg_ref[0, 2 * p], lg_ref[0, 2 * p + 1])
        lb = jnp.where(m_lo, lg_ref[1, 2 * p], lg_ref[1, 2 * p + 1])
        tab_ref[p, 0] = jnp.exp(lf * (rowi + 1.0))
        tab_ref[p, 1] = jnp.exp(lf * (C - 1.0 - rowi))
        tab_ref[p, 2] = jnp.exp(lb * (C - rowi))
        tab_ref[p, 3] = jnp.exp(lb * rowi)
    if has_h0:
        hst_ref[...] = h0_ref[...]
    else:
        hst_ref[...] = jnp.zeros_like(hst_ref)

    def chunk_decay(d, p):
        return jnp.exp(jnp.where(hrow, lg_ref[d, 2 * p], lg_ref[d, 2 * p + 1]) * float(C))

    def forward(c, carry):
        r0 = pl.multiple_of(c * C, C)
        rows = pl.ds(r0, C)
        for p in range(2):
            cols = slice(p * LANE, (p + 1) * LANE)
            qp = q_ref[rows, cols].astype(F32)
            kp = k_ref[rows, cols].astype(F32) * (RET_QK ** -0.5)
            if rope:
                qp = _rope(qp, cos_ref[rows, :], sin_ref[rows, :])
                kp = _rope(kp, cos_ref[rows, :], sin_ref[rows, :])
            qs_ref[rows, cols] = qp.astype(BF16)
            ks_ref[rows, cols] = kp.astype(BF16)
            kpb = kp.astype(BF16)
            hp = hst_ref[0, p]
            hpb = hp.astype(BF16)
            upd = jnp.zeros((LANE, LANE), F32)
            for e in range(2):
                h = 2 * p + e
                mh = m_lo if e == 0 else jnp.logical_not(m_lo)
                hcols = slice(h * LANE, (h + 1) * LANE)
                qm = jnp.where(mh, qp, 0.0)
                s = _dot_nt(qm.astype(BF16), kpb)
                vh = v_ref[rows, hcols]
                o = _dot((s * dm_ref[h]).astype(BF16), vh)
                o += _dot((qm * tab_ref[p, 0]).astype(BF16), hpb)
                oacc_ref[rows, hcols] = o
                upd += _dot_tn(jnp.where(mh, kp * tab_ref[p, 1], 0.0).astype(BF16), vh)
            hst_ref[0, p] = hp * chunk_decay(0, p) + upd
        return carry

    lax.fori_loop(0, nc, forward, 0)

    def backward(t, carry):
        c = nc - 1 - t
        r0 = pl.multiple_of(c * C, C)
        rows = pl.ds(r0, C)
        for p in range(2):
            cols = slice(p * LANE, (p + 1) * LANE)
            qp = qs_ref[rows, cols].astype(F32)
            kp = ks_ref[rows, cols].astype(F32)
            hp = hst_ref[1, p]
            hpb = hp.astype(BF16)
            upd = jnp.zeros((LANE, LANE), F32)
            for e in range(2):
                h = 2 * p + e
                mh = m_lo if e == 0 else jnp.logical_not(m_lo)
                hcols = slice(h * LANE, (h + 1) * LANE)
                vh = v_ref[rows, hcols]
                o = oacc_ref[rows, hcols] + _dot(jnp.where(mh, qp * tab_ref[p, 2], 0.0).astype(BF16), hpb)
                upd += _dot_tn(jnp.where(mh, kp * tab_ref[p, 3], 0.0).astype(BF16), vh)
                oc = o - jnp.mean(o, axis=-1, keepdims=True)
                yn = oc * lax.rsqrt(jnp.mean(oc * oc, axis=-1, keepdims=True) + EPS) * gain_ref[:, hcols]
                y_ref[rows, hcols] = (_silu(g_ref[rows, hcols].astype(F32)) * yn).astype(y_ref.dtype)
            hst_ref[1, p] = hp * chunk_decay(1, p) + upd
        return carry

    lax.fori_loop(0, nc, backward, 0)
    if has_state:
        st_ref[...] = hst_ref[...]


def _retention(um, lg, gain, h0, rope_tabs, n_seq, T, row_block0):
    C = RET_CHUNK
    rope = rope_tabs is not None
    has_h0 = h0 is not None
    has_state = not has_h0

    def col(off, w):
        return pl.BlockSpec((T, w), lambda b: (row_block0 + b, off // w))
    in_specs = [col(MAIN_RET_Q, 2 * LANE), col(MAIN_RET_K, 2 * LANE), col(MAIN_RET_V, GROUP_W), col(MAIN_RET_G, GROUP_W)]
    args = [um, um, um, um]
    if rope:
        in_specs += [pl.BlockSpec((T, LANE), lambda b: (0, 0))] * 2
        args += list(rope_tabs)
    in_specs += [pl.BlockSpec(memory_space=pltpu.SMEM), pl.BlockSpec((1, GROUP_W), lambda b: (0, 0))]
    args += [lg, gain]
    st_spec = pl.BlockSpec((None, 2, 2, LANE, LANE), lambda b: (b, 0, 0, 0, 0))
    if has_h0:
        in_specs.append(st_spec)
        args.append(h0)
    out_specs = [pl.BlockSpec((T, GROUP_W), lambda b: (b, 0))]
    out_shape = [jax.ShapeDtypeStruct((n_seq * T, GROUP_W), BF16)]
    if has_state:
        out_specs.append(st_spec)
        out_shape.append(jax.ShapeDtypeStruct((n_seq, 2, 2, LANE, LANE), F32))
    res = pl.pallas_call(
        functools.partial(_ret_kernel, T=T, C=C, rope=rope, has_h0=has_h0, has_state=has_state),
        grid=(n_seq,),
        in_specs=in_specs,
        out_specs=out_specs,
        out_shape=out_shape,
        scratch_shapes=[pltpu.VMEM((T, 2 * LANE), BF16), pltpu.VMEM((T, 2 * LANE), BF16),
                        pltpu.VMEM((T, GROUP_W), F32), pltpu.VMEM((RET_HEADS, C, C), F32),
                        pltpu.VMEM((2, 4, C, LANE), F32), pltpu.VMEM((2, 2, LANE, LANE), F32)],
        compiler_params=pltpu.CompilerParams(dimension_semantics=("parallel",), vmem_limit_bytes=VMEM_LIMIT),
        name="retention",
    )(*args)
    return (res[0], res[1]) if has_state else (res[0], None)


def _split3(x):
    x1 = x.astype(BF16)
    r1 = x - x1.astype(F32)
    x2 = r1.astype(BF16)
    x3 = (r1 - x2.astype(F32)).astype(BF16)
    return x1, x2, x3


def _dot01_l(m, x):
    x1, x2, x3 = _split3(x)
    return _dot(m, x1) + _dot(m, x2) + _dot(m, x3)


def _dot01_r(x, m):
    x1, x2, x3 = _split3(x)
    return _dot(x1, m) + _dot(x2, m) + _dot(x3, m)


def _softplus(x):
    return jnp.maximum(x, 0.0) + jnp.log(1.0 + jnp.exp(-jnp.abs(x)))


def _head_expanders():
    e8 = np.zeros((LANE, SSD_HEADS * LANE), np.float32)
    e64 = np.zeros((LANE, GROUP_W), np.float32)
    for h in range(SSD_HEADS):
        e8[h, h * LANE:(h + 1) * LANE] = 1.0
        e64[h, h * SSD_HEAD_DIM:(h + 1) * SSD_HEAD_DIM] = 1.0
    return jnp.asarray(e8, BF16), jnp.asarray(e64, BF16)


def _ssd_kernel(*refs, T, has_h0, has_state):
    it = iter(refs)
    z_ref, xs_ref, bc_ref, dt_ref = next(it), next(it), next(it), next(it)
    wx_ref, wbc_ref, bx_ref, bbc_ref = next(it), next(it), next(it), next(it)
    a_ref, dtb_ref, d_ref, gain_ref, e8_ref, e64_ref = (next(it) for _ in range(6))
    h0_ref = next(it) if has_h0 else None
    y_ref = next(it)
    st_ref = next(it) if has_state else None
    xc_s, bcs_s, oacc_ref, hst_ref = next(it), next(it), next(it), next(it)

    C = SSD_CHUNK
    nc = T // C
    ii = lax.broadcasted_iota(jnp.int32, (C, C), 0)
    jj = lax.broadcasted_iota(jnp.int32, (C, C), 1)
    tril, triu = ii >= jj, ii <= jj
    tril_b, triu_b = tril.astype(BF16), triu.astype(BF16)
    lane = lax.broadcasted_iota(jnp.int32, (C, LANE), 1)
    m_lo = lane < SSD_STATE
    masks = (m_lo, jnp.logical_not(m_lo))
    hrow = lax.broadcasted_iota(jnp.int32, (LANE, LANE), 0) < SSD_STATE

    if has_h0:
        hst_ref[...] = h0_ref[...]
    else:
        hst_ref[...] = jnp.zeros_like(hst_ref)

    def decays(rows, d):
        dtv = _softplus(dt_ref[rows, :] + dtb_ref[d:d + 1, :])
        la = dtv * a_ref[d:d + 1, :]
        return dtv, _dot01_l(tril_b if d == 0 else triu_b, la)

    def conv(ref, w_ref, b_ref, c, r0):
        width = ref.shape[1]
        x = ref[pl.ds(r0, C), :].astype(F32)
        p0 = pl.multiple_of(jnp.maximum(r0 - 16, 0), 16)
        n0 = pl.multiple_of(jnp.minimum(r0 + C, T - 16), 16)
        prev_row = jnp.where(c > 0, ref[pl.ds(p0, 16), :].astype(F32)[15:16], 0.0)
        next_row = jnp.where(c < nc - 1, ref[pl.ds(n0, 16), :].astype(F32)[0:1], 0.0)
        rowi = lax.broadcasted_iota(jnp.int32, (C, width), 0)
        prev = jnp.where(rowi == 0, prev_row, pltpu.roll(x, 1, 0))
        nxt = jnp.where(rowi == C - 1, next_row, pltpu.roll(x, C - 1, 0))
        return _silu(prev * w_ref[0:1, :] + x * w_ref[1:2, :] + nxt * w_ref[2:3, :] + b_ref[...])

    def decay_matrix(qrow, pp):
        return jnp.where(hrow, qrow[:, pp * LANE:(pp + 1) * LANE],
                         qrow[:, (2 + pp) * LANE:(3 + pp) * LANE])

    def forward(c, carry):
        r0 = pl.multiple_of(c * C, C)
        rows = pl.ds(r0, C)
        xc = conv(xs_ref, wx_ref, bx_ref, c, r0)
        bcv = conv(bc_ref, wbc_ref, bbc_ref, c, r0)
        xc_s[rows, :] = xc.astype(BF16)
        bcs_s[rows, :] = bcv.astype(BF16)
        bmat, cmat = bcv[:, :LANE], bcv[:, LANE:]
        bmb = bmat.astype(BF16)
        dtf, bf = decays(rows, 0)
        dtb, bb = decays(rows, 1)
        bf_t, dtf_t, bb_t, dtb_t = bf.T, dtf.T, bb.T, dtb.T
        colf = _dot01_r(bf, e8_ref[...])
        colb = _dot01_r(bb, e8_ref[...])
        qdf = _dot01_r(jnp.exp(bf), e64_ref[...])
        kwf = _dot01_r(dtf * jnp.exp(bf[C - 1:C, :] - bf), e64_ref[...])
        cms = [jnp.where(masks[g], cmat, 0.0).astype(BF16) for g in range(SSD_GROUPS)]
        bms = [jnp.where(masks[g], bmat, 0.0).astype(BF16) for g in range(SSD_GROUPS)]
        scores = [_dot_nt(cms[g], bmb) for g in range(SSD_GROUPS)]
        for pp in range(2):
            hp = hst_ref[0, pp]
            hpb = hp.astype(BF16)
            upd = jnp.zeros((LANE, LANE), F32)
            for g in range(SSD_GROUPS):
                cols = slice((2 * g + pp) * LANE, (2 * g + pp + 1) * LANE)
                xs_pair = xc[:, cols]
                xsb = xs_pair.astype(BF16)
                outs = []
                for e in range(2):
                    h = 4 * g + 2 * pp + e
                    hc = slice(h * LANE, (h + 1) * LANE)
                    ef = jnp.where(tril, jnp.exp(jnp.minimum(colf[:, hc] - bf_t[h:h + 1, :], 0.0)), 0.0) * dtf_t[h:h + 1, :]
                    eb = jnp.where(triu, jnp.exp(jnp.minimum(colb[:, hc] - bb_t[h:h + 1, :], 0.0)), 0.0) * dtb_t[h:h + 1, :]
                    outs.append(_dot((scores[g] * (ef + eb)).astype(BF16), xsb))
                o = jnp.where(m_lo, outs[0], outs[1])
                o += _dot(cms[g], hpb) * qdf[:, cols]
                oacc_ref[rows, cols] = o
                upd += _dot_tn(bms[g], (xs_pair * kwf[:, cols]).astype(BF16))
            hst_ref[0, pp] = hp * decay_matrix(qdf[C - 1:C, :], pp) + upd
        return carry

    lax.fori_loop(0, nc, forward, 0)

    def backward(t, carry):
        c = nc - 1 - t
        r0 = pl.multiple_of(c * C, C)
        rows = pl.ds(r0, C)
        xc = xc_s[rows, :].astype(F32)
        bcv = bcs_s[rows, :]
        bmat, cmat = bcv[:, :LANE], bcv[:, LANE:]
        zero = jnp.zeros_like(bmat)
        dtb, bb = decays(rows, 1)
        qdb = _dot01_r(jnp.exp(bb), e64_ref[...])
        kwb = _dot01_r(dtb * jnp.exp(bb[0:1, :] - bb), e64_ref[...])
        blocks = {}
        for pp in range(2):
            hp = hst_ref[1, pp]
            hpb = hp.astype(BF16)
            upd = jnp.zeros((LANE, LANE), F32)
            for g in range(SSD_GROUPS):
                blk = 2 * g + pp
                cols = slice(blk * LANE, (blk + 1) * LANE)
                xs_pair = xc[:, cols]
                o = oacc_ref[rows, cols] + _dot(jnp.where(masks[g], cmat, zero), hpb) * qdb[:, cols]
                upd += _dot_tn(jnp.where(masks[g], bmat, zero), (xs_pair * kwb[:, cols]).astype(BF16))
                y = (o + d_ref[:, cols] * xs_pair) * _silu(z_ref[rows, cols].astype(F32))
                blocks[blk] = y
            hst_ref[1, pp] = hp * decay_matrix(qdb[0:1, :], pp) + upd
        for g in range(SSD_GROUPS):
            y0, y1 = blocks[2 * g], blocks[2 * g + 1]
            ss = jnp.sum(y0 * y0, axis=-1, keepdims=True) + jnp.sum(y1 * y1, axis=-1, keepdims=True)
            r = lax.rsqrt(ss / (2 * LANE) + EPS)
            for i, yb in enumerate((y0, y1)):
                cols = slice((2 * g + i) * LANE, (2 * g + i + 1) * LANE)
                y_ref[rows, cols] = (yb * r * gain_ref[:, cols]).astype(y_ref.dtype)
        return carry

    lax.fori_loop(0, nc, backward, 0)
    if has_state:
        st_ref[...] = hst_ref[...]


def _ssd_pack_state(st):
    n = st.shape[0]
    st = st.reshape(n, 2, SSD_GROUPS, 2, 2, SSD_STATE, SSD_HEAD_DIM)
    return st.transpose(0, 1, 3, 2, 5, 4, 6).reshape(n, 2, 2, LANE, LANE)


def _ssd_unpack_state(st):
    n = st.shape[0]
    st = st.reshape(n, 2, 2, SSD_GROUPS, SSD_STATE, 2, SSD_HEAD_DIM)
    return st.transpose(0, 1, 3, 2, 5, 4, 6).reshape(n, 2, SSD_HEADS, SSD_STATE, SSD_HEAD_DIM)


def _ssd_params(ssd_conv_w, ssd_conv_b, ssd_a_log, ssd_dt_bias, ssd_d, ssd_norm_g):
    def lanes8(v):
        return jnp.pad(v.astype(F32), ((0, 0), (0, 0), (0, LANE - SSD_HEADS)))
    e8, e64 = _head_expanders()
    return dict(wx=ssd_conv_w[:, :, :GROUP_W], wbc=ssd_conv_w[:, :, GROUP_W:],
                bx=ssd_conv_b[:, None, :GROUP_W], bbc=ssd_conv_b[:, None, GROUP_W:],
                a=lanes8(-jnp.exp(ssd_a_log.astype(F32))), dtb=lanes8(ssd_dt_bias),
                d=jnp.repeat(ssd_d, SSD_HEAD_DIM, axis=-1)[:, None, :], gain=ssd_norm_g[:, None, :],
                e8=e8, e64=e64)


def _ssd_scan(um, us, sp, layer, h0, n_seq, T, row_block0):
    has_h0 = h0 is not None
    has_state = not has_h0

    def col(off, w):
        return pl.BlockSpec((T, w), lambda b: (row_block0 + b, off // w))

    def per_layer(arr):
        return pl.BlockSpec((None,) + arr.shape[1:], lambda b: (layer,) + (0,) * (arr.ndim - 1))

    def const(arr):
        return pl.BlockSpec(arr.shape, lambda b: (0,) * arr.ndim)
    names = ('wx', 'wbc', 'bx', 'bbc', 'a', 'dtb', 'd', 'gain')
    in_specs = [col(MAIN_SSD_Z, GROUP_W), col(MAIN_SSD_XS, GROUP_W), col(MAIN_SSD_BC, 2 * LANE),
                pl.BlockSpec((T, LANE), lambda b: (row_block0 + b, SIDE_DT // LANE))]
    in_specs += [per_layer(sp[n]) for n in names] + [const(sp['e8']), const(sp['e64'])]
    args = [um, um, um, us] + [sp[n] for n in names] + [sp['e8'], sp['e64']]
    st_spec = pl.BlockSpec((None, 2, 2, LANE, LANE), lambda b: (b, 0, 0, 0, 0))
    if has_h0:
        in_specs.append(st_spec)
        args.append(h0)
    out_specs = [pl.BlockSpec((T, GROUP_W), lambda b: (b, 0))]
    out_shape = [jax.ShapeDtypeStruct((n_seq * T, GROUP_W), BF16)]
    if has_state:
        out_specs.append(st_spec)
        out_shape.append(jax.ShapeDtypeStruct((n_seq, 2, 2, LANE, LANE), F32))
    res = pl.pallas_call(
        functools.partial(_ssd_kernel, T=T, has_h0=has_h0, has_state=has_state),
        grid=(n_seq,),
        in_specs=in_specs,
        out_specs=out_specs,
        out_shape=out_shape,
        scratch_shapes=[pltpu.VMEM((T, GROUP_W), BF16), pltpu.VMEM((T, 2 * LANE), BF16),
                        pltpu.VMEM((T, GROUP_W), F32), pltpu.VMEM((2, 2, LANE, LANE), F32)],
        compiler_params=pltpu.CompilerParams(dimension_semantics=("parallel",), vmem_limit_bytes=VMEM_LIMIT),
        name="ssd",
    )(*args)
    return (res[0], res[1]) if has_state else (res[0], None)


GLA_MACRO = 128


def _log_sigmoid(x):
    return jnp.minimum(x, 0.0) - jnp.log(1.0 + jnp.exp(-jnp.abs(x)))


def _gla_kernel(*refs, T, has_h0, has_state):
    it = iter(refs)
    q_ref, k_ref, v_ref, r_ref, g1_ref = (next(it) for _ in range(5))
    wg_ref, bg_ref, gain_ref, ind_ref = (next(it) for _ in range(4))
    h0_ref = next(it) if has_h0 else None
    y_ref = next(it)
    st_ref = next(it) if has_state else None
    oacc_ref, hst_ref, qbuf, kbuf, bfbuf, bbbuf, vbuf, obuf = (next(it) for _ in range(8))

    C, L = GLA_MACRO, GLA_CHUNK
    nb = C // L
    nc = T // C
    ii = lax.broadcasted_iota(jnp.int32, (C, C), 0)
    jj = lax.broadcasted_iota(jnp.int32, (C, C), 1)
    same = (ii // L) == (jj // L)
    tri_l = jnp.logical_and(same, jj <= ii).astype(BF16)
    tri_u = jnp.logical_and(same, jj >= ii).astype(BF16)
    ones_b = same.astype(BF16)
    lane = lax.broadcasted_iota(jnp.int32, (C, LANE), 1)
    masks = (lane < GLA_QK, lane >= GLA_QK)
    HALF = 8
    ri8 = lax.broadcasted_iota(jnp.int32, (HALF, 2 * LANE), 0)
    bdmask = (lax.broadcasted_iota(jnp.int32, (C, nb * LANE), 0) // L
              == lax.broadcasted_iota(jnp.int32, (C, nb * LANE), 1) // LANE)

    if has_h0:
        hst_ref[...] = h0_ref[...]
    else:
        hst_ref[...] = jnp.zeros_like(hst_ref)

    def log_decay(rows, d):
        g1 = g1_ref[rows, :]
        hi = g1.astype(BF16)
        lo = (g1 - hi.astype(F32)).astype(BF16)
        logits = _dot(hi, wg_ref[d]) + _dot(lo, wg_ref[d]) + bg_ref[d]
        return _log_sigmoid(logits) / GLA_GATE_TEMP

    def recurrence(d, rows, qt, kt, dec, blocks):
        zero = jnp.zeros((), BF16)

        def block_diag(x):
            return jnp.where(bdmask, jnp.concatenate([x] * nb, axis=1), zero)
        upds = []
        for h in range(GLA_HEADS):
            p, e = divmod(h, 2)
            km = jnp.where(masks[e], kt[:, p * LANE:(p + 1) * LANE], 0.0).astype(BF16)
            upds.append(_dot_tn(v_ref[rows, h * LANE:(h + 1) * LANE], block_diag(km)))
        snaps = []
        for h in range(GLA_HEADS):
            lanes = slice((h // 2) * LANE, (h // 2 + 1) * LANE)
            ht = hst_ref[d, h]
            snap = [None] * nb
            for blk in blocks:
                snap[blk] = ht.astype(BF16)
                ht = ht * dec[blk * L:blk * L + 1, lanes] + upds[h][:, blk * LANE:(blk + 1) * LANE]
            hst_ref[d, h] = ht
            snaps.append(jnp.concatenate(snap, axis=1))
        outs = []
        for h in range(GLA_HEADS):
            p, e = divmod(h, 2)
            qm = jnp.where(masks[e], qt[:, p * LANE:(p + 1) * LANE], 0.0).astype(BF16)
            outs.append(_dot_nt(block_diag(qm), snaps[h]))
        return outs

    def intra_block(b0):
        pieces = []
        for j in range(L):
            kj, bfj, bbj = kbuf[b0 + j:b0 + j + 1, :], bfbuf[b0 + j:b0 + j + 1, :], bbbuf[b0 + j:b0 + j + 1, :]
            halves = []
            for s in range(L // HALF):
                rs = slice(b0 + s * HALF, b0 + (s + 1) * HALF)
                qk = qbuf[rs, :] * kj
                if j < s * HALF:
                    e = qk * jnp.exp(bfbuf[rs, :] - bfj)
                elif j >= (s + 1) * HALF:
                    e = qk * jnp.exp(bbbuf[rs, :] - bbj)
                else:
                    rel = ri8 + (s * HALF - j)
                    e = qk * jnp.exp(jnp.where(rel >= 0, bfbuf[rs, :] - bfj, bbbuf[rs, :] - bbj))
                    e = jnp.where(rel == 0, 2.0 * e, e)
                halves.append(e)
            pieces.append(jnp.concatenate(halves, axis=0).astype(BF16))
        spread = _dot(jnp.concatenate(pieces, axis=0), ind_ref[...])
        acc = spread[0:L, :] * vbuf[b0:b0 + 1, :]
        for j in range(1, L):
            acc += spread[j * L:(j + 1) * L, :] * vbuf[b0 + j:b0 + j + 1, :]
        return acc

    def forward(c, carry):
        r0 = pl.multiple_of(c * C, C)
        rows = pl.ds(r0, C)
        q = q_ref[rows, :].astype(F32) * (GLA_QK ** -0.5)
        k = k_ref[rows, :].astype(F32)
        la_f, la_b = log_decay(rows, 0), log_decay(rows, 1)
        bf, tot_f = _dot01_l(tri_l, la_f), _dot01_l(ones_b, la_f)
        bb = _dot01_l(tri_u, la_b)
        qbuf[...] = q
        kbuf[...] = k
        bfbuf[...] = bf
        bbbuf[...] = bb
        vbuf[...] = v_ref[rows, :].astype(F32)
        for blk in range(nb):
            obuf[blk * L:(blk + 1) * L, :] = intra_block(blk * L)
        qt, kt, dec = q * jnp.exp(bf), k * jnp.exp(tot_f - bf), jnp.exp(tot_f)
        inter = recurrence(0, rows, qt, kt, dec, range(nb))
        for h in range(GLA_HEADS):
            hcols = slice(h * LANE, (h + 1) * LANE)
            oacc_ref[rows, hcols] = obuf[:, hcols] + inter[h]
        return carry

    lax.fori_loop(0, nc, forward, 0)

    def backward(t, carry):
        c = nc - 1 - t
        r0 = pl.multiple_of(c * C, C)
        rows = pl.ds(r0, C)
        q = q_ref[rows, :].astype(F32) * (GLA_QK ** -0.5)
        k = k_ref[rows, :].astype(F32)
        la_b = log_decay(rows, 1)
        bb, tot_b = _dot01_l(tri_u, la_b), _dot01_l(ones_b, la_b)
        qt, kt, dec = q * jnp.exp(bb), k * jnp.exp(tot_b - bb), jnp.exp(tot_b)
        inter = recurrence(1, rows, qt, kt, dec, range(nb - 1, -1, -1))
        for h in range(GLA_HEADS):
            hcols = slice(h * LANE, (h + 1) * LANE)
            o = oacc_ref[rows, hcols] + inter[h]
            yn = o * lax.rsqrt(jnp.mean(o * o, axis=-1, keepdims=True) + EPS) * gain_ref[:, hcols]
            y_ref[rows, hcols] = (_silu(r_ref[rows, hcols].astype(F32)) * yn).astype(y_ref.dtype)
        return carry

    lax.fori_loop(0, nc, backward, 0)
    if has_state:
        st_ref[...] = hst_ref[...]


def _gla_pack_state(st):
    n = st.shape[0]
    ht = jnp.swapaxes(st, -1, -2)
    z = jnp.zeros_like(ht)
    even = jnp.concatenate([ht, z], axis=-1)
    odd = jnp.concatenate([z, ht], axis=-1)
    sel = (jnp.arange(GLA_HEADS) % 2 == 0)[None, None, :, None, None]
    return jnp.where(sel, even, odd)


def _gla_unpack_state(st):
    even, odd = st[..., :GLA_QK], st[..., GLA_QK:]
    sel = (jnp.arange(GLA_HEADS) % 2 == 0)[None, None, :, None, None]
    return jnp.swapaxes(jnp.where(sel, even, odd), -1, -2)


def _gla_params(gla_w_g2, gla_b_g, gla_norm_g):
    wg = jnp.zeros((DEPTH, 2, LANE, 2 * LANE), F32)
    g1_lane = SIDE_G1 - SIDE_DT
    wg = wg.at[:, :, g1_lane:g1_lane + GLA_GATE_RANK, :].set(gla_w_g2)
    ind = np.zeros((2 * LANE, GROUP_W), np.float32)
    for h in range(GLA_HEADS):
        ind[h * GLA_QK:(h + 1) * GLA_QK, h * GLA_V:(h + 1) * GLA_V] = 1.0
    return dict(wg=wg.astype(BF16), bg=gla_b_g[:, :, None, :], gain=gla_norm_g[:, None, :],
                ind=jnp.asarray(ind, BF16))


def _gla_scan(um, us, gp, layer, h0, n_seq, T, row_block0):
    has_h0 = h0 is not None
    has_state = not has_h0
    C = GLA_MACRO

    def col(off, w):
        return pl.BlockSpec((T, w), lambda b: (row_block0 + b, off // w))

    def per_layer(arr):
        return pl.BlockSpec((None,) + arr.shape[1:], lambda b: (layer,) + (0,) * (arr.ndim - 1))
    in_specs = [col(MAIN_GLA_Q, 2 * LANE), col(MAIN_GLA_K, 2 * LANE), col(MAIN_GLA_V, GROUP_W), col(MAIN_GLA_R, GROUP_W),
                pl.BlockSpec((T, LANE), lambda b: (row_block0 + b, SIDE_DT // LANE)),
                per_layer(gp['wg']), per_layer(gp['bg']), per_layer(gp['gain']),
                pl.BlockSpec(gp['ind'].shape, lambda b: (0, 0))]
    args = [um, um, um, um, us, gp['wg'], gp['bg'], gp['gain'], gp['ind']]
    st_spec = pl.BlockSpec((None, 2, GLA_HEADS, LANE, LANE), lambda b: (b, 0, 0, 0, 0))
    if has_h0:
        in_specs.append(st_spec)
        args.append(h0)
    out_specs = [pl.BlockSpec((T, GROUP_W), lambda b: (b, 0))]
    out_shape = [jax.ShapeDtypeStruct((n_seq * T, GROUP_W), BF16)]
    if has_state:
        out_specs.append(st_spec)
        out_shape.append(jax.ShapeDtypeStruct((n_seq, 2, GLA_HEADS, LANE, LANE), F32))
    res = pl.pallas_call(
        functools.partial(_gla_kernel, T=T, has_h0=has_h0, has_state=has_state),
        grid=(n_seq,),
        in_specs=in_specs,
        out_specs=out_specs,
        out_shape=out_shape,
        scratch_shapes=[pltpu.VMEM((T, GROUP_W), F32), pltpu.VMEM((2, GLA_HEADS, LANE, LANE), F32),
                        pltpu.VMEM((C, 2 * LANE), F32), pltpu.VMEM((C, 2 * LANE), F32),
                        pltpu.VMEM((C, 2 * LANE), F32), pltpu.VMEM((C, 2 * LANE), F32),
                        pltpu.VMEM((C, GROUP_W), F32), pltpu.VMEM((C, GROUP_W), F32)],
        compiler_params=pltpu.CompilerParams(dimension_semantics=("parallel",), vmem_limit_bytes=VMEM_LIMIT),
        name="gla",
    )(*args)
    return (res[0], res[1]) if has_state else (res[0], None)


def kernel(x_prompt, x_sample, cache_mla_kv, state_ssd, state_ret, state_gla, c, c_ctx, w_mod, b_mod, norm_ffn1, ffn1_wg, ffn1_wu, ffn1_wd, norm_mix, w_in, ssd_conv_w, ssd_conv_b, ssd_a_log, ssd_dt_bias, ssd_d, ssd_norm_g, mla_q_lat_gain, mla_w_q_up, mla_q_gain, mla_kv_lat_gain, mla_w_kv_up, mla_k_gain, ret_decay_logit, ret_norm_g, gla_w_g2, gla_b_g, gla_norm_g, w_out, norm_ffn2, ffn2_wg, ffn2_wu, ffn2_wd):
    sp = _ssd_params(ssd_conv_w, ssd_conv_b, ssd_a_log, ssd_dt_bias, ssd_d, ssd_norm_g)
    ssd_h0 = jnp.stack([_ssd_pack_state(state_ssd[:, l]) for l in range(DEPTH)], axis=1)
    gp = _gla_params(gla_w_g2, gla_b_g, gla_norm_g)
    gla_h0 = jnp.stack([_gla_pack_state(state_gla[:, l]) for l in range(DEPTH)], axis=1)

    main_idx, side_idx = _in_proj_columns()
    w_main = _gather_columns(w_in, main_idx).astype(BF16)
    w_side = _gather_columns(w_in, side_idx).astype(BF16)
    w_out_b = w_out.astype(BF16)
    f1 = (ffn1_wg.astype(BF16), ffn1_wu.astype(BF16), ffn1_wd.astype(BF16))
    f2 = (ffn2_wg.astype(BF16), ffn2_wu.astype(BF16), ffn2_wd.astype(BF16))
    g_ffn1 = norm_ffn1.reshape(DEPTH, 1, D_MODEL)
    g_mix = norm_mix.reshape(DEPTH, 1, D_MODEL)
    g_ffn2 = norm_ffn2.reshape(DEPTH, 1, D_MODEL)
    mw = _mla_weights(mla_q_lat_gain, mla_w_q_up, mla_q_gain, mla_kv_lat_gain, mla_w_kv_up, mla_k_gain)
    cos_lat, sin_lat = _rope_tables(DEC_SEQ)
    cos_all = jnp.concatenate([jnp.ones((TOK_TM, LANE), F32), cos_lat], axis=0)
    sin_all = jnp.concatenate([jnp.zeros((TOK_TM, LANE), F32), sin_lat], axis=0)
    ret_lg = jax.nn.log_sigmoid(ret_decay_logit.astype(F32))
    ret_gain = ret_norm_g.reshape(DEPTH, 1, GROUP_W)
    ret_h0 = state_ret.reshape(DEC_BATCH, DEPTH, 2, 2, LANE, LANE)

    c_all = jnp.zeros((MOD_ROWS, D_MODEL), F32).at[0].set(c_ctx).at[1:1 + DEC_BATCH].set(c)
    mod = _modulation(c_all, w_mod, b_mod).reshape(DEPTH * MOD_ROWS, 1, N_MOD * D_MODEL)

    x = jnp.concatenate([x_prompt.reshape(CTX_ROWS, D_MODEL), x_sample.reshape(LAT_ROWS, D_MODEL)], axis=0)
    kv_list, ssd_list, ret_list, gla_list = [], [], [], []
    for l in range(DEPTH):
        x = _ffn((x,), mod, g_ffn1, *f1, l, 0)
        um, us = _in_proj(x, mod, g_mix, w_main, w_side, l)

        ym_c, ym_l, kv_lat = _mla_layer(um, us, cache_mla_kv[:, l], mw, cos_all, sin_all, l)
        yr_c, s_ret = _retention(um, ret_lg[l], ret_gain[l], None, None, BATCH, SEQ, 0)
        yr_l, _ = _retention(um, ret_lg[l], ret_gain[l], ret_h0[:, l], (cos_lat, sin_lat),
                             DEC_BATCH, DEC_SEQ, CTX_ROWS // DEC_SEQ)
        ys_c, s_ssd = _ssd_scan(um, us, sp, l, None, BATCH, SEQ, 0)
        ys_l, _ = _ssd_scan(um, us, sp, l, ssd_h0[:, l], DEC_BATCH, DEC_SEQ, CTX_ROWS // DEC_SEQ)
        yg_c, s_gla = _gla_scan(um, us, gp, l, None, BATCH, SEQ, 0)
        yg_l, _ = _gla_scan(um, us, gp, l, gla_h0[:, l], DEC_BATCH, DEC_SEQ, CTX_ROWS // DEC_SEQ)

        kv_list.append(kv_lat[:CTX_ROWS, :MLA_CACHE_W].reshape(BATCH, SEQ, MLA_CACHE_W))
        ssd_list.append(_ssd_unpack_state(s_ssd))
        ret_list.append(s_ret.reshape(BATCH, 2, RET_HEADS, RET_QK, RET_V))
        gla_list.append(_gla_unpack_state(s_gla))
        x = _out_proj(x, [ys_c, ym_c, yr_c, yg_c], [ys_l, ym_l, yr_l, yg_l], mod, w_out_b, l)
        x = _ffn((x,), mod, g_ffn2, *f2, l, 6, split_out=(l == DEPTH - 1))
    y_p = x[0].reshape(BATCH, SEQ, D_MODEL)
    y_s = x[1].reshape(DEC_BATCH, DEC_SEQ, D_MODEL)
    return (y_p, y_s, jnp.stack(kv_list, axis=1), jnp.stack(ssd_list, axis=1),
            jnp.stack(ret_list, axis=1), jnp.stack(gla_list, axis=1))
```

```python
import functools

import jax
import jax.numpy as jnp
import numpy as np
from jax import lax
from jax.experimental import pallas as pl
from jax.experimental.pallas import tpu as pltpu

F32 = jnp.float32
BF16 = jnp.bfloat16

D_MODEL = 2048
BATCH = 32
SEQ = 256
DEPTH = 4
DEC_BATCH = 4
DEC_SEQ = 4096
PAST_LEN = 256
GRID_W = 64
ROPE_BASE = 10000.0
EPS = 1e-6
D_FF = 5632
N_MOD = 9
GROUP_W = D_MODEL // 4
ATTN_BLOCK = 128

SSD_HEAD_DIM = 64
SSD_HEADS = GROUP_W // SSD_HEAD_DIM
SSD_STATE = 64
SSD_GROUPS = 2
SSD_CONV_K = 3
SSD_CHUNK = 128
SSD_CONV_CH = GROUP_W + 2 * SSD_GROUPS * SSD_STATE
SSD_IN = GROUP_W + SSD_CONV_CH + SSD_HEADS

MLA_HEADS = 4
MLA_NOPE = 128
MLA_ROPE = 64
MLA_V = GROUP_W // MLA_HEADS
MLA_Q_RANK = 384
MLA_KV_RANK = 128
MLA_QK = MLA_NOPE + MLA_ROPE
MLA_IN = MLA_Q_RANK + MLA_KV_RANK + MLA_ROPE
MLA_CACHE_W = MLA_KV_RANK + MLA_ROPE

RET_HEADS = 4
RET_QK = 64
RET_V = GROUP_W // RET_HEADS
RET_CHUNK = 128
RET_IN = 2 * RET_HEADS * RET_QK + 2 * GROUP_W

GLA_HEADS = 4
GLA_QK = 64
GLA_V = GROUP_W // GLA_HEADS
GLA_GATE_RANK = 16
GLA_GATE_TEMP = 16.0
GLA_CHUNK = 16
GLA_IN = 2 * GLA_HEADS * GLA_QK + GROUP_W + GLA_GATE_RANK + GROUP_W

IN_W = SSD_IN + MLA_IN + RET_IN + GLA_IN

LANE = 128
CTX_ROWS = BATCH * SEQ
LAT_ROWS = DEC_BATCH * DEC_SEQ
ROWS = CTX_ROWS + LAT_ROWS
MOD_ROWS = 8
VMEM_LIMIT = 56 * 1024 * 1024
TOK_TM = 512
MLA_HEAD_W = 2 * LANE

MAIN_SSD_Z = 0
MAIN_SSD_XS = 512
MAIN_RET_V = 1024
MAIN_RET_G = 1536
MAIN_GLA_V = 2048
MAIN_GLA_R = 2560
MAIN_SSD_BC = 3072
MAIN_RET_Q = 3328
MAIN_RET_K = 3584
MAIN_GLA_Q = 3840
MAIN_GLA_K = 4096
MAIN_MLA_CQ = 4352
MAIN_W = 4864
MAIN_TN = MAIN_W // 2
SIDE_CKV = 0
SIDE_KPE = SIDE_CKV + MLA_KV_RANK
SIDE_DT = 2 * LANE
SIDE_G1 = SIDE_DT + SSD_HEADS
SIDE_W = 3 * LANE


def _in_proj_columns():
    o_ssd, o_mla, o_ret, o_gla = 0, SSD_IN, SSD_IN + MLA_IN, SSD_IN + MLA_IN + RET_IN
    main = np.full((MAIN_W,), -1, np.int64)

    def put(dst, src, n):
        main[dst:dst + n] = src + np.arange(n)
    qk = RET_HEADS * RET_QK
    put(MAIN_SSD_Z, o_ssd, GROUP_W)
    put(MAIN_SSD_XS, o_ssd + GROUP_W, GROUP_W)
    put(MAIN_SSD_BC, o_ssd + 2 * GROUP_W, 2 * SSD_GROUPS * SSD_STATE)
    put(MAIN_MLA_CQ, o_mla, MLA_Q_RANK)
    put(MAIN_RET_Q, o_ret, qk)
    put(MAIN_RET_K, o_ret + qk, qk)
    put(MAIN_RET_V, o_ret + 2 * qk, GROUP_W)
    put(MAIN_RET_G, o_ret + 2 * qk + GROUP_W, GROUP_W)
    put(MAIN_GLA_Q, o_gla, qk)
    put(MAIN_GLA_K, o_gla + qk, qk)
    put(MAIN_GLA_V, o_gla + 2 * qk, GROUP_W)
    put(MAIN_GLA_R, o_gla + 2 * qk + GROUP_W + GLA_GATE_RANK, GROUP_W)
    side = np.full((SIDE_W,), -1, np.int64)
    side[SIDE_CKV:SIDE_CKV + MLA_KV_RANK + MLA_ROPE] = o_mla + MLA_Q_RANK + np.arange(MLA_KV_RANK + MLA_ROPE)
    side[SIDE_DT:SIDE_DT + SSD_HEADS] = o_ssd + GROUP_W + SSD_CONV_CH + np.arange(SSD_HEADS)
    side[SIDE_G1:SIDE_G1 + GLA_GATE_RANK] = o_gla + 2 * qk + GROUP_W + np.arange(GLA_GATE_RANK)
    return main, side


def _gather_columns(w, idx):
    pieces, i, n = [], 0, len(idx)
    while i < n:
        j = i + 1
        if idx[i] < 0:
            while j < n and idx[j] < 0:
                j += 1
            pieces.append(jnp.zeros(w.shape[:-1] + (j - i,), w.dtype))
        else:
            while j < n and idx[j] == idx[j - 1] + 1:
                j += 1
            pieces.append(w[..., int(idx[i]):int(idx[i]) + (j - i)])
        i = j
    return jnp.concatenate(pieces, axis=-1)


def _mod_row(i, tm):
    ctx_tiles = CTX_ROWS // tm
    per_seq = DEC_SEQ // tm
    return jnp.where(i < ctx_tiles, 0, 1 + (i - ctx_tiles) // per_seq)


def _mod_spec(layer, which, tm, n_grid):
    if n_grid == 1:
        return pl.BlockSpec((1, 1, D_MODEL), lambda i: (layer * MOD_ROWS + _mod_row(i, tm), 0, which))
    return pl.BlockSpec((1, 1, D_MODEL), lambda i, j: (layer * MOD_ROWS + _mod_row(i, tm), 0, which))


def _dot(a, b):
    return jnp.dot(a, b, preferred_element_type=F32)


def _dot_nt(a, b):
    return lax.dot_general(a, b, (((1,), (1,)), ((), ())), preferred_element_type=F32)


def _dot_tn(a, b):
    return lax.dot_general(a, b, (((0,), (0,)), ((), ())), preferred_element_type=F32)


def _silu(x):
    return x * jax.nn.sigmoid(x)


def _mod_kernel(c_ref, w_ref, b_ref, o_ref):
    s = _silu(c_ref[...])
    hi = s.astype(BF16)
    lo = (s - hi.astype(F32)).astype(BF16)
    w = w_ref[...].astype(BF16)
    o_ref[...] = _dot(hi, w) + _dot(lo, w) + b_ref[...]


def _modulation(c_all, w_mod, b_mod):
    tn = 1024
    n = N_MOD * D_MODEL
    return pl.pallas_call(
        _mod_kernel,
        grid=(DEPTH, n // tn),
        in_specs=[pl.BlockSpec((MOD_ROWS, D_MODEL), lambda l, j: (0, 0)),
                  pl.BlockSpec((None, D_MODEL, tn), lambda l, j: (l, 0, j)),
                  pl.BlockSpec((None, 1, tn), lambda l, j: (l, 0, j))],
        out_specs=pl.BlockSpec((None, MOD_ROWS, tn), lambda l, j: (l, 0, j)),
        out_shape=jax.ShapeDtypeStruct((DEPTH, MOD_ROWS, n), F32),
        compiler_params=pltpu.CompilerParams(dimension_semantics=("arbitrary", "arbitrary"),
                                             vmem_limit_bytes=VMEM_LIMIT),
        name="modulation",
    )(c_all, w_mod, b_mod.reshape(DEPTH, 1, n))


NORM_ROWS = 16


def _norm_modulate(x_ref, g_ref, shift_ref, scale_ref, h_ref):
    gs = g_ref[...] * (1.0 + scale_ref[0])
    shift = shift_ref[0]

    def body(r, carry):
        rows = pl.ds(pl.multiple_of(r * NORM_ROWS, NORM_ROWS), NORM_ROWS)
        x = x_ref[rows, :]
        inv = lax.rsqrt(jnp.mean(x * x, axis=-1, keepdims=True) + EPS)
        h_ref[rows, :] = ((x * inv) * gs + shift).astype(BF16)
        return carry

    lax.fori_loop(0, x_ref.shape[0] // NORM_ROWS, body, 0, unroll=8)


FFN_TM = 1024
FFN_VMEM_LIMIT = 60 * 1024 * 1024
FFN_TF = 512
FFN_TN = 256
FFN_NF = D_FF // FFN_TF
FFN_NN = D_MODEL // FFN_TN
MXU_W = 256


def _ffn_kernel(*refs, n_src, n_out, ctx_tiles):
    it = iter(refs)
    srcs = [(next(it), next(it)) for _ in range(n_src)]
    g_ref, shift_ref, scale_ref, gate_ref, wg_ref, wu_ref, wd_ref = (next(it) for _ in range(7))
    outs = [next(it) for _ in range(n_out)]
    h_ref, a_ref = next(it), next(it)
    j = pl.program_id(1)
    is_ctx = pl.program_id(0) < ctx_tiles

    def for_group(k, n_groups, fn):
        if n_groups == 1:
            fn()
        else:
            pl.when(is_ctx if k == 0 else jnp.logical_not(is_ctx))(fn)

    @pl.when(j == 0)
    def _():
        for k, (x_ref, _) in enumerate(srcs):
            for_group(k, n_src, functools.partial(_norm_modulate, x_ref, g_ref, shift_ref, scale_ref, h_ref))

    @pl.when(j < FFN_NF)
    def _():
        h = h_ref[...]
        base = pl.multiple_of(j * FFN_TF, FFN_TF)
        for s in range(FFN_TF // MXU_W):
            cols = slice(s * MXU_W, (s + 1) * MXU_W)
            g = _dot(h, wg_ref[:, cols])
            u = _dot(h, wu_ref[:, cols])
            a_ref[:, pl.ds(base + s * MXU_W, MXU_W)] = (_silu(g) * u).astype(BF16)

    @pl.when(j >= FFN_NF)
    def _():
        y = 0.5 * gate_ref[0] * _dot(a_ref[...], wd_ref[...])
        n_groups = max(n_src, n_out)
        for k in range(n_groups):
            xt_ref, o_ref = srcs[min(k, n_src - 1)][1], outs[min(k, n_out - 1)]

            def store(xt_ref=xt_ref, o_ref=o_ref):
                o_ref[...] = xt_ref[...] + y
            for_group(k, n_groups, store)


def _ffn(xs, mod, norm_g, wg, wu, wd, layer, mod_base, tm=FFN_TM, split_out=False):
    n_src, n_out = len(xs), 2 if split_out else 1
    ctx_tiles = CTX_ROWS // tm

    def up_tile(i, j):
        return (layer, 0, jnp.minimum(j, FFN_NF - 1))

    def down_tile(j):
        return jnp.maximum(j - FFN_NF, 0)

    def row_of(k, n_groups, i):
        if n_groups == 1:
            return i
        return jnp.minimum(i, ctx_tiles - 1) if k == 0 else jnp.maximum(i - ctx_tiles, 0)

    def col_of(k, n_groups, i, j):
        if n_groups == 1:
            return down_tile(j)
        if k == 0:
            return jnp.where(i < ctx_tiles, down_tile(j), FFN_NN - 1)
        return jnp.where(i >= ctx_tiles, down_tile(j), 0)
    gate_blocks = D_MODEL // FFN_TN
    in_specs, args = [], []
    for k, x in enumerate(xs):
        in_specs += [pl.BlockSpec((tm, D_MODEL), lambda i, j, k=k: (row_of(k, n_src, i), 0)),
                     pl.BlockSpec((tm, FFN_TN), lambda i, j, k=k: (row_of(k, n_src, i), col_of(k, n_src, i, j)))]
        args += [x, x]
    in_specs += [pl.BlockSpec((None, 1, D_MODEL), lambda i, j: (layer, 0, 0)),
                 _mod_spec(layer, mod_base + 0, tm, 2),
                 _mod_spec(layer, mod_base + 1, tm, 2),
                 pl.BlockSpec((1, 1, FFN_TN), lambda i, j: (layer * MOD_ROWS + _mod_row(i, tm), 0,
                                                            (mod_base + 2) * gate_blocks + down_tile(j))),
                 pl.BlockSpec((None, D_MODEL, FFN_TF), up_tile),
                 pl.BlockSpec((None, D_MODEL, FFN_TF), up_tile),
                 pl.BlockSpec((None, D_FF, FFN_TN), lambda i, j: (layer, 0, down_tile(j)))]
    args += [norm_g, mod, mod, mod, wg, wu, wd]
    out_rows = (CTX_ROWS, LAT_ROWS) if split_out else (ROWS,)
    res = pl.pallas_call(
        functools.partial(_ffn_kernel, n_src=n_src, n_out=n_out, ctx_tiles=ctx_tiles),
        grid=(ROWS // tm, FFN_NF + FFN_NN),
        in_specs=in_specs,
        out_specs=[pl.BlockSpec((tm, FFN_TN), lambda i, j, k=k: (row_of(k, n_out, i), col_of(k, n_out, i, j)))
                   for k in range(n_out)],
        out_shape=[jax.ShapeDtypeStruct((r, D_MODEL), F32) for r in out_rows],
        scratch_shapes=[pltpu.VMEM((tm, D_MODEL), BF16), pltpu.VMEM((tm, D_FF), BF16)],
        compiler_params=pltpu.CompilerParams(dimension_semantics=("parallel", "arbitrary"),
                                             vmem_limit_bytes=FFN_VMEM_LIMIT),
        name="ffn",
    )(*args)
    return res if split_out else res[0]


def _in_proj_kernel(x_ref, g_ref, shift_ref, scale_ref, w_ref, ws_ref, o_ref, os_ref, h_ref):
    @pl.when(pl.program_id(1) == 0)
    def _():
        x = x_ref[...]
        inv = lax.rsqrt(jnp.mean(x * x, axis=-1, keepdims=True) + EPS)
        h_ref[...] = ((x * inv) * (g_ref[...] * (1.0 + scale_ref[0])) + shift_ref[0]).astype(BF16)
        os_ref[...] = _dot(h_ref[...], ws_ref[...])

    o_ref[...] = _dot(h_ref[...], w_ref[...]).astype(o_ref.dtype)


def _in_proj(x, mod, norm_g, w_main, w_side, layer, tm=TOK_TM):
    return pl.pallas_call(
        _in_proj_kernel,
        grid=(ROWS // tm, MAIN_W // MAIN_TN),
        in_specs=[pl.BlockSpec((tm, D_MODEL), lambda i, j: (i, 0)),
                  pl.BlockSpec((None, 1, D_MODEL), lambda i, j: (layer, 0, 0)),
                  _mod_spec(layer, 3, tm, 2),
                  _mod_spec(layer, 4, tm, 2),
                  pl.BlockSpec((None, D_MODEL, MAIN_TN), lambda i, j: (layer, 0, j)),
                  pl.BlockSpec((None, D_MODEL, SIDE_W), lambda i, j: (layer, 0, 0))],
        out_specs=[pl.BlockSpec((tm, MAIN_TN), lambda i, j: (i, j)),
                   pl.BlockSpec((tm, SIDE_W), lambda i, j: (i, 0))],
        out_shape=[jax.ShapeDtypeStruct((ROWS, MAIN_W), BF16),
                   jax.ShapeDtypeStruct((ROWS, SIDE_W), F32)],
        scratch_shapes=[pltpu.VMEM((tm, D_MODEL), BF16)],
        compiler_params=pltpu.CompilerParams(dimension_semantics=("parallel", "arbitrary"),
                                             vmem_limit_bytes=VMEM_LIMIT),
        name="in_proj",
    )(x, norm_g, mod, mod, w_main, w_side)


def _out_proj_kernel(x_ref, *refs, ctx_tiles):
    ctx_refs, lat_refs, (gate_ref, w_ref, o_ref) = refs[0:4], refs[4:8], refs[8:]

    def run(y_refs):
        acc = _dot(y_refs[0][...], w_ref[0:GROUP_W, :])
        for g in range(1, 4):
            acc += _dot(y_refs[g][...], w_ref[g * GROUP_W:(g + 1) * GROUP_W, :])
        o_ref[...] = x_ref[...] + gate_ref[0] * acc

    @pl.when(pl.program_id(0) < ctx_tiles)
    def _():
        run(ctx_refs)

    @pl.when(pl.program_id(0) >= ctx_tiles)
    def _():
        run(lat_refs)


def _out_proj(x, ys_ctx, ys_lat, mod, w, layer, tm=TOK_TM):
    ctx_tiles = CTX_ROWS // tm
    row = pl.BlockSpec((tm, D_MODEL), lambda i: (i, 0))
    cspec = pl.BlockSpec((tm, GROUP_W), lambda i: (jnp.minimum(i, ctx_tiles - 1), 0))
    lspec = pl.BlockSpec((tm, GROUP_W), lambda i: (jnp.maximum(i - ctx_tiles, 0), 0))
    return pl.pallas_call(
        functools.partial(_out_proj_kernel, ctx_tiles=ctx_tiles),
        grid=(ROWS // tm,),
        in_specs=[row] + [cspec] * 4 + [lspec] * 4 + [
            _mod_spec(layer, 5, tm, 1),
            pl.BlockSpec((None, D_MODEL, D_MODEL), lambda i: (layer, 0, 0))],
        out_specs=row,
        out_shape=jax.ShapeDtypeStruct((ROWS, D_MODEL), F32),
        compiler_params=pltpu.CompilerParams(dimension_semantics=("parallel",),
                                             vmem_limit_bytes=VMEM_LIMIT),
        name="out_proj",
    )(x, *ys_ctx, *ys_lat, mod, w)


def _rope_tables(T):
    n_rows = T // GRID_W
    row = jnp.repeat(jnp.arange(n_rows, dtype=F32), GRID_W)
    col = jnp.tile(jnp.arange(GRID_W, dtype=F32), n_rows)
    d_axis = MLA_ROPE // 2
    inv = ROPE_BASE ** (-jnp.arange(0, d_axis, 2, dtype=F32) / d_axis)
    ar, ac = row[:, None] * inv, col[:, None] * inv
    cos = jnp.concatenate([jnp.cos(ar), jnp.cos(ar), jnp.cos(ac), jnp.cos(ac)], axis=-1)
    sin = jnp.concatenate([-jnp.sin(ar), jnp.sin(ar), -jnp.sin(ac), jnp.sin(ac)], axis=-1)
    return jnp.tile(cos, (1, 2)), jnp.tile(sin, (1, 2))


def _swap16(x):
    lane = lax.broadcasted_iota(jnp.int32, x.shape, 1)
    up = pltpu.roll(x, LANE - 16, 1)
    down = pltpu.roll(x, 16, 1)
    return jnp.where((lane % 32) < 16, up, down)


def _rope(x, cos, sin):
    return x * cos + _swap16(x) * sin


def _mla_q_kernel(c0_ref, c1_ref, c2_ref, gl_ref, w_ref, gq_ref, cos_ref, sin_ref, o_ref):
    cs = [r[...].astype(F32) for r in (c0_ref, c1_ref, c2_ref)]
    ss = sum(jnp.sum(c * c, axis=-1, keepdims=True) for c in cs)
    r = lax.rsqrt(ss / MLA_Q_RANK + EPS)
    q = sum(_dot((cs[i] * r * gl_ref[:, i * LANE:(i + 1) * LANE]).astype(BF16),
                 w_ref[i * LANE:(i + 1) * LANE, :]) for i in range(3))
    cos, sin = cos_ref[...], sin_ref[...]
    scale = MLA_QK ** -0.5
    for h in range(MLA_HEADS):
        a = q[:, h * MLA_HEAD_W:h * MLA_HEAD_W + LANE]
        b = q[:, h * MLA_HEAD_W + LANE:(h + 1) * MLA_HEAD_W]
        ssq = jnp.sum(a * a, axis=-1, keepdims=True) + jnp.sum(b * b, axis=-1, keepdims=True)
        rh = lax.rsqrt(ssq / MLA_QK + EPS) * scale
        o_ref[:, h * MLA_HEAD_W:h * MLA_HEAD_W + LANE] = (a * rh * gq_ref[:, :LANE]).astype(BF16)
        o_ref[:, h * MLA_HEAD_W + LANE:(h + 1) * MLA_HEAD_W] = _rope(b * rh * gq_ref[:, LANE:], cos, sin).astype(BF16)


def _rope_tile_index(i, tm):
    ctx_tiles = CTX_ROWS // tm
    return jnp.where(i < ctx_tiles, 0, 1 + (i - ctx_tiles) % (DEC_SEQ // tm))


def _mla_q(um, gl, wq, gq, cos_all, sin_all, layer, tm=TOK_TM):
    cq = MAIN_MLA_CQ // LANE
    tab = pl.BlockSpec((tm, LANE), lambda i: (_rope_tile_index(i, tm), 0))
    return pl.pallas_call(
        _mla_q_kernel,
        grid=(ROWS // tm,),
        in_specs=[pl.BlockSpec((tm, LANE), lambda i: (i, cq)),
                  pl.BlockSpec((tm, LANE), lambda i: (i, cq + 1)),
                  pl.BlockSpec((tm, LANE), lambda i: (i, cq + 2)),
                  pl.BlockSpec((None, 1, MLA_Q_RANK), lambda i: (layer, 0, 0)),
                  pl.BlockSpec((None, MLA_Q_RANK, MLA_HEADS * MLA_HEAD_W), lambda i: (layer, 0, 0)),
                  pl.BlockSpec((None, 1, MLA_HEAD_W), lambda i: (layer, 0, 0)),
                  tab, tab],
        out_specs=pl.BlockSpec((tm, MLA_HEADS * MLA_HEAD_W), lambda i: (i, 0)),
        out_shape=jax.ShapeDtypeStruct((ROWS, MLA_HEADS * MLA_HEAD_W), BF16),
        compiler_params=pltpu.CompilerParams(dimension_semantics=("parallel",), vmem_limit_bytes=VMEM_LIMIT),
        name="mla_q",
    )(um, um, um, gl, wq, gq, cos_all, sin_all)


def _mla_kv_kernel(s_ref, gl_ref, w_ref, gk_ref, cos_ref, sin_ref, lat_ref, k_ref, v_ref, *, normalize):
    ckv = s_ref[:, :LANE]
    kpe = s_ref[:, LANE:]
    if normalize:
        ckv = ckv * lax.rsqrt(jnp.mean(ckv * ckv, axis=-1, keepdims=True) + EPS) * gl_ref[...]
    lat_ref[:, :LANE] = ckv
    lat_ref[:, LANE:] = kpe
    kv = _dot(ckv.astype(BF16), w_ref[...])
    ss_pe = jnp.sum(kpe * kpe, axis=-1, keepdims=True)
    cos, sin = cos_ref[...], sin_ref[...]
    for h in range(MLA_HEADS):
        a = kv[:, h * LANE:(h + 1) * LANE]
        rh = lax.rsqrt((jnp.sum(a * a, axis=-1, keepdims=True) + ss_pe) / MLA_QK + EPS)
        k_ref[:, h * MLA_HEAD_W:h * MLA_HEAD_W + LANE] = (a * rh * gk_ref[:, :LANE]).astype(BF16)
        k_ref[:, h * MLA_HEAD_W + LANE:(h + 1) * MLA_HEAD_W] = _rope(kpe * rh * gk_ref[:, LANE:], cos, sin).astype(BF16)
    v_ref[...] = kv[:, MLA_HEADS * LANE:].astype(BF16)


def _mla_kv(src, gl, wkv, gk, cos_all, sin_all, layer, *, normalize, tab_index, tm=TOK_TM):
    rows = src.shape[0]
    tab = pl.BlockSpec((tm, LANE), lambda i: (tab_index(i, tm), 0))
    return pl.pallas_call(
        functools.partial(_mla_kv_kernel, normalize=normalize),
        grid=(rows // tm,),
        in_specs=[pl.BlockSpec((tm, 2 * LANE), lambda i: (i, 0)),
                  pl.BlockSpec((None, 1, MLA_KV_RANK), lambda i: (layer, 0, 0)),
                  pl.BlockSpec((None, MLA_KV_RANK, 2 * MLA_HEADS * LANE), lambda i: (layer, 0, 0)),
                  pl.BlockSpec((None, 1, MLA_HEAD_W), lambda i: (layer, 0, 0)),
                  tab, tab],
        out_specs=[pl.BlockSpec((tm, 2 * LANE), lambda i: (i, 0)),
                   pl.BlockSpec((tm, MLA_HEADS * MLA_HEAD_W), lambda i: (i, 0)),
                   pl.BlockSpec((tm, GROUP_W), lambda i: (i, 0))],
        out_shape=[jax.ShapeDtypeStruct((rows, 2 * LANE), F32),
                   jax.ShapeDtypeStruct((rows, MLA_HEADS * MLA_HEAD_W), BF16),
                   jax.ShapeDtypeStruct((rows, GROUP_W), BF16)],
        compiler_params=pltpu.CompilerParams(dimension_semantics=("parallel",), vmem_limit_bytes=VMEM_LIMIT),
        name="mla_kv",
    )(src, gl, wkv, gk, cos_all, sin_all)


ATTN_TQ_LAT = 256
ATTN_HEADS_PER_STEP = 4


def _attn_kernel(q_ref, *refs):
    o_ref = refs[-1]
    kv = [(refs[i], refs[i + 1]) for i in range(0, len(refs) - 1, 2)]
    heads = range(ATTN_HEADS_PER_STEP)
    scores = [[_dot_nt(q_ref[:, h * MLA_HEAD_W:(h + 1) * MLA_HEAD_W], k_ref[:, h * MLA_HEAD_W:(h + 1) * MLA_HEAD_W])
               for k_ref, _ in kv] for h in heads]
    for h in heads:
        m = functools.reduce(jnp.maximum, [jnp.max(s, axis=-1, keepdims=True) for s in scores[h]])
        ps = [jnp.exp(s - m) for s in scores[h]]
        l = sum(jnp.sum(p, axis=-1, keepdims=True) for p in ps)
        o = sum(_dot(p.astype(BF16), v_ref[:, h * MLA_V:(h + 1) * MLA_V]) for p, (_, v_ref) in zip(ps, kv))
        o_ref[:, h * MLA_V:(h + 1) * MLA_V] = (o / l).astype(o_ref.dtype)


def _attention(q, sources, n_seq, t, tq, q_row0):
    nq = t // tq
    hs = ATTN_HEADS_PER_STEP
    in_specs = [pl.BlockSpec((tq, hs * MLA_HEAD_W), lambda b, h, i: (q_row0 // tq + b * nq + i, h))]
    args = [q]
    for k, v, s_len, row0 in sources:
        in_specs += [pl.BlockSpec((s_len, hs * MLA_HEAD_W), lambda b, h, i, o=row0 // s_len: (o + b, h)),
                     pl.BlockSpec((s_len, hs * MLA_V), lambda b, h, i, o=row0 // s_len: (o + b, h))]
        args += [k, v]
    return pl.pallas_call(
        _attn_kernel,
        grid=(n_seq, MLA_HEADS // hs, nq),
        in_specs=in_specs,
        out_specs=pl.BlockSpec((tq, hs * MLA_V), lambda b, h, i: (b * nq + i, h)),
        out_shape=jax.ShapeDtypeStruct((n_seq * t, GROUP_W), BF16),
        compiler_params=pltpu.CompilerParams(dimension_semantics=("parallel", "parallel", "arbitrary"),
                                             vmem_limit_bytes=VMEM_LIMIT),
        name="mla_attention",
    )(*args)


def _mla_weights(mla_q_lat_gain, mla_w_q_up, mla_q_gain, mla_kv_lat_gain, mla_w_kv_up, mla_k_gain):
    qcol = np.full((MLA_HEADS * MLA_HEAD_W,), -1, np.int64)
    kvcol = np.zeros((2 * MLA_HEADS * LANE,), np.int64)
    for h in range(MLA_HEADS):
        qcol[h * MLA_HEAD_W:h * MLA_HEAD_W + MLA_QK] = h * MLA_QK + np.arange(MLA_QK)
        kvcol[h * LANE:(h + 1) * LANE] = h * (MLA_NOPE + MLA_V) + np.arange(MLA_NOPE)
        kvcol[(MLA_HEADS + h) * LANE:(MLA_HEADS + h + 1) * LANE] = h * (MLA_NOPE + MLA_V) + MLA_NOPE + np.arange(MLA_V)
    pad = jnp.zeros((DEPTH, MLA_HEAD_W - MLA_QK), F32)
    return dict(
        gl=mla_q_lat_gain.reshape(DEPTH, 1, MLA_Q_RANK),
        wq=_gather_columns(mla_w_q_up, qcol).astype(BF16),
        gq=jnp.concatenate([mla_q_gain, pad], axis=-1).reshape(DEPTH, 1, MLA_HEAD_W),
        gkv=mla_kv_lat_gain.reshape(DEPTH, 1, MLA_KV_RANK),
        wkv=_gather_columns(mla_w_kv_up, kvcol).astype(BF16),
        gk=jnp.concatenate([mla_k_gain, pad], axis=-1).reshape(DEPTH, 1, MLA_HEAD_W))


def _mla_layer(um, us, cache_l, mw, cos_all, sin_all, layer):
    q = _mla_q(um, mw['gl'], mw['wq'], mw['gq'], cos_all, sin_all, layer)
    kv_lat, k, v = _mla_kv(us, mw['gkv'], mw['wkv'], mw['gk'], cos_all, sin_all, layer,
                           normalize=True, tab_index=_rope_tile_index)
    cache2 = jnp.pad(cache_l.reshape(DEC_BATCH * PAST_LEN, MLA_CACHE_W), ((0, 0), (0, 2 * LANE - MLA_CACHE_W)))
    _, k_c, v_c = _mla_kv(cache2, mw['gkv'], mw['wkv'], mw['gk'], cos_all, sin_all, layer,
                          normalize=False, tab_index=lambda i, tm: 0)
    y_ctx = _attention(q, [(k, v, SEQ, 0)], BATCH, SEQ, SEQ, 0)
    y_lat = _attention(q, [(k, v, DEC_SEQ, CTX_ROWS), (k_c, v_c, PAST_LEN, 0)], DEC_BATCH, DEC_SEQ, ATTN_TQ_LAT,
                       CTX_ROWS)
    return y_ctx, y_lat, kv_lat


def _ret_kernel(*refs, T, C, rope, has_h0, has_state):
    it = iter(refs)
    q_ref, k_ref, v_ref, g_ref = next(it), next(it), next(it), next(it)
    cos_ref, sin_ref = (next(it), next(it)) if rope else (None, None)
    lg_ref, gain_ref = next(it), next(it)
    h0_ref = next(it) if has_h0 else None
    y_ref = next(it)
    st_ref = next(it) if has_state else None
    qs_ref, ks_ref, oacc_ref, dm_ref, tab_ref, hst_ref = (next(it) for _ in range(6))

    nc = T // C
    ii = lax.broadcasted_iota(jnp.int32, (C, C), 0)
    jj = lax.broadcasted_iota(jnp.int32, (C, C), 1)
    dif = (ii - jj).astype(F32)
    lane = lax.broadcasted_iota(jnp.int32, (C, LANE), 1)
    rowi = lax.broadcasted_iota(jnp.int32, (C, LANE), 0).astype(F32)
    m_lo = lane < RET_QK
    hrow = lax.broadcasted_iota(jnp.int32, (LANE, LANE), 0) < RET_QK

    for h in range(RET_HEADS):
        lf, lb = lg_ref[0, h], lg_ref[1, h]
        dm_ref[h] = (jnp.where(dif >= 0, jnp.exp(lf * jnp.maximum(dif, 0.0)), 0.0)
                     + jnp.where(dif <= 0, jnp.exp(lb * jnp.maximum(-dif, 0.0)), 0.0))
    for p in range(2):
        lf = jnp.where(m_lo, lg_ref[0, 2 * p], lg_ref[0, 2 * p + 1])
        lb = jnp.where(m_lo, lg_ref[1, 2 * p], lg_ref[1, 2 * p + 1])
        tab_ref[p, 0] = jnp.exp(lf * (rowi + 1.0))
        tab_ref[p, 1] = jnp.exp(lf * (C - 1.0 - rowi))
        tab_ref[p, 2] = jnp.exp(lb * (C - rowi))
        tab_ref[p, 3] = jnp.exp(lb * rowi)
    if has_h0:
        hst_ref[...] = h0_ref[...]
    else:
        hst_ref[...] = jnp.zeros_like(hst_ref)

    def chunk_decay(d, p):
        return jnp.exp(jnp.where(hrow, lg_ref[d, 2 * p], lg_ref[d, 2 * p + 1]) * float(C))

    def forward(c, carry):
        r0 = pl.multiple_of(c * C, C)
        rows = pl.ds(r0, C)
        for p in range(2):
            cols = slice(p * LANE, (p + 1) * LANE)
            qp = q_ref[rows, cols].astype(F32)
            kp = k_ref[rows, cols].astype(F32) * (RET_QK ** -0.5)
            if rope:
                qp = _rope(qp, cos_ref[rows, :], sin_ref[rows, :])
                kp = _rope(kp, cos_ref[rows, :], sin_ref[rows, :])
            qs_ref[rows, cols] = qp.astype(BF16)
            ks_ref[rows, cols] = kp.astype(BF16)
            kpb = kp.astype(BF16)
            hp = hst_ref[0, p]
            hpb = hp.astype(BF16)
            upd = jnp.zeros((LANE, LANE), F32)
            for e in range(2):
                h = 2 * p + e
                mh = m_lo if e == 0 else jnp.logical_not(m_lo)
                hcols = slice(h * LANE, (h + 1) * LANE)
                qm = jnp.where(mh, qp, 0.0)
                s = _dot_nt(qm.astype(BF16), kpb)
                vh = v_ref[rows, hcols]
                o = _dot((s * dm_ref[h]).astype(BF16), vh)
                o += _dot((qm * tab_ref[p, 0]).astype(BF16), hpb)
                oacc_ref[rows, hcols] = o
                upd += _dot_tn(jnp.where(mh, kp * tab_ref[p, 1], 0.0).astype(BF16), vh)
            hst_ref[0, p] = hp * chunk_decay(0, p) + upd
        return carry

    lax.fori_loop(0, nc, forward, 0)

    def backward(t, carry):
        c = nc - 1 - t
        r0 = pl.multiple_of(c * C, C)
        rows = pl.ds(r0, C)
        for p in range(2):
            cols = slice(p * LANE, (p + 1) * LANE)
            qp = qs_ref[rows, cols].astype(F32)
            kp = ks_ref[rows, cols].astype(F32)
            hp = hst_ref[1, p]
            hpb = hp.astype(BF16)
            upd = jnp.zeros((LANE, LANE), F32)
            for e in range(2):
                h = 2 * p + e
                mh = m_lo if e == 0 else jnp.logical_not(m_lo)
                hcols = slice(h * LANE, (h + 1) * LANE)
                vh = v_ref[rows, hcols]
                o = oacc_ref[rows, hcols] + _dot(jnp.where(mh, qp * tab_ref[p, 2], 0.0).astype(BF16), hpb)
                upd += _dot_tn(jnp.where(mh, kp * tab_ref[p, 3], 0.0).astype(BF16), vh)
                oc = o - jnp.mean(o, axis=-1, keepdims=True)
                yn = oc * lax.rsqrt(jnp.mean(oc * oc, axis=-1, keepdims=True) + EPS) * gain_ref[:, hcols]
                y_ref[rows, hcols] = (_silu(g_ref[rows, hcols].astype(F32)) * yn).astype(y_ref.dtype)
            hst_ref[1, p] = hp * chunk_decay(1, p) + upd
        return carry

    lax.fori_loop(0, nc, backward, 0)
    if has_state:
        st_ref[...] = hst_ref[...]


def _retention(um, lg, gain, h0, rope_tabs, n_seq, T, row_block0):
    C = RET_CHUNK
    rope = rope_tabs is not None
    has_h0 = h0 is not None
    has_state = not has_h0

    def col(off, w):
        return pl.BlockSpec((T, w), lambda b: (row_block0 + b, off // w))
    in_specs = [col(MAIN_RET_Q, 2 * LANE), col(MAIN_RET_K, 2 * LANE), col(MAIN_RET_V, GROUP_W), col(MAIN_RET_G, GROUP_W)]
    args = [um, um, um, um]
    if rope:
        in_specs += [pl.BlockSpec((T, LANE), lambda b: (0, 0))] * 2
        args += list(rope_tabs)
    in_specs += [pl.BlockSpec(memory_space=pltpu.SMEM), pl.BlockSpec((1, GROUP_W), lambda b: (0, 0))]
    args += [lg, gain]
    st_spec = pl.BlockSpec((None, 2, 2, LANE, LANE), lambda b: (b, 0, 0, 0, 0))
    if has_h0:
        in_specs.append(st_spec)
        args.append(h0)
    out_specs = [pl.BlockSpec((T, GROUP_W), lambda b: (b, 0))]
    out_shape = [jax.ShapeDtypeStruct((n_seq * T, GROUP_W), BF16)]
    if has_state:
        out_specs.append(st_spec)
        out_shape.append(jax.ShapeDtypeStruct((n_seq, 2, 2, LANE, LANE), F32))
    res = pl.pallas_call(
        functools.partial(_ret_kernel, T=T, C=C, rope=rope, has_h0=has_h0, has_state=has_state),
        grid=(n_seq,),
        in_specs=in_specs,
        out_specs=out_specs,
        out_shape=out_shape,
        scratch_shapes=[pltpu.VMEM((T, 2 * LANE), BF16), pltpu.VMEM((T, 2 * LANE), BF16),
                        pltpu.VMEM((T, GROUP_W), F32), pltpu.VMEM((RET_HEADS, C, C), F32),
                        pltpu.VMEM((2, 4, C, LANE), F32), pltpu.VMEM((2, 2, LANE, LANE), F32)],
        compiler_params=pltpu.CompilerParams(dimension_semantics=("parallel",), vmem_limit_bytes=VMEM_LIMIT),
        name="retention",
    )(*args)
    return (res[0], res[1]) if has_state else (res[0], None)


def _split3(x):
    x1 = x.astype(BF16)
    r1 = x - x1.astype(F32)
    x2 = r1.astype(BF16)
    x3 = (r1 - x2.astype(F32)).astype(BF16)
    return x1, x2, x3


def _dot01_l(m, x):
    x1, x2, x3 = _split3(x)
    return _dot(m, x1) + _dot(m, x2) + _dot(m, x3)


def _dot01_r(x, m):
    x1, x2, x3 = _split3(x)
    return _dot(x1, m) + _dot(x2, m) + _dot(x3, m)


def _softplus(x):
    return jnp.maximum(x, 0.0) + jnp.log(1.0 + jnp.exp(-jnp.abs(x)))


def _head_expander():
    e64 = np.zeros((LANE, GROUP_W), np.float32)
    for h in range(SSD_HEADS):
        e64[h, h * SSD_HEAD_DIM:(h + 1) * SSD_HEAD_DIM] = 1.0
    return jnp.asarray(e64, BF16)


def _ssd_kernel(*refs, T, has_h0, has_state):
    it = iter(refs)
    z_ref, xs_ref, bc_ref, dt_ref = next(it), next(it), next(it), next(it)
    wx_ref, wbc_ref, bx_ref, bbc_ref = next(it), next(it), next(it), next(it)
    a_ref, dtb_ref, d_ref, gain_ref, e64_ref = (next(it) for _ in range(5))
    h0_ref = next(it) if has_h0 else None
    y_ref = next(it)
    st_ref = next(it) if has_state else None
    xc_s, bcs_s, oacc_ref, hst_ref = next(it), next(it), next(it), next(it)

    C = SSD_CHUNK
    nc = T // C
    ii = lax.broadcasted_iota(jnp.int32, (C, C), 0)
    jj = lax.broadcasted_iota(jnp.int32, (C, C), 1)
    tril, triu = ii >= jj, ii <= jj
    tril_b, triu_b = tril.astype(BF16), triu.astype(BF16)
    lane = lax.broadcasted_iota(jnp.int32, (C, LANE), 1)
    m_lo = lane < SSD_STATE
    masks = (m_lo, jnp.logical_not(m_lo))
    hrow = lax.broadcasted_iota(jnp.int32, (LANE, LANE), 0) < SSD_STATE

    if has_h0:
        hst_ref[...] = h0_ref[...]
    else:
        hst_ref[...] = jnp.zeros_like(hst_ref)

    def decays(rows, d):
        dtv = _softplus(dt_ref[rows, :] + dtb_ref[d:d + 1, :])
        la = dtv * a_ref[d:d + 1, :]
        return dtv, _dot01_l(tril_b if d == 0 else triu_b, la)

    def conv(ref, w_ref, b_ref, c, r0):
        width = ref.shape[1]
        x = ref[pl.ds(r0, C), :].astype(F32)
        p0 = pl.multiple_of(jnp.maximum(r0 - 16, 0), 16)
        n0 = pl.multiple_of(jnp.minimum(r0 + C, T - 16), 16)
        prev_row = jnp.where(c > 0, ref[pl.ds(p0, 16), :].astype(F32)[15:16], 0.0)
        next_row = jnp.where(c < nc - 1, ref[pl.ds(n0, 16), :].astype(F32)[0:1], 0.0)
        rowi = lax.broadcasted_iota(jnp.int32, (C, width), 0)
        prev = jnp.where(rowi == 0, prev_row, pltpu.roll(x, 1, 0))
        nxt = jnp.where(rowi == C - 1, next_row, pltpu.roll(x, C - 1, 0))
        return _silu(prev * w_ref[0:1, :] + x * w_ref[1:2, :] + nxt * w_ref[2:3, :] + b_ref[...])

    def decay_matrix(qrow, pp):
        return jnp.where(hrow, qrow[:, pp * LANE:(pp + 1) * LANE],
                         qrow[:, (2 + pp) * LANE:(3 + pp) * LANE])

    def forward(c, carry):
        r0 = pl.multiple_of(c * C, C)
        rows = pl.ds(r0, C)
        xc = conv(xs_ref, wx_ref, bx_ref, c, r0)
        bcv = conv(bc_ref, wbc_ref, bbc_ref, c, r0)
        xc_s[rows, :] = xc.astype(BF16)
        bcs_s[rows, :] = bcv.astype(BF16)
        bmat, cmat = bcv[:, :LANE], bcv[:, LANE:]
        bmb = bmat.astype(BF16)
        dtf, bf = decays(rows, 0)
        dtb, bb = decays(rows, 1)
        bf_t, dtf_t, bb_t, dtb_t = bf.T, dtf.T, bb.T, dtb.T
        qdf = _dot01_r(jnp.exp(bf), e64_ref[...])
        kwf = _dot01_r(dtf * jnp.exp(bf[C - 1:C, :] - bf), e64_ref[...])
        cms = [jnp.where(masks[g], cmat, 0.0).astype(BF16) for g in range(SSD_GROUPS)]
        bms = [jnp.where(masks[g], bmat, 0.0).astype(BF16) for g in range(SSD_GROUPS)]
        scores = [_dot_nt(cms[g], bmb) for g in range(SSD_GROUPS)]
        for pp in range(2):
            hp = hst_ref[0, pp]
            hpb = hp.astype(BF16)
            upd = jnp.zeros((LANE, LANE), F32)
            for g in range(SSD_GROUPS):
                cols = slice((2 * g + pp) * LANE, (2 * g + pp + 1) * LANE)
                xs_pair = xc[:, cols]
                xsb = xs_pair.astype(BF16)
                outs = []
                for e in range(2):
                    h = 4 * g + 2 * pp + e
                    hc = slice(h * LANE, (h + 1) * LANE)
                    bf_col = jnp.broadcast_to(bf[:, h:h + 1], (C, LANE))
                    bb_col = jnp.broadcast_to(bb[:, h:h + 1], (C, LANE))
                    ef = jnp.where(tril, jnp.exp(jnp.minimum(bf_col - bf_t[h:h + 1, :], 0.0)), 0.0) * dtf_t[h:h + 1, :]
                    eb = jnp.where(triu, jnp.exp(jnp.minimum(bb_col - bb_t[h:h + 1, :], 0.0)), 0.0) * dtb_t[h:h + 1, :]
                    outs.append(_dot((scores[g] * (ef + eb)).astype(BF16), xsb))
                o = jnp.where(m_lo, outs[0], outs[1])
                o += _dot(cms[g], hpb) * qdf[:, cols]
                oacc_ref[rows, cols] = o
                upd += _dot_tn(bms[g], (xs_pair * kwf[:, cols]).astype(BF16))
            hst_ref[0, pp] = hp * decay_matrix(qdf[C - 1:C, :], pp) + upd
        return carry

    lax.fori_loop(0, nc, forward, 0)

    def backward(t, carry):
        c = nc - 1 - t
        r0 = pl.multiple_of(c * C, C)
        rows = pl.ds(r0, C)
        xc = xc_s[rows, :].astype(F32)
        bcv = bcs_s[rows, :]
        bmat, cmat = bcv[:, :LANE], bcv[:, LANE:]
        zero = jnp.zeros_like(bmat)
        dtb, bb = decays(rows, 1)
        qdb = _dot01_r(jnp.exp(bb), e64_ref[...])
        kwb = _dot01_r(dtb * jnp.exp(bb[0:1, :] - bb), e64_ref[...])
        blocks = {}
        for pp in range(2):
            hp = hst_ref[1, pp]
            hpb = hp.astype(BF16)
            upd = jnp.zeros((LANE, LANE), F32)
            for g in range(SSD_GROUPS):
                blk = 2 * g + pp
                cols = slice(blk * LANE, (blk + 1) * LANE)
                xs_pair = xc[:, cols]
                o = oacc_ref[rows, cols] + _dot(jnp.where(masks[g], cmat, zero), hpb) * qdb[:, cols]
                upd += _dot_tn(jnp.where(masks[g], bmat, zero), (xs_pair * kwb[:, cols]).astype(BF16))
                y = (o + d_ref[:, cols] * xs_pair) * _silu(z_ref[rows, cols].astype(F32))
                blocks[blk] = y
            hst_ref[1, pp] = hp * decay_matrix(qdb[0:1, :], pp) + upd
        for g in range(SSD_GROUPS):
            y0, y1 = blocks[2 * g], blocks[2 * g + 1]
            ss = jnp.sum(y0 * y0, axis=-1, keepdims=True) + jnp.sum(y1 * y1, axis=-1, keepdims=True)
            r = lax.rsqrt(ss / (2 * LANE) + EPS)
            for i, yb in enumerate((y0, y1)):
                cols = slice((2 * g + i) * LANE, (2 * g + i + 1) * LANE)
                y_ref[rows, cols] = (yb * r * gain_ref[:, cols]).astype(y_ref.dtype)
        return carry

    lax.fori_loop(0, nc, backward, 0)
    if has_state:
        st_ref[...] = hst_ref[...]


def _ssd_pack_state(st):
    n = st.shape[0]
    st = st.reshape(n, 2, SSD_GROUPS, 2, 2, SSD_STATE, SSD_HEAD_DIM)
    return st.transpose(0, 1, 3, 2, 5, 4, 6).reshape(n, 2, 2, LANE, LANE)


def _ssd_unpack_state(st):
    n = st.shape[0]
    st = st.reshape(n, 2, 2, SSD_GROUPS, SSD_STATE, 2, SSD_HEAD_DIM)
    return st.transpose(0, 1, 3, 2, 5, 4, 6).reshape(n, 2, SSD_HEADS, SSD_STATE, SSD_HEAD_DIM)


def _ssd_params(ssd_conv_w, ssd_conv_b, ssd_a_log, ssd_dt_bias, ssd_d, ssd_norm_g):
    def lanes8(v):
        return jnp.pad(v.astype(F32), ((0, 0), (0, 0), (0, LANE - SSD_HEADS)))
    e64 = _head_expander()
    return dict(wx=ssd_conv_w[:, :, :GROUP_W], wbc=ssd_conv_w[:, :, GROUP_W:],
                bx=ssd_conv_b[:, None, :GROUP_W], bbc=ssd_conv_b[:, None, GROUP_W:],
                a=lanes8(-jnp.exp(ssd_a_log.astype(F32))), dtb=lanes8(ssd_dt_bias),
                d=jnp.repeat(ssd_d, SSD_HEAD_DIM, axis=-1)[:, None, :], gain=ssd_norm_g[:, None, :],
                e64=e64)


def _ssd_scan(um, us, sp, layer, h0, n_seq, T, row_block0):
    has_h0 = h0 is not None
    has_state = not has_h0

    def col(off, w):
        return pl.BlockSpec((T, w), lambda b: (row_block0 + b, off // w))

    def per_layer(arr):
        return pl.BlockSpec((None,) + arr.shape[1:], lambda b: (layer,) + (0,) * (arr.ndim - 1))

    def const(arr):
        return pl.BlockSpec(arr.shape, lambda b: (0,) * arr.ndim)
    names = ('wx', 'wbc', 'bx', 'bbc', 'a', 'dtb', 'd', 'gain')
    in_specs = [col(MAIN_SSD_Z, GROUP_W), col(MAIN_SSD_XS, GROUP_W), col(MAIN_SSD_BC, 2 * LANE),
                pl.BlockSpec((T, LANE), lambda b: (row_block0 + b, SIDE_DT // LANE))]
    in_specs += [per_layer(sp[n]) for n in names] + [const(sp['e64'])]
    args = [um, um, um, us] + [sp[n] for n in names] + [sp['e64']]
    st_spec = pl.BlockSpec((None, 2, 2, LANE, LANE), lambda b: (b, 0, 0, 0, 0))
    if has_h0:
        in_specs.append(st_spec)
        args.append(h0)
    out_specs = [pl.BlockSpec((T, GROUP_W), lambda b: (b, 0))]
    out_shape = [jax.ShapeDtypeStruct((n_seq * T, GROUP_W), BF16)]
    if has_state:
        out_specs.append(st_spec)
        out_shape.append(jax.ShapeDtypeStruct((n_seq, 2, 2, LANE, LANE), F32))
    res = pl.pallas_call(
        functools.partial(_ssd_kernel, T=T, has_h0=has_h0, has_state=has_state),
        grid=(n_seq,),
        in_specs=in_specs,
        out_specs=out_specs,
        out_shape=out_shape,
        scratch_shapes=[pltpu.VMEM((T, GROUP_W), BF16), pltpu.VMEM((T, 2 * LANE), BF16),
                        pltpu.VMEM((T, GROUP_W), F32), pltpu.VMEM((2, 2, LANE, LANE), F32)],
        compiler_params=pltpu.CompilerParams(dimension_semantics=("parallel",), vmem_limit_bytes=VMEM_LIMIT),
        name="ssd",
    )(*args)
    return (res[0], res[1]) if has_state else (res[0], None)


GLA_MACRO = 128


def _log_sigmoid(x):
    return jnp.minimum(x, 0.0) - jnp.log(1.0 + jnp.exp(-jnp.abs(x)))


def _gla_kernel(*refs, T, has_h0, has_state):
    it = iter(refs)
    q_ref, k_ref, v_ref, r_ref, g1_ref = (next(it) for _ in range(5))
    wg_ref, bg_ref, gain_ref, ind_ref = (next(it) for _ in range(4))
    h0_ref = next(it) if has_h0 else None
    y_ref = next(it)
    st_ref = next(it) if has_state else None
    oacc_ref, hst_ref, qbuf, kbuf, bfbuf, bbbuf, vbuf, obuf = (next(it) for _ in range(8))

    C, L = GLA_MACRO, GLA_CHUNK
    nb = C // L
    nc = T // C
    ii = lax.broadcasted_iota(jnp.int32, (C, C), 0)
    jj = lax.broadcasted_iota(jnp.int32, (C, C), 1)
    same = (ii // L) == (jj // L)
    tri_l = jnp.logical_and(same, jj <= ii).astype(BF16)
    tri_u = jnp.logical_and(same, jj >= ii).astype(BF16)
    ones_b = same.astype(BF16)
    lane = lax.broadcasted_iota(jnp.int32, (C, LANE), 1)
    masks = (lane < GLA_QK, lane >= GLA_QK)
    HALF = 8
    ri8 = lax.broadcasted_iota(jnp.int32, (HALF, 2 * LANE), 0)
    bdmask = (lax.broadcasted_iota(jnp.int32, (C, nb * LANE), 0) // L
              == lax.broadcasted_iota(jnp.int32, (C, nb * LANE), 1) // LANE)

    if has_h0:
        hst_ref[...] = h0_ref[...]
    else:
        hst_ref[...] = jnp.zeros_like(hst_ref)

    def log_decay(rows, d):
        g1 = g1_ref[rows, :]
        hi = g1.astype(BF16)
        lo = (g1 - hi.astype(F32)).astype(BF16)
        logits = _dot(hi, wg_ref[d]) + _dot(lo, wg_ref[d]) + bg_ref[d]
        return _log_sigmoid(logits) / GLA_GATE_TEMP

    def recurrence(d, rows, qt, kt, dec, blocks):
        zero = jnp.zeros((), BF16)

        def block_diag(x):
            return jnp.where(bdmask, jnp.concatenate([x] * nb, axis=1), zero)
        upds = []
        for h in range(GLA_HEADS):
            p, e = divmod(h, 2)
            km = jnp.where(masks[e], kt[:, p * LANE:(p + 1) * LANE], 0.0).astype(BF16)
            upds.append(_dot_tn(v_ref[rows, h * LANE:(h + 1) * LANE], block_diag(km)))
        snaps = []
        for h in range(GLA_HEADS):
            lanes = slice((h // 2) * LANE, (h // 2 + 1) * LANE)
            ht = hst_ref[d, h]
            snap = [None] * nb
            for blk in blocks:
                snap[blk] = ht.astype(BF16)
                ht = ht * dec[blk * L:blk * L + 1, lanes] + upds[h][:, blk * LANE:(blk + 1) * LANE]
            hst_ref[d, h] = ht
            snaps.append(jnp.concatenate(snap, axis=1))
        outs = []
        for h in range(GLA_HEADS):
            p, e = divmod(h, 2)
            qm = jnp.where(masks[e], qt[:, p * LANE:(p + 1) * LANE], 0.0).astype(BF16)
            outs.append(_dot_nt(block_diag(qm), snaps[h]))
        return outs

    def intra_block(b0):
        pieces = []
        for j in range(L):
            kj, bfj, bbj = kbuf[b0 + j:b0 + j + 1, :], bfbuf[b0 + j:b0 + j + 1, :], bbbuf[b0 + j:b0 + j + 1, :]
            halves = []
            for s in range(L // HALF):
                rs = slice(b0 + s * HALF, b0 + (s + 1) * HALF)
                qk = qbuf[rs, :] * kj
                if j < s * HALF:
                    e = qk * jnp.exp(bfbuf[rs, :] - bfj)
                elif j >= (s + 1) * HALF:
                    e = qk * jnp.exp(bbbuf[rs, :] - bbj)
                else:
                    rel = ri8 + (s * HALF - j)
                    e = qk * jnp.exp(jnp.where(rel >= 0, bfbuf[rs, :] - bfj, bbbuf[rs, :] - bbj))
                    e = jnp.where(rel == 0, 2.0 * e, e)
                halves.append(e)
            pieces.append(jnp.concatenate(halves, axis=0).astype(BF16))
        spread = _dot(jnp.concatenate(pieces, axis=0), ind_ref[...])
        acc = spread[0:L, :] * vbuf[b0:b0 + 1, :]
        for j in range(1, L):
            acc += spread[j * L:(j + 1) * L, :] * vbuf[b0 + j:b0 + j + 1, :]
        return acc

    def forward_part(rows):
        q = q_ref[rows, :].astype(F32) * (GLA_QK ** -0.5)
        k = k_ref[rows, :].astype(F32)
        la_f, la_b = log_decay(rows, 0), log_decay(rows, 1)
        bf, tot_f = _dot01_l(tri_l, la_f), _dot01_l(ones_b, la_f)
        bb = _dot01_l(tri_u, la_b)
        qbuf[...] = q
        kbuf[...] = k
        bfbuf[...] = bf
        bbbuf[...] = bb
        vbuf[...] = v_ref[rows, :].astype(F32)
        for blk in range(nb):
            obuf[blk * L:(blk + 1) * L, :] = intra_block(blk * L)
        qt, kt, dec = q * jnp.exp(bf), k * jnp.exp(tot_f - bf), jnp.exp(tot_f)
        inter = recurrence(0, rows, qt, kt, dec, range(nb))
        return [obuf[:, h * LANE:(h + 1) * LANE] + inter[h] for h in range(GLA_HEADS)]

    def backward_part(rows):
        q = q_ref[rows, :].astype(F32) * (GLA_QK ** -0.5)
        k = k_ref[rows, :].astype(F32)
        la_b = log_decay(rows, 1)
        bb, tot_b = _dot01_l(tri_u, la_b), _dot01_l(ones_b, la_b)
        qt, kt, dec = q * jnp.exp(bb), k * jnp.exp(tot_b - bb), jnp.exp(tot_b)
        return recurrence(1, rows, qt, kt, dec, range(nb - 1, -1, -1))

    def emit(rows, parts, final):
        for h in range(GLA_HEADS):
            hcols = slice(h * LANE, (h + 1) * LANE)
            if not final:
                oacc_ref[rows, hcols] = parts[h]
            else:
                o = oacc_ref[rows, hcols] + parts[h]
                yn = o * lax.rsqrt(jnp.mean(o * o, axis=-1, keepdims=True) + EPS) * gain_ref[:, hcols]
                y_ref[rows, hcols] = (_silu(r_ref[rows, hcols].astype(F32)) * yn).astype(y_ref.dtype)

    def step(t, final):
        rows_f = pl.ds(pl.multiple_of(t * C, C), C)
        rows_b = pl.ds(pl.multiple_of((nc - 1 - t) * C, C), C)
        parts_b = backward_part(rows_b)
        parts_f = forward_part(rows_f)
        emit(rows_f, parts_f, final)
        emit(rows_b, parts_b, final)

    def first_half(t, carry):
        step(t, False)
        return carry

    def second_half(t, carry):
        step(t, True)
        return carry

    lax.fori_loop(0, nc // 2, first_half, 0)
    lax.fori_loop(nc // 2, nc, second_half, 0)
    if has_state:
        st_ref[...] = hst_ref[...]


def _gla_pack_state(st):
    n = st.shape[0]
    ht = jnp.swapaxes(st, -1, -2)
    z = jnp.zeros_like(ht)
    even = jnp.concatenate([ht, z], axis=-1)
    odd = jnp.concatenate([z, ht], axis=-1)
    sel = (jnp.arange(GLA_HEADS) % 2 == 0)[None, None, :, None, None]
    return jnp.where(sel, even, odd)


def _gla_unpack_state(st):
    even, odd = st[..., :GLA_QK], st[..., GLA_QK:]
    sel = (jnp.arange(GLA_HEADS) % 2 == 0)[None, None, :, None, None]
    return jnp.swapaxes(jnp.where(sel, even, odd), -1, -2)


def _gla_params(gla_w_g2, gla_b_g, gla_norm_g):
    wg = jnp.zeros((DEPTH, 2, LANE, 2 * LANE), F32)
    g1_lane = SIDE_G1 - SIDE_DT
    wg = wg.at[:, :, g1_lane:g1_lane + GLA_GATE_RANK, :].set(gla_w_g2)
    ind = np.zeros((2 * LANE, GROUP_W), np.float32)
    for h in range(GLA_HEADS):
        ind[h * GLA_QK:(h + 1) * GLA_QK, h * GLA_V:(h + 1) * GLA_V] = 1.0
    return dict(wg=wg.astype(BF16), bg=gla_b_g[:, :, None, :], gain=gla_norm_g[:, None, :],
                ind=jnp.asarray(ind, BF16))


def _gla_scan(um, us, gp, layer, h0, n_seq, T, row_block0):
    has_h0 = h0 is not None
    has_state = not has_h0
    C = GLA_MACRO
    assert T % (2 * C) == 0, "the merged forward/backward loop pairs chunk t with chunk nc-1-t"

    def col(off, w):
        return pl.BlockSpec((T, w), lambda b: (row_block0 + b, off // w))

    def per_layer(arr):
        return pl.BlockSpec((None,) + arr.shape[1:], lambda b: (layer,) + (0,) * (arr.ndim - 1))
    in_specs = [col(MAIN_GLA_Q, 2 * LANE), col(MAIN_GLA_K, 2 * LANE), col(MAIN_GLA_V, GROUP_W), col(MAIN_GLA_R, GROUP_W),
                pl.BlockSpec((T, LANE), lambda b: (row_block0 + b, SIDE_DT // LANE)),
                per_layer(gp['wg']), per_layer(gp['bg']), per_layer(gp['gain']),
                pl.BlockSpec(gp['ind'].shape, lambda b: (0, 0))]
    args = [um, um, um, um, us, gp['wg'], gp['bg'], gp['gain'], gp['ind']]
    st_spec = pl.BlockSpec((None, 2, GLA_HEADS, LANE, LANE), lambda b: (b, 0, 0, 0, 0))
    if has_h0:
        in_specs.append(st_spec)
        args.append(h0)
    out_specs = [pl.BlockSpec((T, GROUP_W), lambda b: (b, 0))]
    out_shape = [jax.ShapeDtypeStruct((n_seq * T, GROUP_W), BF16)]
    if has_state:
        out_specs.append(st_spec)
        out_shape.append(jax.ShapeDtypeStruct((n_seq, 2, GLA_HEADS, LANE, LANE), F32))
    res = pl.pallas_call(
        functools.partial(_gla_kernel, T=T, has_h0=has_h0, has_state=has_state),
        grid=(n_seq,),
        in_specs=in_specs,
        out_specs=out_specs,
        out_shape=out_shape,
        scratch_shapes=[pltpu.VMEM((T, GROUP_W), F32), pltpu.VMEM((2, GLA_HEADS, LANE, LANE), F32),
                        pltpu.VMEM((C, 2 * LANE), F32), pltpu.VMEM((C, 2 * LANE), F32),
                        pltpu.VMEM((C, 2 * LANE), F32), pltpu.VMEM((C, 2 * LANE), F32),
                        pltpu.VMEM((C, GROUP_W), F32), pltpu.VMEM((C, GROUP_W), F32)],
        compiler_params=pltpu.CompilerParams(dimension_semantics=("parallel",), vmem_limit_bytes=VMEM_LIMIT),
        name="gla",
    )(*args)
    return (res[0], res[1]) if has_state else (res[0], None)


def kernel(x_prompt, x_sample, cache_mla_kv, state_ssd, state_ret, state_gla, c, c_ctx, w_mod, b_mod, norm_ffn1, ffn1_wg, ffn1_wu, ffn1_wd, norm_mix, w_in, ssd_conv_w, ssd_conv_b, ssd_a_log, ssd_dt_bias, ssd_d, ssd_norm_g, mla_q_lat_gain, mla_w_q_up, mla_q_gain, mla_kv_lat_gain, mla_w_kv_up, mla_k_gain, ret_decay_logit, ret_norm_g, gla_w_g2, gla_b_g, gla_norm_g, w_out, norm_ffn2, ffn2_wg, ffn2_wu, ffn2_wd):
    sp = _ssd_params(ssd_conv_w, ssd_conv_b, ssd_a_log, ssd_dt_bias, ssd_d, ssd_norm_g)
    ssd_h0 = jnp.stack([_ssd_pack_state(state_ssd[:, l]) for l in range(DEPTH)], axis=1)
    gp = _gla_params(gla_w_g2, gla_b_g, gla_norm_g)
    gla_h0 = jnp.stack([_gla_pack_state(state_gla[:, l]) for l in range(DEPTH)], axis=1)

    main_idx, side_idx = _in_proj_columns()
    w_main = _gather_columns(w_in, main_idx).astype(BF16)
    w_side = _gather_columns(w_in, side_idx).astype(BF16)
    w_out_b = w_out.astype(BF16)
    f1 = (ffn1_wg.astype(BF16), ffn1_wu.astype(BF16), ffn1_wd.astype(BF16))
    f2 = (ffn2_wg.astype(BF16), ffn2_wu.astype(BF16), ffn2_wd.astype(BF16))
    g_ffn1 = norm_ffn1.reshape(DEPTH, 1, D_MODEL)
    g_mix = norm_mix.reshape(DEPTH, 1, D_MODEL)
    g_ffn2 = norm_ffn2.reshape(DEPTH, 1, D_MODEL)
    mw = _mla_weights(mla_q_lat_gain, mla_w_q_up, mla_q_gain, mla_kv_lat_gain, mla_w_kv_up, mla_k_gain)
    cos_lat, sin_lat = _rope_tables(DEC_SEQ)
    cos_all = jnp.concatenate([jnp.ones((TOK_TM, LANE), F32), cos_lat], axis=0)
    sin_all = jnp.concatenate([jnp.zeros((TOK_TM, LANE), F32), sin_lat], axis=0)
    ret_lg = jax.nn.log_sigmoid(ret_decay_logit.astype(F32))
    ret_gain = ret_norm_g.reshape(DEPTH, 1, GROUP_W)
    ret_h0 = state_ret.reshape(DEC_BATCH, DEPTH, 2, 2, LANE, LANE)

    c_all = jnp.zeros((MOD_ROWS, D_MODEL), F32).at[0].set(c_ctx).at[1:1 + DEC_BATCH].set(c)
    mod = _modulation(c_all, w_mod, b_mod).reshape(DEPTH * MOD_ROWS, 1, N_MOD * D_MODEL)

    x = jnp.concatenate([x_prompt.reshape(CTX_ROWS, D_MODEL), x_sample.reshape(LAT_ROWS, D_MODEL)], axis=0)
    kv_list, ssd_list, ret_list, gla_list = [], [], [], []
    for l in range(DEPTH):
        x = _ffn((x,), mod, g_ffn1, *f1, l, 0)
        um, us = _in_proj(x, mod, g_mix, w_main, w_side, l)

        ym_c, ym_l, kv_lat = _mla_layer(um, us, cache_mla_kv[:, l], mw, cos_all, sin_all, l)
        yr_c, s_ret = _retention(um, ret_lg[l], ret_gain[l], None, None, BATCH, SEQ, 0)
        yr_l, _ = _retention(um, ret_lg[l], ret_gain[l], ret_h0[:, l], (cos_lat, sin_lat),
                             DEC_BATCH, DEC_SEQ, CTX_ROWS // DEC_SEQ)
        ys_c, s_ssd = _ssd_scan(um, us, sp, l, None, BATCH, SEQ, 0)
        ys_l, _ = _ssd_scan(um, us, sp, l, ssd_h0[:, l], DEC_BATCH, DEC_SEQ, CTX_ROWS // DEC_SEQ)
        yg_c, s_gla = _gla_scan(um, us, gp, l, None, BATCH, SEQ, 0)
        yg_l, _ = _gla_scan(um, us, gp, l, gla_h0[:, l], DEC_BATCH, DEC_SEQ, CTX_ROWS // DEC_SEQ)

        kv_list.append(kv_lat[:CTX_ROWS, :MLA_CACHE_W].reshape(BATCH, SEQ, MLA_CACHE_W))
        ssd_list.append(_ssd_unpack_state(s_ssd))
        ret_list.append(s_ret.reshape(BATCH, 2, RET_HEADS, RET_QK, RET_V))
        gla_list.append(_gla_unpack_state(s_gla))
        x = _out_proj(x, [ys_c, ym_c, yr_c, yg_c], [ys_l, ym_l, yr_l, yg_l], mod, w_out_b, l)
        x = _ffn((x,), mod, g_ffn2, *f2, l, 6, split_out=(l == DEPTH - 1))
    y_p = x[0].reshape(BATCH, SEQ, D_MODEL)
    y_s = x[1].reshape(DEC_BATCH, DEC_SEQ, D_MODEL)
    return (y_p, y_s, jnp.stack(kv_list, axis=1), jnp.stack(ssd_list, axis=1),
            jnp.stack(ret_list, axis=1), jnp.stack(gla_list, axis=1))
```

```python
import functools

import jax
import jax.numpy as jnp
import numpy as np
from jax import lax
from jax.experimental import pallas as pl
from jax.experimental.pallas import tpu as pltpu

F32 = jnp.float32
BF16 = jnp.bfloat16

D_MODEL = 2048
BATCH = 32
SEQ = 256
DEPTH = 4
DEC_BATCH = 4
DEC_SEQ = 4096
PAST_LEN = 256
GRID_W = 64
ROPE_BASE = 10000.0
EPS = 1e-6
D_FF = 5632
N_MOD = 9
GROUP_W = D_MODEL // 4
ATTN_BLOCK = 128

SSD_HEAD_DIM = 64
SSD_HEADS = GROUP_W // SSD_HEAD_DIM
SSD_STATE = 64
SSD_GROUPS = 2
SSD_CONV_K = 3
SSD_CHUNK = 128
SSD_CONV_CH = GROUP_W + 2 * SSD_GROUPS * SSD_STATE
SSD_IN = GROUP_W + SSD_CONV_CH + SSD_HEADS

MLA_HEADS = 4
MLA_NOPE = 128
MLA_ROPE = 64
MLA_V = GROUP_W // MLA_HEADS
MLA_Q_RANK = 384
MLA_KV_RANK = 128
MLA_QK = MLA_NOPE + MLA_ROPE
MLA_IN = MLA_Q_RANK + MLA_KV_RANK + MLA_ROPE
MLA_CACHE_W = MLA_KV_RANK + MLA_ROPE

RET_HEADS = 4
RET_QK = 64
RET_V = GROUP_W // RET_HEADS
RET_CHUNK = 128
RET_IN = 2 * RET_HEADS * RET_QK + 2 * GROUP_W

GLA_HEADS = 4
GLA_QK = 64
GLA_V = GROUP_W // GLA_HEADS
GLA_GATE_RANK = 16
GLA_GATE_TEMP = 16.0
GLA_CHUNK = 16
GLA_IN = 2 * GLA_HEADS * GLA_QK + GROUP_W + GLA_GATE_RANK + GROUP_W

IN_W = SSD_IN + MLA_IN + RET_IN + GLA_IN

LANE = 128
CTX_ROWS = BATCH * SEQ
LAT_ROWS = DEC_BATCH * DEC_SEQ
ROWS = CTX_ROWS + LAT_ROWS
MOD_ROWS = 8
VMEM_LIMIT = 56 * 1024 * 1024
TOK_TM = 512
MLA_HEAD_W = 2 * LANE

MAIN_SSD_Z = 0
MAIN_SSD_XS = 512
MAIN_RET_V = 1024
MAIN_RET_G = 1536
MAIN_GLA_V = 2048
MAIN_GLA_R = 2560
MAIN_SSD_BC = 3072
MAIN_RET_Q = 3328
MAIN_RET_K = 3584
MAIN_GLA_Q = 3840
MAIN_GLA_K = 4096
MAIN_MLA_CQ = 4352
MAIN_W = 4864
MAIN_TN = MAIN_W
SIDE_CKV = 0
SIDE_KPE = SIDE_CKV + MLA_KV_RANK
SIDE_DT = 2 * LANE
SIDE_G1 = SIDE_DT + SSD_HEADS
SIDE_W = 3 * LANE


def _in_proj_columns():
    o_ssd, o_mla, o_ret, o_gla = 0, SSD_IN, SSD_IN + MLA_IN, SSD_IN + MLA_IN + RET_IN
    main = np.full((MAIN_W,), -1, np.int64)

    def put(dst, src, n):
        main[dst:dst + n] = src + np.arange(n)
    qk = RET_HEADS * RET_QK
    put(MAIN_SSD_Z, o_ssd, GROUP_W)
    put(MAIN_SSD_XS, o_ssd + GROUP_W, GROUP_W)
    put(MAIN_SSD_BC, o_ssd + 2 * GROUP_W, 2 * SSD_GROUPS * SSD_STATE)
    put(MAIN_MLA_CQ, o_mla, MLA_Q_RANK)
    put(MAIN_RET_Q, o_ret, qk)
    put(MAIN_RET_K, o_ret + qk, qk)
    put(MAIN_RET_V, o_ret + 2 * qk, GROUP_W)
    put(MAIN_RET_G, o_ret + 2 * qk + GROUP_W, GROUP_W)
    put(MAIN_GLA_Q, o_gla, qk)
    put(MAIN_GLA_K, o_gla + qk, qk)
    put(MAIN_GLA_V, o_gla + 2 * qk, GROUP_W)
    put(MAIN_GLA_R, o_gla + 2 * qk + GROUP_W + GLA_GATE_RANK, GROUP_W)
    side = np.full((SIDE_W,), -1, np.int64)
    side[SIDE_CKV:SIDE_CKV + MLA_KV_RANK + MLA_ROPE] = o_mla + MLA_Q_RANK + np.arange(MLA_KV_RANK + MLA_ROPE)
    side[SIDE_DT:SIDE_DT + SSD_HEADS] = o_ssd + GROUP_W + SSD_CONV_CH + np.arange(SSD_HEADS)
    side[SIDE_G1:SIDE_G1 + GLA_GATE_RANK] = o_gla + 2 * qk + GROUP_W + np.arange(GLA_GATE_RANK)
    return main, side


def _gather_columns(w, idx):
    pieces, i, n = [], 0, len(idx)
    while i < n:
        j = i + 1
        if idx[i] < 0:
            while j < n and idx[j] < 0:
                j += 1
            pieces.append(jnp.zeros(w.shape[:-1] + (j - i,), w.dtype))
        else:
            while j < n and idx[j] == idx[j - 1] + 1:
                j += 1
            pieces.append(w[..., int(idx[i]):int(idx[i]) + (j - i)])
        i = j
    return jnp.concatenate(pieces, axis=-1)


def _mod_row(i, tm):
    ctx_tiles = CTX_ROWS // tm
    per_seq = DEC_SEQ // tm
    return jnp.where(i < ctx_tiles, 0, 1 + (i - ctx_tiles) // per_seq)


def _mod_spec(layer, which, tm, n_grid):
    if n_grid == 1:
        return pl.BlockSpec((1, 1, D_MODEL), lambda i: (layer * MOD_ROWS + _mod_row(i, tm), 0, which))
    return pl.BlockSpec((1, 1, D_MODEL), lambda i, j: (layer * MOD_ROWS + _mod_row(i, tm), 0, which))


def _dot(a, b):
    return jnp.dot(a, b, preferred_element_type=F32)


def _dot_nt(a, b):
    return lax.dot_general(a, b, (((1,), (1,)), ((), ())), preferred_element_type=F32)


def _dot_tn(a, b):
    return lax.dot_general(a, b, (((0,), (0,)), ((), ())), preferred_element_type=F32)


def _silu(x):
    return x * jax.nn.sigmoid(x)


def _mod_kernel(c_ref, w_ref, b_ref, o_ref):
    s = _silu(c_ref[...])
    hi = s.astype(BF16)
    lo = (s - hi.astype(F32)).astype(BF16)
    w = w_ref[...].astype(BF16)
    o_ref[...] = _dot(hi, w) + _dot(lo, w) + b_ref[...]


def _modulation(c_all, w_mod, b_mod):
    tn = 1024
    n = N_MOD * D_MODEL
    return pl.pallas_call(
        _mod_kernel,
        grid=(DEPTH, n // tn),
        in_specs=[pl.BlockSpec((MOD_ROWS, D_MODEL), lambda l, j: (0, 0)),
                  pl.BlockSpec((None, D_MODEL, tn), lambda l, j: (l, 0, j)),
                  pl.BlockSpec((None, 1, tn), lambda l, j: (l, 0, j))],
        out_specs=pl.BlockSpec((None, MOD_ROWS, tn), lambda l, j: (l, 0, j)),
        out_shape=jax.ShapeDtypeStruct((DEPTH, MOD_ROWS, n), F32),
        compiler_params=pltpu.CompilerParams(dimension_semantics=("arbitrary", "arbitrary"),
                                             vmem_limit_bytes=VMEM_LIMIT),
        name="modulation",
    )(c_all, w_mod, b_mod.reshape(DEPTH, 1, n))


NORM_ROWS = 16


def _norm_modulate(x_ref, g_ref, shift_ref, scale_ref, h_ref):
    gs = g_ref[...] * (1.0 + scale_ref[0])
    shift = shift_ref[0]

    def body(r, carry):
        rows = pl.ds(pl.multiple_of(r * NORM_ROWS, NORM_ROWS), NORM_ROWS)
        x = x_ref[rows, :]
        inv = lax.rsqrt(jnp.mean(x * x, axis=-1, keepdims=True) + EPS)
        h_ref[rows, :] = ((x * inv) * gs + shift).astype(BF16)
        return carry

    lax.fori_loop(0, x_ref.shape[0] // NORM_ROWS, body, 0, unroll=8)


FFN_TM = 1024
FFN_VMEM_LIMIT = 60 * 1024 * 1024
FFN_TF = 512
FFN_TN = 256
FFN_NF = D_FF // FFN_TF
FFN_NN = D_MODEL // FFN_TN
MXU_W = 256


def _ffn_kernel(x_ref, xt_ref, g_ref, shift_ref, scale_ref, gate_ref, wg_ref, wu_ref, wd_ref, *refs, ctx_tiles):
    outs, (h_ref, a_ref) = refs[:-2], refs[-2:]
    j = pl.program_id(1)
    is_ctx = pl.program_id(0) < ctx_tiles

    @pl.when(j == 0)
    def _():
        _norm_modulate(x_ref, g_ref, shift_ref, scale_ref, h_ref)

    @pl.when(j < FFN_NF)
    def _():
        h = h_ref[...]
        base = pl.multiple_of(j * FFN_TF, FFN_TF)
        for s in range(FFN_TF // MXU_W):
            cols = slice(s * MXU_W, (s + 1) * MXU_W)
            g = _dot(h, wg_ref[:, cols])
            u = _dot(h, wu_ref[:, cols])
            a_ref[:, pl.ds(base + s * MXU_W, MXU_W)] = (_silu(g) * u).astype(BF16)

    @pl.when(j >= FFN_NF)
    def _():
        y = xt_ref[...] + 0.5 * gate_ref[0] * _dot(a_ref[...], wd_ref[...])
        if len(outs) == 1:
            outs[0][...] = y
        else:
            @pl.when(is_ctx)
            def _():
                outs[0][...] = y

            @pl.when(jnp.logical_not(is_ctx))
            def _():
                outs[1][...] = y


def _ffn(x, mod, norm_g, wg, wu, wd, layer, mod_base, tm=FFN_TM, split_out=False):
    n_out = 2 if split_out else 1
    ctx_tiles = CTX_ROWS // tm

    def up_tile(i, j):
        return (layer, 0, jnp.minimum(j, FFN_NF - 1))

    def down_tile(j):
        return jnp.maximum(j - FFN_NF, 0)

    def row_of(k, n_groups, i):
        if n_groups == 1:
            return i
        return jnp.minimum(i, ctx_tiles - 1) if k == 0 else jnp.maximum(i - ctx_tiles, 0)

    def col_of(k, n_groups, i, j):
        if n_groups == 1:
            return down_tile(j)
        if k == 0:
            return jnp.where(i < ctx_tiles, down_tile(j), FFN_NN - 1)
        return jnp.where(i >= ctx_tiles, down_tile(j), 0)
    gate_blocks = D_MODEL // FFN_TN
    in_specs = [pl.BlockSpec((tm, D_MODEL), lambda i, j: (i, 0)),
                pl.BlockSpec((tm, FFN_TN), lambda i, j: (i, down_tile(j))),
                pl.BlockSpec((None, 1, D_MODEL), lambda i, j: (layer, 0, 0)),
                _mod_spec(layer, mod_base + 0, tm, 2),
                _mod_spec(layer, mod_base + 1, tm, 2),
                pl.BlockSpec((1, 1, FFN_TN), lambda i, j: (layer * MOD_ROWS + _mod_row(i, tm), 0,
                                                           (mod_base + 2) * gate_blocks + down_tile(j))),
                pl.BlockSpec((None, D_MODEL, FFN_TF), up_tile),
                pl.BlockSpec((None, D_MODEL, FFN_TF), up_tile),
                pl.BlockSpec((None, D_FF, FFN_TN), lambda i, j: (layer, 0, down_tile(j)))]
    args = [x, x, norm_g, mod, mod, mod, wg, wu, wd]
    out_rows = (CTX_ROWS, LAT_ROWS) if split_out else (ROWS,)
    res = pl.pallas_call(
        functools.partial(_ffn_kernel, ctx_tiles=ctx_tiles),
        grid=(ROWS // tm, FFN_NF + FFN_NN),
        in_specs=in_specs,
        out_specs=[pl.BlockSpec((tm, FFN_TN), lambda i, j, k=k: (row_of(k, n_out, i), col_of(k, n_out, i, j)))
                   for k in range(n_out)],
        out_shape=[jax.ShapeDtypeStruct((r, D_MODEL), F32) for r in out_rows],
        scratch_shapes=[pltpu.VMEM((tm, D_MODEL), BF16), pltpu.VMEM((tm, D_FF), BF16)],
        compiler_params=pltpu.CompilerParams(dimension_semantics=("parallel", "arbitrary"),
                                             vmem_limit_bytes=FFN_VMEM_LIMIT),
        name="ffn",
    )(*args)
    return res if split_out else res[0]


def _in_proj_kernel(x_ref, g_ref, shift_ref, scale_ref, w_ref, ws_ref, o_ref, os_ref, h_ref):
    @pl.when(pl.program_id(1) == 0)
    def _():
        x = x_ref[...]
        inv = lax.rsqrt(jnp.mean(x * x, axis=-1, keepdims=True) + EPS)
        h_ref[...] = ((x * inv) * (g_ref[...] * (1.0 + scale_ref[0])) + shift_ref[0]).astype(BF16)
        os_ref[...] = _dot(h_ref[...], ws_ref[...])

    o_ref[...] = _dot(h_ref[...], w_ref[...]).astype(o_ref.dtype)


def _in_proj(x, mod, norm_g, w_main, w_side, layer, tm=TOK_TM):
    return pl.pallas_call(
        _in_proj_kernel,
        grid=(ROWS // tm, MAIN_W // MAIN_TN),
        in_specs=[pl.BlockSpec((tm, D_MODEL), lambda i, j: (i, 0)),
                  pl.BlockSpec((None, 1, D_MODEL), lambda i, j: (layer, 0, 0)),
                  _mod_spec(layer, 3, tm, 2),
                  _mod_spec(layer, 4, tm, 2),
                  pl.BlockSpec((None, D_MODEL, MAIN_TN), lambda i, j: (layer, 0, j), pipeline_mode=pl.Buffered(1)),
                  pl.BlockSpec((None, D_MODEL, SIDE_W), lambda i, j: (layer, 0, 0), pipeline_mode=pl.Buffered(1))],
        out_specs=[pl.BlockSpec((tm, MAIN_TN), lambda i, j: (i, j)),
                   pl.BlockSpec((tm, SIDE_W), lambda i, j: (i, 0))],
        out_shape=[jax.ShapeDtypeStruct((ROWS, MAIN_W), BF16),
                   jax.ShapeDtypeStruct((ROWS, SIDE_W), F32)],
        scratch_shapes=[pltpu.VMEM((tm, D_MODEL), BF16)],
        compiler_params=pltpu.CompilerParams(dimension_semantics=("parallel", "arbitrary"),
                                             vmem_limit_bytes=VMEM_LIMIT),
        name="in_proj",
    )(x, norm_g, mod, mod, w_main, w_side)


def _out_proj_kernel(x_ref, *refs, ctx_tiles):
    ctx_refs, lat_refs, (gate_ref, w_ref, o_ref) = refs[0:4], refs[4:8], refs[8:]

    def run(y_refs):
        acc = _dot(y_refs[0][...], w_ref[0:GROUP_W, :])
        for g in range(1, 4):
            acc += _dot(y_refs[g][...], w_ref[g * GROUP_W:(g + 1) * GROUP_W, :])
        o_ref[...] = x_ref[...] + gate_ref[0] * acc

    @pl.when(pl.program_id(0) < ctx_tiles)
    def _():
        run(ctx_refs)

    @pl.when(pl.program_id(0) >= ctx_tiles)
    def _():
        run(lat_refs)


def _out_proj(x, ys_ctx, ys_lat, mod, w, layer, tm=TOK_TM):
    ctx_tiles = CTX_ROWS // tm
    row = pl.BlockSpec((tm, D_MODEL), lambda i: (i, 0))
    cspec = pl.BlockSpec((tm, GROUP_W), lambda i: (jnp.minimum(i, ctx_tiles - 1), 0))
    lspec = pl.BlockSpec((tm, GROUP_W), lambda i: (jnp.maximum(i - ctx_tiles, 0), 0))
    return pl.pallas_call(
        functools.partial(_out_proj_kernel, ctx_tiles=ctx_tiles),
        grid=(ROWS // tm,),
        in_specs=[row] + [cspec] * 4 + [lspec] * 4 + [
            _mod_spec(layer, 5, tm, 1),
            pl.BlockSpec((None, D_MODEL, D_MODEL), lambda i: (layer, 0, 0))],
        out_specs=row,
        out_shape=jax.ShapeDtypeStruct((ROWS, D_MODEL), F32),
        compiler_params=pltpu.CompilerParams(dimension_semantics=("parallel",),
                                             vmem_limit_bytes=VMEM_LIMIT),
        name="out_proj",
    )(x, *ys_ctx, *ys_lat, mod, w)


def _rope_tables(T):
    n_rows = T // GRID_W
    row = jnp.repeat(jnp.arange(n_rows, dtype=F32), GRID_W)
    col = jnp.tile(jnp.arange(GRID_W, dtype=F32), n_rows)
    d_axis = MLA_ROPE // 2
    inv = ROPE_BASE ** (-jnp.arange(0, d_axis, 2, dtype=F32) / d_axis)
    ar, ac = row[:, None] * inv, col[:, None] * inv
    cos = jnp.concatenate([jnp.cos(ar), jnp.cos(ar), jnp.cos(ac), jnp.cos(ac)], axis=-1)
    sin = jnp.concatenate([-jnp.sin(ar), jnp.sin(ar), -jnp.sin(ac), jnp.sin(ac)], axis=-1)
    return jnp.tile(cos, (1, 2)), jnp.tile(sin, (1, 2))


def _swap16(x):
    lane = lax.broadcasted_iota(jnp.int32, x.shape, 1)
    up = pltpu.roll(x, LANE - 16, 1)
    down = pltpu.roll(x, 16, 1)
    return jnp.where((lane % 32) < 16, up, down)


def _rope(x, cos, sin):
    return x * cos + _swap16(x) * sin


def _mla_q_kernel(c0_ref, c1_ref, c2_ref, gl_ref, w_ref, gq_ref, cos_ref, sin_ref, o_ref):
    cs = [r[...].astype(F32) for r in (c0_ref, c1_ref, c2_ref)]
    ss = sum(jnp.sum(c * c, axis=-1, keepdims=True) for c in cs)
    r = lax.rsqrt(ss / MLA_Q_RANK + EPS)
    q = sum(_dot((cs[i] * r * gl_ref[:, i * LANE:(i + 1) * LANE]).astype(BF16),
                 w_ref[i * LANE:(i + 1) * LANE, :]) for i in range(3))
    cos, sin = cos_ref[...], sin_ref[...]
    scale = MLA_QK ** -0.5
    for h in range(MLA_HEADS):
        a = q[:, h * MLA_HEAD_W:h * MLA_HEAD_W + LANE]
        b = q[:, h * MLA_HEAD_W + LANE:(h + 1) * MLA_HEAD_W]
        ssq = jnp.sum(a * a, axis=-1, keepdims=True) + jnp.sum(b * b, axis=-1, keepdims=True)
        rh = lax.rsqrt(ssq / MLA_QK + EPS) * scale
        o_ref[:, h * MLA_HEAD_W:h * MLA_HEAD_W + LANE] = (a * rh * gq_ref[:, :LANE]).astype(BF16)
        o_ref[:, h * MLA_HEAD_W + LANE:(h + 1) * MLA_HEAD_W] = _rope(b * rh * gq_ref[:, LANE:], cos, sin).astype(BF16)


def _rope_tile_index(i, tm):
    ctx_tiles = CTX_ROWS // tm
    return jnp.where(i < ctx_tiles, 0, 1 + (i - ctx_tiles) % (DEC_SEQ // tm))


def _mla_q(um, gl, wq, gq, cos_all, sin_all, layer, tm=TOK_TM):
    cq = MAIN_MLA_CQ // LANE
    tab = pl.BlockSpec((tm, LANE), lambda i: (_rope_tile_index(i, tm), 0))
    return pl.pallas_call(
        _mla_q_kernel,
        grid=(ROWS // tm,),
        in_specs=[pl.BlockSpec((tm, LANE), lambda i: (i, cq)),
                  pl.BlockSpec((tm, LANE), lambda i: (i, cq + 1)),
                  pl.BlockSpec((tm, LANE), lambda i: (i, cq + 2)),
                  pl.BlockSpec((None, 1, MLA_Q_RANK), lambda i: (layer, 0, 0)),
                  pl.BlockSpec((None, MLA_Q_RANK, MLA_HEADS * MLA_HEAD_W), lambda i: (layer, 0, 0)),
                  pl.BlockSpec((None, 1, MLA_HEAD_W), lambda i: (layer, 0, 0)),
                  tab, tab],
        out_specs=pl.BlockSpec((tm, MLA_HEADS * MLA_HEAD_W), lambda i: (i, 0)),
        out_shape=jax.ShapeDtypeStruct((ROWS, MLA_HEADS * MLA_HEAD_W), BF16),
        compiler_params=pltpu.CompilerParams(dimension_semantics=("parallel",), vmem_limit_bytes=VMEM_LIMIT),
        name="mla_q",
    )(um, um, um, gl, wq, gq, cos_all, sin_all)


def _mla_kv_kernel(s_ref, gl_ref, w_ref, gk_ref, cos_ref, sin_ref, lat_ref, k_ref, v_ref, *, normalize):
    ckv = s_ref[:, :LANE]
    kpe = s_ref[:, LANE:]
    if normalize:
        ckv = ckv * lax.rsqrt(jnp.mean(ckv * ckv, axis=-1, keepdims=True) + EPS) * gl_ref[...]
    lat_ref[:, :LANE] = ckv
    lat_ref[:, LANE:] = kpe
    kv = _dot(ckv.astype(BF16), w_ref[...])
    ss_pe = jnp.sum(kpe * kpe, axis=-1, keepdims=True)
    cos, sin = cos_ref[...], sin_ref[...]
    for h in range(MLA_HEADS):
        a = kv[:, h * LANE:(h + 1) * LANE]
        rh = lax.rsqrt((jnp.sum(a * a, axis=-1, keepdims=True) + ss_pe) / MLA_QK + EPS)
        k_ref[:, h * MLA_HEAD_W:h * MLA_HEAD_W + LANE] = (a * rh * gk_ref[:, :LANE]).astype(BF16)
        k_ref[:, h * MLA_HEAD_W + LANE:(h + 1) * MLA_HEAD_W] = _rope(kpe * rh * gk_ref[:, LANE:], cos, sin).astype(BF16)
    v_ref[...] = kv[:, MLA_HEADS * LANE:].astype(BF16)


def _mla_kv(src, gl, wkv, gk, cos_all, sin_all, layer, *, normalize, tab_index, tm=TOK_TM):
    rows = src.shape[0]
    tab = pl.BlockSpec((tm, LANE), lambda i: (tab_index(i, tm), 0))
    return pl.pallas_call(
        functools.partial(_mla_kv_kernel, normalize=normalize),
        grid=(rows // tm,),
        in_specs=[pl.BlockSpec((tm, 2 * LANE), lambda i: (i, 0)),
                  pl.BlockSpec((None, 1, MLA_KV_RANK), lambda i: (layer, 0, 0)),
                  pl.BlockSpec((None, MLA_KV_RANK, 2 * MLA_HEADS * LANE), lambda i: (layer, 0, 0)),
                  pl.BlockSpec((None, 1, MLA_HEAD_W), lambda i: (layer, 0, 0)),
                  tab, tab],
        out_specs=[pl.BlockSpec((tm, 2 * LANE), lambda i: (i, 0)),
                   pl.BlockSpec((tm, MLA_HEADS * MLA_HEAD_W), lambda i: (i, 0)),
                   pl.BlockSpec((tm, GROUP_W), lambda i: (i, 0))],
        out_shape=[jax.ShapeDtypeStruct((rows, 2 * LANE), F32),
                   jax.ShapeDtypeStruct((rows, MLA_HEADS * MLA_HEAD_W), BF16),
                   jax.ShapeDtypeStruct((rows, GROUP_W), BF16)],
        compiler_params=pltpu.CompilerParams(dimension_semantics=("parallel",), vmem_limit_bytes=VMEM_LIMIT),
        name="mla_kv",
    )(src, gl, wkv, gk, cos_all, sin_all)


ATTN_TQ_LAT = 256
ATTN_HEADS_PER_STEP = 4


def _attn_kernel(q_ref, *refs):
    o_ref = refs[-1]
    kv = [(refs[i], refs[i + 1]) for i in range(0, len(refs) - 1, 2)]
    heads = range(ATTN_HEADS_PER_STEP)
    scores = [[_dot_nt(q_ref[:, h * MLA_HEAD_W:(h + 1) * MLA_HEAD_W], k_ref[:, h * MLA_HEAD_W:(h + 1) * MLA_HEAD_W])
               for k_ref, _ in kv] for h in heads]
    for h in heads:
        m = functools.reduce(jnp.maximum, [jnp.max(s, axis=-1, keepdims=True) for s in scores[h]])
        ps = [jnp.exp(s - m) for s in scores[h]]
        l = sum(jnp.sum(p, axis=-1, keepdims=True) for p in ps)
        o = sum(_dot(p.astype(BF16), v_ref[:, h * MLA_V:(h + 1) * MLA_V]) for p, (_, v_ref) in zip(ps, kv))
        o_ref[:, h * MLA_V:(h + 1) * MLA_V] = (o / l).astype(o_ref.dtype)


def _attention(q, sources, n_seq, t, tq, q_row0):
    nq = t // tq
    hs = ATTN_HEADS_PER_STEP
    in_specs = [pl.BlockSpec((tq, hs * MLA_HEAD_W), lambda b, h, i: (q_row0 // tq + b * nq + i, h))]
    args = [q]
    for k, v, s_len, row0 in sources:
        in_specs += [pl.BlockSpec((s_len, hs * MLA_HEAD_W), lambda b, h, i, o=row0 // s_len: (o + b, h)),
                     pl.BlockSpec((s_len, hs * MLA_V), lambda b, h, i, o=row0 // s_len: (o + b, h))]
        args += [k, v]
    return pl.pallas_call(
        _attn_kernel,
        grid=(n_seq, MLA_HEADS // hs, nq),
        in_specs=in_specs,
        out_specs=pl.BlockSpec((tq, hs * MLA_V), lambda b, h, i: (b * nq + i, h)),
        out_shape=jax.ShapeDtypeStruct((n_seq * t, GROUP_W), BF16),
        compiler_params=pltpu.CompilerParams(dimension_semantics=("parallel", "parallel", "arbitrary"),
                                             vmem_limit_bytes=VMEM_LIMIT),
        name="mla_attention",
    )(*args)


def _mla_weights(mla_q_lat_gain, mla_w_q_up, mla_q_gain, mla_kv_lat_gain, mla_w_kv_up, mla_k_gain):
    qcol = np.full((MLA_HEADS * MLA_HEAD_W,), -1, np.int64)
    kvcol = np.zeros((2 * MLA_HEADS * LANE,), np.int64)
    for h in range(MLA_HEADS):
        qcol[h * MLA_HEAD_W:h * MLA_HEAD_W + MLA_QK] = h * MLA_QK + np.arange(MLA_QK)
        kvcol[h * LANE:(h + 1) * LANE] = h * (MLA_NOPE + MLA_V) + np.arange(MLA_NOPE)
        kvcol[(MLA_HEADS + h) * LANE:(MLA_HEADS + h + 1) * LANE] = h * (MLA_NOPE + MLA_V) + MLA_NOPE + np.arange(MLA_V)
    pad = jnp.zeros((DEPTH, MLA_HEAD_W - MLA_QK), F32)
    return dict(
        gl=mla_q_lat_gain.reshape(DEPTH, 1, MLA_Q_RANK),
        wq=_gather_columns(mla_w_q_up, qcol).astype(BF16),
        gq=jnp.concatenate([mla_q_gain, pad], axis=-1).reshape(DEPTH, 1, MLA_HEAD_W),
        gkv=mla_kv_lat_gain.reshape(DEPTH, 1, MLA_KV_RANK),
        wkv=_gather_columns(mla_w_kv_up, kvcol).astype(BF16),
        gk=jnp.concatenate([mla_k_gain, pad], axis=-1).reshape(DEPTH, 1, MLA_HEAD_W))


def _mla_layer(um, us, cache_l, mw, cos_all, sin_all, layer):
    q = _mla_q(um, mw['gl'], mw['wq'], mw['gq'], cos_all, sin_all, layer)
    kv_lat, k, v = _mla_kv(us, mw['gkv'], mw['wkv'], mw['gk'], cos_all, sin_all, layer,
                           normalize=True, tab_index=_rope_tile_index)
    cache2 = jnp.pad(cache_l.reshape(DEC_BATCH * PAST_LEN, MLA_CACHE_W), ((0, 0), (0, 2 * LANE - MLA_CACHE_W)))
    _, k_c, v_c = _mla_kv(cache2, mw['gkv'], mw['wkv'], mw['gk'], cos_all, sin_all, layer,
                          normalize=False, tab_index=lambda i, tm: 0)
    y_ctx = _attention(q, [(k, v, SEQ, 0)], BATCH, SEQ, SEQ, 0)
    y_lat = _attention(q, [(k, v, DEC_SEQ, CTX_ROWS), (k_c, v_c, PAST_LEN, 0)], DEC_BATCH, DEC_SEQ, ATTN_TQ_LAT,
                       CTX_ROWS)
    return y_ctx, y_lat, kv_lat


def _ret_kernel(*refs, T, C, rope, has_h0, has_state):
    it = iter(refs)
    q_ref, k_ref, v_ref, g_ref = next(it), next(it), next(it), next(it)
    cos_ref, sin_ref = (next(it), next(it)) if rope else (None, None)
    lg_ref, gain_ref = next(it), next(it)
    h0_ref = next(it) if has_h0 else None
    y_ref = next(it)
    st_ref = next(it) if has_state else None
    qs_ref, ks_ref, oacc_ref, dm_ref, tab_ref, hst_ref = (next(it) for _ in range(6))

    nc = T // C
    ii = lax.broadcasted_iota(jnp.int32, (C, C), 0)
    jj = lax.broadcasted_iota(jnp.int32, (C, C), 1)
    dif = (ii - jj).astype(F32)
    lane = lax.broadcasted_iota(jnp.int32, (C, LANE), 1)
    rowi = lax.broadcasted_iota(jnp.int32, (C, LANE), 0).astype(F32)
    m_lo = lane < RET_QK
    hrow = lax.broadcasted_iota(jnp.int32, (LANE, LANE), 0) < RET_QK

    for h in range(RET_HEADS):
        lf, lb = lg_ref[0, h], lg_ref[1, h]
        dm_ref[h] = (jnp.where(dif >= 0, jnp.exp(lf * jnp.maximum(dif, 0.0)), 0.0)
                     + jnp.where(dif <= 0, jnp.exp(lb * jnp.maximum(-dif, 0.0)), 0.0))
    for p in range(2):
        lf = jnp.where(m_lo, lg_ref[0, 2 * p], lg_ref[0, 2 * p + 1])
        lb = jnp.where(m_lo, lg_ref[1, 2 * p], lg_ref[1, 2 * p + 1])
        tab_ref[p, 0] = jnp.exp(lf * (rowi + 1.0))
        tab_ref[p, 1] = jnp.exp(lf * (C - 1.0 - rowi))
        tab_ref[p, 2] = jnp.exp(lb * (C - rowi))
        tab_ref[p, 3] = jnp.exp(lb * rowi)
    if has_h0:
        hst_ref[...] = h0_ref[...]
    else:
        hst_ref[...] = jnp.zeros_like(hst_ref)

    def chunk_decay(d, p):
        return jnp.exp(jnp.where(hrow, lg_ref[d, 2 * p], lg_ref[d, 2 * p + 1]) * float(C))

    def forward(c, carry):
        r0 = pl.multiple_of(c * C, C)
        rows = pl.ds(r0, C)
        for p in range(2):
            cols = slice(p * LANE, (p + 1) * LANE)
            qp = q_ref[rows, cols].astype(F32)
            kp = k_ref[rows, cols].astype(F32) * (RET_QK ** -0.5)
            if rope:
                qp = _rope(qp, cos_ref[rows, :], sin_ref[rows, :])
                kp = _rope(kp, cos_ref[rows, :], sin_ref[rows, :])
            qs_ref[rows, cols] = qp.astype(BF16)
            ks_ref[rows, cols] = kp.astype(BF16)
            kpb = kp.astype(BF16)
            hp = hst_ref[0, p]
            hpb = hp.astype(BF16)
            upd = jnp.zeros((LANE, LANE), F32)
            for e in range(2):
                h = 2 * p + e
                mh = m_lo if e == 0 else jnp.logical_not(m_lo)
                hcols = slice(h * LANE, (h + 1) * LANE)
                qm = jnp.where(mh, qp, 0.0)
                s = _dot_nt(qm.astype(BF16), kpb)
                vh = v_ref[rows, hcols]
                o = _dot((s * dm_ref[h]).astype(BF16), vh)
                o += _dot((qm * tab_ref[p, 0]).astype(BF16), hpb)
                oacc_ref[rows, hcols] = o
                upd += _dot_tn(jnp.where(mh, kp * tab_ref[p, 1], 0.0).astype(BF16), vh)
            hst_ref[0, p] = hp * chunk_decay(0, p) + upd
        return carry

    lax.fori_loop(0, nc, forward, 0)

    def backward(t, carry):
        c = nc - 1 - t
        r0 = pl.multiple_of(c * C, C)
        rows = pl.ds(r0, C)
        for p in range(2):
            cols = slice(p * LANE, (p + 1) * LANE)
            qp = qs_ref[rows, cols].astype(F32)
            kp = ks_ref[rows, cols].astype(F32)
            hp = hst_ref[1, p]
            hpb = hp.astype(BF16)
            upd = jnp.zeros((LANE, LANE), F32)
            for e in range(2):
                h = 2 * p + e
                mh = m_lo if e == 0 else jnp.logical_not(m_lo)
                hcols = slice(h * LANE, (h + 1) * LANE)
                vh = v_ref[rows, hcols]
                o = oacc_ref[rows, hcols] + _dot(jnp.where(mh, qp * tab_ref[p, 2], 0.0).astype(BF16), hpb)
                upd += _dot_tn(jnp.where(mh, kp * tab_ref[p, 3], 0.0).astype(BF16), vh)
                oc = o - jnp.mean(o, axis=-1, keepdims=True)
                yn = oc * lax.rsqrt(jnp.mean(oc * oc, axis=-1, keepdims=True) + EPS) * gain_ref[:, hcols]
                y_ref[rows, hcols] = (_silu(g_ref[rows, hcols].astype(F32)) * yn).astype(y_ref.dtype)
            hst_ref[1, p] = hp * chunk_decay(1, p) + upd
        return carry

    lax.fori_loop(0, nc, backward, 0)
    if has_state:
        st_ref[...] = hst_ref[...]


def _retention(um, lg, gain, h0, rope_tabs, n_seq, T, row_block0):
    C = RET_CHUNK
    rope = rope_tabs is not None
    has_h0 = h0 is not None
    has_state = not has_h0

    def col(off, w):
        return pl.BlockSpec((T, w), lambda b: (row_block0 + b, off // w))
    in_specs = [col(MAIN_RET_Q, 2 * LANE), col(MAIN_RET_K, 2 * LANE), col(MAIN_RET_V, GROUP_W), col(MAIN_RET_G, GROUP_W)]
    args = [um, um, um, um]
    if rope:
        in_specs += [pl.BlockSpec((T, LANE), lambda b: (0, 0))] * 2
        args += list(rope_tabs)
    in_specs += [pl.BlockSpec(memory_space=pltpu.SMEM), pl.BlockSpec((1, GROUP_W), lambda b: (0, 0))]
    args += [lg, gain]
    st_spec = pl.BlockSpec((None, 2, 2, LANE, LANE), lambda b: (b, 0, 0, 0, 0))
    if has_h0:
        in_specs.append(st_spec)
        args.append(h0)
    out_specs = [pl.BlockSpec((T, GROUP_W), lambda b: (b, 0))]
    out_shape = [jax.ShapeDtypeStruct((n_seq * T, GROUP_W), BF16)]
    if has_state:
        out_specs.append(st_spec)
        out_shape.append(jax.ShapeDtypeStruct((n_seq, 2, 2, LANE, LANE), F32))
    res = pl.pallas_call(
        functools.partial(_ret_kernel, T=T, C=C, rope=rope, has_h0=has_h0, has_state=has_state),
        grid=(n_seq,),
        in_specs=in_specs,
        out_specs=out_specs,
        out_shape=out_shape,
        scratch_shapes=[pltpu.VMEM((T, 2 * LANE), BF16), pltpu.VMEM((T, 2 * LANE), BF16),
                        pltpu.VMEM((T, GROUP_W), F32), pltpu.VMEM((RET_HEADS, C, C), F32),
                        pltpu.VMEM((2, 4, C, LANE), F32), pltpu.VMEM((2, 2, LANE, LANE), F32)],
        compiler_params=pltpu.CompilerParams(dimension_semantics=("parallel",), vmem_limit_bytes=VMEM_LIMIT),
        name="retention",
    )(*args)
    return (res[0], res[1]) if has_state else (res[0], None)


def _split3(x):
    x1 = x.astype(BF16)
    r1 = x - x1.astype(F32)
    x2 = r1.astype(BF16)
    x3 = (r1 - x2.astype(F32)).astype(BF16)
    return x1, x2, x3


def _dot01_l(m, x):
    x1, x2, x3 = _split3(x)
    return _dot(m, x1) + _dot(m, x2) + _dot(m, x3)


def _dot01_r(x, m):
    x1, x2, x3 = _split3(x)
    return _dot(x1, m) + _dot(x2, m) + _dot(x3, m)


def _softplus(x):
    return jnp.maximum(x, 0.0) + jnp.log(1.0 + jnp.exp(-jnp.abs(x)))


def _head_expander():
    e64 = np.zeros((LANE, GROUP_W), np.float32)
    for h in range(SSD_HEADS):
        e64[h, h * SSD_HEAD_DIM:(h + 1) * SSD_HEAD_DIM] = 1.0
    return jnp.asarray(e64, BF16)


def _ssd_kernel(*refs, T, has_h0, has_state):
    it = iter(refs)
    z_ref, xs_ref, bc_ref, dt_ref = next(it), next(it), next(it), next(it)
    wx_ref, wbc_ref, bx_ref, bbc_ref = next(it), next(it), next(it), next(it)
    a_ref, dtb_ref, d_ref, gain_ref, e64_ref = (next(it) for _ in range(5))
    h0_ref = next(it) if has_h0 else None
    y_ref = next(it)
    st_ref = next(it) if has_state else None
    xc_s, bcs_s, oacc_ref, hst_ref = next(it), next(it), next(it), next(it)

    C = SSD_CHUNK
    nc = T // C
    ii = lax.broadcasted_iota(jnp.int32, (C, C), 0)
    jj = lax.broadcasted_iota(jnp.int32, (C, C), 1)
    tril, triu = ii >= jj, ii <= jj
    tril_b, triu_b = tril.astype(BF16), triu.astype(BF16)
    lane = lax.broadcasted_iota(jnp.int32, (C, LANE), 1)
    m_lo = lane < SSD_STATE
    masks = (m_lo, jnp.logical_not(m_lo))
    hrow = lax.broadcasted_iota(jnp.int32, (LANE, LANE), 0) < SSD_STATE

    if has_h0:
        hst_ref[...] = h0_ref[...]
    else:
        hst_ref[...] = jnp.zeros_like(hst_ref)

    def decays(rows, d):
        dtv = _softplus(dt_ref[rows, :] + dtb_ref[d:d + 1, :])
        la = dtv * a_ref[d:d + 1, :]
        return dtv, _dot01_l(tril_b if d == 0 else triu_b, la)

    def conv(ref, w_ref, b_ref, c, r0):
        width = ref.shape[1]
        x = ref[pl.ds(r0, C), :].astype(F32)
        p0 = pl.multiple_of(jnp.maximum(r0 - 16, 0), 16)
        n0 = pl.multiple_of(jnp.minimum(r0 + C, T - 16), 16)
        prev_row = jnp.where(c > 0, ref[pl.ds(p0, 16), :].astype(F32)[15:16], 0.0)
        next_row = jnp.where(c < nc - 1, ref[pl.ds(n0, 16), :].astype(F32)[0:1], 0.0)
        rowi = lax.broadcasted_iota(jnp.int32, (C, width), 0)
        prev = jnp.where(rowi == 0, prev_row, pltpu.roll(x, 1, 0))
        nxt = jnp.where(rowi == C - 1, next_row, pltpu.roll(x, C - 1, 0))
        return _silu(prev * w_ref[0:1, :] + x * w_ref[1:2, :] + nxt * w_ref[2:3, :] + b_ref[...])

    def decay_matrix(qrow, pp):
        return jnp.where(hrow, qrow[:, pp * LANE:(pp + 1) * LANE],
                         qrow[:, (2 + pp) * LANE:(3 + pp) * LANE])

    def forward(c, carry):
        r0 = pl.multiple_of(c * C, C)
        rows = pl.ds(r0, C)
        xc = conv(xs_ref, wx_ref, bx_ref, c, r0)
        bcv = conv(bc_ref, wbc_ref, bbc_ref, c, r0)
        xc_s[rows, :] = xc.astype(BF16)
        bcs_s[rows, :] = bcv.astype(BF16)
        bmat, cmat = bcv[:, :LANE], bcv[:, LANE:]
        bmb = bmat.astype(BF16)
        dtf, bf = decays(rows, 0)
        dtb, bb = decays(rows, 1)
        bf_t, dtf_t, bb_t, dtb_t = bf.T, dtf.T, bb.T, dtb.T
        qdf = _dot01_r(jnp.exp(bf), e64_ref[...])
        kwf = _dot01_r(dtf * jnp.exp(bf[C - 1:C, :] - bf), e64_ref[...])
        cms = [jnp.where(masks[g], cmat, 0.0).astype(BF16) for g in range(SSD_GROUPS)]
        bms = [jnp.where(masks[g], bmat, 0.0).astype(BF16) for g in range(SSD_GROUPS)]
        scores = [_dot_nt(cms[g], bmb) for g in range(SSD_GROUPS)]
        for pp in range(2):
            hp = hst_ref[0, pp]
            hpb = hp.astype(BF16)
            upd = jnp.zeros((LANE, LANE), F32)
            for g in range(SSD_GROUPS):
                cols = slice((2 * g + pp) * LANE, (2 * g + pp + 1) * LANE)
                xs_pair = xc[:, cols]
                xsb = xs_pair.astype(BF16)
                outs = []
                for e in range(2):
                    h = 4 * g + 2 * pp + e
                    hc = slice(h * LANE, (h + 1) * LANE)
                    bf_col = jnp.broadcast_to(bf[:, h:h + 1], (C, LANE))
                    bb_col = jnp.broadcast_to(bb[:, h:h + 1], (C, LANE))
                    ef = jnp.where(tril, jnp.exp(jnp.minimum(bf_col - bf_t[h:h + 1, :], 0.0)), 0.0) * dtf_t[h:h + 1, :]
                    eb = jnp.where(triu, jnp.exp(jnp.minimum(bb_col - bb_t[h:h + 1, :], 0.0)), 0.0) * dtb_t[h:h + 1, :]
                    outs.append(_dot((scores[g] * (ef + eb)).astype(BF16), xsb))
                o = jnp.where(m_lo, outs[0], outs[1])
                o += _dot(cms[g], hpb) * qdf[:, cols]
                oacc_ref[rows, cols] = o
                upd += _dot_tn(bms[g], (xs_pair * kwf[:, cols]).astype(BF16))
            hst_ref[0, pp] = hp * decay_matrix(qdf[C - 1:C, :], pp) + upd
        return carry

    lax.fori_loop(0, nc, forward, 0)

    def backward(t, carry):
        c = nc - 1 - t
        r0 = pl.multiple_of(c * C, C)
        rows = pl.ds(r0, C)
        xc = xc_s[rows, :].astype(F32)
        bcv = bcs_s[rows, :]
        bmat, cmat = bcv[:, :LANE], bcv[:, LANE:]
        zero = jnp.zeros_like(bmat)
        dtb, bb = decays(rows, 1)
        qdb = _dot01_r(jnp.exp(bb), e64_ref[...])
        kwb = _dot01_r(dtb * jnp.exp(bb[0:1, :] - bb), e64_ref[...])
        blocks = {}
        for pp in range(2):
            hp = hst_ref[1, pp]
            hpb = hp.astype(BF16)
            upd = jnp.zeros((LANE, LANE), F32)
            for g in range(SSD_GROUPS):
                blk = 2 * g + pp
                cols = slice(blk * LANE, (blk + 1) * LANE)
                xs_pair = xc[:, cols]
                o = oacc_ref[rows, cols] + _dot(jnp.where(masks[g], cmat, zero), hpb) * qdb[:, cols]
                upd += _dot_tn(jnp.where(masks[g], bmat, zero), (xs_pair * kwb[:, cols]).astype(BF16))
                y = (o + d_ref[:, cols] * xs_pair) * _silu(z_ref[rows, cols].astype(F32))
                blocks[blk] = y
            hst_ref[1, pp] = hp * decay_matrix(qdb[0:1, :], pp) + upd
        for g in range(SSD_GROUPS):
            y0, y1 = blocks[2 * g], blocks[2 * g + 1]
            ss = jnp.sum(y0 * y0, axis=-1, keepdims=True) + jnp.sum(y1 * y1, axis=-1, keepdims=True)
            r = lax.rsqrt(ss / (2 * LANE) + EPS)
            for i, yb in enumerate((y0, y1)):
                cols = slice((2 * g + i) * LANE, (2 * g + i + 1) * LANE)
                y_ref[rows, cols] = (yb * r * gain_ref[:, cols]).astype(y_ref.dtype)
        return carry

    lax.fori_loop(0, nc, backward, 0)
    if has_state:
        st_ref[...] = hst_ref[...]


def _ssd_pack_state(st):
    n = st.shape[0]
    st = st.reshape(n, 2, SSD_GROUPS, 2, 2, SSD_STATE, SSD_HEAD_DIM)
    return st.transpose(0, 1, 3, 2, 5, 4, 6).reshape(n, 2, 2, LANE, LANE)


def _ssd_unpack_state(st):
    n = st.shape[0]
    st = st.reshape(n, 2, 2, SSD_GROUPS, SSD_STATE, 2, SSD_HEAD_DIM)
    return st.transpose(0, 1, 3, 2, 5, 4, 6).reshape(n, 2, SSD_HEADS, SSD_STATE, SSD_HEAD_DIM)


def _ssd_params(ssd_conv_w, ssd_conv_b, ssd_a_log, ssd_dt_bias, ssd_d, ssd_norm_g):
    def lanes8(v):
        return jnp.pad(v.astype(F32), ((0, 0), (0, 0), (0, LANE - SSD_HEADS)))
    e64 = _head_expander()
    return dict(wx=ssd_conv_w[:, :, :GROUP_W], wbc=ssd_conv_w[:, :, GROUP_W:],
                bx=ssd_conv_b[:, None, :GROUP_W], bbc=ssd_conv_b[:, None, GROUP_W:],
                a=lanes8(-jnp.exp(ssd_a_log.astype(F32))), dtb=lanes8(ssd_dt_bias),
                d=jnp.repeat(ssd_d, SSD_HEAD_DIM, axis=-1)[:, None, :], gain=ssd_norm_g[:, None, :],
                e64=e64)


def _ssd_scan(um, us, sp, layer, h0, n_seq, T, row_block0):
    has_h0 = h0 is not None
    has_state = not has_h0

    def col(off, w):
        return pl.BlockSpec((T, w), lambda b: (row_block0 + b, off // w))

    def per_layer(arr):
        return pl.BlockSpec((None,) + arr.shape[1:], lambda b: (layer,) + (0,) * (arr.ndim - 1))

    def const(arr):
        return pl.BlockSpec(arr.shape, lambda b: (0,) * arr.ndim)
    names = ('wx', 'wbc', 'bx', 'bbc', 'a', 'dtb', 'd', 'gain')
    in_specs = [col(MAIN_SSD_Z, GROUP_W), col(MAIN_SSD_XS, GROUP_W), col(MAIN_SSD_BC, 2 * LANE),
                pl.BlockSpec((T, LANE), lambda b: (row_block0 + b, SIDE_DT // LANE))]
    in_specs += [per_layer(sp[n]) for n in names] + [const(sp['e64'])]
    args = [um, um, um, us] + [sp[n] for n in names] + [sp['e64']]
    st_spec = pl.BlockSpec((None, 2, 2, LANE, LANE), lambda b: (b, 0, 0, 0, 0))
    if has_h0:
        in_specs.append(st_spec)
        args.append(h0)
    out_specs = [pl.BlockSpec((T, GROUP_W), lambda b: (b, 0))]
    out_shape = [jax.ShapeDtypeStruct((n_seq * T, GROUP_W), BF16)]
    if has_state:
        out_specs.append(st_spec)
        out_shape.append(jax.ShapeDtypeStruct((n_seq, 2, 2, LANE, LANE), F32))
    res = pl.pallas_call(
        functools.partial(_ssd_kernel, T=T, has_h0=has_h0, has_state=has_state),
        grid=(n_seq,),
        in_specs=in_specs,
        out_specs=out_specs,
        out_shape=out_shape,
        scratch_shapes=[pltpu.VMEM((T, GROUP_W), BF16), pltpu.VMEM((T, 2 * LANE), BF16),
                        pltpu.VMEM((T, GROUP_W), F32), pltpu.VMEM((2, 2, LANE, LANE), F32)],
        compiler_params=pltpu.CompilerParams(dimension_semantics=("parallel",), vmem_limit_bytes=VMEM_LIMIT),
        name="ssd",
    )(*args)
    return (res[0], res[1]) if has_state else (res[0], None)


GLA_MACRO = 128


def _log_sigmoid(x):
    return jnp.minimum(x, 0.0) - jnp.log(1.0 + jnp.exp(-jnp.abs(x)))


def _gla_kernel(*refs, T, has_h0, has_state):
    it = iter(refs)
    q_ref, k_ref, v_ref, r_ref, g1_ref = (next(it) for _ in range(5))
    wg_ref, bg_ref, gain_ref, ind_ref = (next(it) for _ in range(4))
    h0_ref = next(it) if has_h0 else None
    y_ref = next(it)
    st_ref = next(it) if has_state else None
    oacc_ref, hst_ref, qbuf, kbuf, bfbuf, bbbuf, vbuf, obuf = (next(it) for _ in range(8))

    C, L = GLA_MACRO, GLA_CHUNK
    nb = C // L
    nc = T // C
    ii = lax.broadcasted_iota(jnp.int32, (C, C), 0)
    jj = lax.broadcasted_iota(jnp.int32, (C, C), 1)
    same = (ii // L) == (jj // L)
    tri_l = jnp.logical_and(same, jj <= ii).astype(BF16)
    tri_u = jnp.logical_and(same, jj >= ii).astype(BF16)
    ones_b = same.astype(BF16)
    lane = lax.broadcasted_iota(jnp.int32, (C, LANE), 1)
    masks = (lane < GLA_QK, lane >= GLA_QK)
    HALF = 8
    ri8 = lax.broadcasted_iota(jnp.int32, (HALF, 2 * LANE), 0)
    bdmask = (lax.broadcasted_iota(jnp.int32, (C, nb * LANE), 0) // L
              == lax.broadcasted_iota(jnp.int32, (C, nb * LANE), 1) // LANE)

    if has_h0:
        hst_ref[...] = h0_ref[...]
    else:
        hst_ref[...] = jnp.zeros_like(hst_ref)

    def log_decay(rows, d):
        g1 = g1_ref[rows, :]
        hi = g1.astype(BF16)
        lo = (g1 - hi.astype(F32)).astype(BF16)
        logits = _dot(hi, wg_ref[d]) + _dot(lo, wg_ref[d]) + bg_ref[d]
        return _log_sigmoid(logits) / GLA_GATE_TEMP

    def recurrence(d, rows, qt, kt, dec, blocks):
        zero = jnp.zeros((), BF16)

        def block_diag(x):
            return jnp.where(bdmask, jnp.concatenate([x] * nb, axis=1), zero)
        upds = []
        for h in range(GLA_HEADS):
            p, e = divmod(h, 2)
            km = jnp.where(masks[e], kt[:, p * LANE:(p + 1) * LANE], 0.0).astype(BF16)
            upds.append(_dot_tn(v_ref[rows, h * LANE:(h + 1) * LANE], block_diag(km)))
        snaps = []
        for h in range(GLA_HEADS):
            lanes = slice((h // 2) * LANE, (h // 2 + 1) * LANE)
            ht = hst_ref[d, h]
            snap = [None] * nb
            for blk in blocks:
                snap[blk] = ht.astype(BF16)
                ht = ht * dec[blk * L:blk * L + 1, lanes] + upds[h][:, blk * LANE:(blk + 1) * LANE]
            hst_ref[d, h] = ht
            snaps.append(jnp.concatenate(snap, axis=1))
        outs = []
        for h in range(GLA_HEADS):
            p, e = divmod(h, 2)
            qm = jnp.where(masks[e], qt[:, p * LANE:(p + 1) * LANE], 0.0).astype(BF16)
            outs.append(_dot_nt(block_diag(qm), snaps[h]))
        return outs

    def intra_block(b0):
        pieces = []
        for j in range(L):
            kj, bfj, bbj = kbuf[b0 + j:b0 + j + 1, :], bfbuf[b0 + j:b0 + j + 1, :], bbbuf[b0 + j:b0 + j + 1, :]
            halves = []
            for s in range(L // HALF):
                rs = slice(b0 + s * HALF, b0 + (s + 1) * HALF)
                qk = qbuf[rs, :] * kj
                if j < s * HALF:
                    e = qk * jnp.exp(bfbuf[rs, :] - bfj)
                elif j >= (s + 1) * HALF:
                    e = qk * jnp.exp(bbbuf[rs, :] - bbj)
                else:
                    rel = ri8 + (s * HALF - j)
                    e = qk * jnp.exp(jnp.where(rel >= 0, bfbuf[rs, :] - bfj, bbbuf[rs, :] - bbj))
                    e = jnp.where(rel == 0, 2.0 * e, e)
                halves.append(e)
            pieces.append(jnp.concatenate(halves, axis=0).astype(BF16))
        spread = _dot(jnp.concatenate(pieces, axis=0), ind_ref[...])
        acc = spread[0:L, :] * vbuf[b0:b0 + 1, :]
        for j in range(1, L):
            acc += spread[j * L:(j + 1) * L, :] * vbuf[b0 + j:b0 + j + 1, :]
        return acc

    def forward_part(rows):
        q = q_ref[rows, :].astype(F32) * (GLA_QK ** -0.5)
        k = k_ref[rows, :].astype(F32)
        la_f, la_b = log_decay(rows, 0), log_decay(rows, 1)
        bf, tot_f = _dot01_l(tri_l, la_f), _dot01_l(ones_b, la_f)
        bb = _dot01_l(tri_u, la_b)
        qbuf[...] = q
        kbuf[...] = k
        bfbuf[...] = bf
        bbbuf[...] = bb
        vbuf[...] = v_ref[rows, :].astype(F32)
        for blk in range(nb):
            obuf[blk * L:(blk + 1) * L, :] = intra_block(blk * L)
        qt, kt, dec = q * jnp.exp(bf), k * jnp.exp(tot_f - bf), jnp.exp(tot_f)
        inter = recurrence(0, rows, qt, kt, dec, range(nb))
        return [obuf[:, h * LANE:(h + 1) * LANE] + inter[h] for h in range(GLA_HEADS)]

    def backward_part(rows):
        q = q_ref[rows, :].astype(F32) * (GLA_QK ** -0.5)
        k = k_ref[rows, :].astype(F32)
        la_b = log_decay(rows, 1)
        bb, tot_b = _dot01_l(tri_u, la_b), _dot01_l(ones_b, la_b)
        qt, kt, dec = q * jnp.exp(bb), k * jnp.exp(tot_b - bb), jnp.exp(tot_b)
        return recurrence(1, rows, qt, kt, dec, range(nb - 1, -1, -1))

    def emit(rows, parts, final):
        for h in range(GLA_HEADS):
            hcols = slice(h * LANE, (h + 1) * LANE)
            if not final:
                oacc_ref[rows, hcols] = parts[h]
            else:
                o = oacc_ref[rows, hcols] + parts[h]
                yn = o * lax.rsqrt(jnp.mean(o * o, axis=-1, keepdims=True) + EPS) * gain_ref[:, hcols]
                y_ref[rows, hcols] = (_silu(r_ref[rows, hcols].astype(F32)) * yn).astype(y_ref.dtype)

    def step(t, final):
        rows_f = pl.ds(pl.multiple_of(t * C, C), C)
        rows_b = pl.ds(pl.multiple_of((nc - 1 - t) * C, C), C)
        parts_b = backward_part(rows_b)
        parts_f = forward_part(rows_f)
        emit(rows_f, parts_f, final)
        emit(rows_b, parts_b, final)

    def first_half(t, carry):
        step(t, False)
        return carry

    def second_half(t, carry):
        step(t, True)
        return carry

    lax.fori_loop(0, nc // 2, first_half, 0)
    lax.fori_loop(nc // 2, nc, second_half, 0)
    if has_state:
        st_ref[...] = hst_ref[...]


def _gla_pack_state(st):
    n = st.shape[0]
    ht = jnp.swapaxes(st, -1, -2)
    z = jnp.zeros_like(ht)
    even = jnp.concatenate([ht, z], axis=-1)
    odd = jnp.concatenate([z, ht], axis=-1)
    sel = (jnp.arange(GLA_HEADS) % 2 == 0)[None, None, :, None, None]
    return jnp.where(sel, even, odd)


def _gla_unpack_state(st):
    even, odd = st[..., :GLA_QK], st[..., GLA_QK:]
    sel = (jnp.arange(GLA_HEADS) % 2 == 0)[None, None, :, None, None]
    return jnp.swapaxes(jnp.where(sel, even, odd), -1, -2)


def _gla_params(gla_w_g2, gla_b_g, gla_norm_g):
    wg = jnp.zeros((DEPTH, 2, LANE, 2 * LANE), F32)
    g1_lane = SIDE_G1 - SIDE_DT
    wg = wg.at[:, :, g1_lane:g1_lane + GLA_GATE_RANK, :].set(gla_w_g2)
    ind = np.zeros((2 * LANE, GROUP_W), np.float32)
    for h in range(GLA_HEADS):
        ind[h * GLA_QK:(h + 1) * GLA_QK, h * GLA_V:(h + 1) * GLA_V] = 1.0
    return dict(wg=wg.astype(BF16), bg=gla_b_g[:, :, None, :], gain=gla_norm_g[:, None, :],
                ind=jnp.asarray(ind, BF16))


def _gla_scan(um, us, gp, layer, h0, n_seq, T, row_block0):
    has_h0 = h0 is not None
    has_state = not has_h0
    C = GLA_MACRO
    assert T % (2 * C) == 0, "the merged forward/backward loop pairs chunk t with chunk nc-1-t"

    def col(off, w):
        return pl.BlockSpec((T, w), lambda b: (row_block0 + b, off // w))

    def per_layer(arr):
        return pl.BlockSpec((None,) + arr.shape[1:], lambda b: (layer,) + (0,) * (arr.ndim - 1))
    in_specs = [col(MAIN_GLA_Q, 2 * LANE), col(MAIN_GLA_K, 2 * LANE), col(MAIN_GLA_V, GROUP_W), col(MAIN_GLA_R, GROUP_W),
                pl.BlockSpec((T, LANE), lambda b: (row_block0 + b, SIDE_DT // LANE)),
                per_layer(gp['wg']), per_layer(gp['bg']), per_layer(gp['gain']),
                pl.BlockSpec(gp['ind'].shape, lambda b: (0, 0))]
    args = [um, um, um, um, us, gp['wg'], gp['bg'], gp['gain'], gp['ind']]
    st_spec = pl.BlockSpec((None, 2, GLA_HEADS, LANE, LANE), lambda b: (b, 0, 0, 0, 0))
    if has_h0:
        in_specs.append(st_spec)
        args.append(h0)
    out_specs = [pl.BlockSpec((T, GROUP_W), lambda b: (b, 0))]
    out_shape = [jax.ShapeDtypeStruct((n_seq * T, GROUP_W), BF16)]
    if has_state:
        out_specs.append(st_spec)
        out_shape.append(jax.ShapeDtypeStruct((n_seq, 2, GLA_HEADS, LANE, LANE), F32))
    res = pl.pallas_call(
        functools.partial(_gla_kernel, T=T, has_h0=has_h0, has_state=has_state),
        grid=(n_seq,),
        in_specs=in_specs,
        out_specs=out_specs,
        out_shape=out_shape,
        scratch_shapes=[pltpu.VMEM((T, GROUP_W), F32), pltpu.VMEM((2, GLA_HEADS, LANE, LANE), F32),
                        pltpu.VMEM((C, 2 * LANE), F32), pltpu.VMEM((C, 2 * LANE), F32),
                        pltpu.VMEM((C, 2 * LANE), F32), pltpu.VMEM((C, 2 * LANE), F32),
                        pltpu.VMEM((C, GROUP_W), F32), pltpu.VMEM((C, GROUP_W), F32)],
        compiler_params=pltpu.CompilerParams(dimension_semantics=("parallel",), vmem_limit_bytes=VMEM_LIMIT),
        name="gla",
    )(*args)
    return (res[0], res[1]) if has_state else (res[0], None)


def kernel(x_prompt, x_sample, cache_mla_kv, state_ssd, state_ret, state_gla, c, c_ctx, w_mod, b_mod, norm_ffn1, ffn1_wg, ffn1_wu, ffn1_wd, norm_mix, w_in, ssd_conv_w, ssd_conv_b, ssd_a_log, ssd_dt_bias, ssd_d, ssd_norm_g, mla_q_lat_gain, mla_w_q_up, mla_q_gain, mla_kv_lat_gain, mla_w_kv_up, mla_k_gain, ret_decay_logit, ret_norm_g, gla_w_g2, gla_b_g, gla_norm_g, w_out, norm_ffn2, ffn2_wg, ffn2_wu, ffn2_wd):
    sp = _ssd_params(ssd_conv_w, ssd_conv_b, ssd_a_log, ssd_dt_bias, ssd_d, ssd_norm_g)
    ssd_h0 = jnp.stack([_ssd_pack_state(state_ssd[:, l]) for l in range(DEPTH)], axis=1)
    gp = _gla_params(gla_w_g2, gla_b_g, gla_norm_g)
    gla_h0 = jnp.stack([_gla_pack_state(state_gla[:, l]) for l in range(DEPTH)], axis=1)

    main_idx, side_idx = _in_proj_columns()
    w_main = _gather_columns(w_in, main_idx).astype(BF16)
    w_side = _gather_columns(w_in, side_idx).astype(BF16)
    w_out_b = w_out.astype(BF16)
    f1 = (ffn1_wg.astype(BF16), ffn1_wu.astype(BF16), ffn1_wd.astype(BF16))
    f2 = (ffn2_wg.astype(BF16), ffn2_wu.astype(BF16), ffn2_wd.astype(BF16))
    g_ffn1 = norm_ffn1.reshape(DEPTH, 1, D_MODEL)
    g_mix = norm_mix.reshape(DEPTH, 1, D_MODEL)
    g_ffn2 = norm_ffn2.reshape(DEPTH, 1, D_MODEL)
    mw = _mla_weights(mla_q_lat_gain, mla_w_q_up, mla_q_gain, mla_kv_lat_gain, mla_w_kv_up, mla_k_gain)
    cos_lat, sin_lat = _rope_tables(DEC_SEQ)
    cos_all = jnp.concatenate([jnp.ones((TOK_TM, LANE), F32), cos_lat], axis=0)
    sin_all = jnp.concatenate([jnp.zeros((TOK_TM, LANE), F32), sin_lat], axis=0)
    ret_lg = jax.nn.log_sigmoid(ret_decay_logit.astype(F32))
    ret_gain = ret_norm_g.reshape(DEPTH, 1, GROUP_W)
    ret_h0 = state_ret.reshape(DEC_BATCH, DEPTH, 2, 2, LANE, LANE)

    c_all = jnp.zeros((MOD_ROWS, D_MODEL), F32).at[0].set(c_ctx).at[1:1 + DEC_BATCH].set(c)
    mod = _modulation(c_all, w_mod, b_mod).reshape(DEPTH * MOD_ROWS, 1, N_MOD * D_MODEL)

    x = jnp.concatenate([x_prompt.reshape(CTX_ROWS, D_MODEL), x_sample.reshape(LAT_ROWS, D_MODEL)], axis=0)
    kv_list, ssd_list, ret_list, gla_list = [], [], [], []
    for l in range(DEPTH):
        x = _ffn(x, mod, g_ffn1, *f1, l, 0)
        um, us = _in_proj(x, mod, g_mix, w_main, w_side, l)

        ym_c, ym_l, kv_lat = _mla_layer(um, us, cache_mla_kv[:, l], mw, cos_all, sin_all, l)
        yr_c, s_ret = _retention(um, ret_lg[l], ret_gain[l], None, None, BATCH, SEQ, 0)
        yr_l, _ = _retention(um, ret_lg[l], ret_gain[l], ret_h0[:, l], (cos_lat, sin_lat),
                             DEC_BATCH, DEC_SEQ, CTX_ROWS // DEC_SEQ)
        ys_c, s_ssd = _ssd_scan(um, us, sp, l, None, BATCH, SEQ, 0)
        ys_l, _ = _ssd_scan(um, us, sp, l, ssd_h0[:, l], DEC_BATCH, DEC_SEQ, CTX_ROWS // DEC_SEQ)
        yg_c, s_gla = _gla_scan(um, us, gp, l, None, BATCH, SEQ, 0)
        yg_l, _ = _gla_scan(um, us, gp, l, gla_h0[:, l], DEC_BATCH, DEC_SEQ, CTX_ROWS // DEC_SEQ)

        kv_list.append(kv_lat[:CTX_ROWS, :MLA_CACHE_W].reshape(BATCH, SEQ, MLA_CACHE_W))
        ssd_list.append(_ssd_unpack_state(s_ssd))
        ret_list.append(s_ret.reshape(BATCH, 2, RET_HEADS, RET_QK, RET_V))
        gla_list.append(_gla_unpack_state(s_gla))
        x = _out_proj(x, [ys_c, ym_c, yr_c, yg_c], [ys_l, ym_l, yr_l, yg_l], mod, w_out_b, l)
        x = _ffn(x, mod, g_ffn2, *f2, l, 6, split_out=(l == DEPTH - 1))
    y_p = x[0].reshape(BATCH, SEQ, D_MODEL)
    y_s = x[1].reshape(DEC_BATCH, DEC_SEQ, D_MODEL)
    return (y_p, y_s, jnp.stack(kv_list, axis=1), jnp.stack(ssd_list, axis=1),
            jnp.stack(ret_list, axis=1), jnp.stack(gla_list, axis=1))
```

```python
import functools

import jax
import jax.numpy as jnp
import numpy as np
from jax import lax
from jax.experimental import pallas as pl
from jax.experimental.pallas import tpu as pltpu

F32 = jnp.float32
BF16 = jnp.bfloat16

D_MODEL = 2048
BATCH = 32
SEQ = 256
DEPTH = 4
DEC_BATCH = 4
DEC_SEQ = 4096
PAST_LEN = 256
GRID_W = 64
ROPE_BASE = 10000.0
EPS = 1e-6
D_FF = 5632
N_MOD = 9
GROUP_W = D_MODEL // 4

SSD_HEAD_DIM = 64
SSD_HEADS = GROUP_W // SSD_HEAD_DIM
SSD_STATE = 64
SSD_GROUPS = 2
SSD_CONV_K = 3
SSD_CHUNK = 128
SSD_CONV_CH = GROUP_W + 2 * SSD_GROUPS * SSD_STATE
SSD_IN = GROUP_W + SSD_CONV_CH + SSD_HEADS

MLA_HEADS = 4
MLA_NOPE = 128
MLA_ROPE = 64
MLA_V = GROUP_W // MLA_HEADS
MLA_Q_RANK = 384
MLA_KV_RANK = 128
MLA_QK = MLA_NOPE + MLA_ROPE
MLA_IN = MLA_Q_RANK + MLA_KV_RANK + MLA_ROPE
MLA_CACHE_W = MLA_KV_RANK + MLA_ROPE

RET_HEADS = 4
RET_QK = 64
RET_V = GROUP_W // RET_HEADS
RET_CHUNK = 128
RET_IN = 2 * RET_HEADS * RET_QK + 2 * GROUP_W

GLA_HEADS = 4
GLA_QK = 64
GLA_V = GROUP_W // GLA_HEADS
GLA_GATE_RANK = 16
GLA_GATE_TEMP = 16.0
GLA_CHUNK = 16
GLA_IN = 2 * GLA_HEADS * GLA_QK + GROUP_W + GLA_GATE_RANK + GROUP_W


LANE = 128
CTX_ROWS = BATCH * SEQ
LAT_ROWS = DEC_BATCH * DEC_SEQ
ROWS = CTX_ROWS + LAT_ROWS
MOD_ROWS = 8
VMEM_LIMIT = 56 * 1024 * 1024
TOK_TM = 512
MLA_HEAD_W = 2 * LANE

MAIN_SSD_Z = 0
MAIN_SSD_XS = 512
MAIN_RET_V = 1024
MAIN_RET_G = 1536
MAIN_GLA_V = 2048
MAIN_GLA_R = 2560
MAIN_SSD_BC = 3072
MAIN_RET_Q = 3328
MAIN_RET_K = 3584
MAIN_GLA_Q = 3840
MAIN_GLA_K = 4096
MAIN_MLA_CQ = 4352
MAIN_W = 4864
MAIN_TN = MAIN_W
SIDE_CKV = 0
SIDE_KPE = SIDE_CKV + MLA_KV_RANK
SIDE_DT = 2 * LANE
SIDE_G1 = SIDE_DT + SSD_HEADS
SIDE_W = 3 * LANE


def _in_proj_columns():
    o_ssd, o_mla, o_ret, o_gla = 0, SSD_IN, SSD_IN + MLA_IN, SSD_IN + MLA_IN + RET_IN
    main = np.full((MAIN_W,), -1, np.int64)

    def put(dst, src, n):
        main[dst:dst + n] = src + np.arange(n)
    qk = RET_HEADS * RET_QK
    put(MAIN_SSD_Z, o_ssd, GROUP_W)
    put(MAIN_SSD_XS, o_ssd + GROUP_W, GROUP_W)
    put(MAIN_SSD_BC, o_ssd + 2 * GROUP_W, 2 * SSD_GROUPS * SSD_STATE)
    put(MAIN_MLA_CQ, o_mla, MLA_Q_RANK)
    put(MAIN_RET_Q, o_ret, qk)
    put(MAIN_RET_K, o_ret + qk, qk)
    put(MAIN_RET_V, o_ret + 2 * qk, GROUP_W)
    put(MAIN_RET_G, o_ret + 2 * qk + GROUP_W, GROUP_W)
    put(MAIN_GLA_Q, o_gla, qk)
    put(MAIN_GLA_K, o_gla + qk, qk)
    put(MAIN_GLA_V, o_gla + 2 * qk, GROUP_W)
    put(MAIN_GLA_R, o_gla + 2 * qk + GROUP_W + GLA_GATE_RANK, GROUP_W)
    side = np.full((SIDE_W,), -1, np.int64)
    side[SIDE_CKV:SIDE_CKV + MLA_KV_RANK + MLA_ROPE] = o_mla + MLA_Q_RANK + np.arange(MLA_KV_RANK + MLA_ROPE)
    side[SIDE_DT:SIDE_DT + SSD_HEADS] = o_ssd + GROUP_W + SSD_CONV_CH + np.arange(SSD_HEADS)
    side[SIDE_G1:SIDE_G1 + GLA_GATE_RANK] = o_gla + 2 * qk + GROUP_W + np.arange(GLA_GATE_RANK)
    return main, side


def _gather_columns(w, idx):
    pieces, i, n = [], 0, len(idx)
    while i < n:
        j = i + 1
        if idx[i] < 0:
            while j < n and idx[j] < 0:
                j += 1
            pieces.append(jnp.zeros(w.shape[:-1] + (j - i,), w.dtype))
        else:
            while j < n and idx[j] == idx[j - 1] + 1:
                j += 1
            pieces.append(w[..., int(idx[i]):int(idx[i]) + (j - i)])
        i = j
    return jnp.concatenate(pieces, axis=-1)


def _mod_row(i, tm):
    ctx_tiles = CTX_ROWS // tm
    per_seq = DEC_SEQ // tm
    return jnp.where(i < ctx_tiles, 0, 1 + (i - ctx_tiles) // per_seq)


def _mod_spec(layer, which, tm, n_grid):
    if n_grid == 1:
        return pl.BlockSpec((1, 1, D_MODEL), lambda i: (layer * MOD_ROWS + _mod_row(i, tm), 0, which))
    return pl.BlockSpec((1, 1, D_MODEL), lambda i, j: (layer * MOD_ROWS + _mod_row(i, tm), 0, which))


def _dot(a, b):
    return jnp.dot(a, b, preferred_element_type=F32)


def _dot_nt(a, b):
    return lax.dot_general(a, b, (((1,), (1,)), ((), ())), preferred_element_type=F32)


def _dot_tn(a, b):
    return lax.dot_general(a, b, (((0,), (0,)), ((), ())), preferred_element_type=F32)


def _silu(x):
    return x * jax.nn.sigmoid(x)


def _mod_kernel(c_ref, w_ref, b_ref, o_ref):
    s = _silu(c_ref[...])
    hi = s.astype(BF16)
    lo = (s - hi.astype(F32)).astype(BF16)
    w = w_ref[...].astype(BF16)
    o_ref[...] = _dot(hi, w) + _dot(lo, w) + b_ref[...]


def _modulation(c_all, w_mod, b_mod):
    tn = 1024
    n = N_MOD * D_MODEL
    return pl.pallas_call(
        _mod_kernel,
        grid=(DEPTH, n // tn),
        in_specs=[pl.BlockSpec((MOD_ROWS, D_MODEL), lambda l, j: (0, 0)),
                  pl.BlockSpec((None, D_MODEL, tn), lambda l, j: (l, 0, j)),
                  pl.BlockSpec((None, 1, tn), lambda l, j: (l, 0, j))],
        out_specs=pl.BlockSpec((None, MOD_ROWS, tn), lambda l, j: (l, 0, j)),
        out_shape=jax.ShapeDtypeStruct((DEPTH, MOD_ROWS, n), F32),
        compiler_params=pltpu.CompilerParams(dimension_semantics=("arbitrary", "arbitrary"),
                                             vmem_limit_bytes=VMEM_LIMIT),
        name="modulation",
    )(c_all, w_mod, b_mod.reshape(DEPTH, 1, n))


NORM_ROWS = 16


def _norm_modulate(x_ref, g_ref, shift_ref, scale_ref, h_ref):
    gs = g_ref[...] * (1.0 + scale_ref[0])
    shift = shift_ref[0]

    def body(r, carry):
        rows = pl.ds(pl.multiple_of(r * NORM_ROWS, NORM_ROWS), NORM_ROWS)
        x = x_ref[rows, :]
        inv = lax.rsqrt(jnp.mean(x * x, axis=-1, keepdims=True) + EPS)
        h_ref[rows, :] = ((x * inv) * gs + shift).astype(BF16)
        return carry

    lax.fori_loop(0, x_ref.shape[0] // NORM_ROWS, body, 0, unroll=8)


FFN_TM = 1024
FFN_VMEM_LIMIT = 60 * 1024 * 1024
FFN_TF = 512
FFN_TN = 256
FFN_NF = D_FF // FFN_TF
FFN_NN = D_MODEL // FFN_TN
MXU_W = 256


def _ffn_kernel(x_ref, xt_ref, g_ref, shift_ref, scale_ref, gate_ref, wg_ref, wu_ref, wd_ref, *refs,
                ctx_tiles, has_alias):
    if has_alias:
        refs = refs[1:]
    outs, (h_ref, a_ref) = refs[:-2], refs[-2:]
    j = pl.program_id(1)
    is_ctx = pl.program_id(0) < ctx_tiles

    @pl.when(j == 0)
    def _():
        _norm_modulate(x_ref, g_ref, shift_ref, scale_ref, h_ref)

    @pl.when(j < FFN_NF)
    def _():
        h = h_ref[...]
        base = pl.multiple_of(j * FFN_TF, FFN_TF)
        for s in range(FFN_TF // MXU_W):
            cols = slice(s * MXU_W, (s + 1) * MXU_W)
            g = _dot(h, wg_ref[:, cols])
            u = _dot(h, wu_ref[:, cols])
            a_ref[:, pl.ds(base + s * MXU_W, MXU_W)] = (_silu(g) * u).astype(BF16)

    @pl.when(j >= FFN_NF)
    def _():
        y = xt_ref[...] + 0.5 * gate_ref[0] * _dot(a_ref[...], wd_ref[...])
        if len(outs) == 1:
            outs[0][...] = y
        else:
            @pl.when(is_ctx)
            def _():
                outs[0][...] = y

            @pl.when(jnp.logical_not(is_ctx))
            def _():
                outs[1][...] = y


def _ffn(x, mod, norm_g, wg, wu, wd, layer, mod_base, tm=FFN_TM, split_out=False, tile0=0, into=None):
    n_out = 2 if split_out else 1
    ctx_tiles = CTX_ROWS // tm
    n_tiles = x.shape[0] // tm

    def up_tile(i, j):
        return (layer, 0, jnp.minimum(j, FFN_NF - 1))

    def down_tile(j):
        return jnp.maximum(j - FFN_NF, 0)

    def row_of(k, n_groups, i):
        if n_groups == 1:
            return i + tile0
        return jnp.minimum(i, ctx_tiles - 1) if k == 0 else jnp.maximum(i - ctx_tiles, 0)

    def mod_vec(which):
        return pl.BlockSpec((1, 1, D_MODEL), lambda i, j: (layer * MOD_ROWS + _mod_row(i + tile0, tm), 0, which))

    def col_of(k, n_groups, i, j):
        if n_groups == 1:
            return down_tile(j)
        if k == 0:
            return jnp.where(i < ctx_tiles, down_tile(j), FFN_NN - 1)
        return jnp.where(i >= ctx_tiles, down_tile(j), 0)
    gate_blocks = D_MODEL // FFN_TN
    in_specs = [pl.BlockSpec((tm, D_MODEL), lambda i, j: (i, 0)),
                pl.BlockSpec((tm, FFN_TN), lambda i, j: (i, down_tile(j))),
                pl.BlockSpec((None, 1, D_MODEL), lambda i, j: (layer, 0, 0)),
                mod_vec(mod_base + 0),
                mod_vec(mod_base + 1),
                pl.BlockSpec((1, 1, FFN_TN), lambda i, j: (layer * MOD_ROWS + _mod_row(i + tile0, tm), 0,
                                                           (mod_base + 2) * gate_blocks + down_tile(j))),
                pl.BlockSpec((None, D_MODEL, FFN_TF), up_tile),
                pl.BlockSpec((None, D_MODEL, FFN_TF), up_tile),
                pl.BlockSpec((None, D_FF, FFN_TN), lambda i, j: (layer, 0, down_tile(j)))]
    args = [x, x, norm_g, mod, mod, mod, wg, wu, wd]
    aliases = {}
    if into is not None:
        in_specs.append(pl.BlockSpec(memory_space=pl.ANY))
        args.append(into)
        aliases = {len(args) - 1: 0}
    out_rows = (CTX_ROWS, LAT_ROWS) if split_out else (ROWS,)
    res = pl.pallas_call(
        functools.partial(_ffn_kernel, ctx_tiles=ctx_tiles, has_alias=into is not None),
        grid=(n_tiles, FFN_NF + FFN_NN),
        in_specs=in_specs,
        input_output_aliases=aliases,
        out_specs=[pl.BlockSpec((tm, FFN_TN), lambda i, j, k=k: (row_of(k, n_out, i), col_of(k, n_out, i, j)))
                   for k in range(n_out)],
        out_shape=[jax.ShapeDtypeStruct((r, D_MODEL), F32) for r in out_rows],
        scratch_shapes=[pltpu.VMEM((tm, D_MODEL), BF16), pltpu.VMEM((tm, D_FF), BF16)],
        compiler_params=pltpu.CompilerParams(dimension_semantics=("parallel", "arbitrary"),
                                             vmem_limit_bytes=FFN_VMEM_LIMIT),
        name="ffn",
    )(*args)
    return res if split_out else res[0]


def _in_proj_kernel(x_ref, g_ref, shift_ref, scale_ref, w_ref, ws_ref, o_ref, os_ref, h_ref):
    @pl.when(pl.program_id(1) == 0)
    def _():
        x = x_ref[...]
        inv = lax.rsqrt(jnp.mean(x * x, axis=-1, keepdims=True) + EPS)
        h_ref[...] = ((x * inv) * (g_ref[...] * (1.0 + scale_ref[0])) + shift_ref[0]).astype(BF16)
        os_ref[...] = _dot(h_ref[...], ws_ref[...])

    o_ref[...] = _dot(h_ref[...], w_ref[...]).astype(o_ref.dtype)


def _in_proj(x, mod, norm_g, w_main, w_side, layer, tm=TOK_TM):
    return pl.pallas_call(
        _in_proj_kernel,
        grid=(ROWS // tm, MAIN_W // MAIN_TN),
        in_specs=[pl.BlockSpec((tm, D_MODEL), lambda i, j: (i, 0)),
                  pl.BlockSpec((None, 1, D_MODEL), lambda i, j: (layer, 0, 0)),
                  _mod_spec(layer, 3, tm, 2),
                  _mod_spec(layer, 4, tm, 2),
                  pl.BlockSpec((None, D_MODEL, MAIN_TN), lambda i, j: (layer, 0, j), pipeline_mode=pl.Buffered(1)),
                  pl.BlockSpec((None, D_MODEL, SIDE_W), lambda i, j: (layer, 0, 0), pipeline_mode=pl.Buffered(1))],
        out_specs=[pl.BlockSpec((tm, MAIN_TN), lambda i, j: (i, j)),
                   pl.BlockSpec((tm, SIDE_W), lambda i, j: (i, 0))],
        out_shape=[jax.ShapeDtypeStruct((ROWS, MAIN_W), BF16),
                   jax.ShapeDtypeStruct((ROWS, SIDE_W), F32)],
        scratch_shapes=[pltpu.VMEM((tm, D_MODEL), BF16)],
        compiler_params=pltpu.CompilerParams(dimension_semantics=("parallel", "arbitrary"),
                                             vmem_limit_bytes=VMEM_LIMIT),
        name="in_proj",
    )(x, norm_g, mod, mod, w_main, w_side)


def _out_proj_kernel(x_ref, *refs, ctx_tiles):
    ctx_refs, lat_refs, (gate_ref, w_ref, o_ref) = refs[0:4], refs[4:8], refs[8:]

    def run(y_refs):
        acc = _dot(y_refs[0][...], w_ref[0:GROUP_W, :])
        for g in range(1, 4):
            acc += _dot(y_refs[g][...], w_ref[g * GROUP_W:(g + 1) * GROUP_W, :])
        o_ref[...] = x_ref[...] + gate_ref[0] * acc

    @pl.when(pl.program_id(0) < ctx_tiles)
    def _():
        run(ctx_refs)

    @pl.when(pl.program_id(0) >= ctx_tiles)
    def _():
        run(lat_refs)


def _out_proj(x, ys_ctx, ys_lat, mod, w, layer, tm=TOK_TM):
    ctx_tiles = CTX_ROWS // tm
    row = pl.BlockSpec((tm, D_MODEL), lambda i: (i, 0))
    cspec = pl.BlockSpec((tm, GROUP_W), lambda i: (jnp.minimum(i, ctx_tiles - 1), 0))
    lspec = pl.BlockSpec((tm, GROUP_W), lambda i: (jnp.maximum(i - ctx_tiles, 0), 0))
    return pl.pallas_call(
        functools.partial(_out_proj_kernel, ctx_tiles=ctx_tiles),
        grid=(ROWS // tm,),
        in_specs=[row] + [cspec] * 4 + [lspec] * 4 + [
            _mod_spec(layer, 5, tm, 1),
            pl.BlockSpec((None, D_MODEL, D_MODEL), lambda i: (layer, 0, 0))],
        out_specs=row,
        out_shape=jax.ShapeDtypeStruct((ROWS, D_MODEL), F32),
        compiler_params=pltpu.CompilerParams(dimension_semantics=("parallel",),
                                             vmem_limit_bytes=VMEM_LIMIT),
        name="out_proj",
    )(x, *ys_ctx, *ys_lat, mod, w)


def _rope_tables(T):
    n_rows = T // GRID_W
    row = jnp.repeat(jnp.arange(n_rows, dtype=F32), GRID_W)
    col = jnp.tile(jnp.arange(GRID_W, dtype=F32), n_rows)
    d_axis = MLA_ROPE // 2
    inv = ROPE_BASE ** (-jnp.arange(0, d_axis, 2, dtype=F32) / d_axis)
    ar, ac = row[:, None] * inv, col[:, None] * inv
    cos = jnp.concatenate([jnp.cos(ar), jnp.cos(ar), jnp.cos(ac), jnp.cos(ac)], axis=-1)
    sin = jnp.concatenate([-jnp.sin(ar), jnp.sin(ar), -jnp.sin(ac), jnp.sin(ac)], axis=-1)
    return jnp.tile(cos, (1, 2)), jnp.tile(sin, (1, 2))


def _swap16(x):
    lane = lax.broadcasted_iota(jnp.int32, x.shape, 1)
    up = pltpu.roll(x, LANE - 16, 1)
    down = pltpu.roll(x, 16, 1)
    return jnp.where((lane % 32) < 16, up, down)


def _rope(x, cos, sin):
    return x * cos + _swap16(x) * sin


def _mla_q_kernel(c0_ref, c1_ref, c2_ref, gl_ref, w_ref, gq_ref, cos_ref, sin_ref, o_ref):
    cs = [r[...].astype(F32) for r in (c0_ref, c1_ref, c2_ref)]
    ss = sum(jnp.sum(c * c, axis=-1, keepdims=True) for c in cs)
    r = lax.rsqrt(ss / MLA_Q_RANK + EPS)
    q = sum(_dot((cs[i] * r * gl_ref[:, i * LANE:(i + 1) * LANE]).astype(BF16),
                 w_ref[i * LANE:(i + 1) * LANE, :]) for i in range(3))
    cos, sin = cos_ref[...], sin_ref[...]
    scale = MLA_QK ** -0.5
    for h in range(MLA_HEADS):
        a = q[:, h * MLA_HEAD_W:h * MLA_HEAD_W + LANE]
        b = q[:, h * MLA_HEAD_W + LANE:(h + 1) * MLA_HEAD_W]
        ssq = jnp.sum(a * a, axis=-1, keepdims=True) + jnp.sum(b * b, axis=-1, keepdims=True)
        rh = lax.rsqrt(ssq / MLA_QK + EPS) * scale
        o_ref[:, h * MLA_HEAD_W:h * MLA_HEAD_W + LANE] = (a * rh * gq_ref[:, :LANE]).astype(BF16)
        o_ref[:, h * MLA_HEAD_W + LANE:(h + 1) * MLA_HEAD_W] = _rope(b * rh * gq_ref[:, LANE:], cos, sin).astype(BF16)


def _rope_tile_index(i, tm):
    ctx_tiles = CTX_ROWS // tm
    return jnp.where(i < ctx_tiles, 0, 1 + (i - ctx_tiles) % (DEC_SEQ // tm))


def _mla_q(um, gl, wq, gq, cos_all, sin_all, layer, tm=TOK_TM):
    cq = MAIN_MLA_CQ // LANE
    tab = pl.BlockSpec((tm, LANE), lambda i: (_rope_tile_index(i, tm), 0))
    return pl.pallas_call(
        _mla_q_kernel,
        grid=(ROWS // tm,),
        in_specs=[pl.BlockSpec((tm, LANE), lambda i: (i, cq)),
                  pl.BlockSpec((tm, LANE), lambda i: (i, cq + 1)),
                  pl.BlockSpec((tm, LANE), lambda i: (i, cq + 2)),
                  pl.BlockSpec((None, 1, MLA_Q_RANK), lambda i: (layer, 0, 0)),
                  pl.BlockSpec((None, MLA_Q_RANK, MLA_HEADS * MLA_HEAD_W), lambda i: (layer, 0, 0)),
                  pl.BlockSpec((None, 1, MLA_HEAD_W), lambda i: (layer, 0, 0)),
                  tab, tab],
        out_specs=pl.BlockSpec((tm, MLA_HEADS * MLA_HEAD_W), lambda i: (i, 0)),
        out_shape=jax.ShapeDtypeStruct((ROWS, MLA_HEADS * MLA_HEAD_W), BF16),
        compiler_params=pltpu.CompilerParams(dimension_semantics=("parallel",), vmem_limit_bytes=VMEM_LIMIT),
        name="mla_q",
    )(um, um, um, gl, wq, gq, cos_all, sin_all)


def _mla_kv_kernel(s_ref, gl_ref, w_ref, gk_ref, cos_ref, sin_ref, lat_ref, k_ref, v_ref, *, normalize):
    ckv = s_ref[:, :LANE]
    kpe = s_ref[:, LANE:]
    if normalize:
        ckv = ckv * lax.rsqrt(jnp.mean(ckv * ckv, axis=-1, keepdims=True) + EPS) * gl_ref[...]
    lat_ref[:, :LANE] = ckv
    lat_ref[:, LANE:] = kpe
    kv = _dot(ckv.astype(BF16), w_ref[...])
    ss_pe = jnp.sum(kpe * kpe, axis=-1, keepdims=True)
    cos, sin = cos_ref[...], sin_ref[...]
    for h in range(MLA_HEADS):
        a = kv[:, h * LANE:(h + 1) * LANE]
        rh = lax.rsqrt((jnp.sum(a * a, axis=-1, keepdims=True) + ss_pe) / MLA_QK + EPS)
        k_ref[:, h * MLA_HEAD_W:h * MLA_HEAD_W + LANE] = (a * rh * gk_ref[:, :LANE]).astype(BF16)
        k_ref[:, h * MLA_HEAD_W + LANE:(h + 1) * MLA_HEAD_W] = _rope(kpe * rh * gk_ref[:, LANE:], cos, sin).astype(BF16)
    v_ref[...] = kv[:, MLA_HEADS * LANE:].astype(BF16)


def _mla_kv(src, gl, wkv, gk, cos_all, sin_all, layer, *, normalize, tab_index, tm=TOK_TM):
    rows = src.shape[0]
    tab = pl.BlockSpec((tm, LANE), lambda i: (tab_index(i, tm), 0))
    return pl.pallas_call(
        functools.partial(_mla_kv_kernel, normalize=normalize),
        grid=(rows // tm,),
        in_specs=[pl.BlockSpec((tm, 2 * LANE), lambda i: (i, 0)),
                  pl.BlockSpec((None, 1, MLA_KV_RANK), lambda i: (layer, 0, 0)),
                  pl.BlockSpec((None, MLA_KV_RANK, 2 * MLA_HEADS * LANE), lambda i: (layer, 0, 0)),
                  pl.BlockSpec((None, 1, MLA_HEAD_W), lambda i: (layer, 0, 0)),
                  tab, tab],
        out_specs=[pl.BlockSpec((tm, 2 * LANE), lambda i: (i, 0)),
                   pl.BlockSpec((tm, MLA_HEADS * MLA_HEAD_W), lambda i: (i, 0)),
                   pl.BlockSpec((tm, GROUP_W), lambda i: (i, 0))],
        out_shape=[jax.ShapeDtypeStruct((rows, 2 * LANE), F32),
                   jax.ShapeDtypeStruct((rows, MLA_HEADS * MLA_HEAD_W), BF16),
                   jax.ShapeDtypeStruct((rows, GROUP_W), BF16)],
        compiler_params=pltpu.CompilerParams(dimension_semantics=("parallel",), vmem_limit_bytes=VMEM_LIMIT),
        name="mla_kv",
    )(src, gl, wkv, gk, cos_all, sin_all)


ATTN_TQ_LAT = 256
ATTN_HEADS_PER_STEP = 4


def _attn_kernel(q_ref, *refs):
    o_ref = refs[-1]
    kv = [(refs[i], refs[i + 1]) for i in range(0, len(refs) - 1, 2)]
    heads = range(ATTN_HEADS_PER_STEP)
    scores = [[_dot_nt(q_ref[:, h * MLA_HEAD_W:(h + 1) * MLA_HEAD_W], k_ref[:, h * MLA_HEAD_W:(h + 1) * MLA_HEAD_W])
               for k_ref, _ in kv] for h in heads]
    for h in heads:
        m = functools.reduce(jnp.maximum, [jnp.max(s, axis=-1, keepdims=True) for s in scores[h]])
        ps = [jnp.exp(s - m) for s in scores[h]]
        l = sum(jnp.sum(p, axis=-1, keepdims=True) for p in ps)
        o = sum(_dot(p.astype(BF16), v_ref[:, h * MLA_V:(h + 1) * MLA_V]) for p, (_, v_ref) in zip(ps, kv))
        o_ref[:, h * MLA_V:(h + 1) * MLA_V] = (o / l).astype(o_ref.dtype)


def _attention(q, sources, n_seq, t, tq, q_row0):
    nq = t // tq
    hs = ATTN_HEADS_PER_STEP
    in_specs = [pl.BlockSpec((tq, hs * MLA_HEAD_W), lambda b, h, i: (q_row0 // tq + b * nq + i, h))]
    args = [q]
    for k, v, s_len, row0 in sources:
        in_specs += [pl.BlockSpec((s_len, hs * MLA_HEAD_W), lambda b, h, i, o=row0 // s_len: (o + b, h)),
                     pl.BlockSpec((s_len, hs * MLA_V), lambda b, h, i, o=row0 // s_len: (o + b, h))]
        args += [k, v]
    return pl.pallas_call(
        _attn_kernel,
        grid=(n_seq, MLA_HEADS // hs, nq),
        in_specs=in_specs,
        out_specs=pl.BlockSpec((tq, hs * MLA_V), lambda b, h, i: (b * nq + i, h)),
        out_shape=jax.ShapeDtypeStruct((n_seq * t, GROUP_W), BF16),
        compiler_params=pltpu.CompilerParams(dimension_semantics=("parallel", "parallel", "arbitrary"),
                                             vmem_limit_bytes=VMEM_LIMIT),
        name="mla_attention",
    )(*args)


def _mla_weights(mla_q_lat_gain, mla_w_q_up, mla_q_gain, mla_kv_lat_gain, mla_w_kv_up, mla_k_gain):
    qcol = np.full((MLA_HEADS * MLA_HEAD_W,), -1, np.int64)
    kvcol = np.zeros((2 * MLA_HEADS * LANE,), np.int64)
    for h in range(MLA_HEADS):
        qcol[h * MLA_HEAD_W:h * MLA_HEAD_W + MLA_QK] = h * MLA_QK + np.arange(MLA_QK)
        kvcol[h * LANE:(h + 1) * LANE] = h * (MLA_NOPE + MLA_V) + np.arange(MLA_NOPE)
        kvcol[(MLA_HEADS + h) * LANE:(MLA_HEADS + h + 1) * LANE] = h * (MLA_NOPE + MLA_V) + MLA_NOPE + np.arange(MLA_V)
    pad = jnp.zeros((DEPTH, MLA_HEAD_W - MLA_QK), F32)
    return dict(
        gl=mla_q_lat_gain.reshape(DEPTH, 1, MLA_Q_RANK),
        wq=_gather_columns(mla_w_q_up, qcol).astype(BF16),
        gq=jnp.concatenate([mla_q_gain, pad], axis=-1).reshape(DEPTH, 1, MLA_HEAD_W),
        gkv=mla_kv_lat_gain.reshape(DEPTH, 1, MLA_KV_RANK),
        wkv=_gather_columns(mla_w_kv_up, kvcol).astype(BF16),
        gk=jnp.concatenate([mla_k_gain, pad], axis=-1).reshape(DEPTH, 1, MLA_HEAD_W))


def _mla_layer(um, us, cache_l, mw, cos_all, sin_all, layer):
    q = _mla_q(um, mw['gl'], mw['wq'], mw['gq'], cos_all, sin_all, layer)
    kv_lat, k, v = _mla_kv(us, mw['gkv'], mw['wkv'], mw['gk'], cos_all, sin_all, layer,
                           normalize=True, tab_index=_rope_tile_index)
    cache2 = jnp.pad(cache_l.reshape(DEC_BATCH * PAST_LEN, MLA_CACHE_W), ((0, 0), (0, 2 * LANE - MLA_CACHE_W)))
    _, k_c, v_c = _mla_kv(cache2, mw['gkv'], mw['wkv'], mw['gk'], cos_all, sin_all, layer,
                          normalize=False, tab_index=lambda i, tm: 0)
    y_ctx = _attention(q, [(k, v, SEQ, 0)], BATCH, SEQ, SEQ, 0)
    y_lat = _attention(q, [(k, v, DEC_SEQ, CTX_ROWS), (k_c, v_c, PAST_LEN, 0)], DEC_BATCH, DEC_SEQ, ATTN_TQ_LAT,
                       CTX_ROWS)
    return y_ctx, y_lat, kv_lat


def _ret_kernel(*refs, T, C, rope, has_h0, has_state):
    it = iter(refs)
    q_ref, k_ref, v_ref, g_ref = next(it), next(it), next(it), next(it)
    cos_ref, sin_ref = (next(it), next(it)) if rope else (None, None)
    lg_ref, gain_ref = next(it), next(it)
    h0_ref = next(it) if has_h0 else None
    y_ref = next(it)
    st_ref = next(it) if has_state else None
    oacc_ref, dm_ref, tab_ref, hst_ref = (next(it) for _ in range(4))

    nc = T // C
    ii = lax.broadcasted_iota(jnp.int32, (C, C), 0)
    jj = lax.broadcasted_iota(jnp.int32, (C, C), 1)
    dif = (ii - jj).astype(F32)
    lane = lax.broadcasted_iota(jnp.int32, (C, LANE), 1)
    rowi = lax.broadcasted_iota(jnp.int32, (C, LANE), 0).astype(F32)
    m_lo = lane < RET_QK
    hrow = lax.broadcasted_iota(jnp.int32, (LANE, LANE), 0) < RET_QK

    for h in range(RET_HEADS):
        lf, lb = lg_ref[0, h], lg_ref[1, h]
        dm_ref[h] = (jnp.where(dif >= 0, jnp.exp(lf * jnp.maximum(dif, 0.0)), 0.0)
                     + jnp.where(dif <= 0, jnp.exp(lb * jnp.maximum(-dif, 0.0)), 0.0))
    for p in range(2):
        lf = jnp.where(m_lo, lg_ref[0, 2 * p], lg_ref[0, 2 * p + 1])
        lb = jnp.where(m_lo, lg_ref[1, 2 * p], lg_ref[1, 2 * p + 1])
        tab_ref[p, 0] = jnp.exp(lf * (rowi + 1.0))
        tab_ref[p, 1] = jnp.exp(lf * (C - 1.0 - rowi))
        tab_ref[p, 2] = jnp.exp(lb * (C - rowi))
        tab_ref[p, 3] = jnp.exp(lb * rowi)
    if has_h0:
        hst_ref[...] = h0_ref[...]
    else:
        hst_ref[...] = jnp.zeros_like(hst_ref)

    def chunk_decay(d, p):
        return jnp.exp(jnp.where(hrow, lg_ref[d, 2 * p], lg_ref[d, 2 * p + 1]) * float(C))

    def load_qk(rows, p):
        cols = slice(p * LANE, (p + 1) * LANE)
        qp = q_ref[rows, cols].astype(F32)
        kp = k_ref[rows, cols].astype(F32) * (RET_QK ** -0.5)
        if rope:
            qp = _rope(qp, cos_ref[rows, :], sin_ref[rows, :])
            kp = _rope(kp, cos_ref[rows, :], sin_ref[rows, :])
        return qp, kp

    def scan_part(rows, d):
        outs = []
        for p in range(2):
            qp, kp = load_qk(rows, p)
            kpb = kp.astype(BF16)
            hp = hst_ref[d, p]
            hpb = hp.astype(BF16)
            upd = jnp.zeros((LANE, LANE), F32)
            for e in range(2):
                h = 2 * p + e
                mh = m_lo if e == 0 else jnp.logical_not(m_lo)
                vh = v_ref[rows, h * LANE:(h + 1) * LANE]
                qm = jnp.where(mh, qp, 0.0)
                o = _dot((qm * tab_ref[p, 2 * d]).astype(BF16), hpb)
                if d == 0:
                    o += _dot((_dot_nt(qm.astype(BF16), kpb) * dm_ref[h]).astype(BF16), vh)
                outs.append(o)
                upd += _dot_tn(jnp.where(mh, kp * tab_ref[p, 2 * d + 1], 0.0).astype(BF16), vh)
            hst_ref[d, p] = hp * chunk_decay(d, p) + upd
        return outs

    def emit(rows, parts, final):
        for h in range(RET_HEADS):
            hcols = slice(h * LANE, (h + 1) * LANE)
            if not final:
                oacc_ref[rows, hcols] = parts[h]
            else:
                o = oacc_ref[rows, hcols] + parts[h]
                oc = o - jnp.mean(o, axis=-1, keepdims=True)
                yn = oc * lax.rsqrt(jnp.mean(oc * oc, axis=-1, keepdims=True) + EPS) * gain_ref[:, hcols]
                y_ref[rows, hcols] = (_silu(g_ref[rows, hcols].astype(F32)) * yn).astype(y_ref.dtype)

    def step(t, final):
        rows_f = pl.ds(pl.multiple_of(t * C, C), C)
        rows_b = pl.ds(pl.multiple_of((nc - 1 - t) * C, C), C)
        parts_b = scan_part(rows_b, 1)
        parts_f = scan_part(rows_f, 0)
        emit(rows_f, parts_f, final)
        emit(rows_b, parts_b, final)

    def first_half(t, carry):
        step(t, False)
        return carry

    def second_half(t, carry):
        step(t, True)
        return carry

    lax.fori_loop(0, nc // 2, first_half, 0)
    lax.fori_loop(nc // 2, nc, second_half, 0)
    if has_state:
        st_ref[...] = hst_ref[...]


def _retention(um, lg, gain, h0, rope_tabs, n_seq, T, row_block0):
    C = RET_CHUNK
    assert T % (2 * C) == 0, "the merged forward/backward loop pairs chunk t with chunk nc-1-t"
    rope = rope_tabs is not None
    has_h0 = h0 is not None
    has_state = not has_h0

    def col(off, w):
        return pl.BlockSpec((T, w), lambda b: (row_block0 + b, off // w))
    in_specs = [col(MAIN_RET_Q, 2 * LANE), col(MAIN_RET_K, 2 * LANE), col(MAIN_RET_V, GROUP_W), col(MAIN_RET_G, GROUP_W)]
    args = [um, um, um, um]
    if rope:
        in_specs += [pl.BlockSpec((T, LANE), lambda b: (0, 0))] * 2
        args += list(rope_tabs)
    in_specs += [pl.BlockSpec(memory_space=pltpu.SMEM), pl.BlockSpec((1, GROUP_W), lambda b: (0, 0))]
    args += [lg, gain]
    st_spec = pl.BlockSpec((None, 2, 2, LANE, LANE), lambda b: (b, 0, 0, 0, 0))
    if has_h0:
        in_specs.append(st_spec)
        args.append(h0)
    out_specs = [pl.BlockSpec((T, GROUP_W), lambda b: (b, 0))]
    out_shape = [jax.ShapeDtypeStruct((n_seq * T, GROUP_W), BF16)]
    if has_state:
        out_specs.append(st_spec)
        out_shape.append(jax.ShapeDtypeStruct((n_seq, 2, 2, LANE, LANE), F32))
    res = pl.pallas_call(
        functools.partial(_ret_kernel, T=T, C=C, rope=rope, has_h0=has_h0, has_state=has_state),
        grid=(n_seq,),
        in_specs=in_specs,
        out_specs=out_specs,
        out_shape=out_shape,
        scratch_shapes=[pltpu.VMEM((T, GROUP_W), F32), pltpu.VMEM((RET_HEADS, C, C), F32),
                        pltpu.VMEM((2, 4, C, LANE), F32), pltpu.VMEM((2, 2, LANE, LANE), F32)],
        compiler_params=pltpu.CompilerParams(dimension_semantics=("parallel",), vmem_limit_bytes=VMEM_LIMIT),
        name="retention",
    )(*args)
    return (res[0], res[1]) if has_state else (res[0], None)


def _split3(x):
    x1 = x.astype(BF16)
    r1 = x - x1.astype(F32)
    x2 = r1.astype(BF16)
    x3 = (r1 - x2.astype(F32)).astype(BF16)
    return x1, x2, x3


def _dot01_l(m, x):
    x1, x2, x3 = _split3(x)
    return _dot(m, x1) + _dot(m, x2) + _dot(m, x3)


def _dot01_r(x, m):
    x1, x2, x3 = _split3(x)
    return _dot(x1, m) + _dot(x2, m) + _dot(x3, m)


def _softplus(x):
    return jnp.maximum(x, 0.0) + jnp.log(1.0 + jnp.exp(-jnp.abs(x)))


def _head_expander():
    e64 = np.zeros((LANE, GROUP_W), np.float32)
    for h in range(SSD_HEADS):
        e64[h, h * SSD_HEAD_DIM:(h + 1) * SSD_HEAD_DIM] = 1.0
    return jnp.asarray(e64, BF16)


def _ssd_kernel(*refs, T, has_h0, has_state):
    it = iter(refs)
    z_ref, xs_ref, bc_ref, dt_ref = next(it), next(it), next(it), next(it)
    wx_ref, wbc_ref, bx_ref, bbc_ref = next(it), next(it), next(it), next(it)
    a_ref, dtb_ref, d_ref, gain_ref, e64_ref = (next(it) for _ in range(5))
    h0_ref = next(it) if has_h0 else None
    y_ref = next(it)
    st_ref = next(it) if has_state else None
    xc_s, bcs_s, oacc_ref, hst_ref = next(it), next(it), next(it), next(it)

    C = SSD_CHUNK
    nc = T // C
    ii = lax.broadcasted_iota(jnp.int32, (C, C), 0)
    jj = lax.broadcasted_iota(jnp.int32, (C, C), 1)
    tril, triu = ii >= jj, ii <= jj
    tril_b, triu_b = tril.astype(BF16), triu.astype(BF16)
    lane = lax.broadcasted_iota(jnp.int32, (C, LANE), 1)
    m_lo = lane < SSD_STATE
    masks = (m_lo, jnp.logical_not(m_lo))
    hrow = lax.broadcasted_iota(jnp.int32, (LANE, LANE), 0) < SSD_STATE

    if has_h0:
        hst_ref[...] = h0_ref[...]
    else:
        hst_ref[...] = jnp.zeros_like(hst_ref)

    def decays(rows, d):
        dtv = _softplus(dt_ref[rows, :] + dtb_ref[d:d + 1, :])
        la = dtv * a_ref[d:d + 1, :]
        return dtv, _dot01_l(tril_b if d == 0 else triu_b, la)

    def conv(ref, w_ref, b_ref, c, r0):
        width = ref.shape[1]
        x = ref[pl.ds(r0, C), :].astype(F32)
        p0 = pl.multiple_of(jnp.maximum(r0 - 16, 0), 16)
        n0 = pl.multiple_of(jnp.minimum(r0 + C, T - 16), 16)
        prev_row = jnp.where(c > 0, ref[pl.ds(p0, 16), :].astype(F32)[15:16], 0.0)
        next_row = jnp.where(c < nc - 1, ref[pl.ds(n0, 16), :].astype(F32)[0:1], 0.0)
        rowi = lax.broadcasted_iota(jnp.int32, (C, width), 0)
        prev = jnp.where(rowi == 0, prev_row, pltpu.roll(x, 1, 0))
        nxt = jnp.where(rowi == C - 1, next_row, pltpu.roll(x, C - 1, 0))
        return _silu(prev * w_ref[0:1, :] + x * w_ref[1:2, :] + nxt * w_ref[2:3, :] + b_ref[...])

    def decay_matrix(qrow, pp):
        return jnp.where(hrow, qrow[:, pp * LANE:(pp + 1) * LANE],
                         qrow[:, (2 + pp) * LANE:(3 + pp) * LANE])

    def forward(c, carry):
        r0 = pl.multiple_of(c * C, C)
        rows = pl.ds(r0, C)
        xc = conv(xs_ref, wx_ref, bx_ref, c, r0)
        bcv = conv(bc_ref, wbc_ref, bbc_ref, c, r0)
        xc_s[rows, :] = xc.astype(BF16)
        bcs_s[rows, :] = bcv.astype(BF16)
        bmat, cmat = bcv[:, :LANE], bcv[:, LANE:]
        bmb = bmat.astype(BF16)
        dtf, bf = decays(rows, 0)
        dtb, bb = decays(rows, 1)
        bf_t, dtf_t, bb_t, dtb_t = bf.T, dtf.T, bb.T, dtb.T
        qdf = _dot01_r(jnp.exp(bf), e64_ref[...])
        kwf = _dot01_r(dtf * jnp.exp(bf[C - 1:C, :] - bf), e64_ref[...])
        cms = [jnp.where(masks[g], cmat, 0.0).astype(BF16) for g in range(SSD_GROUPS)]
        bms = [jnp.where(masks[g], bmat, 0.0).astype(BF16) for g in range(SSD_GROUPS)]
        scores = [_dot_nt(cms[g], bmb) for g in range(SSD_GROUPS)]
        for pp in range(2):
            hp = hst_ref[0, pp]
            hpb = hp.astype(BF16)
            upd = jnp.zeros((LANE, LANE), F32)
            for g in range(SSD_GROUPS):
                cols = slice((2 * g + pp) * LANE, (2 * g + pp + 1) * LANE)
                xs_pair = xc[:, cols]
                xsb = xs_pair.astype(BF16)
                outs = []
                for e in range(2):
                    h = 4 * g + 2 * pp + e
                    bf_col = jnp.broadcast_to(bf[:, h:h + 1], (C, LANE))
                    bb_col = jnp.broadcast_to(bb[:, h:h + 1], (C, LANE))
                    ef = jnp.where(tril, jnp.exp(jnp.minimum(bf_col - bf_t[h:h + 1, :], 0.0)), 0.0) * dtf_t[h:h + 1, :]
                    eb = jnp.where(triu, jnp.exp(jnp.minimum(bb_col - bb_t[h:h + 1, :], 0.0)), 0.0) * dtb_t[h:h + 1, :]
                    outs.append(_dot((scores[g] * (ef + eb)).astype(BF16), xsb))
                o = jnp.where(m_lo, outs[0], outs[1])
                o += _dot(cms[g], hpb) * qdf[:, cols]
                oacc_ref[rows, cols] = o
                upd += _dot_tn(bms[g], (xs_pair * kwf[:, cols]).astype(BF16))
            hst_ref[0, pp] = hp * decay_matrix(qdf[C - 1:C, :], pp) + upd
        return carry

    lax.fori_loop(0, nc, forward, 0)

    def backward(t, carry):
        c = nc - 1 - t
        r0 = pl.multiple_of(c * C, C)
        rows = pl.ds(r0, C)
        xc = xc_s[rows, :].astype(F32)
        bcv = bcs_s[rows, :]
        bmat, cmat = bcv[:, :LANE], bcv[:, LANE:]
        zero = jnp.zeros_like(bmat)
        dtb, bb = decays(rows, 1)
        qdb = _dot01_r(jnp.exp(bb), e64_ref[...])
        kwb = _dot01_r(dtb * jnp.exp(bb[0:1, :] - bb), e64_ref[...])
        blocks = {}
        for pp in range(2):
            hp = hst_ref[1, pp]
            hpb = hp.astype(BF16)
            upd = jnp.zeros((LANE, LANE), F32)
            for g in range(SSD_GROUPS):
                blk = 2 * g + pp
                cols = slice(blk * LANE, (blk + 1) * LANE)
                xs_pair = xc[:, cols]
                o = oacc_ref[rows, cols] + _dot(jnp.where(masks[g], cmat, zero), hpb) * qdb[:, cols]
                upd += _dot_tn(jnp.where(masks[g], bmat, zero), (xs_pair * kwb[:, cols]).astype(BF16))
                y = (o + d_ref[:, cols] * xs_pair) * _silu(z_ref[rows, cols].astype(F32))
                blocks[blk] = y
            hst_ref[1, pp] = hp * decay_matrix(qdb[0:1, :], pp) + upd
        for g in range(SSD_GROUPS):
            y0, y1 = blocks[2 * g], blocks[2 * g + 1]
            ss = jnp.sum(y0 * y0, axis=-1, keepdims=True) + jnp.sum(y1 * y1, axis=-1, keepdims=True)
            r = lax.rsqrt(ss / (2 * LANE) + EPS)
            for i, yb in enumerate((y0, y1)):
                cols = slice((2 * g + i) * LANE, (2 * g + i + 1) * LANE)
                y_ref[rows, cols] = (yb * r * gain_ref[:, cols]).astype(y_ref.dtype)
        return carry

    lax.fori_loop(0, nc, backward, 0)
    if has_state:
        st_ref[...] = hst_ref[...]


def _ssd_pack_state(st):
    n = st.shape[0]
    st = st.reshape(n, 2, SSD_GROUPS, 2, 2, SSD_STATE, SSD_HEAD_DIM)
    return st.transpose(0, 1, 3, 2, 5, 4, 6).reshape(n, 2, 2, LANE, LANE)


def _ssd_unpack_state(st):
    n = st.shape[0]
    st = st.reshape(n, 2, 2, SSD_GROUPS, SSD_STATE, 2, SSD_HEAD_DIM)
    return st.transpose(0, 1, 3, 2, 5, 4, 6).reshape(n, 2, SSD_HEADS, SSD_STATE, SSD_HEAD_DIM)


def _ssd_params(ssd_conv_w, ssd_conv_b, ssd_a_log, ssd_dt_bias, ssd_d, ssd_norm_g):
    def lanes8(v):
        return jnp.pad(v.astype(F32), ((0, 0), (0, 0), (0, LANE - SSD_HEADS)))
    e64 = _head_expander()
    return dict(wx=ssd_conv_w[:, :, :GROUP_W], wbc=ssd_conv_w[:, :, GROUP_W:],
                bx=ssd_conv_b[:, None, :GROUP_W], bbc=ssd_conv_b[:, None, GROUP_W:],
                a=lanes8(-jnp.exp(ssd_a_log.astype(F32))), dtb=lanes8(ssd_dt_bias),
                d=jnp.repeat(ssd_d, SSD_HEAD_DIM, axis=-1)[:, None, :], gain=ssd_norm_g[:, None, :],
                e64=e64)


def _ssd_scan(um, us, sp, layer, h0, n_seq, T, row_block0):
    has_h0 = h0 is not None
    has_state = not has_h0

    def col(off, w):
        return pl.BlockSpec((T, w), lambda b: (row_block0 + b, off // w))

    def per_layer(arr):
        return pl.BlockSpec((None,) + arr.shape[1:], lambda b: (layer,) + (0,) * (arr.ndim - 1))

    def const(arr):
        return pl.BlockSpec(arr.shape, lambda b: (0,) * arr.ndim)
    names = ('wx', 'wbc', 'bx', 'bbc', 'a', 'dtb', 'd', 'gain')
    in_specs = [col(MAIN_SSD_Z, GROUP_W), col(MAIN_SSD_XS, GROUP_W), col(MAIN_SSD_BC, 2 * LANE),
                pl.BlockSpec((T, LANE), lambda b: (row_block0 + b, SIDE_DT // LANE))]
    in_specs += [per_layer(sp[n]) for n in names] + [const(sp['e64'])]
    args = [um, um, um, us] + [sp[n] for n in names] + [sp['e64']]
    st_spec = pl.BlockSpec((None, 2, 2, LANE, LANE), lambda b: (b, 0, 0, 0, 0))
    if has_h0:
        in_specs.append(st_spec)
        args.append(h0)
    out_specs = [pl.BlockSpec((T, GROUP_W), lambda b: (b, 0))]
    out_shape = [jax.ShapeDtypeStruct((n_seq * T, GROUP_W), BF16)]
    if has_state:
        out_specs.append(st_spec)
        out_shape.append(jax.ShapeDtypeStruct((n_seq, 2, 2, LANE, LANE), F32))
    res = pl.pallas_call(
        functools.partial(_ssd_kernel, T=T, has_h0=has_h0, has_state=has_state),
        grid=(n_seq,),
        in_specs=in_specs,
        out_specs=out_specs,
        out_shape=out_shape,
        scratch_shapes=[pltpu.VMEM((T, GROUP_W), BF16), pltpu.VMEM((T, 2 * LANE), BF16),
                        pltpu.VMEM((T, GROUP_W), F32), pltpu.VMEM((2, 2, LANE, LANE), F32)],
        compiler_params=pltpu.CompilerParams(dimension_semantics=("parallel",), vmem_limit_bytes=VMEM_LIMIT),
        name="ssd",
    )(*args)
    return (res[0], res[1]) if has_state else (res[0], None)


GLA_MACRO = 128


def _log_sigmoid(x):
    return jnp.minimum(x, 0.0) - jnp.log(1.0 + jnp.exp(-jnp.abs(x)))


def _gla_kernel(*refs, T, has_h0, has_state):
    it = iter(refs)
    q_ref, k_ref, v_ref, r_ref, g1_ref = (next(it) for _ in range(5))
    wg_ref, bg_ref, gain_ref, ind_ref = (next(it) for _ in range(4))
    h0_ref = next(it) if has_h0 else None
    y_ref = next(it)
    st_ref = next(it) if has_state else None
    oacc_ref, hst_ref, qbuf, kbuf, bfbuf, bbbuf, vbuf, obuf = (next(it) for _ in range(8))

    C, L = GLA_MACRO, GLA_CHUNK
    nb = C // L
    nc = T // C
    ii = lax.broadcasted_iota(jnp.int32, (C, C), 0)
    jj = lax.broadcasted_iota(jnp.int32, (C, C), 1)
    same = (ii // L) == (jj // L)
    tri_l = jnp.logical_and(same, jj <= ii).astype(BF16)
    tri_u = jnp.logical_and(same, jj >= ii).astype(BF16)
    ones_b = same.astype(BF16)
    lane = lax.broadcasted_iota(jnp.int32, (C, LANE), 1)
    masks = (lane < GLA_QK, lane >= GLA_QK)
    HALF = 8
    ri8 = lax.broadcasted_iota(jnp.int32, (HALF, 2 * LANE), 0)
    bdmask = (lax.broadcasted_iota(jnp.int32, (C, nb * LANE), 0) // L
              == lax.broadcasted_iota(jnp.int32, (C, nb * LANE), 1) // LANE)

    if has_h0:
        hst_ref[...] = h0_ref[...]
    else:
        hst_ref[...] = jnp.zeros_like(hst_ref)

    def log_decay(rows, d):
        g1 = g1_ref[rows, :]
        hi = g1.astype(BF16)
        lo = (g1 - hi.astype(F32)).astype(BF16)
        logits = _dot(hi, wg_ref[d]) + _dot(lo, wg_ref[d]) + bg_ref[d]
        return _log_sigmoid(logits) / GLA_GATE_TEMP

    def recurrence(d, rows, qt, kt, dec, blocks):
        zero = jnp.zeros((), BF16)

        def block_diag(x):
            return jnp.where(bdmask, jnp.concatenate([x] * nb, axis=1), zero)
        upds = []
        for h in range(GLA_HEADS):
            p, e = divmod(h, 2)
            km = jnp.where(masks[e], kt[:, p * LANE:(p + 1) * LANE], 0.0).astype(BF16)
            upds.append(_dot_tn(v_ref[rows, h * LANE:(h + 1) * LANE], block_diag(km)))
        snaps = []
        for h in range(GLA_HEADS):
            lanes = slice((h // 2) * LANE, (h // 2 + 1) * LANE)
            ht = hst_ref[d, h]
            snap = [None] * nb
            for blk in blocks:
                snap[blk] = ht.astype(BF16)
                ht = ht * dec[blk * L:blk * L + 1, lanes] + upds[h][:, blk * LANE:(blk + 1) * LANE]
            hst_ref[d, h] = ht
            snaps.append(jnp.concatenate(snap, axis=1))
        outs = []
        for h in range(GLA_HEADS):
            p, e = divmod(h, 2)
            qm = jnp.where(masks[e], qt[:, p * LANE:(p + 1) * LANE], 0.0).astype(BF16)
            outs.append(_dot_nt(block_diag(qm), snaps[h]))
        return outs

    def intra_block(b0):
        pieces = []
        for j in range(L):
            kj, bfj, bbj = kbuf[b0 + j:b0 + j + 1, :], bfbuf[b0 + j:b0 + j + 1, :], bbbuf[b0 + j:b0 + j + 1, :]
            halves = []
            for s in range(L // HALF):
                rs = slice(b0 + s * HALF, b0 + (s + 1) * HALF)
                qk = qbuf[rs, :] * kj
                if j < s * HALF:
                    e = qk * jnp.exp(bfbuf[rs, :] - bfj)
                elif j >= (s + 1) * HALF:
                    e = qk * jnp.exp(bbbuf[rs, :] - bbj)
                else:
                    rel = ri8 + (s * HALF - j)
                    e = qk * jnp.exp(jnp.where(rel >= 0, bfbuf[rs, :] - bfj, bbbuf[rs, :] - bbj))
                    e = jnp.where(rel == 0, 2.0 * e, e)
                halves.append(e)
            pieces.append(jnp.concatenate(halves, axis=0).astype(BF16))
        spread = _dot(jnp.concatenate(pieces, axis=0), ind_ref[...])
        acc = spread[0:L, :] * vbuf[b0:b0 + 1, :]
        for j in range(1, L):
            acc += spread[j * L:(j + 1) * L, :] * vbuf[b0 + j:b0 + j + 1, :]
        return acc

    def forward_part(rows):
        q = q_ref[rows, :].astype(F32) * (GLA_QK ** -0.5)
        k = k_ref[rows, :].astype(F32)
        la_f, la_b = log_decay(rows, 0), log_decay(rows, 1)
        bf, tot_f = _dot01_l(tri_l, la_f), _dot01_l(ones_b, la_f)
        bb = _dot01_l(tri_u, la_b)
        qbuf[...] = q
        kbuf[...] = k
        bfbuf[...] = bf
        bbbuf[...] = bb
        vbuf[...] = v_ref[rows, :].astype(F32)
        for blk in range(nb):
            obuf[blk * L:(blk + 1) * L, :] = intra_block(blk * L)
        qt, kt, dec = q * jnp.exp(bf), k * jnp.exp(tot_f - bf), jnp.exp(tot_f)
        inter = recurrence(0, rows, qt, kt, dec, range(nb))
        return [obuf[:, h * LANE:(h + 1) * LANE] + inter[h] for h in range(GLA_HEADS)]

    def backward_part(rows):
        q = q_ref[rows, :].astype(F32) * (GLA_QK ** -0.5)
        k = k_ref[rows, :].astype(F32)
        la_b = log_decay(rows, 1)
        bb, tot_b = _dot01_l(tri_u, la_b), _dot01_l(ones_b, la_b)
        qt, kt, dec = q * jnp.exp(bb), k * jnp.exp(tot_b - bb), jnp.exp(tot_b)
        return recurrence(1, rows, qt, kt, dec, range(nb - 1, -1, -1))

    def emit(rows, parts, final):
        for h in range(GLA_HEADS):
            hcols = slice(h * LANE, (h + 1) * LANE)
            if not final:
                oacc_ref[rows, hcols] = parts[h]
            else:
                o = oacc_ref[rows, hcols] + parts[h]
                yn = o * lax.rsqrt(jnp.mean(o * o, axis=-1, keepdims=True) + EPS) * gain_ref[:, hcols]
                y_ref[rows, hcols] = (_silu(r_ref[rows, hcols].astype(F32)) * yn).astype(y_ref.dtype)

    def step(t, final):
        rows_f = pl.ds(pl.multiple_of(t * C, C), C)
        rows_b = pl.ds(pl.multiple_of((nc - 1 - t) * C, C), C)
        parts_b = backward_part(rows_b)
        parts_f = forward_part(rows_f)
        emit(rows_f, parts_f, final)
        emit(rows_b, parts_b, final)

    def first_half(t, carry):
        step(t, False)
        return carry

    def second_half(t, carry):
        step(t, True)
        return carry

    lax.fori_loop(0, nc // 2, first_half, 0)
    lax.fori_loop(nc // 2, nc, second_half, 0)
    if has_state:
        st_ref[...] = hst_ref[...]


def _gla_pack_state(st):
    n = st.shape[0]
    ht = jnp.swapaxes(st, -1, -2)
    z = jnp.zeros_like(ht)
    even = jnp.concatenate([ht, z], axis=-1)
    odd = jnp.concatenate([z, ht], axis=-1)
    sel = (jnp.arange(GLA_HEADS) % 2 == 0)[None, None, :, None, None]
    return jnp.where(sel, even, odd)


def _gla_unpack_state(st):
    even, odd = st[..., :GLA_QK], st[..., GLA_QK:]
    sel = (jnp.arange(GLA_HEADS) % 2 == 0)[None, None, :, None, None]
    return jnp.swapaxes(jnp.where(sel, even, odd), -1, -2)


def _gla_params(gla_w_g2, gla_b_g, gla_norm_g):
    wg = jnp.zeros((DEPTH, 2, LANE, 2 * LANE), F32)
    g1_lane = SIDE_G1 - SIDE_DT
    wg = wg.at[:, :, g1_lane:g1_lane + GLA_GATE_RANK, :].set(gla_w_g2)
    ind = np.zeros((2 * LANE, GROUP_W), np.float32)
    for h in range(GLA_HEADS):
        ind[h * GLA_QK:(h + 1) * GLA_QK, h * GLA_V:(h + 1) * GLA_V] = 1.0
    return dict(wg=wg.astype(BF16), bg=gla_b_g[:, :, None, :], gain=gla_norm_g[:, None, :],
                ind=jnp.asarray(ind, BF16))


def _gla_scan(um, us, gp, layer, h0, n_seq, T, row_block0):
    has_h0 = h0 is not None
    has_state = not has_h0
    C = GLA_MACRO
    assert T % (2 * C) == 0, "the merged forward/backward loop pairs chunk t with chunk nc-1-t"

    def col(off, w):
        return pl.BlockSpec((T, w), lambda b: (row_block0 + b, off // w))

    def per_layer(arr):
        return pl.BlockSpec((None,) + arr.shape[1:], lambda b: (layer,) + (0,) * (arr.ndim - 1))
    in_specs = [col(MAIN_GLA_Q, 2 * LANE), col(MAIN_GLA_K, 2 * LANE), col(MAIN_GLA_V, GROUP_W), col(MAIN_GLA_R, GROUP_W),
                pl.BlockSpec((T, LANE), lambda b: (row_block0 + b, SIDE_DT // LANE)),
                per_layer(gp['wg']), per_layer(gp['bg']), per_layer(gp['gain']),
                pl.BlockSpec(gp['ind'].shape, lambda b: (0, 0))]
    args = [um, um, um, um, us, gp['wg'], gp['bg'], gp['gain'], gp['ind']]
    st_spec = pl.BlockSpec((None, 2, GLA_HEADS, LANE, LANE), lambda b: (b, 0, 0, 0, 0))
    if has_h0:
        in_specs.append(st_spec)
        args.append(h0)
    out_specs = [pl.BlockSpec((T, GROUP_W), lambda b: (b, 0))]
    out_shape = [jax.ShapeDtypeStruct((n_seq * T, GROUP_W), BF16)]
    if has_state:
        out_specs.append(st_spec)
        out_shape.append(jax.ShapeDtypeStruct((n_seq, 2, GLA_HEADS, LANE, LANE), F32))
    res = pl.pallas_call(
        functools.partial(_gla_kernel, T=T, has_h0=has_h0, has_state=has_state),
        grid=(n_seq,),
        in_specs=in_specs,
        out_specs=out_specs,
        out_shape=out_shape,
        scratch_shapes=[pltpu.VMEM((T, GROUP_W), F32), pltpu.VMEM((2, GLA_HEADS, LANE, LANE), F32),
                        pltpu.VMEM((C, 2 * LANE), F32), pltpu.VMEM((C, 2 * LANE), F32),
                        pltpu.VMEM((C, 2 * LANE), F32), pltpu.VMEM((C, 2 * LANE), F32),
                        pltpu.VMEM((C, GROUP_W), F32), pltpu.VMEM((C, GROUP_W), F32)],
        compiler_params=pltpu.CompilerParams(dimension_semantics=("parallel",), vmem_limit_bytes=VMEM_LIMIT),
        name="gla",
    )(*args)
    return (res[0], res[1]) if has_state else (res[0], None)


def kernel(x_prompt, x_sample, cache_mla_kv, state_ssd, state_ret, state_gla, c, c_ctx, w_mod, b_mod, norm_ffn1, ffn1_wg, ffn1_wu, ffn1_wd, norm_mix, w_in, ssd_conv_w, ssd_conv_b, ssd_a_log, ssd_dt_bias, ssd_d, ssd_norm_g, mla_q_lat_gain, mla_w_q_up, mla_q_gain, mla_kv_lat_gain, mla_w_kv_up, mla_k_gain, ret_decay_logit, ret_norm_g, gla_w_g2, gla_b_g, gla_norm_g, w_out, norm_ffn2, ffn2_wg, ffn2_wu, ffn2_wd):
    sp = _ssd_params(ssd_conv_w, ssd_conv_b, ssd_a_log, ssd_dt_bias, ssd_d, ssd_norm_g)
    ssd_h0 = _ssd_pack_state(state_ssd.reshape((DEC_BATCH * DEPTH,) + state_ssd.shape[2:])).reshape(
        DEC_BATCH, DEPTH, 2, 2, LANE, LANE)
    gp = _gla_params(gla_w_g2, gla_b_g, gla_norm_g)
    gla_h0 = _gla_pack_state(state_gla.reshape((DEC_BATCH * DEPTH,) + state_gla.shape[2:])).reshape(
        DEC_BATCH, DEPTH, 2, GLA_HEADS, LANE, LANE)

    main_idx, side_idx = _in_proj_columns()
    w_main = _gather_columns(w_in, main_idx).astype(BF16)
    w_side = _gather_columns(w_in, side_idx).astype(BF16)
    w_out_b = w_out.astype(BF16)
    f1 = (ffn1_wg.astype(BF16), ffn1_wu.astype(BF16), ffn1_wd.astype(BF16))
    f2 = (ffn2_wg.astype(BF16), ffn2_wu.astype(BF16), ffn2_wd.astype(BF16))
    g_ffn1 = norm_ffn1.reshape(DEPTH, 1, D_MODEL)
    g_mix = norm_mix.reshape(DEPTH, 1, D_MODEL)
    g_ffn2 = norm_ffn2.reshape(DEPTH, 1, D_MODEL)
    mw = _mla_weights(mla_q_lat_gain, mla_w_q_up, mla_q_gain, mla_kv_lat_gain, mla_w_kv_up, mla_k_gain)
    cos_lat, sin_lat = _rope_tables(DEC_SEQ)
    cos_all = jnp.concatenate([jnp.ones((TOK_TM, LANE), F32), cos_lat], axis=0)
    sin_all = jnp.concatenate([jnp.zeros((TOK_TM, LANE), F32), sin_lat], axis=0)
    ret_lg = jax.nn.log_sigmoid(ret_decay_logit.astype(F32))
    ret_gain = ret_norm_g.reshape(DEPTH, 1, GROUP_W)
    ret_h0 = state_ret.reshape(DEC_BATCH, DEPTH, 2, 2, LANE, LANE)

    c_all = jnp.zeros((MOD_ROWS, D_MODEL), F32).at[0].set(c_ctx).at[1:1 + DEC_BATCH].set(c)
    mod = _modulation(c_all, w_mod, b_mod).reshape(DEPTH * MOD_ROWS, 1, N_MOD * D_MODEL)

    x = None
    kv_list, ssd_list, ret_list, gla_list = [], [], [], []
    for l in range(DEPTH):
        if l == 0:
            x = _ffn(x_prompt.reshape(CTX_ROWS, D_MODEL), mod, g_ffn1, *f1, l, 0)
            x = _ffn(x_sample.reshape(LAT_ROWS, D_MODEL), mod, g_ffn1, *f1, l, 0, tile0=CTX_ROWS // FFN_TM, into=x)
        else:
            x = _ffn(x, mod, g_ffn1, *f1, l, 0)
        um, us = _in_proj(x, mod, g_mix, w_main, w_side, l)

        ym_c, ym_l, kv_lat = _mla_layer(um, us, cache_mla_kv[:, l], mw, cos_all, sin_all, l)
        yr_c, s_ret = _retention(um, ret_lg[l], ret_gain[l], None, None, BATCH, SEQ, 0)
        yr_l, _ = _retention(um, ret_lg[l], ret_gain[l], ret_h0[:, l], (cos_lat, sin_lat),
                             DEC_BATCH, DEC_SEQ, CTX_ROWS // DEC_SEQ)
        ys_c, s_ssd = _ssd_scan(um, us, sp, l, None, BATCH, SEQ, 0)
        ys_l, _ = _ssd_scan(um, us, sp, l, ssd_h0[:, l], DEC_BATCH, DEC_SEQ, CTX_ROWS // DEC_SEQ)
        yg_c, s_gla = _gla_scan(um, us, gp, l, None, BATCH, SEQ, 0)
        yg_l, _ = _gla_scan(um, us, gp, l, gla_h0[:, l], DEC_BATCH, DEC_SEQ, CTX_ROWS // DEC_SEQ)

        kv_list.append(kv_lat[:CTX_ROWS, :MLA_CACHE_W].reshape(BATCH, SEQ, MLA_CACHE_W))
        ssd_list.append(_ssd_unpack_state(s_ssd))
        ret_list.append(s_ret.reshape(BATCH, 2, RET_HEADS, RET_QK, RET_V))
        gla_list.append(_gla_unpack_state(s_gla))
        x = _out_proj(x, [ys_c, ym_c, yr_c, yg_c], [ys_l, ym_l, yr_l, yg_l], mod, w_out_b, l)
        x = _ffn(x, mod, g_ffn2, *f2, l, 6, split_out=(l == DEPTH - 1))
    y_p = x[0].reshape(BATCH, SEQ, D_MODEL)
    y_s = x[1].reshape(DEC_BATCH, DEC_SEQ, D_MODEL)
    return (y_p, y_s, jnp.stack(kv_list, axis=1), jnp.stack(ssd_list, axis=1),
            jnp.stack(ret_list, axis=1), jnp.stack(gla_list, axis=1))
```
